```python
import math
import jax, jax.numpy as jnp
from jax import lax
import numpy as np

D_MODEL = 1024
BATCH = 8
SEQ = 2048
DEPTH = 1
DEC_BATCH = 128
DEC_SEQ = 4
PAST_LEN = 8192
PAGE_SIZE = 128

N_META = 16
D_POOL = D_MODEL // 2
N_POOL_GROUPS = 4
POOL_GROUP_DIM = D_POOL // N_POOL_GROUPS
POOL_WINDOWS = (2, 4, 8, 16)
POOL_BUF = min(max(POOL_WINDOWS) - 1, PAST_LEN)
N_HEADS = 8
HEAD_DIM = 64
D_ATTN = N_HEADS * HEAD_DIM
N_KV_HEADS = 2
GROUP = N_HEADS // N_KV_HEADS
WINDOW = 128
BLOCK = 128
WIN_BUF = min(WINDOW, PAST_LEN)
REL_BUCKETS = 32
REL_MAX_DIST = 128
D_MIX = D_POOL + D_ATTN
D_KV = N_KV_HEADS * HEAD_DIM
D_IN_PROJ = D_POOL + D_ATTN + 2 * D_KV
D_FF = 4 * D_MODEL
ALPHA = (2.0 * DEPTH) ** 0.25
BETA = (8.0 * DEPTH) ** -0.25
LN_EPS = 1e-5

kernel_name = "hymba_pool_swa_sink_decoder_step"


def layer_norm(x, g, b):
    xf = x.astype(jnp.float32)
    mu = jnp.mean(xf, axis=-1, keepdims=True)
    var = jnp.mean(jnp.square(xf - mu), axis=-1, keepdims=True)
    return ((xf - mu) * lax.rsqrt(var + LN_EPS) * g.astype(jnp.float32) + b.astype(jnp.float32)).astype(x.dtype)


def split_in_proj(h, w_in):
    B, T = h.shape[:2]
    proj = jnp.einsum('btd,de->bte', h, w_in)
    u, q, k, v = jnp.split(proj, [D_POOL, D_POOL + D_ATTN, D_POOL + D_ATTN + D_KV], axis=-1)
    return (u, q.reshape(B, T, N_HEADS, HEAD_DIM),
            k.reshape(B, T, N_KV_HEADS, HEAD_DIM), v.reshape(B, T, N_KV_HEADS, HEAD_DIM))


def pool_mix(u, pos, w_pool, scale):
    T = u.shape[1]
    c = jnp.cumsum(u.astype(jnp.float32), axis=1)
    c = jnp.concatenate([jnp.zeros_like(c[:, :1]), c], axis=1)
    idx = jnp.arange(T)
    outs = []
    for g, w in enumerate(POOL_WINDOWS):
        cg = c[:, :, g * POOL_GROUP_DIM:(g + 1) * POOL_GROUP_DIM]
        lo = jnp.maximum(idx + 1 - w, 0)
        s = cg[:, 1:] - jnp.take(cg, lo, axis=1)
        cnt = jnp.minimum(w, pos + 1).astype(jnp.float32)[None, :, None]
        ug = u[:, :, g * POOL_GROUP_DIM:(g + 1) * POOL_GROUP_DIM].astype(jnp.float32)
        outs.append(s / cnt - ug)
    p = jnp.stack(outs, axis=2).astype(u.dtype)
    z = jnp.einsum('btgc,gcd->btgd', p, w_pool)
    return z.reshape(u.shape) * scale


def rel_bias(dist, table):
    n = jnp.maximum(dist, 0)
    max_exact = REL_BUCKETS // 2
    nf = jnp.maximum(n, 1).astype(jnp.float32)
    large = max_exact + (jnp.log(nf / max_exact) / math.log(REL_MAX_DIST / max_exact)
                         * (REL_BUCKETS - max_exact)).astype(jnp.int32)
    large = jnp.minimum(large, REL_BUCKETS - 1)
    bucket = jnp.where(n < max_exact, n, large)
    return jnp.moveaxis(table[bucket].astype(jnp.float32), -1, 0)


def sink_attend(q, k, v, bias, mask, sinks):
    s = jnp.einsum('bnqhgd,bnkhd->bnhgqk', q, k, preferred_element_type=jnp.float32) * (HEAD_DIM ** -0.5)
    s = s + bias.reshape(N_KV_HEADS, GROUP, bias.shape[1], bias.shape[2])
    s = jnp.where(mask[None, :, None, None], s, -jnp.inf)
    sink = sinks.astype(jnp.float32).reshape(1, 1, N_KV_HEADS, GROUP, 1, 1)
    m = jnp.maximum(jnp.max(s, axis=-1, keepdims=True), sink)
    p = jnp.exp(s - m)
    p = p / (jnp.sum(p, axis=-1, keepdims=True) + jnp.exp(sink - m))
    return jnp.einsum('bnhgqk,bnkhd->bnqhgd', p.astype(v.dtype), v)


def prompt_attention(q, k, v, table, sinks):
    B, L = q.shape[:2]
    pad = BLOCK - N_META
    Lp = L + pad
    nb = Lp // BLOCK

    def padf(t):
        return jnp.pad(t, ((0, 0), (pad, 0)) + ((0, 0),) * (t.ndim - 2))

    def prev(t):
        return jnp.concatenate([jnp.zeros_like(t[:, :1]), t[:, :-1]], axis=1)

    qb = padf(q).reshape(B, nb, BLOCK, N_KV_HEADS, GROUP, HEAD_DIM)
    kb = padf(k).reshape(B, nb, BLOCK, N_KV_HEADS, HEAD_DIM)
    vb = padf(v).reshape(B, nb, BLOCK, N_KV_HEADS, HEAD_DIM)
    kk = jnp.concatenate([prev(kb), kb], axis=2)
    vv = jnp.concatenate([prev(vb), vb], axis=2)
    dist = (jnp.arange(BLOCK)[:, None] + BLOCK) - jnp.arange(2 * BLOCK)[None, :]
    key_pos = (jnp.arange(nb)[:, None] * BLOCK - BLOCK + jnp.arange(2 * BLOCK)[None, :]) - pad
    mask = ((dist >= 0) & (dist < WINDOW))[None] & (key_pos >= 0)[:, None, :]
    o = sink_attend(qb, kk, vv, rel_bias(dist, table), mask, sinks)
    return o.reshape(B, Lp, D_ATTN)[:, pad:]


def sample_attention(q, k, v, k_cache, v_cache, table, sinks):
    DB, T = q.shape[:2]
    kk = jnp.concatenate([k_cache, k], axis=1)
    vv = jnp.concatenate([v_cache, v], axis=1)
    dist = (jnp.arange(T)[:, None] + WIN_BUF) - jnp.arange(WIN_BUF + T)[None, :]
    mask = ((dist >= 0) & (dist < WINDOW))[None]
    o = sink_attend(q.reshape(DB, 1, T, N_KV_HEADS, GROUP, HEAD_DIM), kk[:, None], vv[:, None],
                    rel_bias(dist, table), mask, sinks)
    return o.reshape(DB, T, D_ATTN), kk[:, -WIN_BUF:], vv[:, -WIN_BUF:]


def finish_layer(h, z_pool, o_attn, w_out, ln1_g, ln1_b, w_mlp_in, w_mlp_out, ln2_g, ln2_b):
    mix = jnp.einsum('bte,ed->btd', jnp.concatenate([z_pool, o_attn], axis=-1), w_out)
    h = layer_norm(ALPHA * h + mix, ln1_g, ln1_b)
    f = jnp.einsum('btf,fd->btd', jnp.square(jax.nn.relu(jnp.einsum('btd,df->btf', h, w_mlp_in))), w_mlp_out)
    return layer_norm(ALPHA * h + f, ln2_g, ln2_b)


def setup_inputs(seed: int = 0) -> dict:
    key = jax.random.key(seed)
    ks = jax.random.split(key, 24)

    def nrm(k, shape, s=1.0):
        return jax.random.normal(k, shape, jnp.float32) * s

    return {
        "x_prompt": nrm(ks[0], (BATCH, SEQ, D_MODEL)),
        "x_sample": nrm(ks[1], (DEC_BATCH, DEC_SEQ, D_MODEL)),
        "cache_win_k": nrm(ks[2], (DEPTH, DEC_BATCH, WIN_BUF, N_KV_HEADS, HEAD_DIM)),
        "cache_win_v": nrm(ks[3], (DEPTH, DEC_BATCH, WIN_BUF, N_KV_HEADS, HEAD_DIM)),
        "state_pool": nrm(ks[4], (DEPTH, DEC_BATCH, POOL_BUF, D_POOL)),
        "meta_tokens": nrm(ks[5], (N_META, D_MODEL)),
        "ln_emb_g": 1.0 + nrm(ks[6], (D_MODEL,), 0.1),
        "ln_emb_b": nrm(ks[7], (D_MODEL,), 0.02),
        "rel_table": nrm(ks[8], (REL_BUCKETS, N_HEADS), 0.5),
        "w_in": nrm(ks[9], (DEPTH, D_MODEL, D_IN_PROJ), D_MODEL ** -0.5),
        "w_pool": nrm(ks[10], (DEPTH, N_POOL_GROUPS, POOL_GROUP_DIM, POOL_GROUP_DIM), POOL_GROUP_DIM ** -0.5),
        "pool_scale": 1.0 + nrm(ks[11], (DEPTH, D_POOL), 0.1),
        "sinks": nrm(ks[12], (DEPTH, N_HEADS), 0.5),
        "w_out": nrm(ks[13], (DEPTH, D_MIX, D_MODEL), BETA * D_MIX ** -0.5),
        "ln1_g": 1.0 + nrm(ks[14], (DEPTH, D_MODEL), 0.1),
        "ln1_b": nrm(ks[15], (DEPTH, D_MODEL), 0.02),
        "w_mlp_in": nrm(ks[16], (DEPTH, D_MODEL, D_FF), D_MODEL ** -0.5),
        "w_mlp_out": nrm(ks[17], (DEPTH, D_FF, D_MODEL), BETA * D_FF ** -0.5),
        "ln2_g": 1.0 + nrm(ks[18], (DEPTH, D_MODEL), 0.1),
        "ln2_b": nrm(ks[19], (DEPTH, D_MODEL), 0.02),
    }


def reference(x_prompt, x_sample, cache_win_k, cache_win_v, state_pool, meta_tokens, ln_emb_g, ln_emb_b,
              rel_table, w_in, w_pool, pool_scale, sinks, w_out, ln1_g, ln1_b, w_mlp_in, w_mlp_out,
              ln2_g, ln2_b):
    B = x_prompt.shape[0]
    meta = jnp.broadcast_to(meta_tokens[None].astype(x_prompt.dtype), (B, N_META, D_MODEL))
    hp = layer_norm(jnp.concatenate([meta, x_prompt], axis=1), ln_emb_g, ln_emb_b)
    hs = layer_norm(x_sample, ln_emb_g, ln_emb_b)
    T = hs.shape[1]
    pos_p = jnp.arange(hp.shape[1])
    pos_s = PAST_LEN - POOL_BUF + jnp.arange(POOL_BUF + T)

    nk_p, nv_p, np_p, nk_s, nv_s, np_s = [], [], [], [], [], []
    for l in range(DEPTH):
        u, q, k, v = split_in_proj(hp, w_in[l])
        z_p = pool_mix(u, pos_p, w_pool[l], pool_scale[l])
        o_p = prompt_attention(q, k, v, rel_table, sinks[l])
        nk_p.append(k[:, -WIN_BUF:])
        nv_p.append(v[:, -WIN_BUF:])
        np_p.append(u[:, -POOL_BUF:])
        hp = finish_layer(hp, z_p, o_p, w_out[l], ln1_g[l], ln1_b[l], w_mlp_in[l], w_mlp_out[l], ln2_g[l], ln2_b[l])
        u, q, k, v = split_in_proj(hs, w_in[l])
        u_ext = jnp.concatenate([state_pool[l].astype(u.dtype), u], axis=1)
        z_s = pool_mix(u_ext, pos_s, w_pool[l], pool_scale[l])[:, POOL_BUF:]
        o_s, k_buf, v_buf = sample_attention(q, k, v, cache_win_k[l].astype(k.dtype), cache_win_v[l].astype(v.dtype),
                                             rel_table, sinks[l])
        nk_s.append(k_buf)
        nv_s.append(v_buf)
        np_s.append(u_ext[:, -POOL_BUF:])
        hs = finish_layer(hs, z_s, o_s, w_out[l], ln1_g[l], ln1_b[l], w_mlp_in[l], w_mlp_out[l], ln2_g[l], ln2_b[l])

    y_prompt = hp[:, N_META:]
    return (y_prompt, hs, jnp.stack(nk_p), jnp.stack(nv_p), jnp.stack(np_p),
            jnp.stack(nk_s), jnp.stack(nv_s), jnp.stack(np_s))
```

```python
import functools
import math

import jax
import jax.numpy as jnp
import numpy as np
from jax import lax
from jax.experimental import pallas as pl
from jax.experimental.pallas import tpu as pltpu

N_META = 16
POOL_WINDOWS = (2, 4, 8, 16)
POOL_GROUP_DIM = 128
D_POOL = len(POOL_WINDOWS) * POOL_GROUP_DIM
POOL_BUF = max(POOL_WINDOWS) - 1
N_HEADS = 8
HEAD_DIM = 64
D_ATTN = N_HEADS * HEAD_DIM
N_KV_HEADS = 2
D_KV = N_KV_HEADS * HEAD_DIM
WINDOW = 128
BLOCK = 128
REL_BUCKETS = 32
REL_MAX_DIST = 128
PAST_LEN = 8192
DEPTH = 1
ALPHA = (2.0 * DEPTH) ** 0.25
LN_EPS = 1e-5
Q_SCALE = HEAD_DIM ** -0.5

LANES = 128
SUBLANES = 8
VMEM_LIMIT_BYTES = 56 * 1024 * 1024

PROMPT_TILE = 512
FINISH_TILE = 512
FF_CHUNK = 512
SAMPLE_SEQS = 32
SEQ_GROUP = SUBLANES

F32 = jnp.float32
BF16 = jnp.bfloat16
NEG_INF = float("-inf")


def _rel_bucket(dist):
    n = np.maximum(dist, 0)
    max_exact = REL_BUCKETS // 2
    nf = np.maximum(n, 1).astype(np.float64)
    large = max_exact + (np.log(nf / max_exact) / math.log(REL_MAX_DIST / max_exact)
                         * (REL_BUCKETS - max_exact)).astype(np.int32)
    large = np.minimum(large, REL_BUCKETS - 1)
    return np.where(n < max_exact, n, large).astype(np.int32)


def _bucket_tables(tokens):
    dist = (np.arange(BLOCK)[:, None] + BLOCK) - np.arange(2 * BLOCK)[None, :]
    ok = (dist >= 0) & (dist < WINDOW)
    rest = np.where(ok, _rel_bucket(dist), -1)
    first = np.where(np.arange(2 * BLOCK)[None, :] >= BLOCK - N_META, rest, -1)
    prompt = np.stack([first, rest]).astype(np.int32)
    t = np.repeat(np.arange(tokens), SEQ_GROUP)
    s = np.tile(np.arange(SEQ_GROUP), tokens)
    dist = (t[:, None] + WINDOW) - np.arange(WINDOW)[None, :]
    ok = (dist >= 0) & (dist < WINDOW)
    cache = np.where(ok, _rel_bucket(dist), -1).astype(np.int32)
    dist = t[:, None] - t[None, :]
    ok = (dist >= 0) & (s[:, None] == s[None, :])
    new = np.full((tokens * SEQ_GROUP, LANES), -1, np.int32)
    new[:, :tokens * SEQ_GROUP] = np.where(ok, _rel_bucket(dist), -1)
    return prompt, cache, new


def _layer_norm(x, g, b):
    mu = jnp.mean(x, axis=-1, keepdims=True)
    xc = x - mu
    var = jnp.mean(xc * xc, axis=-1, keepdims=True)
    return xc * lax.rsqrt(var + LN_EPS) * g + b


def _dot(a, b):
    return jnp.dot(a, b, preferred_element_type=F32)


def _dot_nt(a, b):
    return lax.dot_general(a, b, (((1,), (1,)), ((), ())), preferred_element_type=F32)


def _kv_lane_variants(x, low):
    xr = pltpu.roll(x, HEAD_DIM, axis=1)
    zero = jnp.zeros_like(x)
    return (jnp.where(low, x, zero).astype(BF16), jnp.where(low, zero, xr).astype(BF16),
            jnp.where(low, xr, zero).astype(BF16), jnp.where(low, zero, x).astype(BF16))


def _prologue_kernel(tab_ref, bkt_p_ref, bkt_c_ref, bkt_n_ref, meta_ref, g_ref, b_ref, w_in_ref,
                     bias_p_ref, bias_c_ref, bias_n_ref, meta_proj_ref):
    def lookup(bucket, h):
        def body(i, acc):
            return jnp.where(bucket == i, tab_ref[i, h], acc)
        return lax.fori_loop(0, REL_BUCKETS, body, jnp.full(bucket.shape, NEG_INF, F32))

    rows = bkt_c_ref.shape[0]
    for h in range(N_HEADS):
        rest = lookup(bkt_p_ref[1], h)
        bias_p_ref[1, h] = rest
        bias_p_ref[0, h] = jnp.where(bkt_p_ref[0] >= 0, rest, NEG_INF)
        bias_c_ref[h * rows:(h + 1) * rows, :] = lookup(bkt_c_ref[...], h)
        bias_n_ref[h * rows:(h + 1) * rows, :] = lookup(bkt_n_ref[...], h)
    hm = _layer_norm(meta_ref[...], g_ref[...], b_ref[...])
    meta_proj_ref[...] = _dot(hm.astype(BF16), w_in_ref[...])


def _prompt_front_kernel(sink_ref, x_ref, g_ref, b_ref, w_in_ref, w_pool_ref, pscale_ref, bias_ref,
                         meta_proj_ref, mix_ref, nk_ref, nv_ref, np_ref, kvar, vvar, uext):
    t = pl.program_id(1)
    tile = x_ref.shape[1]
    halo = 2 * SUBLANES
    low = lax.broadcasted_iota(jnp.int32, (1, LANES), 1) < HEAD_DIM

    @pl.when(t == 0)
    def _():
        pad = jnp.zeros((BLOCK - N_META, LANES), BF16)
        kq = _kv_lane_variants(meta_proj_ref[:, D_POOL + D_ATTN:D_POOL + D_ATTN + D_KV], low)
        vq = _kv_lane_variants(meta_proj_ref[:, D_POOL + D_ATTN + D_KV:], low)
        for i in range(4):
            kvar[i, 0:BLOCK, :] = jnp.concatenate([pad, kq[i]], axis=0)
            vvar[i, 0:BLOCK, :] = jnp.concatenate([pad, vq[i]], axis=0)
        uext[0:halo, :] = meta_proj_ref[:, 0:D_POOL]

    h = _layer_norm(x_ref[0], g_ref[...], b_ref[...])
    proj = _dot(h.astype(BF16), w_in_ref[...])
    u = proj[:, 0:D_POOL]
    q = (proj[:, D_POOL:D_POOL + D_ATTN] * Q_SCALE).astype(BF16)
    k = proj[:, D_POOL + D_ATTN:D_POOL + D_ATTN + D_KV]
    v = proj[:, D_POOL + D_ATTN + D_KV:]

    uext[halo:halo + tile, :] = u
    zs = []
    for g, w in enumerate(POOL_WINDOWS):
        cols = slice(g * POOL_GROUP_DIM, (g + 1) * POOL_GROUP_DIM)
        s = uext[halo:halo + tile, cols]
        for i in range(1, w):
            s = s + uext[halo - i:halo - i + tile, cols]
        p = s * (1.0 / w) - u[:, cols]
        zs.append(_dot(p.astype(BF16), w_pool_ref[g]))
    z = jnp.concatenate(zs, axis=1) * pscale_ref[...]
    mix_ref[0, :, 0:D_POOL] = z.astype(BF16)

    kq = _kv_lane_variants(k, low)
    vq = _kv_lane_variants(v, low)
    for i in range(4):
        kvar[i, BLOCK:BLOCK + tile, :] = kq[i]
        vvar[i, BLOCK:BLOCK + tile, :] = vq[i]
    for j in range(tile // BLOCK):
        rows = slice(j * BLOCK, (j + 1) * BLOCK)
        keys = slice(j * BLOCK, (j + 2) * BLOCK)
        sel = jnp.where(t == 0, 0, 1) if j == 0 else 1
        for kv in range(N_KV_HEADS):
            tiles = (2 * kv, 2 * kv + 1)
            lhs = jnp.concatenate([q[rows, p * LANES:(p + 1) * LANES] for p in tiles], axis=0)
            s_lo = _dot_nt(lhs, kvar[2 * kv, keys, :])
            s_hi = _dot_nt(lhs, kvar[2 * kv + 1, keys, :])
            for n, p in enumerate(tiles):
                acc = None
                for half, s_all in enumerate((s_lo, s_hi)):
                    hd = 2 * p + half
                    s = s_all[n * BLOCK:(n + 1) * BLOCK] + bias_ref[sel, hd]
                    sink = sink_ref[0, hd]
                    m = jnp.maximum(jnp.max(s, axis=-1, keepdims=True), sink)
                    e = jnp.exp(s - m)
                    den = jnp.sum(e, axis=-1, keepdims=True) + jnp.exp(sink - m)
                    o = _dot(e.astype(BF16), vvar[2 * kv + half, keys, :]) * (1.0 / den)
                    acc = o if acc is None else acc + o
                mix_ref[0, rows, D_POOL + p * LANES:D_POOL + (p + 1) * LANES] = acc.astype(BF16)

    for i in range(4):
        kvar[i, 0:BLOCK, :] = kvar[i, tile:tile + BLOCK, :]
        vvar[i, 0:BLOCK, :] = vvar[i, tile:tile + BLOCK, :]
    uext[0:halo, :] = uext[tile:tile + halo, :]
    nk_ref[0] = k[tile - BLOCK:, :]
    nv_ref[0] = v[tile - BLOCK:, :]
    np_ref[0] = u[tile - halo:, :]


def _sample_front_kernel(tokens, sink_ref, x_ref, ck_ref, cv_ref, st_ref, g_ref, b_ref, w_in_ref,
                         w_pool_ref, pscale_ref, bias_c_ref, bias_n_ref,
                         mix_ref, nk_ref, nv_ref, np_ref, o_s):
    ns = x_ref.shape[0]
    d_model = x_ref.shape[1] // tokens
    low = lax.broadcasted_iota(jnp.int32, (1, LANES), 1) < HEAD_DIM

    x = jnp.concatenate([x_ref[:, t * d_model:(t + 1) * d_model] for t in range(tokens)], axis=0)
    h = _layer_norm(x, g_ref[...], b_ref[...])
    proj = _dot(h.astype(BF16), w_in_ref[...])
    u = proj[:, 0:D_POOL]
    q = proj[:, D_POOL:D_POOL + D_ATTN] * Q_SCALE
    k = proj[:, D_POOL + D_ATTN:D_POOL + D_ATTN + D_KV]
    v = proj[:, D_POOL + D_ATTN + D_KV:]

    ext = [st_ref[:, r * D_POOL:(r + 1) * D_POOL] for r in range(POOL_BUF)]
    ext += [u[t * ns:(t + 1) * ns] for t in range(tokens)]
    ps = []
    for t in range(tokens):
        idx = POOL_BUF + t
        pos = PAST_LEN - POOL_BUF + idx
        parts = []
        for g, w in enumerate(POOL_WINDOWS):
            cols = slice(g * POOL_GROUP_DIM, (g + 1) * POOL_GROUP_DIM)
            lo = max(idx + 1 - w, 0)
            s = ext[idx][:, cols]
            for r in range(idx - 1, lo - 1, -1):
                s = s + ext[r][:, cols]
            parts.append(s * (1.0 / min(w, pos + 1)) - ext[idx][:, cols])
        ps.append(jnp.concatenate(parts, axis=1))
    p = jnp.concatenate(ps, axis=0).astype(BF16)
    zs = [_dot(p[:, g * POOL_GROUP_DIM:(g + 1) * POOL_GROUP_DIM], w_pool_ref[g])
          for g in range(len(POOL_WINDOWS))]
    z = (jnp.concatenate(zs, axis=1) * pscale_ref[...]).astype(BF16)
    keep = POOL_BUF - tokens
    np_ref[:, 0:keep * D_POOL] = st_ref[:, tokens * D_POOL:]
    for t in range(tokens):
        np_ref[:, (keep + t) * D_POOL:(keep + t + 1) * D_POOL] = u[t * ns:(t + 1) * ns]

    keep = WINDOW - tokens
    nk_ref[:, 0:keep, :] = ck_ref[:, tokens:, :]
    nv_ref[:, 0:keep, :] = cv_ref[:, tokens:, :]
    for t in range(tokens):
        nk_ref[:, keep + t, :] = k[t * ns:(t + 1) * ns]
        nv_ref[:, keep + t, :] = v[t * ns:(t + 1) * ns]

    qh = []
    for hd in range(N_HEADS):
        tl = q[:, (hd // 2) * LANES:(hd // 2 + 1) * LANES]
        kv = hd // (N_HEADS // N_KV_HEADS)
        want_low = kv == 0
        is_low = hd % 2 == 0
        src = tl if want_low == is_low else pltpu.roll(tl, HEAD_DIM, axis=1)
        qh.append(jnp.where(low, src, 0.0) if want_low else jnp.where(low, 0.0, src))

    gr = SEQ_GROUP
    rows_per_head = tokens * gr
    sub = lax.broadcasted_iota(jnp.int32, (N_HEADS * rows_per_head, 1), 0) % gr
    sink = jnp.concatenate([jnp.full((rows_per_head, 1), sink_ref[0, hd], F32) for hd in range(N_HEADS)],
                           axis=0)
    zpad = jnp.zeros((LANES - rows_per_head, LANES), F32)
    for gi in range(ns // gr):
        base = gi * gr
        lhs = jnp.concatenate([qh[hd][t * ns + base:t * ns + base + gr]
                               for hd in range(N_HEADS) for t in range(tokens)], axis=0).astype(BF16)
        sc = None
        for s in range(gr):
            s_one = _dot_nt(lhs, ck_ref[base + s].astype(BF16))
            sc = s_one if sc is None else jnp.where(sub == s, s_one, sc)
        sc = sc + bias_c_ref[...]
        k_new = jnp.concatenate([k[t * ns + base:t * ns + base + gr] for t in range(tokens)] + [zpad], axis=0)
        v_new = jnp.concatenate([v[t * ns + base:t * ns + base + gr] for t in range(tokens)] + [zpad], axis=0)
        sn = _dot_nt(lhs, k_new.astype(BF16)) + bias_n_ref[...]
        m = jnp.maximum(jnp.maximum(jnp.max(sc, axis=-1, keepdims=True),
                                    jnp.max(sn, axis=-1, keepdims=True)), sink)
        ec = jnp.exp(sc - m)
        en = jnp.exp(sn - m)
        den = (jnp.sum(ec, axis=-1, keepdims=True) + jnp.sum(en, axis=-1, keepdims=True)
               + jnp.exp(sink - m))
        o = _dot(en.astype(BF16), v_new.astype(BF16))
        for s in range(gr):
            o = o + _dot(jnp.where(sub == s, ec, 0.0).astype(BF16), cv_ref[base + s].astype(BF16))
        o = o * (1.0 / den)
        for t in range(tokens):
            for p in range(N_HEADS // 2):
                a = o[(2 * p * tokens + t) * gr:(2 * p * tokens + t + 1) * gr]
                c = o[((2 * p + 1) * tokens + t) * gr:((2 * p + 1) * tokens + t + 1) * gr]
                if p < N_HEADS // 4:
                    c = pltpu.roll(c, HEAD_DIM, axis=1)
                else:
                    a = pltpu.roll(a, HEAD_DIM, axis=1)
                o_s[t * ns + base:t * ns + base + gr, p * LANES:(p + 1) * LANES] = jnp.where(low, a, c)

    for t in range(tokens):
        mix_ref[:, t * d_model:t * d_model + D_POOL] = z[t * ns:(t + 1) * ns]
        mix_ref[:, t * d_model + D_POOL:(t + 1) * d_model] = o_s[t * ns:(t + 1) * ns, :].astype(BF16)


def _finish_kernel(tokens, x_ref, mix_ref, ge_ref, be_ref, w_out_ref, g1_ref, b1_ref, w1_ref, w2_ref,
                   g2_ref, b2_ref, y_ref):
    d_model = x_ref.shape[1] // tokens
    if tokens == 1:
        x, mix = x_ref[...], mix_ref[...]
    else:
        x = jnp.concatenate([x_ref[:, t * d_model:(t + 1) * d_model] for t in range(tokens)], axis=0)
        mix = jnp.concatenate([mix_ref[:, t * d_model:(t + 1) * d_model] for t in range(tokens)], axis=0)
    h = _layer_norm(x, ge_ref[...], be_ref[...])
    h = _layer_norm(ALPHA * h + _dot(mix, w_out_ref[...]), g1_ref[...], b1_ref[...])
    hb = h.astype(BF16)
    f = None
    for c in range(w1_ref.shape[1] // FF_CHUNK):
        a = _dot(hb, w1_ref[:, c * FF_CHUNK:(c + 1) * FF_CHUNK])
        a = jnp.square(jnp.maximum(a, 0.0)).astype(BF16)
        d = _dot(a, w2_ref[c * FF_CHUNK:(c + 1) * FF_CHUNK, :])
        f = d if f is None else f + d
    y = _layer_norm(ALPHA * h + f, g2_ref[...], b2_ref[...])
    if tokens == 1:
        y_ref[...] = y
    else:
        rows = y_ref.shape[0]
        for t in range(tokens):
            y_ref[:, t * d_model:(t + 1) * d_model] = y[t * rows:(t + 1) * rows]


def _const_spec(shape):
    return pl.BlockSpec(shape, lambda *_: (0,) * len(shape), pipeline_mode=pl.Buffered(1))


def _smem_spec():
    return pl.BlockSpec(memory_space=pltpu.SMEM)


def _finish(x, mix, tokens, tile, weights):
    rows, width = x.shape
    row_spec = pl.BlockSpec((tile, width), lambda i: (i, 0))
    return pl.pallas_call(
        functools.partial(_finish_kernel, tokens),
        grid=(rows // tile,),
        in_specs=[row_spec, row_spec] + [_const_spec(w.shape) for w in weights],
        out_specs=row_spec,
        out_shape=jax.ShapeDtypeStruct((rows, width), F32),
        compiler_params=pltpu.CompilerParams(dimension_semantics=("arbitrary",),
                                             vmem_limit_bytes=VMEM_LIMIT_BYTES),
        name="finish_t%d" % tokens,
    )(x, mix, *weights)


def kernel(x_prompt, x_sample, cache_win_k, cache_win_v, state_pool, meta_tokens, ln_emb_g, ln_emb_b,
           rel_table, w_in, w_pool, pool_scale, sinks, w_out, ln1_g, ln1_b, w_mlp_in, w_mlp_out,
           ln2_g, ln2_b):
    batch, seq, d_model = x_prompt.shape
    n_seq, tokens, _ = x_sample.shape
    assert w_in.shape[0] == DEPTH and d_model == D_POOL + D_ATTN
    assert seq % PROMPT_TILE == 0 and (batch * seq) % FINISH_TILE == 0 and n_seq % SAMPLE_SEQS == 0
    assert cache_win_k.shape[2] == WINDOW and state_pool.shape[2] == POOL_BUF and tokens <= POOL_BUF
    d_in = w_in.shape[2]

    row = lambda a: a.reshape(1, -1).astype(F32)
    ge, be = row(ln_emb_g), row(ln_emb_b)
    w_in_b = w_in[0].astype(BF16)
    w_pool_b = w_pool[0].astype(BF16)
    pscale = row(pool_scale[0])
    sink = row(sinks[0])
    fin_w = (ge, be, w_out[0].astype(BF16), row(ln1_g[0]), row(ln1_b[0]), w_mlp_in[0].astype(BF16),
             w_mlp_out[0].astype(BF16), row(ln2_g[0]), row(ln2_b[0]))
    cparams = lambda sem: pltpu.CompilerParams(dimension_semantics=sem, vmem_limit_bytes=VMEM_LIMIT_BYTES)

    bkt_p, bkt_c, bkt_n = _bucket_tables(tokens)
    srows = N_HEADS * tokens * SEQ_GROUP
    bias_p, bias_c, bias_n, meta_proj = pl.pallas_call(
        _prologue_kernel,
        in_specs=[_smem_spec()] + [pl.BlockSpec(memory_space=pltpu.VMEM)] * 7,
        out_specs=[pl.BlockSpec(memory_space=pltpu.VMEM)] * 4,
        out_shape=[jax.ShapeDtypeStruct((2, N_HEADS, BLOCK, 2 * BLOCK), F32),
                   jax.ShapeDtypeStruct((srows, WINDOW), F32),
                   jax.ShapeDtypeStruct((srows, LANES), F32),
                   jax.ShapeDtypeStruct((N_META, d_in), F32)],
        compiler_params=pltpu.CompilerParams(vmem_limit_bytes=VMEM_LIMIT_BYTES),
        name="prologue",
    )(rel_table.astype(F32), jnp.asarray(bkt_p), jnp.asarray(bkt_c), jnp.asarray(bkt_n),
      meta_tokens.astype(F32), ge, be, w_in_b)

    n_tiles = seq // PROMPT_TILE
    halo = 2 * SUBLANES
    mix_p, nk_p, nv_p, np_p = pl.pallas_call(
        _prompt_front_kernel,
        grid=(batch, n_tiles),
        in_specs=[_smem_spec(),
                  pl.BlockSpec((1, PROMPT_TILE, d_model), lambda b, t: (b, t, 0)),
                  _const_spec(ge.shape), _const_spec(be.shape), _const_spec(w_in_b.shape),
                  _const_spec(w_pool_b.shape), _const_spec(pscale.shape), _const_spec(bias_p.shape),
                  _const_spec(meta_proj.shape)],
        out_specs=[pl.BlockSpec((1, PROMPT_TILE, d_model), lambda b, t: (b, t, 0)),
                   pl.BlockSpec((1, BLOCK, D_KV), lambda b, t: (b, 0, 0)),
                   pl.BlockSpec((1, BLOCK, D_KV), lambda b, t: (b, 0, 0)),
                   pl.BlockSpec((1, halo, D_POOL), lambda b, t: (b, 0, 0))],
        out_shape=[jax.ShapeDtypeStruct((batch, seq, d_model), BF16),
                   jax.ShapeDtypeStruct((batch, BLOCK, D_KV), F32),
                   jax.ShapeDtypeStruct((batch, BLOCK, D_KV), F32),
                   jax.ShapeDtypeStruct((batch, halo, D_POOL), F32)],
        scratch_shapes=[pltpu.VMEM((4, PROMPT_TILE + BLOCK, LANES), BF16),
                        pltpu.VMEM((4, PROMPT_TILE + BLOCK, LANES), BF16),
                        pltpu.VMEM((PROMPT_TILE + halo, D_POOL), F32)],
        compiler_params=cparams(("arbitrary", "arbitrary")),
        name="prompt_front",
    )(sink, x_prompt, ge, be, w_in_b, w_pool_b, pscale, bias_p, meta_proj)

    xs = x_sample.reshape(n_seq, tokens * d_model)
    ck = cache_win_k[0].reshape(n_seq, WINDOW, D_KV)
    cv = cache_win_v[0].reshape(n_seq, WINDOW, D_KV)
    st = state_pool[0].reshape(n_seq, POOL_BUF * D_POOL)
    seq_spec = lambda *tail: pl.BlockSpec((SAMPLE_SEQS,) + tail, lambda i: (i,) + (0,) * len(tail))
    mix_s, nk_s, nv_s, np_s = pl.pallas_call(
        functools.partial(_sample_front_kernel, tokens),
        grid=(n_seq // SAMPLE_SEQS,),
        in_specs=[_smem_spec(), seq_spec(tokens * d_model), seq_spec(WINDOW, D_KV), seq_spec(WINDOW, D_KV),
                  seq_spec(POOL_BUF * D_POOL),
                  _const_spec(ge.shape), _const_spec(be.shape), _const_spec(w_in_b.shape),
                  _const_spec(w_pool_b.shape), _const_spec(pscale.shape), _const_spec(bias_c.shape),
                  _const_spec(bias_n.shape)],
        out_specs=[seq_spec(tokens * d_model), seq_spec(WINDOW, D_KV), seq_spec(WINDOW, D_KV),
                   seq_spec(POOL_BUF * D_POOL)],
        out_shape=[jax.ShapeDtypeStruct((n_seq, tokens * d_model), BF16),
                   jax.ShapeDtypeStruct((n_seq, WINDOW, D_KV), F32),
                   jax.ShapeDtypeStruct((n_seq, WINDOW, D_KV), F32),
                   jax.ShapeDtypeStruct((n_seq, POOL_BUF * D_POOL), F32)],
        scratch_shapes=[pltpu.VMEM((tokens * SAMPLE_SEQS, D_ATTN), F32)],
        compiler_params=cparams(("arbitrary",)),
        name="sample_front",
    )(sink, xs, ck, cv, st, ge, be, w_in_b, w_pool_b, pscale, bias_c, bias_n)

    y_p = _finish(x_prompt.reshape(batch * seq, d_model), mix_p.reshape(batch * seq, d_model), 1,
                  FINISH_TILE, fin_w)
    y_s = _finish(xs, mix_s, tokens, n_seq, fin_w)

    kv_shape = (DEPTH, -1, WINDOW, N_KV_HEADS, HEAD_DIM)
    return (y_p.reshape(batch, seq, d_model), y_s.reshape(n_seq, tokens, d_model),
            nk_p.reshape(kv_shape), nv_p.reshape(kv_shape),
            np_p[:, halo - POOL_BUF:, :].reshape(DEPTH, batch, POOL_BUF, D_POOL),
            nk_s.reshape(kv_shape), nv_s.reshape(kv_shape),
            np_s.reshape(DEPTH, n_seq, POOL_BUF, D_POOL))
```

```python
import functools
import math

import jax
import jax.numpy as jnp
import numpy as np
from jax import lax
from jax.experimental import pallas as pl
from jax.experimental.pallas import tpu as pltpu

N_META = 16
POOL_WINDOWS = (2, 4, 8, 16)
POOL_GROUP_DIM = 128
D_POOL = len(POOL_WINDOWS) * POOL_GROUP_DIM
POOL_BUF = max(POOL_WINDOWS) - 1
N_HEADS = 8
HEAD_DIM = 64
D_ATTN = N_HEADS * HEAD_DIM
N_KV_HEADS = 2
D_KV = N_KV_HEADS * HEAD_DIM
WINDOW = 128
BLOCK = 128
REL_BUCKETS = 32
REL_MAX_DIST = 128
PAST_LEN = 8192
DEPTH = 1
ALPHA = (2.0 * DEPTH) ** 0.25
LN_EPS = 1e-5
Q_SCALE = HEAD_DIM ** -0.5

LANES = 128
SUBLANES = 8
VMEM_LIMIT_BYTES = 56 * 1024 * 1024

PROMPT_TILE = 512
FF_CHUNK = 256
SAMPLE_SEQS = 32
SEQ_GROUP = SUBLANES

F32 = jnp.float32
BF16 = jnp.bfloat16
NEG_INF = float("-inf")


def _rel_bucket(dist):
    n = np.maximum(dist, 0)
    max_exact = REL_BUCKETS // 2
    nf = np.maximum(n, 1).astype(np.float64)
    large = max_exact + (np.log(nf / max_exact) / math.log(REL_MAX_DIST / max_exact)
                         * (REL_BUCKETS - max_exact)).astype(np.int32)
    large = np.minimum(large, REL_BUCKETS - 1)
    return np.where(n < max_exact, n, large).astype(np.int32)


def _bucket_tables(tokens):
    dist = (np.arange(BLOCK)[:, None] + BLOCK) - np.arange(2 * BLOCK)[None, :]
    ok = (dist >= 0) & (dist < WINDOW)
    rest = np.where(ok, _rel_bucket(dist), -1)
    first = np.where(np.arange(2 * BLOCK)[None, :] >= BLOCK - N_META, rest, -1)
    prompt = np.stack([first, rest]).astype(np.int32)
    t = np.repeat(np.arange(tokens), SEQ_GROUP)
    s = np.tile(np.arange(SEQ_GROUP), tokens)
    dist = (t[:, None] + WINDOW) - np.arange(WINDOW)[None, :]
    ok = (dist >= 0) & (dist < WINDOW)
    cache = np.where(ok, _rel_bucket(dist), -1).astype(np.int32)
    dist = t[:, None] - t[None, :]
    ok = (dist >= 0) & (s[:, None] == s[None, :])
    new = np.full((tokens * SEQ_GROUP, LANES), -1, np.int32)
    new[:, :tokens * SEQ_GROUP] = np.where(ok, _rel_bucket(dist), -1)
    return prompt, cache, new


def _layer_norm(x, g, b):
    mu = jnp.mean(x, axis=-1, keepdims=True)
    xc = x - mu
    var = jnp.mean(xc * xc, axis=-1, keepdims=True)
    return xc * lax.rsqrt(var + LN_EPS) * g + b


def _dot(a, b):
    return jnp.dot(a, b, preferred_element_type=F32)


def _dot_nt(a, b):
    return lax.dot_general(a, b, (((1,), (1,)), ((), ())), preferred_element_type=F32)


def _kv_lane_variants(x, low):
    xr = pltpu.roll(x, HEAD_DIM, axis=1)
    zero = jnp.zeros_like(x)
    return (jnp.where(low, x, zero).astype(BF16), jnp.where(low, zero, xr).astype(BF16),
            jnp.where(low, xr, zero).astype(BF16), jnp.where(low, zero, x).astype(BF16))


def _prologue_kernel(tab_ref, bkt_p_ref, bkt_c_ref, bkt_n_ref, meta_ref, g_ref, b_ref, w_in_ref,
                     bias_p_ref, bias_c_ref, bias_n_ref, meta_proj_ref):
    def lookup(bucket, h):
        def body(i, acc):
            return jnp.where(bucket == i, tab_ref[i, h], acc)
        return lax.fori_loop(0, REL_BUCKETS, body, jnp.full(bucket.shape, NEG_INF, F32))

    rows = bkt_c_ref.shape[0]
    for h in range(N_HEADS):
        rest = lookup(bkt_p_ref[1], h)
        bias_p_ref[1, h] = rest
        bias_p_ref[0, h] = jnp.where(bkt_p_ref[0] >= 0, rest, NEG_INF)
        bias_c_ref[h * rows:(h + 1) * rows, :] = lookup(bkt_c_ref[...], h)
        bias_n_ref[h * rows:(h + 1) * rows, :] = lookup(bkt_n_ref[...], h)
    hm = _layer_norm(meta_ref[...], g_ref[...], b_ref[...])
    meta_proj_ref[...] = _dot(hm.astype(BF16), w_in_ref[...])


HALO = 2 * SUBLANES


def _carry_init(meta_proj_ref, kvar, vvar, uext):
    low = lax.broadcasted_iota(jnp.int32, (1, LANES), 1) < HEAD_DIM
    pad = jnp.zeros((BLOCK - N_META, LANES), BF16)
    kq = _kv_lane_variants(meta_proj_ref[:, D_POOL + D_ATTN:D_POOL + D_ATTN + D_KV], low)
    vq = _kv_lane_variants(meta_proj_ref[:, D_POOL + D_ATTN + D_KV:], low)
    for i in range(4):
        kvar[i, 0:BLOCK, :] = jnp.concatenate([pad, kq[i]], axis=0)
        vvar[i, 0:BLOCK, :] = jnp.concatenate([pad, vq[i]], axis=0)
    uext[0:HALO, :] = meta_proj_ref[:, 0:D_POOL]


def _front_pieces(first_tile, x_ref, sink_ref, g_ref, b_ref, w_in_ref, w_pool_ref, pscale_ref, bias_ref,
                  mix_ref, nk_ref, nv_ref, np_ref, kvar, vvar, uext, qbuf):
    tile = x_ref.shape[0]
    halo = HALO
    low = lax.broadcasted_iota(jnp.int32, (1, LANES), 1) < HEAD_DIM

    for j in range(tile // BLOCK):
        rows = slice(j * BLOCK, (j + 1) * BLOCK)
        h = _layer_norm(x_ref[rows, :], g_ref[...], b_ref[...])
        proj = _dot(h.astype(BF16), w_in_ref[...])
        u = proj[:, 0:D_POOL]
        k = proj[:, D_POOL + D_ATTN:D_POOL + D_ATTN + D_KV]
        v = proj[:, D_POOL + D_ATTN + D_KV:]
        qbuf[rows, :] = (proj[:, D_POOL:D_POOL + D_ATTN] * Q_SCALE).astype(BF16)
        uext[halo + j * BLOCK:halo + (j + 1) * BLOCK, :] = u
        kq = _kv_lane_variants(k, low)
        vq = _kv_lane_variants(v, low)
        for i in range(4):
            kvar[i, BLOCK + j * BLOCK:BLOCK + (j + 1) * BLOCK, :] = kq[i]
            vvar[i, BLOCK + j * BLOCK:BLOCK + (j + 1) * BLOCK, :] = vq[i]
        if j == tile // BLOCK - 1:
            nk_ref[...] = k
            nv_ref[...] = v
            np_ref[...] = u[BLOCK - halo:, :]
        yield

    for g, w in enumerate(POOL_WINDOWS):
        cols = slice(g * POOL_GROUP_DIM, (g + 1) * POOL_GROUP_DIM)
        ug = uext[halo:halo + tile, cols]
        s = ug
        for i in range(1, w):
            s = s + uext[halo - i:halo - i + tile, cols]
        p = s * (1.0 / w) - ug
        mix_ref[:, cols] = (_dot(p.astype(BF16), w_pool_ref[g]) * pscale_ref[:, cols]).astype(BF16)
        yield

    for j in range(tile // BLOCK):
        rows = slice(j * BLOCK, (j + 1) * BLOCK)
        keys = slice(j * BLOCK, (j + 2) * BLOCK)
        sel = jnp.where(first_tile, 0, 1) if j == 0 else 1
        for kv in range(N_KV_HEADS):
            tiles = (2 * kv, 2 * kv + 1)
            lhs = jnp.concatenate([qbuf[rows, p * LANES:(p + 1) * LANES] for p in tiles], axis=0)
            s_lo = _dot_nt(lhs, kvar[2 * kv, keys, :])
            s_hi = _dot_nt(lhs, kvar[2 * kv + 1, keys, :])
            for n, p in enumerate(tiles):
                acc = None
                for half, s_all in enumerate((s_lo, s_hi)):
                    hd = 2 * p + half
                    s = s_all[n * BLOCK:(n + 1) * BLOCK] + bias_ref[sel, hd]
                    sink = sink_ref[0, hd]
                    m = jnp.maximum(jnp.max(s, axis=-1, keepdims=True), sink)
                    e = jnp.exp(s - m)
                    den = jnp.sum(e, axis=-1, keepdims=True) + jnp.exp(sink - m)
                    o = _dot(e.astype(BF16), vvar[2 * kv + half, keys, :]) * (1.0 / den)
                    acc = o if acc is None else acc + o
                mix_ref[rows, D_POOL + p * LANES:D_POOL + (p + 1) * LANES] = acc.astype(BF16)
            yield

    for i in range(4):
        kvar[i, 0:BLOCK, :] = kvar[i, tile:tile + BLOCK, :]
        vvar[i, 0:BLOCK, :] = vvar[i, tile:tile + BLOCK, :]
    uext[0:halo, :] = uext[tile:tile + halo, :]


def _sample_front_kernel(tokens, sink_ref, x_ref, ck_ref, cv_ref, st_ref, g_ref, b_ref, w_in_ref,
                         w_pool_ref, pscale_ref, bias_c_ref, bias_n_ref,
                         mix_ref, nk_ref, nv_ref, np_ref, o_s):
    ns = x_ref.shape[0]
    d_model = x_ref.shape[1] // tokens
    low = lax.broadcasted_iota(jnp.int32, (1, LANES), 1) < HEAD_DIM

    x = jnp.concatenate([x_ref[:, t * d_model:(t + 1) * d_model] for t in range(tokens)], axis=0)
    h = _layer_norm(x, g_ref[...], b_ref[...])
    proj = _dot(h.astype(BF16), w_in_ref[...])
    u = proj[:, 0:D_POOL]
    q = proj[:, D_POOL:D_POOL + D_ATTN] * Q_SCALE
    k = proj[:, D_POOL + D_ATTN:D_POOL + D_ATTN + D_KV]
    v = proj[:, D_POOL + D_ATTN + D_KV:]

    ext = [st_ref[:, r * D_POOL:(r + 1) * D_POOL] for r in range(POOL_BUF)]
    ext += [u[t * ns:(t + 1) * ns] for t in range(tokens)]
    ps = []
    for t in range(tokens):
        idx = POOL_BUF + t
        pos = PAST_LEN - POOL_BUF + idx
        parts = []
        for g, w in enumerate(POOL_WINDOWS):
            cols = slice(g * POOL_GROUP_DIM, (g + 1) * POOL_GROUP_DIM)
            lo = max(idx + 1 - w, 0)
            s = ext[idx][:, cols]
            for r in range(idx - 1, lo - 1, -1):
                s = s + ext[r][:, cols]
            parts.append(s * (1.0 / min(w, pos + 1)) - ext[idx][:, cols])
        ps.append(jnp.concatenate(parts, axis=1))
    p = jnp.concatenate(ps, axis=0).astype(BF16)
    zs = [_dot(p[:, g * POOL_GROUP_DIM:(g + 1) * POOL_GROUP_DIM], w_pool_ref[g])
          for g in range(len(POOL_WINDOWS))]
    z = (jnp.concatenate(zs, axis=1) * pscale_ref[...]).astype(BF16)
    keep = POOL_BUF - tokens
    np_ref[:, 0:keep * D_POOL] = st_ref[:, tokens * D_POOL:]
    for t in range(tokens):
        np_ref[:, (keep + t) * D_POOL:(keep + t + 1) * D_POOL] = u[t * ns:(t + 1) * ns]

    keep = WINDOW - tokens
    nk_ref[:, 0:keep, :] = ck_ref[:, tokens:, :]
    nv_ref[:, 0:keep, :] = cv_ref[:, tokens:, :]
    for t in range(tokens):
        nk_ref[:, keep + t, :] = k[t * ns:(t + 1) * ns]
        nv_ref[:, keep + t, :] = v[t * ns:(t + 1) * ns]

    qh = []
    for hd in range(N_HEADS):
        tl = q[:, (hd // 2) * LANES:(hd // 2 + 1) * LANES]
        kv = hd // (N_HEADS // N_KV_HEADS)
        want_low = kv == 0
        is_low = hd % 2 == 0
        src = tl if want_low == is_low else pltpu.roll(tl, HEAD_DIM, axis=1)
        qh.append(jnp.where(low, src, 0.0) if want_low else jnp.where(low, 0.0, src))

    gr = SEQ_GROUP
    rows_per_head = tokens * gr
    sub = lax.broadcasted_iota(jnp.int32, (N_HEADS * rows_per_head, 1), 0) % gr
    sink = jnp.concatenate([jnp.full((rows_per_head, 1), sink_ref[0, hd], F32) for hd in range(N_HEADS)],
                           axis=0)
    zpad = jnp.zeros((LANES - rows_per_head, LANES), F32)
    for gi in range(ns // gr):
        base = gi * gr
        lhs = jnp.concatenate([qh[hd][t * ns + base:t * ns + base + gr]
                               for hd in range(N_HEADS) for t in range(tokens)], axis=0).astype(BF16)
        sc = None
        for s in range(gr):
            s_one = _dot_nt(lhs, ck_ref[base + s].astype(BF16))
            sc = s_one if sc is None else jnp.where(sub == s, s_one, sc)
        sc = sc + bias_c_ref[...]
        k_new = jnp.concatenate([k[t * ns + base:t * ns + base + gr] for t in range(tokens)] + [zpad], axis=0)
        v_new = jnp.concatenate([v[t * ns + base:t * ns + base + gr] for t in range(tokens)] + [zpad], axis=0)
        sn = _dot_nt(lhs, k_new.astype(BF16)) + bias_n_ref[...]
        m = jnp.maximum(jnp.maximum(jnp.max(sc, axis=-1, keepdims=True),
                                    jnp.max(sn, axis=-1, keepdims=True)), sink)
        ec = jnp.exp(sc - m)
        en = jnp.exp(sn - m)
        den = (jnp.sum(ec, axis=-1, keepdims=True) + jnp.sum(en, axis=-1, keepdims=True)
               + jnp.exp(sink - m))
        o = _dot(en.astype(BF16), v_new.astype(BF16))
        for s in range(gr):
            o = o + _dot(jnp.where(sub == s, ec, 0.0).astype(BF16), cv_ref[base + s].astype(BF16))
        o = o * (1.0 / den)
        for t in range(tokens):
            for p in range(N_HEADS // 2):
                a = o[(2 * p * tokens + t) * gr:(2 * p * tokens + t + 1) * gr]
                c = o[((2 * p + 1) * tokens + t) * gr:((2 * p + 1) * tokens + t + 1) * gr]
                if p < N_HEADS // 4:
                    c = pltpu.roll(c, HEAD_DIM, axis=1)
                else:
                    a = pltpu.roll(a, HEAD_DIM, axis=1)
                o_s[t * ns + base:t * ns + base + gr, p * LANES:(p + 1) * LANES] = jnp.where(low, a, c)

    for t in range(tokens):
        mix_ref[:, t * d_model:t * d_model + D_POOL] = z[t * ns:(t + 1) * ns]
        mix_ref[:, t * d_model + D_POOL:(t + 1) * d_model] = o_s[t * ns:(t + 1) * ns, :].astype(BF16)


def _finish_rows(x, mix, side, ge_ref, be_ref, w_out_ref, g1_ref, b1_ref, w1_ref, w2_ref, g2_ref, b2_ref):
    h = _layer_norm(x, ge_ref[...], be_ref[...])
    h = _layer_norm(ALPHA * h + _dot(mix, w_out_ref[...]), g1_ref[...], b1_ref[...])
    hb = h.astype(BF16)
    f = None
    for c in range(w1_ref.shape[1] // FF_CHUNK):
        a = _dot(hb, w1_ref[:, c * FF_CHUNK:(c + 1) * FF_CHUNK])
        a = jnp.square(jnp.maximum(a, 0.0)).astype(BF16)
        d = _dot(a, w2_ref[c * FF_CHUNK:(c + 1) * FF_CHUNK, :])
        f = d if f is None else f + d
        next(side, None)
    for _ in side:
        pass
    return _layer_norm(ALPHA * h + f, g2_ref[...], b2_ref[...])


def _main_kernel(tokens, tiles_per_seq, n_tiles,
                 sink_ref, xf_ref, xm_ref, xs_ref, mixs_ref, ge_ref, be_ref, w_in_ref, w_pool_ref,
                 pscale_ref, bias_ref, meta_proj_ref, w_out_ref, g1_ref, b1_ref, w1_ref, w2_ref,
                 g2_ref, b2_ref,
                 y_ref, ys_ref, nk_ref, nv_ref, np_ref,
                 mixbuf, kvar, vvar, uext, qbuf):
    s = pl.program_id(0)
    is_first = s == 0
    first_tile = (jnp.minimum(s, n_tiles - 1) % tiles_per_seq == 0) & (s < n_tiles)
    d_model = xm_ref.shape[1]
    ns = xs_ref.shape[0]

    @pl.when(is_first)
    def _():
        mixbuf[...] = jnp.zeros_like(mixbuf)

    @pl.when(first_tile)
    def _():
        _carry_init(meta_proj_ref, kvar, vvar, uext)

    xs = jnp.concatenate([xs_ref[:, t * d_model:(t + 1) * d_model] for t in range(tokens)], axis=0)
    ms = jnp.concatenate([mixs_ref[:, t * d_model:(t + 1) * d_model] for t in range(tokens)], axis=0)
    x_m = jnp.where(is_first, xs, xm_ref[...])
    mix = jnp.where(is_first, ms, mixbuf[...])
    side = _front_pieces(first_tile, xf_ref, sink_ref, ge_ref, be_ref, w_in_ref, w_pool_ref, pscale_ref,
                         bias_ref, mixbuf, nk_ref.at[0], nv_ref.at[0], np_ref.at[0], kvar, vvar, uext, qbuf)
    y_ref[...] = _finish_rows(x_m, mix, side, ge_ref, be_ref, w_out_ref, g1_ref, b1_ref, w1_ref, w2_ref,
                              g2_ref, b2_ref)

    @pl.when(is_first)
    def _():
        for t in range(tokens):
            ys_ref[:, t * d_model:(t + 1) * d_model] = y_ref[t * ns:(t + 1) * ns, :]


def _const_spec(shape):
    return pl.BlockSpec(shape, lambda *_: (0,) * len(shape), pipeline_mode=pl.Buffered(1))


def _smem_spec():
    return pl.BlockSpec(memory_space=pltpu.SMEM)


def kernel(x_prompt, x_sample, cache_win_k, cache_win_v, state_pool, meta_tokens, ln_emb_g, ln_emb_b,
           rel_table, w_in, w_pool, pool_scale, sinks, w_out, ln1_g, ln1_b, w_mlp_in, w_mlp_out,
           ln2_g, ln2_b):
    batch, seq, d_model = x_prompt.shape
    n_seq, tokens, _ = x_sample.shape
    assert w_in.shape[0] == DEPTH and d_model == D_POOL + D_ATTN
    assert seq % PROMPT_TILE == 0 and n_seq % SAMPLE_SEQS == 0 and tokens * n_seq == PROMPT_TILE
    assert cache_win_k.shape[2] == WINDOW and state_pool.shape[2] == POOL_BUF and tokens <= POOL_BUF
    d_in = w_in.shape[2]

    row = lambda a: a.reshape(1, -1).astype(F32)
    ge, be = row(ln_emb_g), row(ln_emb_b)
    w_in_b = w_in[0].astype(BF16)
    w_pool_b = w_pool[0].astype(BF16)
    pscale = row(pool_scale[0])
    sink = row(sinks[0])
    fin_w = (ge, be, w_out[0].astype(BF16), row(ln1_g[0]), row(ln1_b[0]), w_mlp_in[0].astype(BF16),
             w_mlp_out[0].astype(BF16), row(ln2_g[0]), row(ln2_b[0]))
    cparams = lambda sem: pltpu.CompilerParams(dimension_semantics=sem, vmem_limit_bytes=VMEM_LIMIT_BYTES)

    bkt_p, bkt_c, bkt_n = _bucket_tables(tokens)
    srows = N_HEADS * tokens * SEQ_GROUP
    bias_p, bias_c, bias_n, meta_proj = pl.pallas_call(
        _prologue_kernel,
        in_specs=[_smem_spec()] + [pl.BlockSpec(memory_space=pltpu.VMEM)] * 7,
        out_specs=[pl.BlockSpec(memory_space=pltpu.VMEM)] * 4,
        out_shape=[jax.ShapeDtypeStruct((2, N_HEADS, BLOCK, 2 * BLOCK), F32),
                   jax.ShapeDtypeStruct((srows, WINDOW), F32),
                   jax.ShapeDtypeStruct((srows, LANES), F32),
                   jax.ShapeDtypeStruct((N_META, d_in), F32)],
        compiler_params=pltpu.CompilerParams(vmem_limit_bytes=VMEM_LIMIT_BYTES),
        name="prologue",
    )(rel_table.astype(F32), jnp.asarray(bkt_p), jnp.asarray(bkt_c), jnp.asarray(bkt_n),
      meta_tokens.astype(F32), ge, be, w_in_b)

    xs = x_sample.reshape(n_seq, tokens * d_model)
    ck = cache_win_k[0].reshape(n_seq, WINDOW, D_KV)
    cv = cache_win_v[0].reshape(n_seq, WINDOW, D_KV)
    st = state_pool[0].reshape(n_seq, POOL_BUF * D_POOL)
    seq_spec = lambda *tail: pl.BlockSpec((SAMPLE_SEQS,) + tail, lambda i: (i,) + (0,) * len(tail))
    mix_s, nk_s, nv_s, np_s = pl.pallas_call(
        functools.partial(_sample_front_kernel, tokens),
        grid=(n_seq // SAMPLE_SEQS,),
        in_specs=[_smem_spec(), seq_spec(tokens * d_model), seq_spec(WINDOW, D_KV), seq_spec(WINDOW, D_KV),
                  seq_spec(POOL_BUF * D_POOL),
                  _const_spec(ge.shape), _const_spec(be.shape), _const_spec(w_in_b.shape),
                  _const_spec(w_pool_b.shape), _const_spec(pscale.shape), _const_spec(bias_c.shape),
                  _const_spec(bias_n.shape)],
        out_specs=[seq_spec(tokens * d_model), seq_spec(WINDOW, D_KV), seq_spec(WINDOW, D_KV),
                   seq_spec(POOL_BUF * D_POOL)],
        out_shape=[jax.ShapeDtypeStruct((n_seq, tokens * d_model), BF16),
                   jax.ShapeDtypeStruct((n_seq, WINDOW, D_KV), F32),
                   jax.ShapeDtypeStruct((n_seq, WINDOW, D_KV), F32),
                   jax.ShapeDtypeStruct((n_seq, POOL_BUF * D_POOL), F32)],
        scratch_shapes=[pltpu.VMEM((tokens * SAMPLE_SEQS, D_ATTN), F32)],
        compiler_params=cparams(("arbitrary",)),
        name="sample_front",
    )(sink, xs, ck, cv, st, ge, be, w_in_b, w_pool_b, pscale, bias_c, bias_n)

    tile = PROMPT_TILE
    tiles_per_seq = seq // tile
    n_tiles = batch * tiles_per_seq
    xp = x_prompt.reshape(batch * seq, d_model)
    front_tile = lambda s: jnp.minimum(s, n_tiles - 1)
    finish_tile = lambda s: jnp.maximum(s - 1, 0)
    seq_of = lambda s: front_tile(s) // tiles_per_seq
    consts = (ge, be, w_in_b, w_pool_b, pscale, bias_p, meta_proj) + fin_w[2:]
    y_p, y_s, nk_p, nv_p, np_p = pl.pallas_call(
        functools.partial(_main_kernel, tokens, tiles_per_seq, n_tiles),
        grid=(n_tiles + 1,),
        in_specs=[_smem_spec(),
                  pl.BlockSpec((tile, d_model), lambda s: (front_tile(s), 0)),
                  pl.BlockSpec((tile, d_model), lambda s: (finish_tile(s), 0)),
                  _const_spec(xs.shape), _const_spec(mix_s.shape)] + [_const_spec(c.shape) for c in consts],
        out_specs=[pl.BlockSpec((tile, d_model), lambda s: (finish_tile(s), 0)),
                   pl.BlockSpec(xs.shape, lambda s: (0, 0)),
                   pl.BlockSpec((1, BLOCK, D_KV), lambda s: (seq_of(s), 0, 0)),
                   pl.BlockSpec((1, BLOCK, D_KV), lambda s: (seq_of(s), 0, 0)),
                   pl.BlockSpec((1, HALO, D_POOL), lambda s: (seq_of(s), 0, 0))],
        out_shape=[jax.ShapeDtypeStruct((batch * seq, d_model), F32),
                   jax.ShapeDtypeStruct(xs.shape, F32),
                   jax.ShapeDtypeStruct((batch, BLOCK, D_KV), F32),
                   jax.ShapeDtypeStruct((batch, BLOCK, D_KV), F32),
                   jax.ShapeDtypeStruct((batch, HALO, D_POOL), F32)],
        scratch_shapes=[pltpu.VMEM((tile, d_model), BF16),
                        pltpu.VMEM((4, tile + BLOCK, LANES), BF16),
                        pltpu.VMEM((4, tile + BLOCK, LANES), BF16),
                        pltpu.VMEM((tile + HALO, D_POOL), F32),
                        pltpu.VMEM((tile, D_ATTN), BF16)],
        compiler_params=cparams(("arbitrary",)),
        name="main",
    )(sink, xp, xp, xs, mix_s, *consts)

    kv_shape = (DEPTH, -1, WINDOW, N_KV_HEADS, HEAD_DIM)
    return (y_p.reshape(batch, seq, d_model), y_s.reshape(n_seq, tokens, d_model),
            nk_p.reshape(kv_shape), nv_p.reshape(kv_shape),
            np_p[:, HALO - POOL_BUF:, :].reshape(DEPTH, batch, POOL_BUF, D_POOL),
            nk_s.reshape(kv_shape), nv_s.reshape(kv_shape),
            np_s.reshape(DEPTH, n_seq, POOL_BUF, D_POOL))
```

```python
import functools
import math

import jax
import jax.numpy as jnp
import numpy as np
from jax import lax
from jax.experimental import pallas as pl
from jax.experimental.pallas import tpu as pltpu

N_META = 16
POOL_WINDOWS = (2, 4, 8, 16)
POOL_GROUP_DIM = 128
D_POOL = len(POOL_WINDOWS) * POOL_GROUP_DIM
POOL_BUF = max(POOL_WINDOWS) - 1
N_HEADS = 8
HEAD_DIM = 64
D_ATTN = N_HEADS * HEAD_DIM
N_KV_HEADS = 2
D_KV = N_KV_HEADS * HEAD_DIM
WINDOW = 128
BLOCK = 128
REL_BUCKETS = 32
REL_MAX_DIST = 128
PAST_LEN = 8192
DEPTH = 1
ALPHA = (2.0 * DEPTH) ** 0.25
LN_EPS = 1e-5
Q_SCALE = HEAD_DIM ** -0.5

LANES = 128
SUBLANES = 8
VMEM_LIMIT_BYTES = 56 * 1024 * 1024

PROMPT_TILE = 512
UP_CHUNK = 512
DOWN_CHUNK = 256
SAMPLE_SEQS = 32
SEQ_GROUP = SUBLANES

F32 = jnp.float32
BF16 = jnp.bfloat16
NEG_INF = float("-inf")


def _rel_bucket(dist):
    n = np.maximum(dist, 0)
    max_exact = REL_BUCKETS // 2
    nf = np.maximum(n, 1).astype(np.float64)
    large = max_exact + (np.log(nf / max_exact) / math.log(REL_MAX_DIST / max_exact)
                         * (REL_BUCKETS - max_exact)).astype(np.int32)
    large = np.minimum(large, REL_BUCKETS - 1)
    return np.where(n < max_exact, n, large).astype(np.int32)


def _bucket_tables(tokens):
    dist = (np.arange(BLOCK)[:, None] + BLOCK) - np.arange(2 * BLOCK)[None, :]
    ok = (dist >= 0) & (dist < WINDOW)
    rest = np.where(ok, _rel_bucket(dist), -1)
    first = np.where(np.arange(2 * BLOCK)[None, :] >= BLOCK - N_META, rest, -1)
    prompt = np.stack([first, rest]).astype(np.int32)
    t = np.repeat(np.arange(tokens), SEQ_GROUP)
    s = np.tile(np.arange(SEQ_GROUP), tokens)
    dist = (t[:, None] + WINDOW) - np.arange(WINDOW)[None, :]
    ok = (dist >= 0) & (dist < WINDOW)
    cache = np.where(ok, _rel_bucket(dist), -1).astype(np.int32)
    dist = t[:, None] - t[None, :]
    ok = (dist >= 0) & (s[:, None] == s[None, :])
    new = np.full((tokens * SEQ_GROUP, LANES), -1, np.int32)
    new[:, :tokens * SEQ_GROUP] = np.where(ok, _rel_bucket(dist), -1)
    return prompt, cache, new


def _layer_norm(x, g, b):
    mu = jnp.mean(x, axis=-1, keepdims=True)
    xc = x - mu
    var = jnp.mean(xc * xc, axis=-1, keepdims=True)
    return xc * lax.rsqrt(var + LN_EPS) * g + b


def _dot(a, b):
    return jnp.dot(a, b, preferred_element_type=F32)


def _dot_nt(a, b):
    return lax.dot_general(a, b, (((1,), (1,)), ((), ())), preferred_element_type=F32)


def _kv_lane_variants(x, low):
    xr = pltpu.roll(x, HEAD_DIM, axis=1)
    zero = jnp.zeros_like(x)
    return (jnp.where(low, x, zero).astype(BF16), jnp.where(low, zero, xr).astype(BF16),
            jnp.where(low, xr, zero).astype(BF16), jnp.where(low, zero, x).astype(BF16))


def _prologue_kernel(tab_ref, bkt_p_ref, bkt_c_ref, bkt_n_ref, meta_ref, g_ref, b_ref, w_in_ref,
                     bias_p_ref, bias_c_ref, bias_n_ref, meta_proj_ref):
    def lookup(bucket, h):
        def body(i, acc):
            return jnp.where(bucket == i, tab_ref[i, h], acc)
        return lax.fori_loop(0, REL_BUCKETS, body, jnp.full(bucket.shape, NEG_INF, F32))

    rows = bkt_c_ref.shape[0]
    for h in range(N_HEADS):
        rest = lookup(bkt_p_ref[1], h)
        bias_p_ref[1, h] = rest
        bias_p_ref[0, h] = jnp.where(bkt_p_ref[0] >= 0, rest, NEG_INF)
        bias_c_ref[h * rows:(h + 1) * rows, :] = lookup(bkt_c_ref[...], h)
        bias_n_ref[h * rows:(h + 1) * rows, :] = lookup(bkt_n_ref[...], h)
    hm = _layer_norm(meta_ref[...], g_ref[...], b_ref[...])
    meta_proj_ref[...] = _dot(hm.astype(BF16), w_in_ref[...])


HALO = 2 * SUBLANES


def _carry_init(meta_proj_ref, kvar, vvar, uext):
    low = lax.broadcasted_iota(jnp.int32, (1, LANES), 1) < HEAD_DIM
    pad = jnp.zeros((BLOCK - N_META, LANES), BF16)
    kq = _kv_lane_variants(meta_proj_ref[:, D_POOL + D_ATTN:D_POOL + D_ATTN + D_KV], low)
    vq = _kv_lane_variants(meta_proj_ref[:, D_POOL + D_ATTN + D_KV:], low)
    for i in range(4):
        kvar[i, 0:BLOCK, :] = jnp.concatenate([pad, kq[i]], axis=0)
        vvar[i, 0:BLOCK, :] = jnp.concatenate([pad, vq[i]], axis=0)
    uext[0:HALO, :] = meta_proj_ref[:, 0:D_POOL]


def _front_pieces(first_tile, x_ref, sink_ref, g_ref, b_ref, w_in_ref, w_pool_ref, pscale_ref, bias_ref,
                  mix_ref, nk_ref, nv_ref, np_ref, kvar, vvar, uext, qbuf):
    tile = x_ref.shape[0]
    halo = HALO
    low = lax.broadcasted_iota(jnp.int32, (1, LANES), 1) < HEAD_DIM

    for j in range(tile // BLOCK):
        rows = slice(j * BLOCK, (j + 1) * BLOCK)
        h = _layer_norm(x_ref[rows, :], g_ref[...], b_ref[...])
        proj = _dot(h.astype(BF16), w_in_ref[...])
        u = proj[:, 0:D_POOL]
        k = proj[:, D_POOL + D_ATTN:D_POOL + D_ATTN + D_KV]
        v = proj[:, D_POOL + D_ATTN + D_KV:]
        qbuf[rows, :] = (proj[:, D_POOL:D_POOL + D_ATTN] * Q_SCALE).astype(BF16)
        uext[halo + j * BLOCK:halo + (j + 1) * BLOCK, :] = u
        kq = _kv_lane_variants(k, low)
        vq = _kv_lane_variants(v, low)
        for i in range(4):
            kvar[i, BLOCK + j * BLOCK:BLOCK + (j + 1) * BLOCK, :] = kq[i]
            vvar[i, BLOCK + j * BLOCK:BLOCK + (j + 1) * BLOCK, :] = vq[i]
        if j == tile // BLOCK - 1:
            nk_ref[...] = k
            nv_ref[...] = v
            np_ref[...] = u[BLOCK - halo:, :]
        yield

    for g, w in enumerate(POOL_WINDOWS):
        cols = slice(g * POOL_GROUP_DIM, (g + 1) * POOL_GROUP_DIM)
        ug = uext[halo:halo + tile, cols]
        s = ug
        for i in range(1, w):
            s = s + uext[halo - i:halo - i + tile, cols]
        p = s * (1.0 / w) - ug
        mix_ref[:, cols] = (_dot(p.astype(BF16), w_pool_ref[g]) * pscale_ref[:, cols]).astype(BF16)
        yield

    for j in range(tile // BLOCK):
        rows = slice(j * BLOCK, (j + 1) * BLOCK)
        keys = slice(j * BLOCK, (j + 2) * BLOCK)
        sel = jnp.where(first_tile, 0, 1) if j == 0 else 1
        for kv in range(N_KV_HEADS):
            tiles = (2 * kv, 2 * kv + 1)
            lhs = jnp.concatenate([qbuf[rows, p * LANES:(p + 1) * LANES] for p in tiles], axis=0)
            s_lo = _dot_nt(lhs, kvar[2 * kv, keys, :])
            s_hi = _dot_nt(lhs, kvar[2 * kv + 1, keys, :])
            for n, p in enumerate(tiles):
                acc = None
                for half, s_all in enumerate((s_lo, s_hi)):
                    hd = 2 * p + half
                    s = s_all[n * BLOCK:(n + 1) * BLOCK] + bias_ref[sel, hd]
                    sink = sink_ref[0, hd]
                    m = jnp.maximum(jnp.max(s, axis=-1, keepdims=True), sink)
                    e = jnp.exp(s - m)
                    den = jnp.sum(e, axis=-1, keepdims=True) + jnp.exp(sink - m)
                    o = _dot(e.astype(BF16), vvar[2 * kv + half, keys, :]) * (1.0 / den)
                    acc = o if acc is None else acc + o
                mix_ref[rows, D_POOL + p * LANES:D_POOL + (p + 1) * LANES] = acc.astype(BF16)
            yield

    for i in range(4):
        kvar[i, 0:BLOCK, :] = kvar[i, tile:tile + BLOCK, :]
        vvar[i, 0:BLOCK, :] = vvar[i, tile:tile + BLOCK, :]
    uext[0:halo, :] = uext[tile:tile + halo, :]


def _sample_front_kernel(tokens, sink_ref, x_ref, ck_ref, cv_ref, st_ref, g_ref, b_ref, w_in_ref,
                         w_pool_ref, pscale_ref, bias_c_ref, bias_n_ref,
                         mix_ref, nk_ref, nv_ref, np_ref, o_s):
    ns = x_ref.shape[0]
    d_model = x_ref.shape[1] // tokens
    low = lax.broadcasted_iota(jnp.int32, (1, LANES), 1) < HEAD_DIM

    x = jnp.concatenate([x_ref[:, t * d_model:(t + 1) * d_model] for t in range(tokens)], axis=0)
    h = _layer_norm(x, g_ref[...], b_ref[...])
    proj = _dot(h.astype(BF16), w_in_ref[...])
    u = proj[:, 0:D_POOL]
    q = proj[:, D_POOL:D_POOL + D_ATTN] * Q_SCALE
    k = proj[:, D_POOL + D_ATTN:D_POOL + D_ATTN + D_KV]
    v = proj[:, D_POOL + D_ATTN + D_KV:]

    ext = [st_ref[:, r * D_POOL:(r + 1) * D_POOL] for r in range(POOL_BUF)]
    ext += [u[t * ns:(t + 1) * ns] for t in range(tokens)]
    ps = []
    for t in range(tokens):
        idx = POOL_BUF + t
        pos = PAST_LEN - POOL_BUF + idx
        parts = []
        for g, w in enumerate(POOL_WINDOWS):
            cols = slice(g * POOL_GROUP_DIM, (g + 1) * POOL_GROUP_DIM)
            lo = max(idx + 1 - w, 0)
            s = ext[idx][:, cols]
            for r in range(idx - 1, lo - 1, -1):
                s = s + ext[r][:, cols]
            parts.append(s * (1.0 / min(w, pos + 1)) - ext[idx][:, cols])
        ps.append(jnp.concatenate(parts, axis=1))
    p = jnp.concatenate(ps, axis=0).astype(BF16)
    zs = [_dot(p[:, g * POOL_GROUP_DIM:(g + 1) * POOL_GROUP_DIM], w_pool_ref[g])
          for g in range(len(POOL_WINDOWS))]
    z = (jnp.concatenate(zs, axis=1) * pscale_ref[...]).astype(BF16)
    keep = POOL_BUF - tokens
    np_ref[:, 0:keep * D_POOL] = st_ref[:, tokens * D_POOL:]
    for t in range(tokens):
        np_ref[:, (keep + t) * D_POOL:(keep + t + 1) * D_POOL] = u[t * ns:(t + 1) * ns]

    keep = WINDOW - tokens
    nk_ref[:, 0:keep, :] = ck_ref[:, tokens:, :]
    nv_ref[:, 0:keep, :] = cv_ref[:, tokens:, :]
    for t in range(tokens):
        nk_ref[:, keep + t, :] = k[t * ns:(t + 1) * ns]
        nv_ref[:, keep + t, :] = v[t * ns:(t + 1) * ns]

    qh = []
    for hd in range(N_HEADS):
        tl = q[:, (hd // 2) * LANES:(hd // 2 + 1) * LANES]
        kv = hd // (N_HEADS // N_KV_HEADS)
        want_low = kv == 0
        is_low = hd % 2 == 0
        src = tl if want_low == is_low else pltpu.roll(tl, HEAD_DIM, axis=1)
        qh.append(jnp.where(low, src, 0.0) if want_low else jnp.where(low, 0.0, src))

    gr = SEQ_GROUP
    rows_per_head = tokens * gr
    sub = lax.broadcasted_iota(jnp.int32, (N_HEADS * rows_per_head, 1), 0) % gr
    sink = jnp.concatenate([jnp.full((rows_per_head, 1), sink_ref[0, hd], F32) for hd in range(N_HEADS)],
                           axis=0)
    zpad = jnp.zeros((LANES - rows_per_head, LANES), F32)
    for gi in range(ns // gr):
        base = gi * gr
        lhs = jnp.concatenate([qh[hd][t * ns + base:t * ns + base + gr]
                               for hd in range(N_HEADS) for t in range(tokens)], axis=0).astype(BF16)
        sc = None
        for s in range(gr):
            s_one = _dot_nt(lhs, ck_ref[base + s].astype(BF16))
            sc = s_one if sc is None else jnp.where(sub == s, s_one, sc)
        sc = sc + bias_c_ref[...]
        k_new = jnp.concatenate([k[t * ns + base:t * ns + base + gr] for t in range(tokens)] + [zpad], axis=0)
        v_new = jnp.concatenate([v[t * ns + base:t * ns + base + gr] for t in range(tokens)] + [zpad], axis=0)
        sn = _dot_nt(lhs, k_new.astype(BF16)) + bias_n_ref[...]
        m = jnp.maximum(jnp.maximum(jnp.max(sc, axis=-1, keepdims=True),
                                    jnp.max(sn, axis=-1, keepdims=True)), sink)
        ec = jnp.exp(sc - m)
        en = jnp.exp(sn - m)
        den = (jnp.sum(ec, axis=-1, keepdims=True) + jnp.sum(en, axis=-1, keepdims=True)
               + jnp.exp(sink - m))
        o = _dot(en.astype(BF16), v_new.astype(BF16))
        for s in range(gr):
            o = o + _dot(jnp.where(sub == s, ec, 0.0).astype(BF16), cv_ref[base + s].astype(BF16))
        o = o * (1.0 / den)
        for t in range(tokens):
            for p in range(N_HEADS // 2):
                a = o[(2 * p * tokens + t) * gr:(2 * p * tokens + t + 1) * gr]
                c = o[((2 * p + 1) * tokens + t) * gr:((2 * p + 1) * tokens + t + 1) * gr]
                if p < N_HEADS // 4:
                    c = pltpu.roll(c, HEAD_DIM, axis=1)
                else:
                    a = pltpu.roll(a, HEAD_DIM, axis=1)
                o_s[t * ns + base:t * ns + base + gr, p * LANES:(p + 1) * LANES] = jnp.where(low, a, c)

    for t in range(tokens):
        mix_ref[:, t * d_model:t * d_model + D_POOL] = z[t * ns:(t + 1) * ns]
        mix_ref[:, t * d_model + D_POOL:(t + 1) * d_model] = o_s[t * ns:(t + 1) * ns, :].astype(BF16)


def _pre_pieces(rows, x, mix, ge_ref, be_ref, w_out_ref, g1_ref, b1_ref, h1_s, hb_s):
    h1_s[rows, :] = ALPHA * _layer_norm(x(), ge_ref[...], be_ref[...])
    yield
    h1_s[rows, :] = h1_s[rows, :] + _dot(mix(), w_out_ref[...])
    yield
    h = _layer_norm(h1_s[rows, :], g1_ref[...], b1_ref[...])
    h1_s[rows, :] = h
    hb_s[rows, :] = h.astype(BF16)
    yield


def _mlp_pieces(rows, a_s, h1_s, hb_s, w1_ref, w2_ref, g2_ref, b2_ref, y_ref):
    for c in range(w1_ref.shape[1] // UP_CHUNK):
        cols = slice(c * UP_CHUNK, (c + 1) * UP_CHUNK)
        a = _dot(hb_s[rows, :], w1_ref[:, cols])
        a_s[:, cols] = jnp.square(jnp.maximum(a, 0.0)).astype(BF16)
        if c == 0:
            y_ref[rows, :] = ALPHA * h1_s[rows, :]
        yield
    for n in range(y_ref.shape[1] // DOWN_CHUNK):
        cols = slice(n * DOWN_CHUNK, (n + 1) * DOWN_CHUNK)
        y_ref[rows, cols] = y_ref[rows, cols] + _dot(a_s[...], w2_ref[:, cols])
        yield
    y_ref[rows, :] = _layer_norm(y_ref[rows, :], g2_ref[...], b2_ref[...])
    yield


def _main_kernel(tokens, tiles_per_seq, n_tiles,
                 sink_ref, x_ref, xs_ref, mixs_ref, ge_ref, be_ref, w_in_ref, w_pool_ref,
                 pscale_ref, bias_ref, meta_proj_ref, w_out_ref, g1_ref, b1_ref, w1_ref, w2_ref,
                 g2_ref, b2_ref,
                 y_ref, ys_ref, nk_ref, nv_ref, np_ref,
                 mixbuf, kvar, vvar, uext, qbuf, h1_s, hb_s, a_s):
    s = pl.program_id(0)
    is_first = s == 0
    first_tile = jnp.minimum(s, n_tiles - 1) % tiles_per_seq == 0
    tile, d_model = x_ref.shape
    ns = xs_ref.shape[0]
    half = tile // 2
    halves = (slice(0, half), slice(half, tile))
    pre_w = (ge_ref, be_ref, w_out_ref, g1_ref, b1_ref, h1_s, hb_s)

    @pl.when(is_first)
    def _():
        def lanes(ref, rows):
            t0 = rows.start // ns
            return jnp.concatenate([ref[:, t * d_model:(t + 1) * d_model]
                                    for t in range(t0, t0 + half // ns)], axis=0)
        for rows in halves:
            for _ in _pre_pieces(rows, lambda: lanes(xs_ref, rows), lambda: lanes(mixs_ref, rows), *pre_w):
                pass

    @pl.when(first_tile)
    def _():
        _carry_init(meta_proj_ref, kvar, vvar, uext)

    mlp = [_mlp_pieces(rows, a_s.at[i], h1_s, hb_s, w1_ref, w2_ref, g2_ref, b2_ref, y_ref)
           for i, rows in enumerate(halves)]
    front = _front_pieces(first_tile, x_ref, sink_ref, ge_ref, be_ref, w_in_ref, w_pool_ref, pscale_ref,
                          bias_ref, mixbuf, nk_ref.at[0], nv_ref.at[0], np_ref.at[0], kvar, vvar, uext, qbuf)
    pre = [_pre_pieces(rows, lambda rows=rows: x_ref[rows, :], lambda rows=rows: mixbuf[rows, :], *pre_w)
           for rows in halves]
    a, b, f, pa, pb = mlp[0], mlp[1], front, pre[0], pre[1]
    order = (
        [a, f, a, a, f, a, a, f, a, a, f, a] + [a, f, f, a, f, f, a, f, a, f]
        + [b, a, b, f, b, b, f, b, b, f, b, b, f]
        + [b, f, f, b, pa, pb, f, b, pa, pb, b, pa, pb, b])
    for g in order:
        next(g, None)
    for g in (a, b, f, pa, pb):
        for _ in g:
            raise AssertionError("piece left unscheduled")

    @pl.when(is_first)
    def _():
        for t in range(tokens):
            ys_ref[:, t * d_model:(t + 1) * d_model] = y_ref[t * ns:(t + 1) * ns, :]


def _const_spec(shape):
    return pl.BlockSpec(shape, lambda *_: (0,) * len(shape), pipeline_mode=pl.Buffered(1))


def _smem_spec():
    return pl.BlockSpec(memory_space=pltpu.SMEM)


def kernel(x_prompt, x_sample, cache_win_k, cache_win_v, state_pool, meta_tokens, ln_emb_g, ln_emb_b,
           rel_table, w_in, w_pool, pool_scale, sinks, w_out, ln1_g, ln1_b, w_mlp_in, w_mlp_out,
           ln2_g, ln2_b):
    batch, seq, d_model = x_prompt.shape
    n_seq, tokens, _ = x_sample.shape
    assert w_in.shape[0] == DEPTH and d_model == D_POOL + D_ATTN
    assert seq % PROMPT_TILE == 0 and n_seq % SAMPLE_SEQS == 0 and tokens * n_seq == PROMPT_TILE
    assert cache_win_k.shape[2] == WINDOW and state_pool.shape[2] == POOL_BUF and tokens <= POOL_BUF
    d_in = w_in.shape[2]

    row = lambda a: a.reshape(1, -1).astype(F32)
    ge, be = row(ln_emb_g), row(ln_emb_b)
    w_in_b = w_in[0].astype(BF16)
    w_pool_b = w_pool[0].astype(BF16)
    pscale = row(pool_scale[0])
    sink = row(sinks[0])
    fin_w = (ge, be, w_out[0].astype(BF16), row(ln1_g[0]), row(ln1_b[0]), w_mlp_in[0].astype(BF16),
             w_mlp_out[0].astype(BF16), row(ln2_g[0]), row(ln2_b[0]))
    cparams = lambda sem: pltpu.CompilerParams(dimension_semantics=sem, vmem_limit_bytes=VMEM_LIMIT_BYTES)

    bkt_p, bkt_c, bkt_n = _bucket_tables(tokens)
    srows = N_HEADS * tokens * SEQ_GROUP
    bias_p, bias_c, bias_n, meta_proj = pl.pallas_call(
        _prologue_kernel,
        in_specs=[_smem_spec()] + [pl.BlockSpec(memory_space=pltpu.VMEM)] * 7,
        out_specs=[pl.BlockSpec(memory_space=pltpu.VMEM)] * 4,
        out_shape=[jax.ShapeDtypeStruct((2, N_HEADS, BLOCK, 2 * BLOCK), F32),
                   jax.ShapeDtypeStruct((srows, WINDOW), F32),
                   jax.ShapeDtypeStruct((srows, LANES), F32),
                   jax.ShapeDtypeStruct((N_META, d_in), F32)],
        compiler_params=pltpu.CompilerParams(vmem_limit_bytes=VMEM_LIMIT_BYTES),
        name="prologue",
    )(rel_table.astype(F32), jnp.asarray(bkt_p), jnp.asarray(bkt_c), jnp.asarray(bkt_n),
      meta_tokens.astype(F32), ge, be, w_in_b)

    xs = x_sample.reshape(n_seq, tokens * d_model)
    ck = cache_win_k[0].reshape(n_seq, WINDOW, D_KV)
    cv = cache_win_v[0].reshape(n_seq, WINDOW, D_KV)
    st = state_pool[0].reshape(n_seq, POOL_BUF * D_POOL)
    seq_spec = lambda *tail: pl.BlockSpec((SAMPLE_SEQS,) + tail, lambda i: (i,) + (0,) * len(tail))
    mix_s, nk_s, nv_s, np_s = pl.pallas_call(
        functools.partial(_sample_front_kernel, tokens),
        grid=(n_seq // SAMPLE_SEQS,),
        in_specs=[_smem_spec(), seq_spec(tokens * d_model), seq_spec(WINDOW, D_KV), seq_spec(WINDOW, D_KV),
                  seq_spec(POOL_BUF * D_POOL),
                  _const_spec(ge.shape), _const_spec(be.shape), _const_spec(w_in_b.shape),
                  _const_spec(w_pool_b.shape), _const_spec(pscale.shape), _const_spec(bias_c.shape),
                  _const_spec(bias_n.shape)],
        out_specs=[seq_spec(tokens * d_model), seq_spec(WINDOW, D_KV), seq_spec(WINDOW, D_KV),
                   seq_spec(POOL_BUF * D_POOL)],
        out_shape=[jax.ShapeDtypeStruct((n_seq, tokens * d_model), BF16),
                   jax.ShapeDtypeStruct((n_seq, WINDOW, D_KV), F32),
                   jax.ShapeDtypeStruct((n_seq, WINDOW, D_KV), F32),
                   jax.ShapeDtypeStruct((n_seq, POOL_BUF * D_POOL), F32)],
        scratch_shapes=[pltpu.VMEM((tokens * SAMPLE_SEQS, D_ATTN), F32)],
        compiler_params=cparams(("arbitrary",)),
        name="sample_front",
    )(sink, xs, ck, cv, st, ge, be, w_in_b, w_pool_b, pscale, bias_c, bias_n)

    tile = PROMPT_TILE
    tiles_per_seq = seq // tile
    n_tiles = batch * tiles_per_seq
    xp = x_prompt.reshape(batch * seq, d_model)
    front_tile = lambda s: jnp.minimum(s, n_tiles - 1)
    finish_tile = lambda s: jnp.maximum(s - 1, 0)
    seq_of = lambda s: front_tile(s) // tiles_per_seq
    consts = (ge, be, w_in_b, w_pool_b, pscale, bias_p, meta_proj) + fin_w[2:]
    y_p, y_s, nk_p, nv_p, np_p = pl.pallas_call(
        functools.partial(_main_kernel, tokens, tiles_per_seq, n_tiles),
        grid=(n_tiles + 1,),
        in_specs=[_smem_spec(),
                  pl.BlockSpec((tile, d_model), lambda s: (front_tile(s), 0)),
                  _const_spec(xs.shape), _const_spec(mix_s.shape)] + [_const_spec(c.shape) for c in consts],
        out_specs=[pl.BlockSpec((tile, d_model), lambda s: (finish_tile(s), 0)),
                   pl.BlockSpec(xs.shape, lambda s: (0, 0)),
                   pl.BlockSpec((1, BLOCK, D_KV), lambda s: (seq_of(s), 0, 0)),
                   pl.BlockSpec((1, BLOCK, D_KV), lambda s: (seq_of(s), 0, 0)),
                   pl.BlockSpec((1, HALO, D_POOL), lambda s: (seq_of(s), 0, 0))],
        out_shape=[jax.ShapeDtypeStruct((batch * seq, d_model), F32),
                   jax.ShapeDtypeStruct(xs.shape, F32),
                   jax.ShapeDtypeStruct((batch, BLOCK, D_KV), F32),
                   jax.ShapeDtypeStruct((batch, BLOCK, D_KV), F32),
                   jax.ShapeDtypeStruct((batch, HALO, D_POOL), F32)],
        scratch_shapes=[pltpu.VMEM((tile, d_model), BF16),
                        pltpu.VMEM((4, tile + BLOCK, LANES), BF16),
                        pltpu.VMEM((4, tile + BLOCK, LANES), BF16),
                        pltpu.VMEM((tile + HALO, D_POOL), F32),
                        pltpu.VMEM((tile, D_ATTN), BF16),
                        pltpu.VMEM((tile, d_model), F32),
                        pltpu.VMEM((tile, d_model), BF16),
                        pltpu.VMEM((2, tile // 2, w_mlp_in.shape[2]), BF16)],
        compiler_params=cparams(("arbitrary",)),
        name="main",
    )(sink, xp, xs, mix_s, *consts)

    kv_shape = (DEPTH, -1, WINDOW, N_KV_HEADS, HEAD_DIM)
    return (y_p.reshape(batch, seq, d_model), y_s.reshape(n_seq, tokens, d_model),
            nk_p.reshape(kv_shape), nv_p.reshape(kv_shape),
            np_p[:, HALO - POOL_BUF:, :].reshape(DEPTH, batch, POOL_BUF, D_POOL),
            nk_s.reshape(kv_shape), nv_s.reshape(kv_shape),
            np_s.reshape(DEPTH, n_seq, POOL_BUF, D_POOL))
```

```python
import functools
import math

import jax
import jax.numpy as jnp
import numpy as np
from jax import lax
from jax.experimental import pallas as pl
from jax.experimental.pallas import tpu as pltpu

N_META = 16
POOL_WINDOWS = (2, 4, 8, 16)
POOL_GROUP_DIM = 128
D_POOL = len(POOL_WINDOWS) * POOL_GROUP_DIM
POOL_BUF = max(POOL_WINDOWS) - 1
N_HEADS = 8
HEAD_DIM = 64
D_ATTN = N_HEADS * HEAD_DIM
N_KV_HEADS = 2
D_KV = N_KV_HEADS * HEAD_DIM
WINDOW = 128
BLOCK = 128
REL_BUCKETS = 32
REL_MAX_DIST = 128
PAST_LEN = 8192
DEPTH = 1
ALPHA = (2.0 * DEPTH) ** 0.25
LN_EPS = 1e-5
Q_SCALE = HEAD_DIM ** -0.5

LANES = 128
SUBLANES = 8
VMEM_LIMIT_BYTES = 56 * 1024 * 1024

PROMPT_TILE = 512
UP_CHUNK = 512
DOWN_CHUNK = 256
SAMPLE_SEQS = 32
SEQ_GROUP = SUBLANES

F32 = jnp.float32
BF16 = jnp.bfloat16
NEG_INF = float("-inf")


def _rel_bucket(dist):
    n = np.maximum(dist, 0)
    max_exact = REL_BUCKETS // 2
    nf = np.maximum(n, 1).astype(np.float64)
    large = max_exact + (np.log(nf / max_exact) / math.log(REL_MAX_DIST / max_exact)
                         * (REL_BUCKETS - max_exact)).astype(np.int32)
    large = np.minimum(large, REL_BUCKETS - 1)
    return np.where(n < max_exact, n, large).astype(np.int32)


def _bucket_tables(tokens):
    dist = (np.arange(BLOCK)[:, None] + BLOCK) - np.arange(2 * BLOCK)[None, :]
    ok = (dist >= 0) & (dist < WINDOW)
    rest = np.where(ok, _rel_bucket(dist), -1)
    first = np.where(np.arange(2 * BLOCK)[None, :] >= BLOCK - N_META, rest, -1)
    prompt = np.stack([first, rest]).astype(np.int32)
    t = np.repeat(np.arange(tokens), SEQ_GROUP)
    s = np.tile(np.arange(SEQ_GROUP), tokens)
    dist = (t[:, None] + WINDOW) - np.arange(WINDOW)[None, :]
    ok = (dist >= 0) & (dist < WINDOW)
    cache = np.where(ok, _rel_bucket(dist), -1).astype(np.int32)
    dist = t[:, None] - t[None, :]
    ok = (dist >= 0) & (s[:, None] == s[None, :])
    new = np.full((tokens * SEQ_GROUP, LANES), -1, np.int32)
    new[:, :tokens * SEQ_GROUP] = np.where(ok, _rel_bucket(dist), -1)
    return prompt, cache, new


def _layer_norm(x, g, b):
    mu = jnp.mean(x, axis=-1, keepdims=True)
    xc = x - mu
    var = jnp.mean(xc * xc, axis=-1, keepdims=True)
    return xc * lax.rsqrt(var + LN_EPS) * g + b


def _dot(a, b):
    return jnp.dot(a, b, preferred_element_type=F32)


def _dot_nt(a, b):
    return lax.dot_general(a, b, (((1,), (1,)), ((), ())), preferred_element_type=F32)


def _kv_lane_variants(x, low):
    xr = pltpu.roll(x, HEAD_DIM, axis=1)
    zero = jnp.zeros_like(x)
    return (jnp.where(low, x, zero).astype(BF16), jnp.where(low, zero, xr).astype(BF16),
            jnp.where(low, xr, zero).astype(BF16), jnp.where(low, zero, x).astype(BF16))


def _prologue_kernel(tab_ref, bkt_p_ref, bkt_c_ref, bkt_n_ref, meta_ref, g_ref, b_ref, w_in_ref,
                     bias_p_ref, bias_c_ref, bias_n_ref, meta_proj_ref):
    def lookup(bucket, h):
        def body(i, acc):
            return jnp.where(bucket == i, tab_ref[i, h], acc)
        return lax.fori_loop(0, REL_BUCKETS, body, jnp.full(bucket.shape, NEG_INF, F32))

    rows = bkt_c_ref.shape[0]
    for h in range(N_HEADS):
        rest = lookup(bkt_p_ref[1], h)
        bias_p_ref[1, h] = rest
        bias_p_ref[0, h] = jnp.where(bkt_p_ref[0] >= 0, rest, NEG_INF)
        bias_c_ref[h * rows:(h + 1) * rows, :] = lookup(bkt_c_ref[...], h)
        bias_n_ref[h * rows:(h + 1) * rows, :] = lookup(bkt_n_ref[...], h)
    hm = _layer_norm(meta_ref[...], g_ref[...], b_ref[...])
    meta_proj_ref[...] = _dot(hm.astype(BF16), w_in_ref[...])


HALO = 2 * SUBLANES


def _carry_init(meta_proj_ref, kvar, vvar, uext):
    low = lax.broadcasted_iota(jnp.int32, (1, LANES), 1) < HEAD_DIM
    pad = jnp.zeros((BLOCK - N_META, LANES), BF16)
    kq = _kv_lane_variants(meta_proj_ref[:, D_POOL + D_ATTN:D_POOL + D_ATTN + D_KV], low)
    vq = _kv_lane_variants(meta_proj_ref[:, D_POOL + D_ATTN + D_KV:], low)
    for i in range(4):
        kvar[i, 0:BLOCK, :] = jnp.concatenate([pad, kq[i]], axis=0)
        vvar[i, 0:BLOCK, :] = jnp.concatenate([pad, vq[i]], axis=0)
    uext[0:HALO, :] = meta_proj_ref[:, 0:D_POOL]


def _front_pieces(first_tile, x_ref, sink_ref, g_ref, b_ref, w_in_ref, w_pool_ref, pscale_ref, bias_ref,
                  mix_ref, nk_ref, nv_ref, np_ref, kvar, vvar, uext, qbuf):
    tile = x_ref.shape[0]
    halo = HALO
    low = lax.broadcasted_iota(jnp.int32, (1, LANES), 1) < HEAD_DIM

    for j in range(tile // BLOCK):
        rows = slice(j * BLOCK, (j + 1) * BLOCK)
        h = _layer_norm(x_ref[rows, :], g_ref[...], b_ref[...])
        proj = _dot(h.astype(BF16), w_in_ref[...])
        u = proj[:, 0:D_POOL]
        k = proj[:, D_POOL + D_ATTN:D_POOL + D_ATTN + D_KV]
        v = proj[:, D_POOL + D_ATTN + D_KV:]
        qbuf[rows, :] = (proj[:, D_POOL:D_POOL + D_ATTN] * Q_SCALE).astype(BF16)
        uext[halo + j * BLOCK:halo + (j + 1) * BLOCK, :] = u
        kq = _kv_lane_variants(k, low)
        vq = _kv_lane_variants(v, low)
        for i in range(4):
            kvar[i, BLOCK + j * BLOCK:BLOCK + (j + 1) * BLOCK, :] = kq[i]
            vvar[i, BLOCK + j * BLOCK:BLOCK + (j + 1) * BLOCK, :] = vq[i]
        if j == tile // BLOCK - 1:
            nk_ref[...] = k.T
            nv_ref[...] = v.T
            np_ref[...] = u[BLOCK - halo:, :]
        yield

    for g, w in enumerate(POOL_WINDOWS):
        cols = slice(g * POOL_GROUP_DIM, (g + 1) * POOL_GROUP_DIM)
        ug = uext[halo:halo + tile, cols]
        s = ug
        for i in range(1, w):
            s = s + uext[halo - i:halo - i + tile, cols]
        p = s * (1.0 / w) - ug
        mix_ref[:, cols] = (_dot(p.astype(BF16), w_pool_ref[g]) * pscale_ref[:, cols]).astype(BF16)
        yield

    for j in range(tile // BLOCK):
        rows = slice(j * BLOCK, (j + 1) * BLOCK)
        keys = slice(j * BLOCK, (j + 2) * BLOCK)
        sel = jnp.where(first_tile, 0, 1) if j == 0 else 1
        for kv in range(N_KV_HEADS):
            tiles = (2 * kv, 2 * kv + 1)
            lhs = jnp.concatenate([qbuf[rows, p * LANES:(p + 1) * LANES] for p in tiles], axis=0)
            s_lo = _dot_nt(lhs, kvar[2 * kv, keys, :])
            s_hi = _dot_nt(lhs, kvar[2 * kv + 1, keys, :])
            for n, p in enumerate(tiles):
                acc = None
                for half, s_all in enumerate((s_lo, s_hi)):
                    hd = 2 * p + half
                    s = s_all[n * BLOCK:(n + 1) * BLOCK] + bias_ref[sel, hd]
                    sink = sink_ref[0, hd]
                    m = jnp.maximum(jnp.max(s, axis=-1, keepdims=True), sink)
                    e = jnp.exp(s - m)
                    den = jnp.sum(e, axis=-1, keepdims=True) + jnp.exp(sink - m)
                    o = _dot(e.astype(BF16), vvar[2 * kv + half, keys, :]) * (1.0 / den)
                    acc = o if acc is None else acc + o
                mix_ref[rows, D_POOL + p * LANES:D_POOL + (p + 1) * LANES] = acc.astype(BF16)
            yield

    for i in range(4):
        kvar[i, 0:BLOCK, :] = kvar[i, tile:tile + BLOCK, :]
        vvar[i, 0:BLOCK, :] = vvar[i, tile:tile + BLOCK, :]
    uext[0:halo, :] = uext[tile:tile + halo, :]


def _sample_front_kernel(sink_ref, x_ref, ck_ref, cv_ref, st_ref, g_ref, b_ref, w_in_ref,
                         w_pool_ref, pscale_ref, bias_c_ref, bias_n_ref,
                         mix_ref, nk_ref, nv_ref, np_ref, o_s):
    ns, tokens, d_model = x_ref.shape
    lane = lax.broadcasted_iota(jnp.int32, (1, LANES), 1)
    low = lane < HEAD_DIM

    x = jnp.concatenate([x_ref[:, t, :] for t in range(tokens)], axis=0)
    h = _layer_norm(x, g_ref[...], b_ref[...])
    proj = _dot(h.astype(BF16), w_in_ref[...])
    u = proj[:, 0:D_POOL]
    q = proj[:, D_POOL:D_POOL + D_ATTN] * Q_SCALE
    k = proj[:, D_POOL + D_ATTN:D_POOL + D_ATTN + D_KV]
    v = proj[:, D_POOL + D_ATTN + D_KV:]

    ext = [st_ref[r] for r in range(POOL_BUF)]
    ext += [u[t * ns:(t + 1) * ns] for t in range(tokens)]
    ps = []
    for t in range(tokens):
        idx = POOL_BUF + t
        pos = PAST_LEN - POOL_BUF + idx
        parts = []
        for g, w in enumerate(POOL_WINDOWS):
            cols = slice(g * POOL_GROUP_DIM, (g + 1) * POOL_GROUP_DIM)
            lo = max(idx + 1 - w, 0)
            s = ext[idx][:, cols]
            for r in range(idx - 1, lo - 1, -1):
                s = s + ext[r][:, cols]
            parts.append(s * (1.0 / min(w, pos + 1)) - ext[idx][:, cols])
        ps.append(jnp.concatenate(parts, axis=1))
    p = jnp.concatenate(ps, axis=0).astype(BF16)
    zs = [_dot(p[:, g * POOL_GROUP_DIM:(g + 1) * POOL_GROUP_DIM], w_pool_ref[g])
          for g in range(len(POOL_WINDOWS))]
    z = (jnp.concatenate(zs, axis=1) * pscale_ref[...]).astype(BF16)
    for r in range(POOL_BUF):
        np_ref[r] = ext[r + tokens]

    qh = []
    for hd in range(N_HEADS):
        tl = q[:, (hd // 2) * LANES:(hd // 2 + 1) * LANES]
        kv = hd // (N_HEADS // N_KV_HEADS)
        want_low = kv == 0
        is_low = hd % 2 == 0
        src = tl if want_low == is_low else pltpu.roll(tl, HEAD_DIM, axis=1)
        qh.append(jnp.where(low, src, 0.0) if want_low else jnp.where(low, 0.0, src))

    gr = SEQ_GROUP
    rows_per_head = tokens * gr
    sub = lax.broadcasted_iota(jnp.int32, (N_HEADS * rows_per_head, 1), 0) % gr
    sink = jnp.concatenate([jnp.full((rows_per_head, 1), sink_ref[0, hd], F32) for hd in range(N_HEADS)],
                           axis=0)
    zpad = jnp.zeros((LANES - rows_per_head, LANES), F32)
    ztop = jnp.zeros((WINDOW - SUBLANES, LANES), F32)
    sub8 = lax.broadcasted_iota(jnp.int32, (SUBLANES, 1), 0)
    keep = WINDOW - tokens

    def appended(old_ref, news, j):
        tail = jnp.zeros((SUBLANES, LANES), F32)
        for t in range(tokens):
            row = SUBLANES - tokens + t
            tail = jnp.where(sub8 == row, pltpu.roll(news[t], (row - j) % SUBLANES, axis=0), tail)
        cols = jnp.concatenate([ztop, tail], axis=0).T
        return jnp.where(lane >= keep, cols, pltpu.roll(old_ref[...], keep, axis=1))

    for gi in range(ns // gr):
        base = gi * gr
        lhs = jnp.concatenate([qh[hd][t * ns + base:t * ns + base + gr]
                               for hd in range(N_HEADS) for t in range(tokens)], axis=0).astype(BF16)
        sc = None
        for s in range(gr):
            s_one = _dot(lhs, ck_ref[base + s].astype(BF16))
            sc = s_one if sc is None else jnp.where(sub == s, s_one, sc)
        sc = sc + bias_c_ref[...]
        k_t = [k[t * ns + base:t * ns + base + gr] for t in range(tokens)]
        v_t = [v[t * ns + base:t * ns + base + gr] for t in range(tokens)]
        for s in range(gr):
            nk_ref[base + s] = appended(ck_ref.at[base + s], k_t, s)
            nv_ref[base + s] = appended(cv_ref.at[base + s], v_t, s)
        k_new = jnp.concatenate(k_t + [zpad], axis=0)
        v_new = jnp.concatenate(v_t + [zpad], axis=0)
        sn = _dot_nt(lhs, k_new.astype(BF16)) + bias_n_ref[...]
        m = jnp.maximum(jnp.maximum(jnp.max(sc, axis=-1, keepdims=True),
                                    jnp.max(sn, axis=-1, keepdims=True)), sink)
        ec = jnp.exp(sc - m)
        en = jnp.exp(sn - m)
        den = (jnp.sum(ec, axis=-1, keepdims=True) + jnp.sum(en, axis=-1, keepdims=True)
               + jnp.exp(sink - m))
        o = _dot(en.astype(BF16), v_new.astype(BF16))
        for s in range(gr):
            o = o + _dot_nt(jnp.where(sub == s, ec, 0.0).astype(BF16), cv_ref[base + s].astype(BF16))
        o = o * (1.0 / den)
        for t in range(tokens):
            for p in range(N_HEADS // 2):
                a = o[(2 * p * tokens + t) * gr:(2 * p * tokens + t + 1) * gr]
                c = o[((2 * p + 1) * tokens + t) * gr:((2 * p + 1) * tokens + t + 1) * gr]
                if p < N_HEADS // 4:
                    c = pltpu.roll(c, HEAD_DIM, axis=1)
                else:
                    a = pltpu.roll(a, HEAD_DIM, axis=1)
                o_s[t * ns + base:t * ns + base + gr, p * LANES:(p + 1) * LANES] = jnp.where(low, a, c)

    for t in range(tokens):
        mix_ref[:, t * d_model:t * d_model + D_POOL] = z[t * ns:(t + 1) * ns]
        mix_ref[:, t * d_model + D_POOL:(t + 1) * d_model] = o_s[t * ns:(t + 1) * ns, :].astype(BF16)


def _pre_pieces(rows, x, mix, ge_ref, be_ref, w_out_ref, g1_ref, b1_ref, h1_s, hb_s):
    h1_s[rows, :] = ALPHA * _layer_norm(x(), ge_ref[...], be_ref[...])
    yield
    h1_s[rows, :] = h1_s[rows, :] + _dot(mix(), w_out_ref[...])
    yield
    h = _layer_norm(h1_s[rows, :], g1_ref[...], b1_ref[...])
    h1_s[rows, :] = h
    hb_s[rows, :] = h.astype(BF16)
    yield


def _mlp_pieces(rows, a_s, h1_s, hb_s, w1_ref, w2_ref, g2_ref, b2_ref, y_ref):
    for c in range(w1_ref.shape[1] // UP_CHUNK):
        cols = slice(c * UP_CHUNK, (c + 1) * UP_CHUNK)
        a = _dot(hb_s[rows, :], w1_ref[:, cols])
        a_s[:, cols] = jnp.square(jnp.maximum(a, 0.0)).astype(BF16)
        if c == 0:
            y_ref[rows, :] = ALPHA * h1_s[rows, :]
        yield
    for n in range(y_ref.shape[1] // DOWN_CHUNK):
        cols = slice(n * DOWN_CHUNK, (n + 1) * DOWN_CHUNK)
        y_ref[rows, cols] = y_ref[rows, cols] + _dot(a_s[...], w2_ref[:, cols])
        yield
    y_ref[rows, :] = _layer_norm(y_ref[rows, :], g2_ref[...], b2_ref[...])
    yield


def _main_kernel(tokens, tiles_per_seq, n_tiles,
                 sink_ref, x_ref, xs_ref, mixs_ref, ge_ref, be_ref, w_in_ref, w_pool_ref,
                 pscale_ref, bias_ref, meta_proj_ref, w_out_ref, g1_ref, b1_ref, w1_ref, w2_ref,
                 g2_ref, b2_ref,
                 y_ref, ys_ref, nk_ref, nv_ref, np_ref,
                 mixbuf, kvar, vvar, uext, qbuf, h1_s, hb_s, a_s):
    s = pl.program_id(0)
    is_first = s == 0
    first_tile = jnp.minimum(s, n_tiles - 1) % tiles_per_seq == 0
    tile, d_model = x_ref.shape
    ns = xs_ref.shape[0]
    half = tile // 2
    halves = (slice(0, half), slice(half, tile))
    pre_w = (ge_ref, be_ref, w_out_ref, g1_ref, b1_ref, h1_s, hb_s)

    @pl.when(is_first)
    def _():
        def toks(rows):
            return range(rows.start // ns, rows.stop // ns)
        for rows in halves:
            x = lambda: jnp.concatenate([xs_ref[:, t, :] for t in toks(rows)], axis=0)
            mix = lambda: jnp.concatenate([mixs_ref[:, t * d_model:(t + 1) * d_model] for t in toks(rows)],
                                          axis=0)
            for _ in _pre_pieces(rows, x, mix, *pre_w):
                pass

    @pl.when(first_tile)
    def _():
        _carry_init(meta_proj_ref, kvar, vvar, uext)

    mlp = [_mlp_pieces(rows, a_s.at[i], h1_s, hb_s, w1_ref, w2_ref, g2_ref, b2_ref, y_ref)
           for i, rows in enumerate(halves)]
    front = _front_pieces(first_tile, x_ref, sink_ref, ge_ref, be_ref, w_in_ref, w_pool_ref, pscale_ref,
                          bias_ref, mixbuf, nk_ref.at[0], nv_ref.at[0], np_ref.at[0], kvar, vvar, uext, qbuf)
    pre = [_pre_pieces(rows, lambda rows=rows: x_ref[rows, :], lambda rows=rows: mixbuf[rows, :], *pre_w)
           for rows in halves]
    a, b, f, pa, pb = mlp[0], mlp[1], front, pre[0], pre[1]
    order = (
        [a, f, a, f, a, f, a, f, a, f, a, f, a, a]
        + [a, f, f, a, f, a, f, a, f]
        + [b, a, b, f, b, f, b, f, b, f, b, f, b, b]
        + [b, f, pa, pa, b, pb, pb, pa, b, pb, b, b])
    for g in order:
        next(g, None)
    for g in (a, b, f, pa, pb):
        for _ in g:
            raise AssertionError("piece left unscheduled")

    @pl.when(is_first)
    def _():
        for t in range(tokens):
            ys_ref[:, t, :] = y_ref[t * ns:(t + 1) * ns, :]


def _const_spec(shape):
    return pl.BlockSpec(shape, lambda *_: (0,) * len(shape), pipeline_mode=pl.Buffered(1))


def _smem_spec():
    return pl.BlockSpec(memory_space=pltpu.SMEM)


def _sample_front_call(sink, xs, ck, cv, st, *consts):
    n_seq, tokens, d_model = xs.shape
    seq_spec = lambda *tail: pl.BlockSpec((SAMPLE_SEQS,) + tail, lambda i: (i,) + (0,) * len(tail))
    state_spec = pl.BlockSpec((POOL_BUF, SAMPLE_SEQS, D_POOL), lambda i: (0, i, 0))
    return pl.pallas_call(
        _sample_front_kernel,
        grid=(n_seq // SAMPLE_SEQS,),
        in_specs=[_smem_spec(), seq_spec(tokens, d_model), seq_spec(D_KV, WINDOW), seq_spec(D_KV, WINDOW),
                  state_spec] + [_const_spec(c.shape) for c in consts],
        out_specs=[seq_spec(tokens * d_model), seq_spec(D_KV, WINDOW), seq_spec(D_KV, WINDOW), state_spec],
        out_shape=[jax.ShapeDtypeStruct((n_seq, tokens * d_model), BF16),
                   jax.ShapeDtypeStruct((n_seq, D_KV, WINDOW), F32),
                   jax.ShapeDtypeStruct((n_seq, D_KV, WINDOW), F32),
                   jax.ShapeDtypeStruct((POOL_BUF, n_seq, D_POOL), F32)],
        scratch_shapes=[pltpu.VMEM((tokens * SAMPLE_SEQS, D_ATTN), F32)],
        compiler_params=pltpu.CompilerParams(dimension_semantics=("arbitrary",),
                                             vmem_limit_bytes=VMEM_LIMIT_BYTES),
        name="sample_front",
    )(sink, xs, ck, cv, st, *consts)


def kernel(x_prompt, x_sample, cache_win_k, cache_win_v, state_pool, meta_tokens, ln_emb_g, ln_emb_b,
           rel_table, w_in, w_pool, pool_scale, sinks, w_out, ln1_g, ln1_b, w_mlp_in, w_mlp_out,
           ln2_g, ln2_b):
    batch, seq, d_model = x_prompt.shape
    n_seq, tokens, _ = x_sample.shape
    assert w_in.shape[0] == DEPTH and d_model == D_POOL + D_ATTN
    assert seq % PROMPT_TILE == 0 and n_seq % SAMPLE_SEQS == 0 and tokens * n_seq == PROMPT_TILE
    assert cache_win_k.shape[2] == WINDOW and state_pool.shape[2] == POOL_BUF and tokens <= POOL_BUF
    d_in = w_in.shape[2]

    row = lambda a: a.reshape(1, -1).astype(F32)
    ge, be = row(ln_emb_g), row(ln_emb_b)
    w_in_b = w_in[0].astype(BF16)
    w_pool_b = w_pool[0].astype(BF16)
    pscale = row(pool_scale[0])
    sink = row(sinks[0])
    fin_w = (ge, be, w_out[0].astype(BF16), row(ln1_g[0]), row(ln1_b[0]), w_mlp_in[0].astype(BF16),
             w_mlp_out[0].astype(BF16), row(ln2_g[0]), row(ln2_b[0]))
    cparams = lambda sem: pltpu.CompilerParams(dimension_semantics=sem, vmem_limit_bytes=VMEM_LIMIT_BYTES)

    bkt_p, bkt_c, bkt_n = _bucket_tables(tokens)
    srows = N_HEADS * tokens * SEQ_GROUP
    bias_p, bias_c, bias_n, meta_proj = pl.pallas_call(
        _prologue_kernel,
        in_specs=[_smem_spec()] + [pl.BlockSpec(memory_space=pltpu.VMEM)] * 7,
        out_specs=[pl.BlockSpec(memory_space=pltpu.VMEM)] * 4,
        out_shape=[jax.ShapeDtypeStruct((2, N_HEADS, BLOCK, 2 * BLOCK), F32),
                   jax.ShapeDtypeStruct((srows, WINDOW), F32),
                   jax.ShapeDtypeStruct((srows, LANES), F32),
                   jax.ShapeDtypeStruct((N_META, d_in), F32)],
        compiler_params=pltpu.CompilerParams(vmem_limit_bytes=VMEM_LIMIT_BYTES),
        name="prologue",
    )(rel_table.astype(F32), jnp.asarray(bkt_p), jnp.asarray(bkt_c), jnp.asarray(bkt_n),
      meta_tokens.astype(F32), ge, be, w_in_b)

    to_kd_pos = lambda c: jnp.swapaxes(c[0].reshape(n_seq, WINDOW, D_KV), 1, 2)
    mix_s, nk_s, nv_s, np_s = _sample_front_call(
        sink, x_sample, to_kd_pos(cache_win_k), to_kd_pos(cache_win_v), jnp.swapaxes(state_pool[0], 0, 1),
        ge, be, w_in_b, w_pool_b, pscale, bias_c, bias_n)
    xs = x_sample

    tile = PROMPT_TILE
    tiles_per_seq = seq // tile
    n_tiles = batch * tiles_per_seq
    xp = x_prompt.reshape(batch * seq, d_model)
    front_tile = lambda s: jnp.minimum(s, n_tiles - 1)
    finish_tile = lambda s: jnp.maximum(s - 1, 0)
    seq_of = lambda s: front_tile(s) // tiles_per_seq
    consts = (ge, be, w_in_b, w_pool_b, pscale, bias_p, meta_proj) + fin_w[2:]
    y_p, y_s, nk_p, nv_p, np_p = pl.pallas_call(
        functools.partial(_main_kernel, tokens, tiles_per_seq, n_tiles),
        grid=(n_tiles + 1,),
        in_specs=[_smem_spec(),
                  pl.BlockSpec((tile, d_model), lambda s: (front_tile(s), 0)),
                  _const_spec(xs.shape), _const_spec(mix_s.shape)] + [_const_spec(c.shape) for c in consts],
        out_specs=[pl.BlockSpec((tile, d_model), lambda s: (finish_tile(s), 0)),
                   pl.BlockSpec(xs.shape, lambda s: (0, 0, 0)),
                   pl.BlockSpec((1, BLOCK, D_KV), lambda s: (seq_of(s), 0, 0)),
                   pl.BlockSpec((1, BLOCK, D_KV), lambda s: (seq_of(s), 0, 0)),
                   pl.BlockSpec((1, HALO, D_POOL), lambda s: (seq_of(s), 0, 0))],
        out_shape=[jax.ShapeDtypeStruct((batch * seq, d_model), F32),
                   jax.ShapeDtypeStruct(xs.shape, F32),
                   jax.ShapeDtypeStruct((batch, BLOCK, D_KV), F32),
                   jax.ShapeDtypeStruct((batch, BLOCK, D_KV), F32),
                   jax.ShapeDtypeStruct((batch, HALO, D_POOL), F32)],
        scratch_shapes=[pltpu.VMEM((tile, d_model), BF16),
                        pltpu.VMEM((4, tile + BLOCK, LANES), BF16),
                        pltpu.VMEM((4, tile + BLOCK, LANES), BF16),
                        pltpu.VMEM((tile + HALO, D_POOL), F32),
                        pltpu.VMEM((tile, D_ATTN), BF16),
                        pltpu.VMEM((tile, d_model), F32),
                        pltpu.VMEM((tile, d_model), BF16),
                        pltpu.VMEM((2, tile // 2, w_mlp_in.shape[2]), BF16)],
        compiler_params=cparams(("arbitrary",)),
        name="main",
    )(sink, xp, xs, mix_s, *consts)

    kv_shape = (DEPTH, -1, WINDOW, N_KV_HEADS, HEAD_DIM)
    from_kd_pos = lambda c: jnp.swapaxes(c, 1, 2).reshape(kv_shape)
    return (y_p.reshape(batch, seq, d_model), y_s,
            from_kd_pos(nk_p), from_kd_pos(nv_p),
            np_p[:, HALO - POOL_BUF:, :].reshape(DEPTH, batch, POOL_BUF, D_POOL),
            from_kd_pos(nk_s), from_kd_pos(nv_s), jnp.swapaxes(np_s, 0, 1)[None])
```

```python
import functools
import math

import jax
import jax.numpy as jnp
import numpy as np
from jax import lax
from jax.experimental import pallas as pl
from jax.experimental.pallas import tpu as pltpu

N_META = 16
POOL_WINDOWS = (2, 4, 8, 16)
POOL_GROUP_DIM = 128
D_POOL = len(POOL_WINDOWS) * POOL_GROUP_DIM
POOL_BUF = max(POOL_WINDOWS) - 1
N_HEADS = 8
HEAD_DIM = 64
D_ATTN = N_HEADS * HEAD_DIM
N_KV_HEADS = 2
D_KV = N_KV_HEADS * HEAD_DIM
WINDOW = 128
BLOCK = 128
REL_BUCKETS = 32
REL_MAX_DIST = 128
PAST_LEN = 8192
DEPTH = 1
ALPHA = (2.0 * DEPTH) ** 0.25
LN_EPS = 1e-5
Q_SCALE = HEAD_DIM ** -0.5

LANES = 128
SUBLANES = 8
VMEM_LIMIT_BYTES = 56 * 1024 * 1024

PROMPT_TILE = 512
IN_ROWS = 256
UP_CHUNK = 512
DOWN_CHUNK = 256
SAMPLE_SEQS = 32
SEQ_GROUP = SUBLANES

F32 = jnp.float32
BF16 = jnp.bfloat16
NEG_INF = float("-inf")


def _rel_bucket(dist):
    n = np.maximum(dist, 0)
    max_exact = REL_BUCKETS // 2
    nf = np.maximum(n, 1).astype(np.float64)
    large = max_exact + (np.log(nf / max_exact) / math.log(REL_MAX_DIST / max_exact)
                         * (REL_BUCKETS - max_exact)).astype(np.int32)
    large = np.minimum(large, REL_BUCKETS - 1)
    return np.where(n < max_exact, n, large).astype(np.int32)


def _bucket_tables(tokens):
    dist = (np.arange(BLOCK)[:, None] + BLOCK) - np.arange(2 * BLOCK)[None, :]
    ok = (dist >= 0) & (dist < WINDOW)
    rest = np.where(ok, _rel_bucket(dist), -1)
    first = np.where(np.arange(2 * BLOCK)[None, :] >= BLOCK - N_META, rest, -1)
    prompt = np.stack([first, rest]).astype(np.int32)
    t = np.repeat(np.arange(tokens), SEQ_GROUP)
    s = np.tile(np.arange(SEQ_GROUP), tokens)
    dist = (t[:, None] + WINDOW) - np.arange(WINDOW)[None, :]
    ok = (dist >= 0) & (dist < WINDOW)
    cache = np.where(ok, _rel_bucket(dist), -1).astype(np.int32)
    dist = t[:, None] - t[None, :]
    ok = (dist >= 0) & (s[:, None] == s[None, :])
    new = np.full((tokens * SEQ_GROUP, LANES), -1, np.int32)
    new[:, :tokens * SEQ_GROUP] = np.where(ok, _rel_bucket(dist), -1)
    return prompt, cache, new


def _layer_norm(x, g, b):
    mu = jnp.mean(x, axis=-1, keepdims=True)
    xc = x - mu
    var = jnp.mean(xc * xc, axis=-1, keepdims=True)
    return xc * lax.rsqrt(var + LN_EPS) * g + b


def _dot(a, b):
    return jnp.dot(a, b, preferred_element_type=F32)


def _dot_nt(a, b):
    return lax.dot_general(a, b, (((1,), (1,)), ((), ())), preferred_element_type=F32)


def _kv_lane_variants(x, low):
    xr = pltpu.roll(x, HEAD_DIM, axis=1)
    zero = jnp.zeros_like(x)
    return (jnp.where(low, x, zero).astype(BF16), jnp.where(low, zero, xr).astype(BF16),
            jnp.where(low, xr, zero).astype(BF16), jnp.where(low, zero, x).astype(BF16))


def _prologue_kernel(tab_ref, bkt_p_ref, bkt_c_ref, bkt_n_ref, meta_ref, g_ref, b_ref, w_in_ref,
                     bias_p_ref, bias_c_ref, bias_n_ref, meta_proj_ref):
    def lookup(bucket, h):
        def body(i, acc):
            return jnp.where(bucket == i, tab_ref[i, h], acc)
        return lax.fori_loop(0, REL_BUCKETS, body, jnp.full(bucket.shape, NEG_INF, F32))

    rows = bkt_c_ref.shape[0]
    for h in range(N_HEADS):
        rest = lookup(bkt_p_ref[1], h)
        bias_p_ref[1, h] = rest
        bias_p_ref[0, h] = jnp.where(bkt_p_ref[0] >= 0, rest, NEG_INF)
        bias_c_ref[h * rows:(h + 1) * rows, :] = lookup(bkt_c_ref[...], h)
        bias_n_ref[h * rows:(h + 1) * rows, :] = lookup(bkt_n_ref[...], h)
    hm = _layer_norm(meta_ref[...], g_ref[...], b_ref[...])
    meta_proj_ref[...] = _dot(hm.astype(BF16), w_in_ref[...])


HALO = 2 * SUBLANES


def _carry_init(meta_proj_ref, kvar, vvar, uext):
    low = lax.broadcasted_iota(jnp.int32, (1, LANES), 1) < HEAD_DIM
    pad = jnp.zeros((BLOCK - N_META, LANES), BF16)
    kq = _kv_lane_variants(meta_proj_ref[:, D_POOL + D_ATTN:D_POOL + D_ATTN + D_KV], low)
    vq = _kv_lane_variants(meta_proj_ref[:, D_POOL + D_ATTN + D_KV:], low)
    for i in range(4):
        kvar[i, 0:BLOCK, :] = jnp.concatenate([pad, kq[i]], axis=0)
        vvar[i, 0:BLOCK, :] = jnp.concatenate([pad, vq[i]], axis=0)
    uext[0:HALO, :] = meta_proj_ref[:, 0:D_POOL]


def _front_pieces(first_tile, x_ref, sink_ref, g_ref, b_ref, w_in_ref, w_pool_ref, pscale_ref, bias_ref,
                  mix_ref, nk_ref, nv_ref, np_ref, kvar, vvar, uext, qbuf):
    tile = x_ref.shape[0]
    halo = HALO
    low = lax.broadcasted_iota(jnp.int32, (1, LANES), 1) < HEAD_DIM

    for j in range(tile // IN_ROWS):
        rows = slice(j * IN_ROWS, (j + 1) * IN_ROWS)
        h = _layer_norm(x_ref[rows, :], g_ref[...], b_ref[...])
        proj = _dot(h.astype(BF16), w_in_ref[...])
        u = proj[:, 0:D_POOL]
        k = proj[:, D_POOL + D_ATTN:D_POOL + D_ATTN + D_KV]
        v = proj[:, D_POOL + D_ATTN + D_KV:]
        qbuf[rows, :] = (proj[:, D_POOL:D_POOL + D_ATTN] * Q_SCALE).astype(BF16)
        uext[halo + rows.start:halo + rows.stop, :] = u
        kq = _kv_lane_variants(k, low)
        vq = _kv_lane_variants(v, low)
        for i in range(4):
            kvar[i, BLOCK + rows.start:BLOCK + rows.stop, :] = kq[i]
            vvar[i, BLOCK + rows.start:BLOCK + rows.stop, :] = vq[i]
        if rows.stop == tile:
            nk_ref[...] = k[IN_ROWS - BLOCK:, :].T
            nv_ref[...] = v[IN_ROWS - BLOCK:, :].T
            np_ref[...] = u[IN_ROWS - halo:, :]
        yield

    for g, w in enumerate(POOL_WINDOWS):
        cols = slice(g * POOL_GROUP_DIM, (g + 1) * POOL_GROUP_DIM)
        ug = uext[halo:halo + tile, cols]
        s = ug
        for i in range(1, w):
            s = s + uext[halo - i:halo - i + tile, cols]
        p = s * (1.0 / w) - ug
        mix_ref[:, cols] = (_dot(p.astype(BF16), w_pool_ref[g]) * pscale_ref[:, cols]).astype(BF16)
        yield

    for j in range(tile // BLOCK):
        rows = slice(j * BLOCK, (j + 1) * BLOCK)
        keys = slice(j * BLOCK, (j + 2) * BLOCK)
        sel = jnp.where(first_tile, 0, 1) if j == 0 else 1
        for kv in range(N_KV_HEADS):
            tiles = (2 * kv, 2 * kv + 1)
            lhs = jnp.concatenate([qbuf[rows, p * LANES:(p + 1) * LANES] for p in tiles], axis=0)
            acc = None
            for half in range(2):
                heads = [2 * p + half for p in tiles]
                s = _dot_nt(lhs, kvar[2 * kv + half, keys, :])
                s = s + jnp.concatenate([bias_ref[sel, hd] for hd in heads], axis=0)
                sink = jnp.concatenate([jnp.full((BLOCK, 1), sink_ref[0, hd], F32) for hd in heads], axis=0)
                m = jnp.maximum(jnp.max(s, axis=-1, keepdims=True), sink)
                e = jnp.exp(s - m)
                den = jnp.sum(e, axis=-1, keepdims=True) + jnp.exp(sink - m)
                o = _dot(e.astype(BF16), vvar[2 * kv + half, keys, :]) * (1.0 / den)
                acc = o if acc is None else acc + o
            for n, p in enumerate(tiles):
                mix_ref[rows, D_POOL + p * LANES:D_POOL + (p + 1) * LANES] = (
                    acc[n * BLOCK:(n + 1) * BLOCK].astype(BF16))
            yield

    for i in range(4):
        kvar[i, 0:BLOCK, :] = kvar[i, tile:tile + BLOCK, :]
        vvar[i, 0:BLOCK, :] = vvar[i, tile:tile + BLOCK, :]
    uext[0:halo, :] = uext[tile:tile + halo, :]


def _sample_front_kernel(sink_ref, x_ref, ck_ref, cv_ref, st_ref, g_ref, b_ref, w_in_ref,
                         w_pool_ref, pscale_ref, bias_c_ref, bias_n_ref,
                         mix_ref, nk_ref, nv_ref, np_ref, o_s):
    ns, tokens, d_model = x_ref.shape
    lane = lax.broadcasted_iota(jnp.int32, (1, LANES), 1)
    low = lane < HEAD_DIM

    x = jnp.concatenate([x_ref[:, t, :] for t in range(tokens)], axis=0)
    h = _layer_norm(x, g_ref[...], b_ref[...])
    proj = _dot(h.astype(BF16), w_in_ref[...])
    u = proj[:, 0:D_POOL]
    q = proj[:, D_POOL:D_POOL + D_ATTN] * Q_SCALE
    k = proj[:, D_POOL + D_ATTN:D_POOL + D_ATTN + D_KV]
    v = proj[:, D_POOL + D_ATTN + D_KV:]

    ext = [st_ref[r] for r in range(POOL_BUF)]
    ext += [u[t * ns:(t + 1) * ns] for t in range(tokens)]
    ps = []
    for t in range(tokens):
        idx = POOL_BUF + t
        pos = PAST_LEN - POOL_BUF + idx
        parts = []
        for g, w in enumerate(POOL_WINDOWS):
            cols = slice(g * POOL_GROUP_DIM, (g + 1) * POOL_GROUP_DIM)
            lo = max(idx + 1 - w, 0)
            s = ext[idx][:, cols]
            for r in range(idx - 1, lo - 1, -1):
                s = s + ext[r][:, cols]
            parts.append(s * (1.0 / min(w, pos + 1)) - ext[idx][:, cols])
        ps.append(jnp.concatenate(parts, axis=1))
    p = jnp.concatenate(ps, axis=0).astype(BF16)
    zs = [_dot(p[:, g * POOL_GROUP_DIM:(g + 1) * POOL_GROUP_DIM], w_pool_ref[g])
          for g in range(len(POOL_WINDOWS))]
    z = (jnp.concatenate(zs, axis=1) * pscale_ref[...]).astype(BF16)
    for r in range(POOL_BUF):
        np_ref[r] = ext[r + tokens]

    qh = []
    for hd in range(N_HEADS):
        tl = q[:, (hd // 2) * LANES:(hd // 2 + 1) * LANES]
        kv = hd // (N_HEADS // N_KV_HEADS)
        want_low = kv == 0
        is_low = hd % 2 == 0
        src = tl if want_low == is_low else pltpu.roll(tl, HEAD_DIM, axis=1)
        qh.append(jnp.where(low, src, 0.0) if want_low else jnp.where(low, 0.0, src))

    gr = SEQ_GROUP
    rows_per_head = tokens * gr
    sub = lax.broadcasted_iota(jnp.int32, (N_HEADS * rows_per_head, 1), 0) % gr
    sink = jnp.concatenate([jnp.full((rows_per_head, 1), sink_ref[0, hd], F32) for hd in range(N_HEADS)],
                           axis=0)
    zpad = jnp.zeros((LANES - rows_per_head, LANES), F32)
    ztop = jnp.zeros((WINDOW - SUBLANES, LANES), F32)
    sub8 = lax.broadcasted_iota(jnp.int32, (SUBLANES, 1), 0)
    keep = WINDOW - tokens

    def appended(old_ref, news, j):
        tail = jnp.zeros((SUBLANES, LANES), F32)
        for t in range(tokens):
            row = SUBLANES - tokens + t
            tail = jnp.where(sub8 == row, pltpu.roll(news[t], (row - j) % SUBLANES, axis=0), tail)
        cols = jnp.concatenate([ztop, tail], axis=0).T
        return jnp.where(lane >= keep, cols, pltpu.roll(old_ref[...], keep, axis=1))

    for gi in range(ns // gr):
        base = gi * gr
        lhs = jnp.concatenate([qh[hd][t * ns + base:t * ns + base + gr]
                               for hd in range(N_HEADS) for t in range(tokens)], axis=0).astype(BF16)
        sc = None
        for s in range(gr):
            s_one = _dot(lhs, ck_ref[base + s].astype(BF16))
            sc = s_one if sc is None else jnp.where(sub == s, s_one, sc)
        sc = sc + bias_c_ref[...]
        k_t = [k[t * ns + base:t * ns + base + gr] for t in range(tokens)]
        v_t = [v[t * ns + base:t * ns + base + gr] for t in range(tokens)]
        for s in range(gr):
            nk_ref[base + s] = appended(ck_ref.at[base + s], k_t, s)
            nv_ref[base + s] = appended(cv_ref.at[base + s], v_t, s)
        k_new = jnp.concatenate(k_t + [zpad], axis=0)
        v_new = jnp.concatenate(v_t + [zpad], axis=0)
        sn = _dot_nt(lhs, k_new.astype(BF16)) + bias_n_ref[...]
        m = jnp.maximum(jnp.maximum(jnp.max(sc, axis=-1, keepdims=True),
                                    jnp.max(sn, axis=-1, keepdims=True)), sink)
        ec = jnp.exp(sc - m)
        en = jnp.exp(sn - m)
        den = (jnp.sum(ec, axis=-1, keepdims=True) + jnp.sum(en, axis=-1, keepdims=True)
               + jnp.exp(sink - m))
        o = _dot(en.astype(BF16), v_new.astype(BF16))
        for s in range(gr):
            o = o + _dot_nt(jnp.where(sub == s, ec, 0.0).astype(BF16), cv_ref[base + s].astype(BF16))
        o = o * (1.0 / den)
        for t in range(tokens):
            for p in range(N_HEADS // 2):
                a = o[(2 * p * tokens + t) * gr:(2 * p * tokens + t + 1) * gr]
                c = o[((2 * p + 1) * tokens + t) * gr:((2 * p + 1) * tokens + t + 1) * gr]
                if p < N_HEADS // 4:
                    c = pltpu.roll(c, HEAD_DIM, axis=1)
                else:
                    a = pltpu.roll(a, HEAD_DIM, axis=1)
                o_s[t * ns + base:t * ns + base + gr, p * LANES:(p + 1) * LANES] = jnp.where(low, a, c)

    for t in range(tokens):
        mix_ref[:, t * d_model:t * d_model + D_POOL] = z[t * ns:(t + 1) * ns]
        mix_ref[:, t * d_model + D_POOL:(t + 1) * d_model] = o_s[t * ns:(t + 1) * ns, :].astype(BF16)


def _pre_pieces(rows, x, mix, ge_ref, be_ref, w_out_ref, g1_ref, b1_ref, h1_s, hb_s):
    h1_s[rows, :] = ALPHA * _layer_norm(x(), ge_ref[...], be_ref[...])
    yield
    h1_s[rows, :] = h1_s[rows, :] + _dot(mix(), w_out_ref[...])
    yield
    h = _layer_norm(h1_s[rows, :], g1_ref[...], b1_ref[...])
    h1_s[rows, :] = h
    hb_s[rows, :] = h.astype(BF16)
    yield


def _mlp_pieces(rows, a_s, h1_s, hb_s, w1_ref, w2_ref, g2_ref, b2_ref, y_ref):
    for c in range(w1_ref.shape[1] // UP_CHUNK):
        cols = slice(c * UP_CHUNK, (c + 1) * UP_CHUNK)
        a = _dot(hb_s[rows, :], w1_ref[:, cols])
        a_s[:, cols] = jnp.square(jnp.maximum(a, 0.0)).astype(BF16)
        if c == 0:
            y_ref[rows, :] = ALPHA * h1_s[rows, :]
        yield
    for n in range(y_ref.shape[1] // DOWN_CHUNK):
        cols = slice(n * DOWN_CHUNK, (n + 1) * DOWN_CHUNK)
        y_ref[rows, cols] = y_ref[rows, cols] + _dot(a_s[...], w2_ref[:, cols])
        yield
    y_ref[rows, :] = _layer_norm(y_ref[rows, :], g2_ref[...], b2_ref[...])
    yield


def _weave(main, n_main, sides):
    order = []
    for i in range(n_main):
        order.append(main)
        order += sides[i * len(sides) // n_main:(i + 1) * len(sides) // n_main]
    return order


def _main_kernel(tokens, tiles_per_seq, n_tiles,
                 sink_ref, x_ref, xs_ref, mixs_ref, ge_ref, be_ref, w_in_ref, w_pool_ref,
                 pscale_ref, bias_ref, meta_proj_ref, w_out_ref, g1_ref, b1_ref, w1_ref, w2_ref,
                 g2_ref, b2_ref,
                 y_ref, ys_ref, nk_ref, nv_ref, np_ref,
                 mixbuf, kvar, vvar, uext, qbuf, h1_s, hb_s, a_s):
    s = pl.program_id(0)
    is_first = s == 0
    first_tile = jnp.minimum(s, n_tiles - 1) % tiles_per_seq == 0
    tile, d_model = x_ref.shape
    ns = xs_ref.shape[0]
    half = tile // 2
    halves = (slice(0, half), slice(half, tile))
    pre_w = (ge_ref, be_ref, w_out_ref, g1_ref, b1_ref, h1_s, hb_s)

    @pl.when(is_first)
    def _():
        def toks(rows):
            return range(rows.start // ns, rows.stop // ns)
        for rows in halves:
            x = lambda: jnp.concatenate([xs_ref[:, t, :] for t in toks(rows)], axis=0)
            mix = lambda: jnp.concatenate([mixs_ref[:, t * d_model:(t + 1) * d_model] for t in toks(rows)],
                                          axis=0)
            for _ in _pre_pieces(rows, x, mix, *pre_w):
                pass

    @pl.when(first_tile)
    def _():
        _carry_init(meta_proj_ref, kvar, vvar, uext)

    mlp = [_mlp_pieces(rows, a_s.at[i], h1_s, hb_s, w1_ref, w2_ref, g2_ref, b2_ref, y_ref)
           for i, rows in enumerate(halves)]
    front = _front_pieces(first_tile, x_ref, sink_ref, ge_ref, be_ref, w_in_ref, w_pool_ref, pscale_ref,
                          bias_ref, mixbuf, nk_ref.at[0], nv_ref.at[0], np_ref.at[0], kvar, vvar, uext, qbuf)
    pre = [_pre_pieces(rows, lambda rows=rows: x_ref[rows, :], lambda rows=rows: mixbuf[rows, :], *pre_w)
           for rows in halves]
    a, b, f, pa, pb = mlp[0], mlp[1], front, pre[0], pre[1]
    n_up = w1_ref.shape[1] // UP_CHUNK
    n_down = d_model // DOWN_CHUNK
    order = (
        _weave(a, n_up, [f] * 4)
        + _weave(a, n_down, [f] * 5)
        + _weave(b, n_up, [a] + [f] * 5)
        + _weave(b, n_down, [f, pa, pa, pb, pb, pa, pb]) + [b])
    for g in order:
        next(g, None)
    for g in (a, b, f, pa, pb):
        for _ in g:
            raise AssertionError("piece left unscheduled")

    @pl.when(is_first)
    def _():
        for t in range(tokens):
            ys_ref[:, t, :] = y_ref[t * ns:(t + 1) * ns, :]


def _const_spec(shape):
    return pl.BlockSpec(shape, lambda *_: (0,) * len(shape), pipeline_mode=pl.Buffered(1))


def _smem_spec():
    return pl.BlockSpec(memory_space=pltpu.SMEM)


def _sample_front_call(sink, xs, ck, cv, st, *consts):
    n_seq, tokens, d_model = xs.shape
    seq_spec = lambda *tail: pl.BlockSpec((SAMPLE_SEQS,) + tail, lambda i: (i,) + (0,) * len(tail))
    state_spec = pl.BlockSpec((POOL_BUF, SAMPLE_SEQS, D_POOL), lambda i: (0, i, 0))
    return pl.pallas_call(
        _sample_front_kernel,
        grid=(n_seq // SAMPLE_SEQS,),
        in_specs=[_smem_spec(), seq_spec(tokens, d_model), seq_spec(D_KV, WINDOW), seq_spec(D_KV, WINDOW),
                  state_spec] + [_const_spec(c.shape) for c in consts],
        out_specs=[seq_spec(tokens * d_model), seq_spec(D_KV, WINDOW), seq_spec(D_KV, WINDOW), state_spec],
        out_shape=[jax.ShapeDtypeStruct((n_seq, tokens * d_model), BF16),
                   jax.ShapeDtypeStruct((n_seq, D_KV, WINDOW), F32),
                   jax.ShapeDtypeStruct((n_seq, D_KV, WINDOW), F32),
                   jax.ShapeDtypeStruct((POOL_BUF, n_seq, D_POOL), F32)],
        scratch_shapes=[pltpu.VMEM((tokens * SAMPLE_SEQS, D_ATTN), F32)],
        compiler_params=pltpu.CompilerParams(dimension_semantics=("arbitrary",),
                                             vmem_limit_bytes=VMEM_LIMIT_BYTES),
        name="sample_front",
    )(sink, xs, ck, cv, st, *consts)


def kernel(x_prompt, x_sample, cache_win_k, cache_win_v, state_pool, meta_tokens, ln_emb_g, ln_emb_b,
           rel_table, w_in, w_pool, pool_scale, sinks, w_out, ln1_g, ln1_b, w_mlp_in, w_mlp_out,
           ln2_g, ln2_b):
    batch, seq, d_model = x_prompt.shape
    n_seq, tokens, _ = x_sample.shape
    assert w_in.shape[0] == DEPTH and d_model == D_POOL + D_ATTN
    assert seq % PROMPT_TILE == 0 and n_seq % SAMPLE_SEQS == 0 and tokens * n_seq == PROMPT_TILE
    assert cache_win_k.shape[2] == WINDOW and state_pool.shape[2] == POOL_BUF and tokens <= POOL_BUF
    d_in = w_in.shape[2]

    row = lambda a: a.reshape(1, -1).astype(F32)
    ge, be = row(ln_emb_g), row(ln_emb_b)
    w_in_b = w_in[0].astype(BF16)
    w_pool_b = w_pool[0].astype(BF16)
    pscale = row(pool_scale[0])
    sink = row(sinks[0])
    fin_w = (ge, be, w_out[0].astype(BF16), row(ln1_g[0]), row(ln1_b[0]), w_mlp_in[0].astype(BF16),
             w_mlp_out[0].astype(BF16), row(ln2_g[0]), row(ln2_b[0]))
    cparams = lambda sem: pltpu.CompilerParams(dimension_semantics=sem, vmem_limit_bytes=VMEM_LIMIT_BYTES)

    bkt_p, bkt_c, bkt_n = _bucket_tables(tokens)
    srows = N_HEADS * tokens * SEQ_GROUP
    bias_p, bias_c, bias_n, meta_proj = pl.pallas_call(
        _prologue_kernel,
        in_specs=[_smem_spec()] + [pl.BlockSpec(memory_space=pltpu.VMEM)] * 7,
        out_specs=[pl.BlockSpec(memory_space=pltpu.VMEM)] * 4,
        out_shape=[jax.ShapeDtypeStruct((2, N_HEADS, BLOCK, 2 * BLOCK), F32),
                   jax.ShapeDtypeStruct((srows, WINDOW), F32),
                   jax.ShapeDtypeStruct((srows, LANES), F32),
                   jax.ShapeDtypeStruct((N_META, d_in), F32)],
        compiler_params=pltpu.CompilerParams(vmem_limit_bytes=VMEM_LIMIT_BYTES),
        name="prologue",
    )(rel_table.astype(F32), jnp.asarray(bkt_p), jnp.asarray(bkt_c), jnp.asarray(bkt_n),
      meta_tokens.astype(F32), ge, be, w_in_b)

    to_kd_pos = lambda c: jnp.swapaxes(c[0].reshape(n_seq, WINDOW, D_KV), 1, 2)
    mix_s, nk_s, nv_s, np_s = _sample_front_call(
        sink, x_sample, to_kd_pos(cache_win_k), to_kd_pos(cache_win_v), jnp.swapaxes(state_pool[0], 0, 1),
        ge, be, w_in_b, w_pool_b, pscale, bias_c, bias_n)
    xs = x_sample

    tile = PROMPT_TILE
    tiles_per_seq = seq // tile
    n_tiles = batch * tiles_per_seq
    xp = x_prompt.reshape(batch * seq, d_model)
    front_tile = lambda s: jnp.minimum(s, n_tiles - 1)
    finish_tile = lambda s: jnp.maximum(s - 1, 0)
    seq_of = lambda s: front_tile(s) // tiles_per_seq
    consts = (ge, be, w_in_b, w_pool_b, pscale, bias_p, meta_proj) + fin_w[2:]
    y_p, y_s, nk_p, nv_p, np_p = pl.pallas_call(
        functools.partial(_main_kernel, tokens, tiles_per_seq, n_tiles),
        grid=(n_tiles + 1,),
        in_specs=[_smem_spec(),
                  pl.BlockSpec((tile, d_model), lambda s: (front_tile(s), 0)),
                  _const_spec(xs.shape), _const_spec(mix_s.shape)] + [_const_spec(c.shape) for c in consts],
        out_specs=[pl.BlockSpec((tile, d_model), lambda s: (finish_tile(s), 0)),
                   pl.BlockSpec(xs.shape, lambda s: (0, 0, 0)),
                   pl.BlockSpec((1, BLOCK, D_KV), lambda s: (seq_of(s), 0, 0)),
                   pl.BlockSpec((1, BLOCK, D_KV), lambda s: (seq_of(s), 0, 0)),
                   pl.BlockSpec((1, HALO, D_POOL), lambda s: (seq_of(s), 0, 0))],
        out_shape=[jax.ShapeDtypeStruct((batch * seq, d_model), F32),
                   jax.ShapeDtypeStruct(xs.shape, F32),
                   jax.ShapeDtypeStruct((batch, BLOCK, D_KV), F32),
                   jax.ShapeDtypeStruct((batch, BLOCK, D_KV), F32),
                   jax.ShapeDtypeStruct((batch, HALO, D_POOL), F32)],
        scratch_shapes=[pltpu.VMEM((tile, d_model), BF16),
                        pltpu.VMEM((4, tile + BLOCK, LANES), BF16),
                        pltpu.VMEM((4, tile + BLOCK, LANES), BF16),
                        pltpu.VMEM((tile + HALO, D_POOL), F32),
                        pltpu.VMEM((tile, D_ATTN), BF16),
                        pltpu.VMEM((tile, d_model), F32),
                        pltpu.VMEM((tile, d_model), BF16),
                        pltpu.VMEM((2, tile // 2, w_mlp_in.shape[2]), BF16)],
        compiler_params=cparams(("arbitrary",)),
        name="main",
    )(sink, xp, xs, mix_s, *consts)

    kv_shape = (DEPTH, -1, WINDOW, N_KV_HEADS, HEAD_DIM)
    from_kd_pos = lambda c: jnp.swapaxes(c, 1, 2).reshape(kv_shape)
    return (y_p.reshape(batch, seq, d_model), y_s,
            from_kd_pos(nk_p), from_kd_pos(nv_p),
            np_p[:, HALO - POOL_BUF:, :].reshape(DEPTH, batch, POOL_BUF, D_POOL),
            from_kd_pos(nk_s), from_kd_pos(nv_s), jnp.swapaxes(np_s, 0, 1)[None])
```

```python
import functools
import math

import jax
import jax.numpy as jnp
import numpy as np
from jax import lax
from jax.experimental import pallas as pl
from jax.experimental.pallas import tpu as pltpu

N_META = 16
POOL_WINDOWS = (2, 4, 8, 16)
POOL_GROUP_DIM = 128
D_POOL = len(POOL_WINDOWS) * POOL_GROUP_DIM
POOL_BUF = max(POOL_WINDOWS) - 1
N_HEADS = 8
HEAD_DIM = 64
D_ATTN = N_HEADS * HEAD_DIM
N_KV_HEADS = 2
D_KV = N_KV_HEADS * HEAD_DIM
WINDOW = 128
BLOCK = 128
REL_BUCKETS = 32
REL_MAX_DIST = 128
PAST_LEN = 8192
DEPTH = 1
ALPHA = (2.0 * DEPTH) ** 0.25
LN_EPS = 1e-5
Q_SCALE = HEAD_DIM ** -0.5

LANES = 128
SUBLANES = 8
VMEM_LIMIT_BYTES = 56 * 1024 * 1024

PROMPT_TILE = 512
IN_ROWS = 256
UP_CHUNK = 512
DOWN_CHUNK = 256
SAMPLE_SEQS = 32
SEQ_GROUP = SUBLANES

F32 = jnp.float32
BF16 = jnp.bfloat16
NEG_INF = float("-inf")


def _rel_bucket(dist):
    n = np.maximum(dist, 0)
    max_exact = REL_BUCKETS // 2
    nf = np.maximum(n, 1).astype(np.float64)
    large = max_exact + (np.log(nf / max_exact) / math.log(REL_MAX_DIST / max_exact)
                         * (REL_BUCKETS - max_exact)).astype(np.int32)
    large = np.minimum(large, REL_BUCKETS - 1)
    return np.where(n < max_exact, n, large).astype(np.int32)


def _bucket_tables(tokens):
    prompt = np.tile(_rel_bucket((BLOCK - np.arange(BLOCK)) % BLOCK)[None, :], (SUBLANES, 1)).astype(np.int32)
    t = np.repeat(np.arange(tokens), SEQ_GROUP)
    s = np.tile(np.arange(SEQ_GROUP), tokens)
    dist = (t[:, None] + WINDOW) - np.arange(WINDOW)[None, :]
    ok = (dist >= 0) & (dist < WINDOW)
    cache = np.where(ok, _rel_bucket(dist), -1).astype(np.int32)
    dist = t[:, None] - t[None, :]
    ok = (dist >= 0) & (s[:, None] == s[None, :])
    new = np.full((tokens * SEQ_GROUP, LANES), -1, np.int32)
    new[:, :tokens * SEQ_GROUP] = np.where(ok, _rel_bucket(dist), -1)
    return prompt, cache, new


def _layer_norm(x, g, b):
    mu = jnp.mean(x, axis=-1, keepdims=True)
    xc = x - mu
    var = jnp.mean(xc * xc, axis=-1, keepdims=True)
    return xc * lax.rsqrt(var + LN_EPS) * g + b


def _dot(a, b):
    return jnp.dot(a, b, preferred_element_type=F32)


def _dot_nt(a, b):
    return lax.dot_general(a, b, (((1,), (1,)), ((), ())), preferred_element_type=F32)


def _kv_lane_variants(x, low):
    xr = pltpu.roll(x, HEAD_DIM, axis=1)
    zero = jnp.zeros_like(x)
    return (jnp.where(low, x, zero).astype(BF16), jnp.where(low, zero, xr).astype(BF16),
            jnp.where(low, xr, zero).astype(BF16), jnp.where(low, zero, x).astype(BF16))


def _prologue_kernel(tab_ref, bkt_p_ref, bkt_c_ref, bkt_n_ref, meta_ref, g_ref, b_ref, wf_in_ref, wf_pool_ref,
                     bias_p_ref, bias_c_ref, bias_n_ref, meta_proj_ref, w_in_ref, w_pool_ref):
    w_in_ref[...] = wf_in_ref[...].astype(BF16)
    w_pool_ref[...] = wf_pool_ref[...].astype(BF16)
    def lookup(bucket, h):
        def body(i, acc):
            return jnp.where(bucket == i, tab_ref[i, h], acc)
        return lax.fori_loop(0, REL_BUCKETS, body, jnp.full(bucket.shape, NEG_INF, F32))

    rows = bkt_c_ref.shape[0]
    q = lax.broadcasted_iota(jnp.int32, (BLOCK, BLOCK), 0)
    k = lax.broadcasted_iota(jnp.int32, (BLOCK, BLOCK), 1)
    for h in range(N_HEADS):
        per_dist = lookup(bkt_p_ref[...], h)
        toeplitz = pltpu.roll(jnp.concatenate([per_dist] * (BLOCK // SUBLANES), axis=0), 0, 1,
                              stride=1, stride_axis=0)
        cur = jnp.where(k <= q, toeplitz, NEG_INF)
        for first in range(2):
            prev_ok = (k > q) & (k >= BLOCK - N_META) if first == 0 else k > q
            bias_p_ref[first, h, :, 0:BLOCK] = jnp.where(prev_ok, toeplitz, NEG_INF)
            bias_p_ref[first, h, :, BLOCK:] = cur
        bias_c_ref[h * rows:(h + 1) * rows, :] = lookup(bkt_c_ref[...], h)
        bias_n_ref[h * rows:(h + 1) * rows, :] = lookup(bkt_n_ref[...], h)
    hm = _layer_norm(meta_ref[...], g_ref[...], b_ref[...])
    meta_proj_ref[...] = _dot(hm.astype(BF16), w_in_ref[...])


HALO = 2 * SUBLANES


def _carry_init(meta_proj_ref, kvar, vvar, uext):
    low = lax.broadcasted_iota(jnp.int32, (1, LANES), 1) < HEAD_DIM
    pad = jnp.zeros((BLOCK - N_META, LANES), BF16)
    kq = _kv_lane_variants(meta_proj_ref[:, D_POOL + D_ATTN:D_POOL + D_ATTN + D_KV], low)
    vq = _kv_lane_variants(meta_proj_ref[:, D_POOL + D_ATTN + D_KV:], low)
    for i in range(4):
        kvar[i, 0:BLOCK, :] = jnp.concatenate([pad, kq[i]], axis=0)
        vvar[i, 0:BLOCK, :] = jnp.concatenate([pad, vq[i]], axis=0)
    uext[0:HALO, :] = meta_proj_ref[:, 0:D_POOL]


def _front_pieces(first_tile, x_ref, sink_ref, g_ref, b_ref, w_in_ref, w_pool_ref, pscale_ref, bias_ref,
                  mix_ref, nk_ref, nv_ref, np_ref, kvar, vvar, uext, qbuf):
    tile = x_ref.shape[0]
    halo = HALO
    low = lax.broadcasted_iota(jnp.int32, (1, LANES), 1) < HEAD_DIM

    for j in range(tile // IN_ROWS):
        rows = slice(j * IN_ROWS, (j + 1) * IN_ROWS)
        h = _layer_norm(x_ref[rows, :], g_ref[...], b_ref[...])
        proj = _dot(h.astype(BF16), w_in_ref[...])
        u = proj[:, 0:D_POOL]
        k = proj[:, D_POOL + D_ATTN:D_POOL + D_ATTN + D_KV]
        v = proj[:, D_POOL + D_ATTN + D_KV:]
        qbuf[rows, :] = (proj[:, D_POOL:D_POOL + D_ATTN] * Q_SCALE).astype(BF16)
        uext[halo + rows.start:halo + rows.stop, :] = u
        kq = _kv_lane_variants(k, low)
        vq = _kv_lane_variants(v, low)
        for i in range(4):
            kvar[i, BLOCK + rows.start:BLOCK + rows.stop, :] = kq[i]
            vvar[i, BLOCK + rows.start:BLOCK + rows.stop, :] = vq[i]
        if rows.stop == tile:
            nk_ref[...] = k[IN_ROWS - BLOCK:, :].T
            nv_ref[...] = v[IN_ROWS - BLOCK:, :].T
            np_ref[...] = u[IN_ROWS - halo:, :]
        yield

    for g, w in enumerate(POOL_WINDOWS):
        cols = slice(g * POOL_GROUP_DIM, (g + 1) * POOL_GROUP_DIM)
        ug = uext[halo:halo + tile, cols]
        s = ug
        for i in range(1, w):
            s = s + uext[halo - i:halo - i + tile, cols]
        p = s * (1.0 / w) - ug
        mix_ref[:, cols] = (_dot(p.astype(BF16), w_pool_ref[g]) * pscale_ref[:, cols]).astype(BF16)
        yield

    for j in range(tile // BLOCK):
        rows = slice(j * BLOCK, (j + 1) * BLOCK)
        keys = slice(j * BLOCK, (j + 2) * BLOCK)
        sel = jnp.where(first_tile, 0, 1) if j == 0 else 1
        for kv in range(N_KV_HEADS):
            tiles = (2 * kv, 2 * kv + 1)
            lhs = jnp.concatenate([qbuf[rows, p * LANES:(p + 1) * LANES] for p in tiles], axis=0)
            acc = None
            for half in range(2):
                heads = [2 * p + half for p in tiles]
                s = _dot_nt(lhs, kvar[2 * kv + half, keys, :])
                s = s + jnp.concatenate([bias_ref[sel, hd] for hd in heads], axis=0)
                sink = jnp.concatenate([jnp.full((BLOCK, 1), sink_ref[0, hd], F32) for hd in heads], axis=0)
                m = jnp.maximum(jnp.max(s, axis=-1, keepdims=True), sink)
                e = jnp.exp(s - m)
                den = jnp.sum(e, axis=-1, keepdims=True) + jnp.exp(sink - m)
                o = _dot(e.astype(BF16), vvar[2 * kv + half, keys, :]) * (1.0 / den)
                acc = o if acc is None else acc + o
            for n, p in enumerate(tiles):
                mix_ref[rows, D_POOL + p * LANES:D_POOL + (p + 1) * LANES] = (
                    acc[n * BLOCK:(n + 1) * BLOCK].astype(BF16))
            yield

    for i in range(4):
        kvar[i, 0:BLOCK, :] = kvar[i, tile:tile + BLOCK, :]
        vvar[i, 0:BLOCK, :] = vvar[i, tile:tile + BLOCK, :]
    uext[0:halo, :] = uext[tile:tile + halo, :]


def _sample_front_kernel(sink_ref, x_ref, ck_ref, cv_ref, st_ref, wf_out_ref, wf1_ref, wf2_ref,
                         g_ref, b_ref, w_in_ref, w_pool_ref, pscale_ref, bias_c_ref, bias_n_ref,
                         mix_ref, nk_ref, nv_ref, np_ref, wb_out_ref, wb1_ref, wb2_ref, o_s):
    for src, dst in ((wf_out_ref, wb_out_ref), (wf1_ref, wb1_ref), (wf2_ref, wb2_ref)):
        dst[...] = src[...].astype(BF16)
    ns, tokens, d_model = x_ref.shape
    lane = lax.broadcasted_iota(jnp.int32, (1, LANES), 1)
    low = lane < HEAD_DIM

    x = jnp.concatenate([x_ref[:, t, :] for t in range(tokens)], axis=0)
    h = _layer_norm(x, g_ref[...], b_ref[...])
    proj = _dot(h.astype(BF16), w_in_ref[...])
    u = proj[:, 0:D_POOL]
    q = proj[:, D_POOL:D_POOL + D_ATTN] * Q_SCALE
    k = proj[:, D_POOL + D_ATTN:D_POOL + D_ATTN + D_KV]
    v = proj[:, D_POOL + D_ATTN + D_KV:]

    ext = [st_ref[r] for r in range(POOL_BUF)]
    ext += [u[t * ns:(t + 1) * ns] for t in range(tokens)]
    ps = []
    for t in range(tokens):
        idx = POOL_BUF + t
        pos = PAST_LEN - POOL_BUF + idx
        parts = []
        for g, w in enumerate(POOL_WINDOWS):
            cols = slice(g * POOL_GROUP_DIM, (g + 1) * POOL_GROUP_DIM)
            lo = max(idx + 1 - w, 0)
            s = ext[idx][:, cols]
            for r in range(idx - 1, lo - 1, -1):
                s = s + ext[r][:, cols]
            parts.append(s * (1.0 / min(w, pos + 1)) - ext[idx][:, cols])
        ps.append(jnp.concatenate(parts, axis=1))
    p = jnp.concatenate(ps, axis=0).astype(BF16)
    zs = [_dot(p[:, g * POOL_GROUP_DIM:(g + 1) * POOL_GROUP_DIM], w_pool_ref[g])
          for g in range(len(POOL_WINDOWS))]
    z = (jnp.concatenate(zs, axis=1) * pscale_ref[...]).astype(BF16)
    for r in range(POOL_BUF):
        np_ref[r] = ext[r + tokens]

    qh = []
    for hd in range(N_HEADS):
        tl = q[:, (hd // 2) * LANES:(hd // 2 + 1) * LANES]
        kv = hd // (N_HEADS // N_KV_HEADS)
        want_low = kv == 0
        is_low = hd % 2 == 0
        src = tl if want_low == is_low else pltpu.roll(tl, HEAD_DIM, axis=1)
        qh.append(jnp.where(low, src, 0.0) if want_low else jnp.where(low, 0.0, src))

    gr = SEQ_GROUP
    rows_per_head = tokens * gr
    sub = lax.broadcasted_iota(jnp.int32, (N_HEADS * rows_per_head, 1), 0) % gr
    sink = jnp.concatenate([jnp.full((rows_per_head, 1), sink_ref[0, hd], F32) for hd in range(N_HEADS)],
                           axis=0)
    zpad = jnp.zeros((LANES - rows_per_head, LANES), F32)
    ztop = jnp.zeros((WINDOW - SUBLANES, LANES), F32)
    sub8 = lax.broadcasted_iota(jnp.int32, (SUBLANES, 1), 0)
    keep = WINDOW - tokens

    def appended(old_ref, news, j):
        tail = jnp.zeros((SUBLANES, LANES), F32)
        for t in range(tokens):
            row = SUBLANES - tokens + t
            tail = jnp.where(sub8 == row, pltpu.roll(news[t], (row - j) % SUBLANES, axis=0), tail)
        cols = jnp.concatenate([ztop, tail], axis=0).T
        return jnp.where(lane >= keep, cols, pltpu.roll(old_ref[...], keep, axis=1))

    for base in range(0, ns, gr):
        k_t = [k[t * ns + base:t * ns + base + gr] for t in range(tokens)]
        v_t = [v[t * ns + base:t * ns + base + gr] for t in range(tokens)]
        for s in range(gr):
            nk_ref[base + s] = appended(ck_ref.at[base + s], k_t, s)
            nv_ref[base + s] = appended(cv_ref.at[base + s], v_t, s)

    for gi in range(ns // gr):
        base = gi * gr
        lhs = jnp.concatenate([qh[hd][t * ns + base:t * ns + base + gr]
                               for hd in range(N_HEADS) for t in range(tokens)], axis=0).astype(BF16)
        sc = None
        for s in range(gr):
            s_one = _dot(lhs, ck_ref[base + s].astype(BF16))
            sc = s_one if sc is None else jnp.where(sub == s, s_one, sc)
        sc = sc + bias_c_ref[...]
        k_t = [k[t * ns + base:t * ns + base + gr] for t in range(tokens)]
        v_t = [v[t * ns + base:t * ns + base + gr] for t in range(tokens)]
        k_new = jnp.concatenate(k_t + [zpad], axis=0)
        v_new = jnp.concatenate(v_t + [zpad], axis=0)
        sn = _dot_nt(lhs, k_new.astype(BF16)) + bias_n_ref[...]
        m = jnp.maximum(jnp.maximum(jnp.max(sc, axis=-1, keepdims=True),
                                    jnp.max(sn, axis=-1, keepdims=True)), sink)
        ec = jnp.exp(sc - m)
        en = jnp.exp(sn - m)
        den = (jnp.sum(ec, axis=-1, keepdims=True) + jnp.sum(en, axis=-1, keepdims=True)
               + jnp.exp(sink - m))
        o = _dot(en.astype(BF16), v_new.astype(BF16))
        for s in range(gr):
            o = o + _dot_nt(jnp.where(sub == s, ec, 0.0).astype(BF16), cv_ref[base + s].astype(BF16))
        o = o * (1.0 / den)
        for t in range(tokens):
            for p in range(N_HEADS // 2):
                a = o[(2 * p * tokens + t) * gr:(2 * p * tokens + t + 1) * gr]
                c = o[((2 * p + 1) * tokens + t) * gr:((2 * p + 1) * tokens + t + 1) * gr]
                if p < N_HEADS // 4:
                    c = pltpu.roll(c, HEAD_DIM, axis=1)
                else:
                    a = pltpu.roll(a, HEAD_DIM, axis=1)
                o_s[t * ns + base:t * ns + base + gr, p * LANES:(p + 1) * LANES] = jnp.where(low, a, c)

    for t in range(tokens):
        mix_ref[:, t * d_model:t * d_model + D_POOL] = z[t * ns:(t + 1) * ns]
        mix_ref[:, t * d_model + D_POOL:(t + 1) * d_model] = o_s[t * ns:(t + 1) * ns, :].astype(BF16)


def _pre_pieces(rows, x, mix, ge_ref, be_ref, w_out_ref, g1_ref, b1_ref, h1_s, hb_s):
    h1_s[rows, :] = ALPHA * _layer_norm(x(), ge_ref[...], be_ref[...])
    yield
    h1_s[rows, :] = h1_s[rows, :] + _dot(mix(), w_out_ref[...])
    yield
    h = _layer_norm(h1_s[rows, :], g1_ref[...], b1_ref[...])
    h1_s[rows, :] = h
    hb_s[rows, :] = h.astype(BF16)
    yield


def _mlp_pieces(rows, a_s, h1_s, hb_s, w1_ref, w2_ref, g2_ref, b2_ref, y_ref):
    for c in range(w1_ref.shape[1] // UP_CHUNK):
        cols = slice(c * UP_CHUNK, (c + 1) * UP_CHUNK)
        a = _dot(hb_s[rows, :], w1_ref[:, cols])
        a_s[:, cols] = jnp.square(jnp.maximum(a, 0.0)).astype(BF16)
        if c == 0:
            y_ref[rows, :] = ALPHA * h1_s[rows, :]
        yield
    for n in range(y_ref.shape[1] // DOWN_CHUNK):
        cols = slice(n * DOWN_CHUNK, (n + 1) * DOWN_CHUNK)
        y_ref[rows, cols] = y_ref[rows, cols] + _dot(a_s[...], w2_ref[:, cols])
        yield
    y_ref[rows, :] = _layer_norm(y_ref[rows, :], g2_ref[...], b2_ref[...])
    yield


def _weave(main, n_main, sides):
    order = []
    for i in range(n_main):
        order.append(main)
        order += sides[i * len(sides) // n_main:(i + 1) * len(sides) // n_main]
    return order


def _main_kernel(tokens, tiles_per_seq, n_tiles,
                 sink_ref, x_ref, xs_ref, mixs_ref, ge_ref, be_ref, w_in_ref, w_pool_ref,
                 pscale_ref, bias_ref, meta_proj_ref, w_out_ref, g1_ref, b1_ref, w1_ref, w2_ref,
                 g2_ref, b2_ref,
                 y_ref, ys_ref, nk_ref, nv_ref, np_ref,
                 mixbuf, kvar, vvar, uext, qbuf, h1_s, hb_s, a_s):
    s = pl.program_id(0)
    is_first = s == 0
    first_tile = jnp.minimum(s, n_tiles - 1) % tiles_per_seq == 0
    tile, d_model = x_ref.shape
    ns = xs_ref.shape[0]
    half = tile // 2
    halves = (slice(0, half), slice(half, tile))
    pre_w = (ge_ref, be_ref, w_out_ref, g1_ref, b1_ref, h1_s, hb_s)

    @pl.when(is_first)
    def _():
        def toks(rows):
            return range(rows.start // ns, rows.stop // ns)
        for rows in halves:
            x = lambda: jnp.concatenate([xs_ref[:, t, :] for t in toks(rows)], axis=0)
            mix = lambda: jnp.concatenate([mixs_ref[:, t * d_model:(t + 1) * d_model] for t in toks(rows)],
                                          axis=0)
            for _ in _pre_pieces(rows, x, mix, *pre_w):
                pass

    @pl.when(first_tile)
    def _():
        _carry_init(meta_proj_ref, kvar, vvar, uext)

    mlp = [_mlp_pieces(rows, a_s.at[i], h1_s, hb_s, w1_ref, w2_ref, g2_ref, b2_ref, y_ref)
           for i, rows in enumerate(halves)]
    front = _front_pieces(first_tile, x_ref, sink_ref, ge_ref, be_ref, w_in_ref, w_pool_ref, pscale_ref,
                          bias_ref, mixbuf, nk_ref.at[0], nv_ref.at[0], np_ref.at[0], kvar, vvar, uext, qbuf)
    pre = [_pre_pieces(rows, lambda rows=rows: x_ref[rows, :], lambda rows=rows: mixbuf[rows, :], *pre_w)
           for rows in halves]
    a, b, f, pa, pb = mlp[0], mlp[1], front, pre[0], pre[1]
    n_up = w1_ref.shape[1] // UP_CHUNK
    n_down = d_model // DOWN_CHUNK
    order = (
        _weave(a, n_up, [f] * 4)
        + _weave(a, n_down, [f] * 5)
        + _weave(b, n_up, [a] + [f] * 5)
        + _weave(b, n_down, [f, pa, pa, pb, pb, pa, pb]) + [b])
    for g in order:
        next(g, None)
    for g in (a, b, f, pa, pb):
        for _ in g:
            raise AssertionError("piece left unscheduled")

    @pl.when(is_first)
    def _():
        for t in range(tokens):
            ys_ref[:, t, :] = y_ref[t * ns:(t + 1) * ns, :]


def _const_spec(shape):
    return pl.BlockSpec(shape, lambda *_: (0,) * len(shape), pipeline_mode=pl.Buffered(1))


def _smem_spec():
    return pl.BlockSpec(memory_space=pltpu.SMEM)


def _prologue_call(tokens, rel_table, meta_tokens, ge, be, w_in, w_pool):
    bkt_p, bkt_c, bkt_n = _bucket_tables(tokens)
    srows = N_HEADS * tokens * SEQ_GROUP
    return pl.pallas_call(
        _prologue_kernel,
        in_specs=[_smem_spec()] + [pl.BlockSpec(memory_space=pltpu.VMEM)] * 8,
        out_specs=[pl.BlockSpec(memory_space=pltpu.VMEM)] * 6,
        out_shape=[jax.ShapeDtypeStruct((2, N_HEADS, BLOCK, 2 * BLOCK), F32),
                   jax.ShapeDtypeStruct((srows, WINDOW), F32),
                   jax.ShapeDtypeStruct((srows, LANES), F32),
                   jax.ShapeDtypeStruct((meta_tokens.shape[0], w_in.shape[1]), F32),
                   jax.ShapeDtypeStruct(w_in.shape, BF16),
                   jax.ShapeDtypeStruct(w_pool.shape, BF16)],
        compiler_params=pltpu.CompilerParams(vmem_limit_bytes=VMEM_LIMIT_BYTES),
        name="prologue",
    )(rel_table, jnp.asarray(bkt_p), jnp.asarray(bkt_c), jnp.asarray(bkt_n), meta_tokens, ge, be, w_in, w_pool)


def _sample_front_call(sink, xs, ck, cv, st, weights, *consts):
    n_seq, tokens, d_model = xs.shape
    steps = n_seq // SAMPLE_SEQS
    seq_spec = lambda *tail: pl.BlockSpec((SAMPLE_SEQS,) + tail, lambda i: (i,) + (0,) * len(tail))
    state_spec = pl.BlockSpec((POOL_BUF, SAMPLE_SEQS, D_POOL), lambda i: (0, i, 0))
    slab_specs = [pl.BlockSpec((w.shape[0] // steps, w.shape[1]), lambda i: (i, 0)) for w in weights]
    return pl.pallas_call(
        _sample_front_kernel,
        grid=(steps,),
        in_specs=[_smem_spec(), seq_spec(tokens, d_model), seq_spec(D_KV, WINDOW), seq_spec(D_KV, WINDOW),
                  state_spec] + slab_specs + [_const_spec(c.shape) for c in consts],
        out_specs=[seq_spec(tokens * d_model), seq_spec(D_KV, WINDOW), seq_spec(D_KV, WINDOW), state_spec]
        + slab_specs,
        out_shape=[jax.ShapeDtypeStruct((n_seq, tokens * d_model), BF16),
                   jax.ShapeDtypeStruct((n_seq, D_KV, WINDOW), F32),
                   jax.ShapeDtypeStruct((n_seq, D_KV, WINDOW), F32),
                   jax.ShapeDtypeStruct((POOL_BUF, n_seq, D_POOL), F32)]
        + [jax.ShapeDtypeStruct(w.shape, BF16) for w in weights],
        scratch_shapes=[pltpu.VMEM((tokens * SAMPLE_SEQS, D_ATTN), F32)],
        compiler_params=pltpu.CompilerParams(dimension_semantics=("arbitrary",),
                                             vmem_limit_bytes=VMEM_LIMIT_BYTES),
        name="sample_front",
    )(sink, xs, ck, cv, st, *weights, *consts)


def kernel(x_prompt, x_sample, cache_win_k, cache_win_v, state_pool, meta_tokens, ln_emb_g, ln_emb_b,
           rel_table, w_in, w_pool, pool_scale, sinks, w_out, ln1_g, ln1_b, w_mlp_in, w_mlp_out,
           ln2_g, ln2_b):
    batch, seq, d_model = x_prompt.shape
    n_seq, tokens, _ = x_sample.shape
    assert w_in.shape[0] == DEPTH and d_model == D_POOL + D_ATTN
    assert seq % PROMPT_TILE == 0 and n_seq % SAMPLE_SEQS == 0 and tokens * n_seq == PROMPT_TILE
    assert cache_win_k.shape[2] == WINDOW and state_pool.shape[2] == POOL_BUF and tokens <= POOL_BUF
    d_in = w_in.shape[2]

    row = lambda a: a.reshape(1, -1).astype(F32)
    ge, be = row(ln_emb_g), row(ln_emb_b)
    pscale = row(pool_scale[0])
    sink = row(sinks[0])
    cparams = lambda sem: pltpu.CompilerParams(dimension_semantics=sem, vmem_limit_bytes=VMEM_LIMIT_BYTES)

    bias_p, bias_c, bias_n, meta_proj, w_in_b, w_pool_b = _prologue_call(
        tokens, rel_table.astype(F32), meta_tokens.astype(F32), ge, be, w_in[0], w_pool[0])

    to_kd_pos = lambda c: jnp.swapaxes(c[0].reshape(n_seq, WINDOW, D_KV), 1, 2)
    mix_s, nk_s, nv_s, np_s, w_out_b, w1_b, w2_b = _sample_front_call(
        sink, x_sample, to_kd_pos(cache_win_k), to_kd_pos(cache_win_v), jnp.swapaxes(state_pool[0], 0, 1),
        (w_out[0], w_mlp_in[0], w_mlp_out[0]), ge, be, w_in_b, w_pool_b, pscale, bias_c, bias_n)
    fin_w = (ge, be, w_out_b, row(ln1_g[0]), row(ln1_b[0]), w1_b, w2_b, row(ln2_g[0]), row(ln2_b[0]))
    xs = x_sample

    tile = PROMPT_TILE
    tiles_per_seq = seq // tile
    n_tiles = batch * tiles_per_seq
    xp = x_prompt.reshape(batch * seq, d_model)
    front_tile = lambda s: jnp.minimum(s, n_tiles - 1)
    finish_tile = lambda s: jnp.maximum(s - 1, 0)
    seq_of = lambda s: front_tile(s) // tiles_per_seq
    consts = (ge, be, w_in_b, w_pool_b, pscale, bias_p, meta_proj) + fin_w[2:]
    y_p, y_s, nk_p, nv_p, np_p = pl.pallas_call(
        functools.partial(_main_kernel, tokens, tiles_per_seq, n_tiles),
        grid=(n_tiles + 1,),
        in_specs=[_smem_spec(),
                  pl.BlockSpec((tile, d_model), lambda s: (front_tile(s), 0)),
                  _const_spec(xs.shape), _const_spec(mix_s.shape)] + [_const_spec(c.shape) for c in consts],
        out_specs=[pl.BlockSpec((tile, d_model), lambda s: (finish_tile(s), 0)),
                   pl.BlockSpec(xs.shape, lambda s: (0, 0, 0)),
                   pl.BlockSpec((1, BLOCK, D_KV), lambda s: (seq_of(s), 0, 0)),
                   pl.BlockSpec((1, BLOCK, D_KV), lambda s: (seq_of(s), 0, 0)),
                   pl.BlockSpec((1, HALO, D_POOL), lambda s: (seq_of(s), 0, 0))],
        out_shape=[jax.ShapeDtypeStruct((batch * seq, d_model), F32),
                   jax.ShapeDtypeStruct(xs.shape, F32),
                   jax.ShapeDtypeStruct((batch, BLOCK, D_KV), F32),
                   jax.ShapeDtypeStruct((batch, BLOCK, D_KV), F32),
                   jax.ShapeDtypeStruct((batch, HALO, D_POOL), F32)],
        scratch_shapes=[pltpu.VMEM((tile, d_model), BF16),
                        pltpu.VMEM((4, tile + BLOCK, LANES), BF16),
                        pltpu.VMEM((4, tile + BLOCK, LANES), BF16),
                        pltpu.VMEM((tile + HALO, D_POOL), F32),
                        pltpu.VMEM((tile, D_ATTN), BF16),
                        pltpu.VMEM((tile, d_model), F32),
                        pltpu.VMEM((tile, d_model), BF16),
                        pltpu.VMEM((2, tile // 2, w_mlp_in.shape[2]), BF16)],
        compiler_params=cparams(("arbitrary",)),
        name="main",
    )(sink, xp, xs, mix_s, *consts)

    kv_shape = (DEPTH, -1, WINDOW, N_KV_HEADS, HEAD_DIM)
    from_kd_pos = lambda c: jnp.swapaxes(c, 1, 2).reshape(kv_shape)
    return (y_p.reshape(batch, seq, d_model), y_s,
            from_kd_pos(nk_p), from_kd_pos(nv_p),
            np_p[:, HALO - POOL_BUF:, :].reshape(DEPTH, batch, POOL_BUF, D_POOL),
            from_kd_pos(nk_s), from_kd_pos(nv_s), jnp.swapaxes(np_s, 0, 1)[None])
```

```python
import functools
import math

import jax
import jax.numpy as jnp
import numpy as np
from jax import lax
from jax.experimental import pallas as pl
from jax.experimental.pallas import tpu as pltpu

N_META = 16
POOL_WINDOWS = (2, 4, 8, 16)
POOL_GROUP_DIM = 128
D_POOL = len(POOL_WINDOWS) * POOL_GROUP_DIM
POOL_BUF = max(POOL_WINDOWS) - 1
N_HEADS = 8
HEAD_DIM = 64
D_ATTN = N_HEADS * HEAD_DIM
N_KV_HEADS = 2
D_KV = N_KV_HEADS * HEAD_DIM
WINDOW = 128
BLOCK = 128
REL_BUCKETS = 32
REL_MAX_DIST = 128
PAST_LEN = 8192
DEPTH = 1
ALPHA = (2.0 * DEPTH) ** 0.25
LN_EPS = 1e-5
Q_SCALE = HEAD_DIM ** -0.5

LANES = 128
SUBLANES = 8
VMEM_LIMIT_BYTES = 60 * 1024 * 1024

PROMPT_TILE = 512
IN_ROWS = 256
UP_CHUNK = 512
DOWN_CHUNK = 256
SAMPLE_SEQS = 32
SEQ_GROUP = SUBLANES

F32 = jnp.float32
BF16 = jnp.bfloat16
NEG_INF = float("-inf")


def _rel_bucket(dist):
    n = np.maximum(dist, 0)
    max_exact = REL_BUCKETS // 2
    nf = np.maximum(n, 1).astype(np.float64)
    large = max_exact + (np.log(nf / max_exact) / math.log(REL_MAX_DIST / max_exact)
                         * (REL_BUCKETS - max_exact)).astype(np.int32)
    large = np.minimum(large, REL_BUCKETS - 1)
    return np.where(n < max_exact, n, large).astype(np.int32)


def _bucket_tables(tokens):
    prompt = np.tile(_rel_bucket((BLOCK - np.arange(BLOCK)) % BLOCK)[None, :], (SUBLANES, 1)).astype(np.int32)
    t = np.repeat(np.arange(tokens), SEQ_GROUP)
    s = np.tile(np.arange(SEQ_GROUP), tokens)
    dist = (t[:, None] + WINDOW) - np.arange(WINDOW)[None, :]
    ok = (dist >= 0) & (dist < WINDOW)
    cache = np.where(ok, _rel_bucket(dist), -1).astype(np.int32)
    dist = t[:, None] - t[None, :]
    ok = (dist >= 0) & (s[:, None] == s[None, :])
    new = np.full((tokens * SEQ_GROUP, LANES), -1, np.int32)
    new[:, :tokens * SEQ_GROUP] = np.where(ok, _rel_bucket(dist), -1)
    return prompt, cache, new


def _pool_band():
    band = np.zeros((len(POOL_WINDOWS), BLOCK, HALO + BLOCK), np.float32)
    for g, w in enumerate(POOL_WINDOWS):
        for r in range(BLOCK):
            band[g, r, HALO + r - w + 1:HALO + r + 1] = 1.0 / w
            band[g, r, HALO + r] -= 1.0
    return band


def _split_hi_lo(x):
    hi = x.astype(BF16)
    return hi, (x - hi.astype(F32)).astype(BF16)


def _layer_norm(x, g, b):
    mu = jnp.mean(x, axis=-1, keepdims=True)
    xc = x - mu
    var = jnp.mean(xc * xc, axis=-1, keepdims=True)
    return xc * lax.rsqrt(var + LN_EPS) * g + b


def _dot(a, b):
    return jnp.dot(a, b, preferred_element_type=F32)


def _dot_nt(a, b):
    return lax.dot_general(a, b, (((1,), (1,)), ((), ())), preferred_element_type=F32)


def _kv_lane_variants(x, low):
    xr = pltpu.roll(x, HEAD_DIM, axis=1)
    zero = jnp.zeros_like(x)
    return (jnp.where(low, x, zero).astype(BF16), jnp.where(low, zero, xr).astype(BF16),
            jnp.where(low, xr, zero).astype(BF16), jnp.where(low, zero, x).astype(BF16))


def _prologue_kernel(tab_ref, bkt_p_ref, bkt_c_ref, bkt_n_ref, meta_ref, g_ref, b_ref, wf_in_ref, wf_pool_ref,
                     bias_p_ref, bias_c_ref, bias_n_ref, meta_proj_ref, w_in_ref, w_pool_ref, w_pool_bd_ref):
    w_in_ref[...] = wf_in_ref[...].astype(BF16)
    w_pool_ref[...] = wf_pool_ref[...].astype(BF16)
    w_pool_bd_ref[...] = jnp.zeros_like(w_pool_bd_ref)
    for g in range(len(POOL_WINDOWS)):
        cols = slice(g * POOL_GROUP_DIM, (g + 1) * POOL_GROUP_DIM)
        w_pool_bd_ref[cols, cols] = w_pool_ref[g]
    def lookup(bucket, h):
        def body(i, acc):
            return jnp.where(bucket == i, tab_ref[i, h], acc)
        return lax.fori_loop(0, REL_BUCKETS, body, jnp.full(bucket.shape, NEG_INF, F32))

    rows = bkt_c_ref.shape[0]
    q = lax.broadcasted_iota(jnp.int32, (BLOCK, BLOCK), 0)
    k = lax.broadcasted_iota(jnp.int32, (BLOCK, BLOCK), 1)
    for h in range(N_HEADS):
        per_dist = lookup(bkt_p_ref[...], h)
        toeplitz = pltpu.roll(jnp.concatenate([per_dist] * (BLOCK // SUBLANES), axis=0), 0, 1,
                              stride=1, stride_axis=0)
        cur = jnp.where(k <= q, toeplitz, NEG_INF)
        for first in range(2):
            prev_ok = (k > q) & (k >= BLOCK - N_META) if first == 0 else k > q
            bias_p_ref[first, h, :, 0:BLOCK] = jnp.where(prev_ok, toeplitz, NEG_INF)
            bias_p_ref[first, h, :, BLOCK:] = cur
        bias_c_ref[h * rows:(h + 1) * rows, :] = lookup(bkt_c_ref[...], h)
        bias_n_ref[h * rows:(h + 1) * rows, :] = lookup(bkt_n_ref[...], h)
    hm = _layer_norm(meta_ref[...], g_ref[...], b_ref[...])
    meta_proj_ref[...] = _dot(hm.astype(BF16), w_in_ref[...])


HALO = 2 * SUBLANES


def _carry_init(meta_proj_ref, kvar, vvar, uext):
    low = lax.broadcasted_iota(jnp.int32, (1, LANES), 1) < HEAD_DIM
    pad = jnp.zeros((BLOCK - N_META, LANES), BF16)
    kq = _kv_lane_variants(meta_proj_ref[:, D_POOL + D_ATTN:D_POOL + D_ATTN + D_KV], low)
    vq = _kv_lane_variants(meta_proj_ref[:, D_POOL + D_ATTN + D_KV:], low)
    for i in range(4):
        kvar[i, 0:BLOCK, :] = jnp.concatenate([pad, kq[i]], axis=0)
        vvar[i, 0:BLOCK, :] = jnp.concatenate([pad, vq[i]], axis=0)
    for i, part in enumerate(_split_hi_lo(meta_proj_ref[:, 0:D_POOL])):
        uext[i, 0:HALO, :] = part


def _front_pieces(first_tile, x_ref, sink_ref, g_ref, b_ref, w_in_ref, band_ref, w_pool_ref, pscale_ref, bias_ref,
                  mix_ref, nk_ref, nv_ref, np_ref, kvar, vvar, uext, qbuf, pbuf):
    tile = x_ref.shape[0]
    halo = HALO
    low = lax.broadcasted_iota(jnp.int32, (1, LANES), 1) < HEAD_DIM

    for j in range(tile // IN_ROWS):
        rows = slice(j * IN_ROWS, (j + 1) * IN_ROWS)
        h = _layer_norm(x_ref[rows, :], g_ref[...], b_ref[...])
        proj = _dot(h.astype(BF16), w_in_ref[...])
        u = proj[:, 0:D_POOL]
        k = proj[:, D_POOL + D_ATTN:D_POOL + D_ATTN + D_KV]
        v = proj[:, D_POOL + D_ATTN + D_KV:]
        qbuf[rows, :] = (proj[:, D_POOL:D_POOL + D_ATTN] * Q_SCALE).astype(BF16)
        for i, part in enumerate(_split_hi_lo(u)):
            uext[i, halo + rows.start:halo + rows.stop, :] = part
        kq = _kv_lane_variants(k, low)
        vq = _kv_lane_variants(v, low)
        for i in range(4):
            kvar[i, BLOCK + rows.start:BLOCK + rows.stop, :] = kq[i]
            vvar[i, BLOCK + rows.start:BLOCK + rows.stop, :] = vq[i]
        if rows.stop == tile:
            nk_ref[...] = k[IN_ROWS - BLOCK:, :].T
            nv_ref[...] = v[IN_ROWS - BLOCK:, :].T
            np_ref[...] = u[IN_ROWS - halo:, :]
        yield

    n_blk = tile // BLOCK
    for g in range(len(POOL_WINDOWS)):
        cols = slice(g * POOL_GROUP_DIM, (g + 1) * POOL_GROUP_DIM)
        p = None
        for i in range(2):
            rhs = jnp.concatenate([uext[i, j * BLOCK:(j + 1) * BLOCK + halo, cols] for j in range(n_blk)], axis=1)
            d = _dot(band_ref[g], rhs)
            p = d if p is None else p + d
        for j in range(n_blk):
            pbuf[j * BLOCK:(j + 1) * BLOCK, cols] = p[:, j * LANES:(j + 1) * LANES].astype(BF16)
        if g == len(POOL_WINDOWS) - 1:
            mix_ref[:, 0:D_POOL] = (_dot(pbuf[...], w_pool_ref[...]) * pscale_ref[...]).astype(BF16)
        yield

    for j in range(tile // BLOCK):
        rows = slice(j * BLOCK, (j + 1) * BLOCK)
        keys = slice(j * BLOCK, (j + 2) * BLOCK)
        sel = jnp.where(first_tile, 0, 1) if j == 0 else 1
        for kv in range(N_KV_HEADS):
            tiles = (2 * kv, 2 * kv + 1)
            lhs = jnp.concatenate([qbuf[rows, p * LANES:(p + 1) * LANES] for p in tiles], axis=0)
            nk = 2 * BLOCK
            s_all = _dot_nt(lhs, jnp.concatenate([kvar[2 * kv, keys, :], kvar[2 * kv + 1, keys, :]], axis=0))
            es, dens = [], []
            for half in range(2):
                heads = [2 * p + half for p in tiles]
                s = s_all[:, half * nk:(half + 1) * nk]
                s = s + jnp.concatenate([bias_ref[sel, hd] for hd in heads], axis=0)
                sink = jnp.concatenate([jnp.full((BLOCK, 1), sink_ref[0, hd], F32) for hd in heads], axis=0)
                m = jnp.maximum(jnp.max(s, axis=-1, keepdims=True), sink)
                e = jnp.exp(s - m)
                dens.append(jnp.sum(e, axis=-1, keepdims=True) + jnp.exp(sink - m))
                es.append(e.astype(BF16))
            o = _dot(jnp.concatenate(es, axis=0),
                     jnp.concatenate([vvar[2 * kv, keys, :], vvar[2 * kv + 1, keys, :]], axis=1))
            acc = (o[0:nk, 0:LANES] * (1.0 / dens[0]) + o[nk:, LANES:] * (1.0 / dens[1]))
            for n, p in enumerate(tiles):
                mix_ref[rows, D_POOL + p * LANES:D_POOL + (p + 1) * LANES] = (
                    acc[n * BLOCK:(n + 1) * BLOCK].astype(BF16))
            yield

    for i in range(4):
        kvar[i, 0:BLOCK, :] = kvar[i, tile:tile + BLOCK, :]
        vvar[i, 0:BLOCK, :] = vvar[i, tile:tile + BLOCK, :]
    uext[:, 0:halo, :] = uext[:, tile:tile + halo, :]


def _sample_front_kernel(sink_ref, x_ref, ck_ref, cv_ref, st_ref, wf_out_ref, wf1_ref, wf2_ref,
                         g_ref, b_ref, w_in_ref, w_pool_ref, pscale_ref, bias_c_ref, bias_n_ref,
                         mix_ref, nk_ref, nv_ref, np_ref, wb_out_ref, wb1_ref, wb2_ref, o_s):
    for src, dst in ((wf_out_ref, wb_out_ref), (wf1_ref, wb1_ref), (wf2_ref, wb2_ref)):
        dst[...] = src[...].astype(BF16)
    ns, tokens, d_model = x_ref.shape
    lane = lax.broadcasted_iota(jnp.int32, (1, LANES), 1)
    low = lane < HEAD_DIM

    x = jnp.concatenate([x_ref[:, t, :] for t in range(tokens)], axis=0)
    h = _layer_norm(x, g_ref[...], b_ref[...])
    proj = _dot(h.astype(BF16), w_in_ref[...])
    u = proj[:, 0:D_POOL]
    q = proj[:, D_POOL:D_POOL + D_ATTN] * Q_SCALE
    k = proj[:, D_POOL + D_ATTN:D_POOL + D_ATTN + D_KV]
    v = proj[:, D_POOL + D_ATTN + D_KV:]

    ext = [st_ref[r] for r in range(POOL_BUF)]
    ext += [u[t * ns:(t + 1) * ns] for t in range(tokens)]
    ps = []
    for t in range(tokens):
        idx = POOL_BUF + t
        pos = PAST_LEN - POOL_BUF + idx
        parts = []
        for g, w in enumerate(POOL_WINDOWS):
            cols = slice(g * POOL_GROUP_DIM, (g + 1) * POOL_GROUP_DIM)
            lo = max(idx + 1 - w, 0)
            s = ext[idx][:, cols]
            for r in range(idx - 1, lo - 1, -1):
                s = s + ext[r][:, cols]
            parts.append(s * (1.0 / min(w, pos + 1)) - ext[idx][:, cols])
        ps.append(jnp.concatenate(parts, axis=1))
    p = jnp.concatenate(ps, axis=0).astype(BF16)
    zs = [_dot(p[:, g * POOL_GROUP_DIM:(g + 1) * POOL_GROUP_DIM], w_pool_ref[g])
          for g in range(len(POOL_WINDOWS))]
    z = (jnp.concatenate(zs, axis=1) * pscale_ref[...]).astype(BF16)
    for r in range(POOL_BUF):
        np_ref[r] = ext[r + tokens]

    qh = []
    for hd in range(N_HEADS):
        tl = q[:, (hd // 2) * LANES:(hd // 2 + 1) * LANES]
        kv = hd // (N_HEADS // N_KV_HEADS)
        want_low = kv == 0
        is_low = hd % 2 == 0
        src = tl if want_low == is_low else pltpu.roll(tl, HEAD_DIM, axis=1)
        qh.append(jnp.where(low, src, 0.0) if want_low else jnp.where(low, 0.0, src))

    gr = SEQ_GROUP
    rows_per_head = tokens * gr
    sub = lax.broadcasted_iota(jnp.int32, (N_HEADS * rows_per_head, 1), 0) % gr
    sink = jnp.concatenate([jnp.full((rows_per_head, 1), sink_ref[0, hd], F32) for hd in range(N_HEADS)],
                           axis=0)
    zpad = jnp.zeros((LANES - rows_per_head, LANES), F32)
    ztop = jnp.zeros((WINDOW - SUBLANES, LANES), F32)
    sub8 = lax.broadcasted_iota(jnp.int32, (SUBLANES, 1), 0)
    keep = WINDOW - tokens

    def appended(old_ref, news, j):
        tail = jnp.zeros((SUBLANES, LANES), F32)
        for t in range(tokens):
            row = SUBLANES - tokens + t
            tail = jnp.where(sub8 == row, pltpu.roll(news[t], (row - j) % SUBLANES, axis=0), tail)
        cols = jnp.concatenate([ztop, tail], axis=0).T
        return jnp.where(lane >= keep, cols, pltpu.roll(old_ref[...], keep, axis=1))

    for base in range(0, ns, gr):
        k_t = [k[t * ns + base:t * ns + base + gr] for t in range(tokens)]
        v_t = [v[t * ns + base:t * ns + base + gr] for t in range(tokens)]
        for s in range(gr):
            nk_ref[base + s] = appended(ck_ref.at[base + s], k_t, s)
            nv_ref[base + s] = appended(cv_ref.at[base + s], v_t, s)

    for gi in range(ns // gr):
        base = gi * gr
        lhs = jnp.concatenate([qh[hd][t * ns + base:t * ns + base + gr]
                               for hd in range(N_HEADS) for t in range(tokens)], axis=0).astype(BF16)
        sc = None
        for s in range(gr):
            s_one = _dot(lhs, ck_ref[base + s].astype(BF16))
            sc = s_one if sc is None else jnp.where(sub == s, s_one, sc)
        sc = sc + bias_c_ref[...]
        k_t = [k[t * ns + base:t * ns + base + gr] for t in range(tokens)]
        v_t = [v[t * ns + base:t * ns + base + gr] for t in range(tokens)]
        k_new = jnp.concatenate(k_t + [zpad], axis=0)
        v_new = jnp.concatenate(v_t + [zpad], axis=0)
        sn = _dot_nt(lhs, k_new.astype(BF16)) + bias_n_ref[...]
        m = jnp.maximum(jnp.maximum(jnp.max(sc, axis=-1, keepdims=True),
                                    jnp.max(sn, axis=-1, keepdims=True)), sink)
        ec = jnp.exp(sc - m)
        en = jnp.exp(sn - m)
        den = (jnp.sum(ec, axis=-1, keepdims=True) + jnp.sum(en, axis=-1, keepdims=True)
               + jnp.exp(sink - m))
        o = _dot(en.astype(BF16), v_new.astype(BF16))
        for s in range(gr):
            o = o + _dot_nt(jnp.where(sub == s, ec, 0.0).astype(BF16), cv_ref[base + s].astype(BF16))
        o = o * (1.0 / den)
        for t in range(tokens):
            for p in range(N_HEADS // 2):
                a = o[(2 * p * tokens + t) * gr:(2 * p * tokens + t + 1) * gr]
                c = o[((2 * p + 1) * tokens + t) * gr:((2 * p + 1) * tokens + t + 1) * gr]
                if p < N_HEADS // 4:
                    c = pltpu.roll(c, HEAD_DIM, axis=1)
                else:
                    a = pltpu.roll(a, HEAD_DIM, axis=1)
                o_s[t * ns + base:t * ns + base + gr, p * LANES:(p + 1) * LANES] = jnp.where(low, a, c)

    for t in range(tokens):
        mix_ref[:, t * d_model:t * d_model + D_POOL] = z[t * ns:(t + 1) * ns]
        mix_ref[:, t * d_model + D_POOL:(t + 1) * d_model] = o_s[t * ns:(t + 1) * ns, :].astype(BF16)


def _pre_pieces(rows, x, mix, ge_ref, be_ref, w_out_ref, g1_ref, b1_ref, h1_s, hb_s):
    h1_s[rows, :] = ALPHA * _layer_norm(x(), ge_ref[...], be_ref[...])
    yield
    h1_s[rows, :] = h1_s[rows, :] + _dot(mix(), w_out_ref[...])
    yield
    h = _layer_norm(h1_s[rows, :], g1_ref[...], b1_ref[...])
    h1_s[rows, :] = h
    hb_s[rows, :] = h.astype(BF16)
    yield


def _mlp_pieces(rows, a_s, h1_s, hb_s, w1_ref, w2_ref, g2_ref, b2_ref, y_ref):
    for c in range(w1_ref.shape[1] // UP_CHUNK):
        cols = slice(c * UP_CHUNK, (c + 1) * UP_CHUNK)
        a = _dot(hb_s[rows, :], w1_ref[:, cols])
        a_s[:, cols] = jnp.square(jnp.maximum(a, 0.0)).astype(BF16)
        if c == 0:
            y_ref[rows, :] = ALPHA * h1_s[rows, :]
        yield
    for n in range(y_ref.shape[1] // DOWN_CHUNK):
        cols = slice(n * DOWN_CHUNK, (n + 1) * DOWN_CHUNK)
        y_ref[rows, cols] = y_ref[rows, cols] + _dot(a_s[...], w2_ref[:, cols])
        yield
    y_ref[rows, :] = _layer_norm(y_ref[rows, :], g2_ref[...], b2_ref[...])
    yield


def _weave(main, n_main, sides):
    order = []
    for i in range(n_main):
        order.append(main)
        order += sides[i * len(sides) // n_main:(i + 1) * len(sides) // n_main]
    return order


def _main_kernel(tokens, tiles_per_seq, n_tiles,
                 sink_ref, x_ref, xs_ref, mixs_ref, ge_ref, be_ref, w_in_ref, band_ref, w_pool_ref,
                 pscale_ref, bias_ref, meta_proj_ref, w_out_ref, g1_ref, b1_ref, w1_ref, w2_ref,
                 g2_ref, b2_ref,
                 y_ref, ys_ref, nk_ref, nv_ref, np_ref,
                 mixbuf, kvar, vvar, uext, qbuf, pbuf, h1_s, hb_s, a_s):
    s = pl.program_id(0)
    is_first = s == 0
    first_tile = jnp.minimum(s, n_tiles - 1) % tiles_per_seq == 0
    tile, d_model = x_ref.shape
    ns = xs_ref.shape[0]
    half = tile // 2
    halves = (slice(0, half), slice(half, tile))
    pre_w = (ge_ref, be_ref, w_out_ref, g1_ref, b1_ref, h1_s, hb_s)

    @pl.when(is_first)
    def _():
        def toks(rows):
            return range(rows.start // ns, rows.stop // ns)
        for rows in halves:
            x = lambda: jnp.concatenate([xs_ref[:, t, :] for t in toks(rows)], axis=0)
            mix = lambda: jnp.concatenate([mixs_ref[:, t * d_model:(t + 1) * d_model] for t in toks(rows)],
                                          axis=0)
            for _ in _pre_pieces(rows, x, mix, *pre_w):
                pass

    @pl.when(first_tile)
    def _():
        _carry_init(meta_proj_ref, kvar, vvar, uext)

    mlp = [_mlp_pieces(rows, a_s, h1_s, hb_s, w1_ref, w2_ref, g2_ref, b2_ref, y_ref) for rows in halves]
    front = _front_pieces(first_tile, x_ref, sink_ref, ge_ref, be_ref, w_in_ref, band_ref, w_pool_ref, pscale_ref,
                          bias_ref, mixbuf, nk_ref.at[0], nv_ref.at[0], np_ref.at[0], kvar, vvar, uext, qbuf, pbuf)
    pre = [_pre_pieces(rows, lambda rows=rows: x_ref[rows, :], lambda rows=rows: mixbuf[rows, :], *pre_w)
           for rows in halves]
    a, b, f, pa, pb = mlp[0], mlp[1], front, pre[0], pre[1]
    n_up = w1_ref.shape[1] // UP_CHUNK
    n_down = d_model // DOWN_CHUNK
    order = (
        _weave(a, n_up, [f] * 4)
        + _weave(a, n_down, [f] * 5)
        + _weave(b, n_up, [a] + [f] * 5)
        + _weave(b, n_down, [f, pa, pa, pb, pb, pa, pb]) + [b])
    for g in order:
        next(g, None)
    for g in (a, b, f, pa, pb):
        for _ in g:
            raise AssertionError("piece left unscheduled")

    @pl.when(is_first)
    def _():
        for t in range(tokens):
            ys_ref[:, t, :] = y_ref[t * ns:(t + 1) * ns, :]


def _const_spec(shape):
    return pl.BlockSpec(shape, lambda *_: (0,) * len(shape), pipeline_mode=pl.Buffered(1))


def _smem_spec():
    return pl.BlockSpec(memory_space=pltpu.SMEM)


def _prologue_call(tokens, rel_table, meta_tokens, ge, be, w_in, w_pool):
    bkt_p, bkt_c, bkt_n = _bucket_tables(tokens)
    srows = N_HEADS * tokens * SEQ_GROUP
    return pl.pallas_call(
        _prologue_kernel,
        in_specs=[_smem_spec()] + [pl.BlockSpec(memory_space=pltpu.VMEM)] * 8,
        out_specs=[pl.BlockSpec(memory_space=pltpu.VMEM)] * 7,
        out_shape=[jax.ShapeDtypeStruct((2, N_HEADS, BLOCK, 2 * BLOCK), F32),
                   jax.ShapeDtypeStruct((srows, WINDOW), F32),
                   jax.ShapeDtypeStruct((srows, LANES), F32),
                   jax.ShapeDtypeStruct((meta_tokens.shape[0], w_in.shape[1]), F32),
                   jax.ShapeDtypeStruct(w_in.shape, BF16),
                   jax.ShapeDtypeStruct(w_pool.shape, BF16),
                   jax.ShapeDtypeStruct((D_POOL, D_POOL), BF16)],
        compiler_params=pltpu.CompilerParams(vmem_limit_bytes=VMEM_LIMIT_BYTES),
        name="prologue",
    )(rel_table, jnp.asarray(bkt_p), jnp.asarray(bkt_c), jnp.asarray(bkt_n), meta_tokens, ge, be, w_in, w_pool)


def _sample_front_call(sink, xs, ck, cv, st, weights, *consts):
    n_seq, tokens, d_model = xs.shape
    steps = n_seq // SAMPLE_SEQS
    seq_spec = lambda *tail: pl.BlockSpec((SAMPLE_SEQS,) + tail, lambda i: (i,) + (0,) * len(tail))
    state_spec = pl.BlockSpec((POOL_BUF, SAMPLE_SEQS, D_POOL), lambda i: (0, i, 0))
    slab_specs = [pl.BlockSpec((w.shape[0] // steps, w.shape[1]), lambda i: (i, 0)) for w in weights]
    return pl.pallas_call(
        _sample_front_kernel,
        grid=(steps,),
        in_specs=[_smem_spec(), seq_spec(tokens, d_model), seq_spec(D_KV, WINDOW), seq_spec(D_KV, WINDOW),
                  state_spec] + slab_specs + [_const_spec(c.shape) for c in consts],
        out_specs=[seq_spec(tokens * d_model), seq_spec(D_KV, WINDOW), seq_spec(D_KV, WINDOW), state_spec]
        + slab_specs,
        out_shape=[jax.ShapeDtypeStruct((n_seq, tokens * d_model), BF16),
                   jax.ShapeDtypeStruct((n_seq, D_KV, WINDOW), F32),
                   jax.ShapeDtypeStruct((n_seq, D_KV, WINDOW), F32),
                   jax.ShapeDtypeStruct((POOL_BUF, n_seq, D_POOL), F32)]
        + [jax.ShapeDtypeStruct(w.shape, BF16) for w in weights],
        scratch_shapes=[pltpu.VMEM((tokens * SAMPLE_SEQS, D_ATTN), F32)],
        compiler_params=pltpu.CompilerParams(dimension_semantics=("arbitrary",),
                                             vmem_limit_bytes=VMEM_LIMIT_BYTES),
        name="sample_front",
    )(sink, xs, ck, cv, st, *weights, *consts)


def kernel(x_prompt, x_sample, cache_win_k, cache_win_v, state_pool, meta_tokens, ln_emb_g, ln_emb_b,
           rel_table, w_in, w_pool, pool_scale, sinks, w_out, ln1_g, ln1_b, w_mlp_in, w_mlp_out,
           ln2_g, ln2_b):
    batch, seq, d_model = x_prompt.shape
    n_seq, tokens, _ = x_sample.shape
    assert w_in.shape[0] == DEPTH and d_model == D_POOL + D_ATTN
    assert seq % PROMPT_TILE == 0 and n_seq % SAMPLE_SEQS == 0 and tokens * n_seq == PROMPT_TILE
    assert cache_win_k.shape[2] == WINDOW and state_pool.shape[2] == POOL_BUF and tokens <= POOL_BUF
    d_in = w_in.shape[2]

    row = lambda a: a.reshape(1, -1).astype(F32)
    ge, be = row(ln_emb_g), row(ln_emb_b)
    pscale = row(pool_scale[0])
    sink = row(sinks[0])
    cparams = lambda sem: pltpu.CompilerParams(dimension_semantics=sem, vmem_limit_bytes=VMEM_LIMIT_BYTES)

    bias_p, bias_c, bias_n, meta_proj, w_in_b, w_pool_b, w_pool_bd = _prologue_call(
        tokens, rel_table.astype(F32), meta_tokens.astype(F32), ge, be, w_in[0], w_pool[0])

    to_kd_pos = lambda c: jnp.swapaxes(c[0].reshape(n_seq, WINDOW, D_KV), 1, 2)
    mix_s, nk_s, nv_s, np_s, w_out_b, w1_b, w2_b = _sample_front_call(
        sink, x_sample, to_kd_pos(cache_win_k), to_kd_pos(cache_win_v), jnp.swapaxes(state_pool[0], 0, 1),
        (w_out[0], w_mlp_in[0], w_mlp_out[0]), ge, be, w_in_b, w_pool_b, pscale, bias_c, bias_n)
    fin_w = (ge, be, w_out_b, row(ln1_g[0]), row(ln1_b[0]), w1_b, w2_b, row(ln2_g[0]), row(ln2_b[0]))
    xs = x_sample

    tile = PROMPT_TILE
    tiles_per_seq = seq // tile
    n_tiles = batch * tiles_per_seq
    xp = x_prompt.reshape(batch * seq, d_model)
    front_tile = lambda s: jnp.minimum(s, n_tiles - 1)
    finish_tile = lambda s: jnp.maximum(s - 1, 0)
    seq_of = lambda s: front_tile(s) // tiles_per_seq
    band = jnp.asarray(_pool_band(), BF16)
    consts = (ge, be, w_in_b, band, w_pool_bd, pscale, bias_p, meta_proj) + fin_w[2:]
    y_p, y_s, nk_p, nv_p, np_p = pl.pallas_call(
        functools.partial(_main_kernel, tokens, tiles_per_seq, n_tiles),
        grid=(n_tiles + 1,),
        in_specs=[_smem_spec(),
                  pl.BlockSpec((tile, d_model), lambda s: (front_tile(s), 0)),
                  _const_spec(xs.shape), _const_spec(mix_s.shape)] + [_const_spec(c.shape) for c in consts],
        out_specs=[pl.BlockSpec((tile, d_model), lambda s: (finish_tile(s), 0)),
                   pl.BlockSpec(xs.shape, lambda s: (0, 0, 0)),
                   pl.BlockSpec((1, BLOCK, D_KV), lambda s: (seq_of(s), 0, 0)),
                   pl.BlockSpec((1, BLOCK, D_KV), lambda s: (seq_of(s), 0, 0)),
                   pl.BlockSpec((1, HALO, D_POOL), lambda s: (seq_of(s), 0, 0))],
        out_shape=[jax.ShapeDtypeStruct((batch * seq, d_model), F32),
                   jax.ShapeDtypeStruct(xs.shape, F32),
                   jax.ShapeDtypeStruct((batch, BLOCK, D_KV), F32),
                   jax.ShapeDtypeStruct((batch, BLOCK, D_KV), F32),
                   jax.ShapeDtypeStruct((batch, HALO, D_POOL), F32)],
        scratch_shapes=[pltpu.VMEM((tile, d_model), BF16),
                        pltpu.VMEM((4, tile + BLOCK, LANES), BF16),
                        pltpu.VMEM((4, tile + BLOCK, LANES), BF16),
                        pltpu.VMEM((2, tile + HALO, D_POOL), BF16),
                        pltpu.VMEM((tile, D_ATTN), BF16),
                        pltpu.VMEM((tile, D_POOL), BF16),
                        pltpu.VMEM((tile, d_model), F32),
                        pltpu.VMEM((tile, d_model), BF16),
                        pltpu.VMEM((tile // 2, w_mlp_in.shape[2]), BF16)],
        compiler_params=pltpu.CompilerParams(dimension_semantics=("arbitrary",),
                                             vmem_limit_bytes=VMEM_LIMIT_BYTES),
        name="main",
    )(sink, xp, xs, mix_s, *consts)

    kv_shape = (DEPTH, -1, WINDOW, N_KV_HEADS, HEAD_DIM)
    from_kd_pos = lambda c: jnp.swapaxes(c, 1, 2).reshape(kv_shape)
    return (y_p.reshape(batch, seq, d_model), y_s,
            from_kd_pos(nk_p), from_kd_pos(nv_p),
            np_p[:, HALO - POOL_BUF:, :].reshape(DEPTH, batch, POOL_BUF, D_POOL),
            from_kd_pos(nk_s), from_kd_pos(nv_s), jnp.swapaxes(np_s, 0, 1)[None])
```

```python
import functools
import math

import jax
import jax.numpy as jnp
import numpy as np
from jax import lax
from jax.experimental import pallas as pl
from jax.experimental.pallas import tpu as pltpu

N_META = 16
POOL_WINDOWS = (2, 4, 8, 16)
POOL_GROUP_DIM = 128
D_POOL = len(POOL_WINDOWS) * POOL_GROUP_DIM
POOL_BUF = max(POOL_WINDOWS) - 1
N_HEADS = 8
HEAD_DIM = 64
D_ATTN = N_HEADS * HEAD_DIM
N_KV_HEADS = 2
D_KV = N_KV_HEADS * HEAD_DIM
WINDOW = 128
BLOCK = 128
REL_BUCKETS = 32
REL_MAX_DIST = 128
PAST_LEN = 8192
DEPTH = 1
ALPHA = (2.0 * DEPTH) ** 0.25
LN_EPS = 1e-5
Q_SCALE = HEAD_DIM ** -0.5

LANES = 128
SUBLANES = 8
VMEM_LIMIT_BYTES = 60 * 1024 * 1024

PROMPT_TILE = 512
IN_ROWS = 256
UP_CHUNK = 512
DOWN_CHUNK = 256
SAMPLE_SEQS = 32
SEQ_GROUP = SUBLANES

F32 = jnp.float32
BF16 = jnp.bfloat16
NEG_INF = float("-inf")


def _rel_bucket(dist):
    n = np.maximum(dist, 0)
    max_exact = REL_BUCKETS // 2
    nf = np.maximum(n, 1).astype(np.float64)
    large = max_exact + (np.log(nf / max_exact) / math.log(REL_MAX_DIST / max_exact)
                         * (REL_BUCKETS - max_exact)).astype(np.int32)
    large = np.minimum(large, REL_BUCKETS - 1)
    return np.where(n < max_exact, n, large).astype(np.int32)


def _bucket_tables(tokens):
    prompt = np.tile(_rel_bucket((BLOCK - np.arange(BLOCK)) % BLOCK)[None, :], (SUBLANES, 1)).astype(np.int32)
    t = np.repeat(np.arange(tokens), SEQ_GROUP)
    s = np.tile(np.arange(SEQ_GROUP), tokens)
    dist = (t[:, None] + WINDOW) - np.arange(WINDOW)[None, :]
    ok = (dist >= 0) & (dist < WINDOW)
    cache = np.where(ok, _rel_bucket(dist), -1).astype(np.int32)
    dist = t[:, None] - t[None, :]
    ok = (dist >= 0) & (s[:, None] == s[None, :])
    new = np.full((tokens * SEQ_GROUP, LANES), -1, np.int32)
    new[:, :tokens * SEQ_GROUP] = np.where(ok, _rel_bucket(dist), -1)
    return prompt, cache, new


def _layer_norm(x, g, b):
    mu = jnp.mean(x, axis=-1, keepdims=True)
    xc = x - mu
    var = jnp.mean(xc * xc, axis=-1, keepdims=True)
    return xc * lax.rsqrt(var + LN_EPS) * g + b


def _dot(a, b):
    return jnp.dot(a, b, preferred_element_type=F32)


def _dot_nt(a, b):
    return lax.dot_general(a, b, (((1,), (1,)), ((), ())), preferred_element_type=F32)


def _kv_lane_variants(x, low):
    xr = pltpu.roll(x, HEAD_DIM, axis=1)
    zero = jnp.zeros_like(x)
    return (jnp.where(low, x, zero).astype(BF16), jnp.where(low, zero, xr).astype(BF16),
            jnp.where(low, xr, zero).astype(BF16), jnp.where(low, zero, x).astype(BF16))


def _prologue_kernel(tab_ref, bkt_p_ref, bkt_c_ref, bkt_n_ref, meta_ref, g_ref, b_ref, wf_in_ref, wf_pool_ref,
                     bias_p_ref, bias_c_ref, bias_n_ref, meta_proj_ref, w_in_ref, w_pool_ref, w_pool_bd_ref):
    w_in_ref[...] = wf_in_ref[...].astype(BF16)
    w_pool_ref[...] = wf_pool_ref[...].astype(BF16)
    w_pool_bd_ref[...] = jnp.zeros_like(w_pool_bd_ref)
    for g in range(len(POOL_WINDOWS)):
        cols = slice(g * POOL_GROUP_DIM, (g + 1) * POOL_GROUP_DIM)
        w_pool_bd_ref[cols, cols] = w_pool_ref[g]
    def lookup(bucket, h):
        def body(i, acc):
            return jnp.where(bucket == i, tab_ref[i, h], acc)
        return lax.fori_loop(0, REL_BUCKETS, body, jnp.full(bucket.shape, NEG_INF, F32))

    rows = bkt_c_ref.shape[0]
    q = lax.broadcasted_iota(jnp.int32, (BLOCK, BLOCK), 0)
    k = lax.broadcasted_iota(jnp.int32, (BLOCK, BLOCK), 1)
    for h in range(N_HEADS):
        per_dist = lookup(bkt_p_ref[...], h)
        toeplitz = pltpu.roll(jnp.concatenate([per_dist] * (BLOCK // SUBLANES), axis=0), 0, 1,
                              stride=1, stride_axis=0)
        cur = jnp.where(k <= q, toeplitz, NEG_INF)
        for first in range(2):
            prev_ok = (k > q) & (k >= BLOCK - N_META) if first == 0 else k > q
            bias_p_ref[first, h, :, 0:BLOCK] = jnp.where(prev_ok, toeplitz, NEG_INF)
            bias_p_ref[first, h, :, BLOCK:] = cur
        bias_c_ref[h * rows:(h + 1) * rows, :] = lookup(bkt_c_ref[...], h)
        bias_n_ref[h * rows:(h + 1) * rows, :] = lookup(bkt_n_ref[...], h)
    hm = _layer_norm(meta_ref[...], g_ref[...], b_ref[...])
    meta_proj_ref[...] = _dot(hm.astype(BF16), w_in_ref[...])


HALO = 2 * SUBLANES


def _carry_init(meta_proj_ref, kvar, vvar, uext):
    low = lax.broadcasted_iota(jnp.int32, (1, LANES), 1) < HEAD_DIM
    pad = jnp.zeros((BLOCK - N_META, LANES), BF16)
    kq = _kv_lane_variants(meta_proj_ref[:, D_POOL + D_ATTN:D_POOL + D_ATTN + D_KV], low)
    vq = _kv_lane_variants(meta_proj_ref[:, D_POOL + D_ATTN + D_KV:], low)
    for i in range(4):
        kvar[i, 0:BLOCK, :] = jnp.concatenate([pad, kq[i]], axis=0)
        vvar[i, 0:BLOCK, :] = jnp.concatenate([pad, vq[i]], axis=0)
    uext[0:HALO, :] = meta_proj_ref[:, 0:D_POOL]


def _front_pieces(first_tile, x_ref, sink_ref, g_ref, b_ref, w_in_ref, w_pool_ref, pscale_ref, bias_ref,
                  mix_ref, nk_ref, nv_ref, np_ref, kvar, vvar, uext, qbuf, pbuf, hres):
    tile = x_ref.shape[0]
    halo = HALO
    low = lax.broadcasted_iota(jnp.int32, (1, LANES), 1) < HEAD_DIM

    for j in range(tile // IN_ROWS):
        rows = slice(j * IN_ROWS, (j + 1) * IN_ROWS)
        h = _layer_norm(x_ref[rows, :], g_ref[...], b_ref[...])
        hres[rows, :] = ALPHA * h
        proj = _dot(h.astype(BF16), w_in_ref[...])
        u = proj[:, 0:D_POOL]
        k = proj[:, D_POOL + D_ATTN:D_POOL + D_ATTN + D_KV]
        v = proj[:, D_POOL + D_ATTN + D_KV:]
        qbuf[rows, :] = (proj[:, D_POOL:D_POOL + D_ATTN] * Q_SCALE).astype(BF16)
        uext[halo + rows.start:halo + rows.stop, :] = u
        kq = _kv_lane_variants(k, low)
        vq = _kv_lane_variants(v, low)
        for i in range(4):
            kvar[i, BLOCK + rows.start:BLOCK + rows.stop, :] = kq[i]
            vvar[i, BLOCK + rows.start:BLOCK + rows.stop, :] = vq[i]
        if rows.stop == tile:
            nk_ref[...] = k[IN_ROWS - BLOCK:, :].T
            nv_ref[...] = v[IN_ROWS - BLOCK:, :].T
            np_ref[...] = u[IN_ROWS - halo:, :]
        yield

    for g, w in enumerate(POOL_WINDOWS):
        cols = slice(g * POOL_GROUP_DIM, (g + 1) * POOL_GROUP_DIM)
        ug = uext[halo:halo + tile, cols]
        s = ug
        for i in range(1, w):
            s = s + uext[halo - i:halo - i + tile, cols]
        pbuf[:, cols] = (s * (1.0 / w) - ug).astype(BF16)
        if g == len(POOL_WINDOWS) - 1:
            mix_ref[:, 0:D_POOL] = (_dot(pbuf[...], w_pool_ref[...]) * pscale_ref[...]).astype(BF16)
        yield

    for j in range(tile // BLOCK):
        rows = slice(j * BLOCK, (j + 1) * BLOCK)
        keys = slice(j * BLOCK, (j + 2) * BLOCK)
        sel = jnp.where(first_tile, 0, 1) if j == 0 else 1
        for kv in range(N_KV_HEADS):
            tiles = (2 * kv, 2 * kv + 1)
            lhs = jnp.concatenate([qbuf[rows, p * LANES:(p + 1) * LANES] for p in tiles], axis=0)
            nk = 2 * BLOCK
            s_all = _dot_nt(lhs, jnp.concatenate([kvar[2 * kv, keys, :], kvar[2 * kv + 1, keys, :]], axis=0))
            es, dens = [], []
            for half in range(2):
                heads = [2 * p + half for p in tiles]
                s = s_all[:, half * nk:(half + 1) * nk]
                s = s + jnp.concatenate([bias_ref[sel, hd] for hd in heads], axis=0)
                sink = jnp.concatenate([jnp.full((BLOCK, 1), sink_ref[0, hd], F32) for hd in heads], axis=0)
                m = jnp.maximum(jnp.max(s, axis=-1, keepdims=True), sink)
                e = jnp.exp(s - m)
                dens.append(jnp.sum(e, axis=-1, keepdims=True) + jnp.exp(sink - m))
                es.append(e.astype(BF16))
            o = _dot(jnp.concatenate(es, axis=0),
                     jnp.concatenate([vvar[2 * kv, keys, :], vvar[2 * kv + 1, keys, :]], axis=1))
            acc = (o[0:nk, 0:LANES] * (1.0 / dens[0]) + o[nk:, LANES:] * (1.0 / dens[1]))
            for n, p in enumerate(tiles):
                mix_ref[rows, D_POOL + p * LANES:D_POOL + (p + 1) * LANES] = (
                    acc[n * BLOCK:(n + 1) * BLOCK].astype(BF16))
            yield

    for i in range(4):
        kvar[i, 0:BLOCK, :] = kvar[i, tile:tile + BLOCK, :]
        vvar[i, 0:BLOCK, :] = vvar[i, tile:tile + BLOCK, :]
    uext[0:halo, :] = uext[tile:tile + halo, :]


def _sample_front_kernel(sink_ref, x_ref, ck_ref, cv_ref, st_ref, wf_out_ref, wf1_ref, wf2_ref,
                         g_ref, b_ref, w_in_ref, w_pool_ref, pscale_ref, bias_c_ref, bias_n_ref,
                         mix_ref, nk_ref, nv_ref, np_ref, wb_out_ref, wb1_ref, wb2_ref, o_s):
    for src, dst in ((wf_out_ref, wb_out_ref), (wf1_ref, wb1_ref), (wf2_ref, wb2_ref)):
        dst[...] = src[...].astype(BF16)
    ns, tokens, d_model = x_ref.shape
    lane = lax.broadcasted_iota(jnp.int32, (1, LANES), 1)
    low = lane < HEAD_DIM

    x = jnp.concatenate([x_ref[:, t, :] for t in range(tokens)], axis=0)
    h = _layer_norm(x, g_ref[...], b_ref[...])
    proj = _dot(h.astype(BF16), w_in_ref[...])
    u = proj[:, 0:D_POOL]
    q = proj[:, D_POOL:D_POOL + D_ATTN] * Q_SCALE
    k = proj[:, D_POOL + D_ATTN:D_POOL + D_ATTN + D_KV]
    v = proj[:, D_POOL + D_ATTN + D_KV:]

    ext = [st_ref[r] for r in range(POOL_BUF)]
    ext += [u[t * ns:(t + 1) * ns] for t in range(tokens)]
    ps = []
    for t in range(tokens):
        idx = POOL_BUF + t
        pos = PAST_LEN - POOL_BUF + idx
        parts = []
        for g, w in enumerate(POOL_WINDOWS):
            cols = slice(g * POOL_GROUP_DIM, (g + 1) * POOL_GROUP_DIM)
            lo = max(idx + 1 - w, 0)
            s = ext[idx][:, cols]
            for r in range(idx - 1, lo - 1, -1):
                s = s + ext[r][:, cols]
            parts.append(s * (1.0 / min(w, pos + 1)) - ext[idx][:, cols])
        ps.append(jnp.concatenate(parts, axis=1))
    p = jnp.concatenate(ps, axis=0).astype(BF16)
    zs = [_dot(p[:, g * POOL_GROUP_DIM:(g + 1) * POOL_GROUP_DIM], w_pool_ref[g])
          for g in range(len(POOL_WINDOWS))]
    z = (jnp.concatenate(zs, axis=1) * pscale_ref[...]).astype(BF16)
    for r in range(POOL_BUF):
        np_ref[r] = ext[r + tokens]

    qh = []
    for hd in range(N_HEADS):
        tl = q[:, (hd // 2) * LANES:(hd // 2 + 1) * LANES]
        kv = hd // (N_HEADS // N_KV_HEADS)
        want_low = kv == 0
        is_low = hd % 2 == 0
        src = tl if want_low == is_low else pltpu.roll(tl, HEAD_DIM, axis=1)
        qh.append(jnp.where(low, src, 0.0) if want_low else jnp.where(low, 0.0, src))

    gr = SEQ_GROUP
    rows_per_head = tokens * gr
    sub = lax.broadcasted_iota(jnp.int32, (N_HEADS * rows_per_head, 1), 0) % gr
    sink = jnp.concatenate([jnp.full((rows_per_head, 1), sink_ref[0, hd], F32) for hd in range(N_HEADS)],
                           axis=0)
    zpad = jnp.zeros((LANES - rows_per_head, LANES), F32)
    ztop = jnp.zeros((WINDOW - SUBLANES, LANES), F32)
    sub8 = lax.broadcasted_iota(jnp.int32, (SUBLANES, 1), 0)
    keep = WINDOW - tokens

    def appended(old_ref, news, j):
        tail = jnp.zeros((SUBLANES, LANES), F32)
        for t in range(tokens):
            row = SUBLANES - tokens + t
            tail = jnp.where(sub8 == row, pltpu.roll(news[t], (row - j) % SUBLANES, axis=0), tail)
        cols = jnp.concatenate([ztop, tail], axis=0).T
        return jnp.where(lane >= keep, cols, pltpu.roll(old_ref[...], keep, axis=1))

    for base in range(0, ns, gr):
        k_t = [k[t * ns + base:t * ns + base + gr] for t in range(tokens)]
        v_t = [v[t * ns + base:t * ns + base + gr] for t in range(tokens)]
        for s in range(gr):
            nk_ref[base + s] = appended(ck_ref.at[base + s], k_t, s)
            nv_ref[base + s] = appended(cv_ref.at[base + s], v_t, s)

    for gi in range(ns // gr):
        base = gi * gr
        lhs = jnp.concatenate([qh[hd][t * ns + base:t * ns + base + gr]
                               for hd in range(N_HEADS) for t in range(tokens)], axis=0).astype(BF16)
        sc = None
        for s in range(gr):
            s_one = _dot(lhs, ck_ref[base + s].astype(BF16))
            sc = s_one if sc is None else jnp.where(sub == s, s_one, sc)
        sc = sc + bias_c_ref[...]
        k_t = [k[t * ns + base:t * ns + base + gr] for t in range(tokens)]
        v_t = [v[t * ns + base:t * ns + base + gr] for t in range(tokens)]
        k_new = jnp.concatenate(k_t + [zpad], axis=0)
        v_new = jnp.concatenate(v_t + [zpad], axis=0)
        sn = _dot_nt(lhs, k_new.astype(BF16)) + bias_n_ref[...]
        m = jnp.maximum(jnp.maximum(jnp.max(sc, axis=-1, keepdims=True),
                                    jnp.max(sn, axis=-1, keepdims=True)), sink)
        ec = jnp.exp(sc - m)
        en = jnp.exp(sn - m)
        den = (jnp.sum(ec, axis=-1, keepdims=True) + jnp.sum(en, axis=-1, keepdims=True)
               + jnp.exp(sink - m))
        o = _dot(en.astype(BF16), v_new.astype(BF16))
        for s in range(gr):
            o = o + _dot_nt(jnp.where(sub == s, ec, 0.0).astype(BF16), cv_ref[base + s].astype(BF16))
        o = o * (1.0 / den)
        for t in range(tokens):
            for p in range(N_HEADS // 2):
                a = o[(2 * p * tokens + t) * gr:(2 * p * tokens + t + 1) * gr]
                c = o[((2 * p + 1) * tokens + t) * gr:((2 * p + 1) * tokens + t + 1) * gr]
                if p < N_HEADS // 4:
                    c = pltpu.roll(c, HEAD_DIM, axis=1)
                else:
                    a = pltpu.roll(a, HEAD_DIM, axis=1)
                o_s[t * ns + base:t * ns + base + gr, p * LANES:(p + 1) * LANES] = jnp.where(low, a, c)

    for t in range(tokens):
        mix_ref[:, t * d_model:t * d_model + D_POOL] = z[t * ns:(t + 1) * ns]
        mix_ref[:, t * d_model + D_POOL:(t + 1) * d_model] = o_s[t * ns:(t + 1) * ns, :].astype(BF16)


def _pre_pieces(rows, res, mix, w_out_ref, g1_ref, b1_ref, h1_s, hb_s):
    h1_s[rows, :] = res() + _dot(mix(), w_out_ref[...])
    yield
    h = _layer_norm(h1_s[rows, :], g1_ref[...], b1_ref[...])
    h1_s[rows, :] = h
    hb_s[rows, :] = h.astype(BF16)
    yield


def _mlp_pieces(rows, a_s, h1_s, hb_s, w1_ref, w2_ref, g2_ref, b2_ref, y_ref):
    for c in range(w1_ref.shape[1] // UP_CHUNK):
        cols = slice(c * UP_CHUNK, (c + 1) * UP_CHUNK)
        a = _dot(hb_s[rows, :], w1_ref[:, cols])
        a_s[:, cols] = jnp.square(jnp.maximum(a, 0.0)).astype(BF16)
        if c == 0:
            y_ref[rows, :] = ALPHA * h1_s[rows, :]
        yield
    for n in range(y_ref.shape[1] // DOWN_CHUNK):
        cols = slice(n * DOWN_CHUNK, (n + 1) * DOWN_CHUNK)
        y_ref[rows, cols] = y_ref[rows, cols] + _dot(a_s[...], w2_ref[:, cols])
        yield
    y_ref[rows, :] = _layer_norm(y_ref[rows, :], g2_ref[...], b2_ref[...])
    yield


def _weave(main, n_main, sides):
    order = []
    for i in range(n_main):
        order.append(main)
        order += sides[i * len(sides) // n_main:(i + 1) * len(sides) // n_main]
    return order


def _main_kernel(tokens, tiles_per_seq, n_tiles,
                 sink_ref, x_ref, xs_ref, mixs_ref, ge_ref, be_ref, w_in_ref, w_pool_ref,
                 pscale_ref, bias_ref, meta_proj_ref, w_out_ref, g1_ref, b1_ref, w1_ref, w2_ref,
                 g2_ref, b2_ref,
                 y_ref, ys_ref, nk_ref, nv_ref, np_ref,
                 mixbuf, kvar, vvar, uext, qbuf, pbuf, hres, h1_s, hb_s, a_s):
    s = pl.program_id(0)
    is_first = s == 0
    first_tile = jnp.minimum(s, n_tiles - 1) % tiles_per_seq == 0
    tile, d_model = x_ref.shape
    ns = xs_ref.shape[0]
    half = tile // 2
    halves = (slice(0, half), slice(half, tile))
    pre_w = (w_out_ref, g1_ref, b1_ref, h1_s, hb_s)

    @pl.when(is_first)
    def _():
        def toks(rows):
            return range(rows.start // ns, rows.stop // ns)
        for rows in halves:
            res = lambda: ALPHA * _layer_norm(jnp.concatenate([xs_ref[:, t, :] for t in toks(rows)], axis=0),
                                              ge_ref[...], be_ref[...])
            mix = lambda: jnp.concatenate([mixs_ref[:, t * d_model:(t + 1) * d_model] for t in toks(rows)],
                                          axis=0)
            for _ in _pre_pieces(rows, res, mix, *pre_w):
                pass

    @pl.when(first_tile)
    def _():
        _carry_init(meta_proj_ref, kvar, vvar, uext)

    mlp = [_mlp_pieces(rows, a_s, h1_s, hb_s, w1_ref, w2_ref, g2_ref, b2_ref, y_ref) for rows in halves]
    front = _front_pieces(first_tile, x_ref, sink_ref, ge_ref, be_ref, w_in_ref, w_pool_ref, pscale_ref,
                          bias_ref, mixbuf, nk_ref.at[0], nv_ref.at[0], np_ref.at[0], kvar, vvar, uext, qbuf, pbuf,
                          hres)
    pre = [_pre_pieces(rows, lambda rows=rows: hres[rows, :], lambda rows=rows: mixbuf[rows, :], *pre_w)
           for rows in halves]
    a, b, f, pa, pb = mlp[0], mlp[1], front, pre[0], pre[1]
    n_up = w1_ref.shape[1] // UP_CHUNK
    n_down = d_model // DOWN_CHUNK
    order = (
        _weave(a, n_up, [f] * 4)
        + _weave(a, n_down, [f] * 5)
        + _weave(b, n_up, [a] + [f] * 5)
        + _weave(b, n_down, [f, pa, pb, pa, pb]) + [b])
    for g in order:
        next(g, None)
    for g in (a, b, f, pa, pb):
        for _ in g:
            raise AssertionError("piece left unscheduled")

    @pl.when(is_first)
    def _():
        for t in range(tokens):
            ys_ref[:, t, :] = y_ref[t * ns:(t + 1) * ns, :]


def _const_spec(shape):
    return pl.BlockSpec(shape, lambda *_: (0,) * len(shape), pipeline_mode=pl.Buffered(1))


def _smem_spec():
    return pl.BlockSpec(memory_space=pltpu.SMEM)


def _prologue_call(tokens, rel_table, meta_tokens, ge, be, w_in, w_pool):
    bkt_p, bkt_c, bkt_n = _bucket_tables(tokens)
    srows = N_HEADS * tokens * SEQ_GROUP
    return pl.pallas_call(
        _prologue_kernel,
        in_specs=[_smem_spec()] + [pl.BlockSpec(memory_space=pltpu.VMEM)] * 8,
        out_specs=[pl.BlockSpec(memory_space=pltpu.VMEM)] * 7,
        out_shape=[jax.ShapeDtypeStruct((2, N_HEADS, BLOCK, 2 * BLOCK), F32),
                   jax.ShapeDtypeStruct((srows, WINDOW), F32),
                   jax.ShapeDtypeStruct((srows, LANES), F32),
                   jax.ShapeDtypeStruct((meta_tokens.shape[0], w_in.shape[1]), F32),
                   jax.ShapeDtypeStruct(w_in.shape, BF16),
                   jax.ShapeDtypeStruct(w_pool.shape, BF16),
                   jax.ShapeDtypeStruct((D_POOL, D_POOL), BF16)],
        compiler_params=pltpu.CompilerParams(vmem_limit_bytes=VMEM_LIMIT_BYTES),
        name="prologue",
    )(rel_table, jnp.asarray(bkt_p), jnp.asarray(bkt_c), jnp.asarray(bkt_n), meta_tokens, ge, be, w_in, w_pool)


def _sample_front_call(sink, xs, ck, cv, st, weights, *consts):
    n_seq, tokens, d_model = xs.shape
    steps = n_seq // SAMPLE_SEQS
    seq_spec = lambda *tail: pl.BlockSpec((SAMPLE_SEQS,) + tail, lambda i: (i,) + (0,) * len(tail))
    state_spec = pl.BlockSpec((POOL_BUF, SAMPLE_SEQS, D_POOL), lambda i: (0, i, 0))
    slab_specs = [pl.BlockSpec((w.shape[0] // steps, w.shape[1]), lambda i: (i, 0)) for w in weights]
    return pl.pallas_call(
        _sample_front_kernel,
        grid=(steps,),
        in_specs=[_smem_spec(), seq_spec(tokens, d_model), seq_spec(D_KV, WINDOW), seq_spec(D_KV, WINDOW),
                  state_spec] + slab_specs + [_const_spec(c.shape) for c in consts],
        out_specs=[seq_spec(tokens * d_model), seq_spec(D_KV, WINDOW), seq_spec(D_KV, WINDOW), state_spec]
        + slab_specs,
        out_shape=[jax.ShapeDtypeStruct((n_seq, tokens * d_model), BF16),
                   jax.ShapeDtypeStruct((n_seq, D_KV, WINDOW), F32),
                   jax.ShapeDtypeStruct((n_seq, D_KV, WINDOW), F32),
                   jax.ShapeDtypeStruct((POOL_BUF, n_seq, D_POOL), F32)]
        + [jax.ShapeDtypeStruct(w.shape, BF16) for w in weights],
        scratch_shapes=[pltpu.VMEM((tokens * SAMPLE_SEQS, D_ATTN), F32)],
        compiler_params=pltpu.CompilerParams(dimension_semantics=("arbitrary",),
                                             vmem_limit_bytes=VMEM_LIMIT_BYTES),
        name="sample_front",
    )(sink, xs, ck, cv, st, *weights, *consts)


def kernel(x_prompt, x_sample, cache_win_k, cache_win_v, state_pool, meta_tokens, ln_emb_g, ln_emb_b,
           rel_table, w_in, w_pool, pool_scale, sinks, w_out, ln1_g, ln1_b, w_mlp_in, w_mlp_out,
           ln2_g, ln2_b):
    batch, seq, d_model = x_prompt.shape
    n_seq, tokens, _ = x_sample.shape
    assert w_in.shape[0] == DEPTH and d_model == D_POOL + D_ATTN
    assert seq % PROMPT_TILE == 0 and n_seq % SAMPLE_SEQS == 0 and tokens * n_seq == PROMPT_TILE
    assert cache_win_k.shape[2] == WINDOW and state_pool.shape[2] == POOL_BUF and tokens <= POOL_BUF
    d_in = w_in.shape[2]

    row = lambda a: a.reshape(1, -1).astype(F32)
    ge, be = row(ln_emb_g), row(ln_emb_b)
    pscale = row(pool_scale[0])
    sink = row(sinks[0])
    cparams = lambda sem: pltpu.CompilerParams(dimension_semantics=sem, vmem_limit_bytes=VMEM_LIMIT_BYTES)

    bias_p, bias_c, bias_n, meta_proj, w_in_b, w_pool_b, w_pool_bd = _prologue_call(
        tokens, rel_table.astype(F32), meta_tokens.astype(F32), ge, be, w_in[0], w_pool[0])

    to_kd_pos = lambda c: jnp.swapaxes(c[0].reshape(n_seq, WINDOW, D_KV), 1, 2)
    mix_s, nk_s, nv_s, np_s, w_out_b, w1_b, w2_b = _sample_front_call(
        sink, x_sample, to_kd_pos(cache_win_k), to_kd_pos(cache_win_v), jnp.swapaxes(state_pool[0], 0, 1),
        (w_out[0], w_mlp_in[0], w_mlp_out[0]), ge, be, w_in_b, w_pool_b, pscale, bias_c, bias_n)
    fin_w = (ge, be, w_out_b, row(ln1_g[0]), row(ln1_b[0]), w1_b, w2_b, row(ln2_g[0]), row(ln2_b[0]))
    xs = x_sample

    tile = PROMPT_TILE
    tiles_per_seq = seq // tile
    n_tiles = batch * tiles_per_seq
    xp = x_prompt.reshape(batch * seq, d_model)
    front_tile = lambda s: jnp.minimum(s, n_tiles - 1)
    finish_tile = lambda s: jnp.maximum(s - 1, 0)
    seq_of = lambda s: front_tile(s) // tiles_per_seq
    consts = (ge, be, w_in_b, w_pool_bd, pscale, bias_p, meta_proj) + fin_w[2:]
    y_p, y_s, nk_p, nv_p, np_p = pl.pallas_call(
        functools.partial(_main_kernel, tokens, tiles_per_seq, n_tiles),
        grid=(n_tiles + 1,),
        in_specs=[_smem_spec(),
                  pl.BlockSpec((tile, d_model), lambda s: (front_tile(s), 0)),
                  _const_spec(xs.shape), _const_spec(mix_s.shape)] + [_const_spec(c.shape) for c in consts],
        out_specs=[pl.BlockSpec((tile, d_model), lambda s: (finish_tile(s), 0)),
                   pl.BlockSpec(xs.shape, lambda s: (0, 0, 0)),
                   pl.BlockSpec((1, BLOCK, D_KV), lambda s: (seq_of(s), 0, 0)),
                   pl.BlockSpec((1, BLOCK, D_KV), lambda s: (seq_of(s), 0, 0)),
                   pl.BlockSpec((1, HALO, D_POOL), lambda s: (seq_of(s), 0, 0))],
        out_shape=[jax.ShapeDtypeStruct((batch * seq, d_model), F32),
                   jax.ShapeDtypeStruct(xs.shape, F32),
                   jax.ShapeDtypeStruct((batch, BLOCK, D_KV), F32),
                   jax.ShapeDtypeStruct((batch, BLOCK, D_KV), F32),
                   jax.ShapeDtypeStruct((batch, HALO, D_POOL), F32)],
        scratch_shapes=[pltpu.VMEM((tile, d_model), BF16),
                        pltpu.VMEM((4, tile + BLOCK, LANES), BF16),
                        pltpu.VMEM((4, tile + BLOCK, LANES), BF16),
                        pltpu.VMEM((tile + HALO, D_POOL), F32),
                        pltpu.VMEM((tile, D_ATTN), BF16),
                        pltpu.VMEM((tile, D_POOL), BF16),
                        pltpu.VMEM((tile, d_model), F32),
                        pltpu.VMEM((tile, d_model), F32),
                        pltpu.VMEM((tile, d_model), BF16),
                        pltpu.VMEM((tile // 2, w_mlp_in.shape[2]), BF16)],
        compiler_params=pltpu.CompilerParams(dimension_semantics=("arbitrary",),
                                             vmem_limit_bytes=VMEM_LIMIT_BYTES),
        name="main",
    )(sink, xp, xs, mix_s, *consts)

    kv_shape = (DEPTH, -1, WINDOW, N_KV_HEADS, HEAD_DIM)
    from_kd_pos = lambda c: jnp.swapaxes(c, 1, 2).reshape(kv_shape)
    return (y_p.reshape(batch, seq, d_model), y_s,
            from_kd_pos(nk_p), from_kd_pos(nv_p),
            np_p[:, HALO - POOL_BUF:, :].reshape(DEPTH, batch, POOL_BUF, D_POOL),
            from_kd_pos(nk_s), from_kd_pos(nv_s), jnp.swapaxes(np_s, 0, 1)[None])
```

```python
import functools
import math

import jax
import jax.numpy as jnp
import numpy as np
from jax import lax
from jax.experimental import pallas as pl
from jax.experimental.pallas import tpu as pltpu

N_META = 16
POOL_WINDOWS = (2, 4, 8, 16)
POOL_GROUP_DIM = 128
D_POOL = len(POOL_WINDOWS) * POOL_GROUP_DIM
POOL_BUF = max(POOL_WINDOWS) - 1
N_HEADS = 8
HEAD_DIM = 64
D_ATTN = N_HEADS * HEAD_DIM
N_KV_HEADS = 2
D_KV = N_KV_HEADS * HEAD_DIM
WINDOW = 128
BLOCK = 128
REL_BUCKETS = 32
REL_MAX_DIST = 128
PAST_LEN = 8192
DEPTH = 1
ALPHA = (2.0 * DEPTH) ** 0.25
LN_EPS = 1e-5
Q_SCALE = HEAD_DIM ** -0.5

LANES = 128
SUBLANES = 8
VMEM_LIMIT_BYTES = 60 * 1024 * 1024

PROMPT_TILE = 512
IN_ROWS = 256
UP_CHUNK = 512
DOWN_CHUNK = 256
SAMPLE_SEQS = 32
SEQ_GROUP = SUBLANES
HALO = 2 * SUBLANES

F32 = jnp.float32
BF16 = jnp.bfloat16
NEG_INF = float("-inf")


def _rel_bucket(dist):
    n = np.maximum(dist, 0)
    max_exact = REL_BUCKETS // 2
    nf = np.maximum(n, 1).astype(np.float64)
    large = max_exact + (np.log(nf / max_exact) / math.log(REL_MAX_DIST / max_exact)
                         * (REL_BUCKETS - max_exact)).astype(np.int32)
    large = np.minimum(large, REL_BUCKETS - 1)
    return np.where(n < max_exact, n, large).astype(np.int32)


def _bucket_tables(tokens):
    prompt = np.tile(_rel_bucket((BLOCK - np.arange(BLOCK)) % BLOCK)[None, :], (SUBLANES, 1)).astype(np.int32)
    t = np.repeat(np.arange(tokens), SEQ_GROUP)
    s = np.tile(np.arange(SEQ_GROUP), tokens)
    dist = (t[:, None] + WINDOW) - np.arange(WINDOW)[None, :]
    ok = (dist >= 0) & (dist < WINDOW)
    cache = np.where(ok, _rel_bucket(dist), -1).astype(np.int32)
    dist = t[:, None] - t[None, :]
    ok = (dist >= 0) & (s[:, None] == s[None, :])
    new = np.full((tokens * SEQ_GROUP, LANES), -1, np.int32)
    new[:, :tokens * SEQ_GROUP] = np.where(ok, _rel_bucket(dist), -1)
    return prompt, cache, new


def _layer_norm(x, g, b):
    mu = jnp.mean(x, axis=-1, keepdims=True)
    xc = x - mu
    var = jnp.mean(xc * xc, axis=-1, keepdims=True)
    return xc * lax.rsqrt(var + LN_EPS) * g + b


def _dot(a, b):
    return jnp.dot(a, b, preferred_element_type=F32)


def _dot_nt(a, b):
    return lax.dot_general(a, b, (((1,), (1,)), ((), ())), preferred_element_type=F32)


def _kv_lane_variants(x, low):
    xr = pltpu.roll(x, HEAD_DIM, axis=1)
    zero = jnp.zeros_like(x)
    return (jnp.where(low, x, zero).astype(BF16), jnp.where(low, zero, xr).astype(BF16),
            jnp.where(low, xr, zero).astype(BF16), jnp.where(low, zero, x).astype(BF16))


def _prologue_kernel(tab_ref, bkt_p_ref, bkt_c_ref, bkt_n_ref, meta_ref, g_ref, b_ref, wf_in_ref, wf_pool_ref,
                     bias_p_ref, bias_c_ref, bias_n_ref, meta_proj_ref, w_in_ref, w_pool_ref, w_pool_bd_ref):
    w_in_ref[...] = wf_in_ref[...].astype(BF16)
    w_pool_ref[...] = wf_pool_ref[...].astype(BF16)
    w_pool_bd_ref[...] = jnp.zeros_like(w_pool_bd_ref)
    for g in range(len(POOL_WINDOWS)):
        cols = slice(g * POOL_GROUP_DIM, (g + 1) * POOL_GROUP_DIM)
        w_pool_bd_ref[cols, cols] = w_pool_ref[g]
    def lookup(bucket, h):
        def body(i, acc):
            return jnp.where(bucket == i, tab_ref[i, h], acc)
        return lax.fori_loop(0, REL_BUCKETS, body, jnp.full(bucket.shape, NEG_INF, F32))

    rows = bkt_c_ref.shape[0]
    q = lax.broadcasted_iota(jnp.int32, (BLOCK, BLOCK), 0)
    k = lax.broadcasted_iota(jnp.int32, (BLOCK, BLOCK), 1)
    for h in range(N_HEADS):
        per_dist = lookup(bkt_p_ref[...], h)
        toeplitz = pltpu.roll(jnp.concatenate([per_dist] * (BLOCK // SUBLANES), axis=0), 0, 1,
                              stride=1, stride_axis=0)
        cur = jnp.where(k <= q, toeplitz, NEG_INF)
        for table, prev_ok in enumerate(((k > q) & (k >= BLOCK - N_META), k > q)):
            bias_p_ref[table, h, :, 0:BLOCK] = jnp.where(prev_ok, toeplitz, NEG_INF)
            bias_p_ref[table, h, :, BLOCK:] = cur
        bias_c_ref[h * rows:(h + 1) * rows, :] = lookup(bkt_c_ref[...], h)
        bias_n_ref[h * rows:(h + 1) * rows, :] = lookup(bkt_n_ref[...], h)
    hm = _layer_norm(meta_ref[...], g_ref[...], b_ref[...])
    meta_proj_ref[...] = _dot(hm.astype(BF16), w_in_ref[...])


def _carry_init(meta_proj_ref, kvar, vvar, uext):
    low = lax.broadcasted_iota(jnp.int32, (1, LANES), 1) < HEAD_DIM
    pad = jnp.zeros((BLOCK - N_META, LANES), BF16)
    kq = _kv_lane_variants(meta_proj_ref[:, D_POOL + D_ATTN:D_POOL + D_ATTN + D_KV], low)
    vq = _kv_lane_variants(meta_proj_ref[:, D_POOL + D_ATTN + D_KV:], low)
    for i in range(4):
        kvar[i, 0:BLOCK, :] = jnp.concatenate([pad, kq[i]], axis=0)
        vvar[i, 0:BLOCK, :] = jnp.concatenate([pad, vq[i]], axis=0)
    uext[0:HALO, :] = meta_proj_ref[:, 0:D_POOL]


def _front_pieces(first_tile, x_ref, sink_ref, g_ref, b_ref, w_in_ref, w_pool_ref, pscale_ref, bias_ref,
                  mix_ref, nk_ref, nv_ref, np_ref, kvar, vvar, uext, qbuf, pbuf, hres):
    tile = x_ref.shape[0]
    halo = HALO
    low = lax.broadcasted_iota(jnp.int32, (1, LANES), 1) < HEAD_DIM

    for j in range(tile // IN_ROWS):
        rows = slice(j * IN_ROWS, (j + 1) * IN_ROWS)
        h = _layer_norm(x_ref[rows, :], g_ref[...], b_ref[...])
        hres[rows, :] = ALPHA * h
        proj = _dot(h.astype(BF16), w_in_ref[...])
        u = proj[:, 0:D_POOL]
        k = proj[:, D_POOL + D_ATTN:D_POOL + D_ATTN + D_KV]
        v = proj[:, D_POOL + D_ATTN + D_KV:]
        qbuf[rows, :] = (proj[:, D_POOL:D_POOL + D_ATTN] * Q_SCALE).astype(BF16)
        uext[halo + rows.start:halo + rows.stop, :] = u
        kq = _kv_lane_variants(k, low)
        vq = _kv_lane_variants(v, low)
        for i in range(4):
            kvar[i, BLOCK + rows.start:BLOCK + rows.stop, :] = kq[i]
            vvar[i, BLOCK + rows.start:BLOCK + rows.stop, :] = vq[i]
        if rows.stop == tile:
            nk_ref[...] = k[IN_ROWS - BLOCK:, :].T
            nv_ref[...] = v[IN_ROWS - BLOCK:, :].T
            np_ref[...] = u[IN_ROWS - halo:, :]
        yield

    for g, w in enumerate(POOL_WINDOWS):
        cols = slice(g * POOL_GROUP_DIM, (g + 1) * POOL_GROUP_DIM)
        ug = uext[halo:halo + tile, cols]
        s = ug
        for i in range(1, w):
            s = s + uext[halo - i:halo - i + tile, cols]
        pbuf[:, cols] = (s * (1.0 / w) - ug).astype(BF16)
        if g == len(POOL_WINDOWS) - 1:
            mix_ref[:, 0:D_POOL] = (_dot(pbuf[...], w_pool_ref[...]) * pscale_ref[...]).astype(BF16)
        yield

    for j in range(tile // BLOCK):
        rows = slice(j * BLOCK, (j + 1) * BLOCK)
        keys = slice(j * BLOCK, (j + 2) * BLOCK)
        sel = jnp.where(first_tile, 0, 1) if j == 0 else 1
        for kv in range(N_KV_HEADS):
            tiles = (2 * kv, 2 * kv + 1)
            lhs = jnp.concatenate([qbuf[rows, p * LANES:(p + 1) * LANES] for p in tiles], axis=0)
            nk = 2 * BLOCK
            s_all = _dot_nt(lhs, jnp.concatenate([kvar[2 * kv, keys, :], kvar[2 * kv + 1, keys, :]], axis=0))
            es, dens = [], []
            for half in range(2):
                heads = [2 * p + half for p in tiles]
                s = s_all[:, half * nk:(half + 1) * nk]
                s = s + jnp.concatenate([bias_ref[sel, hd] for hd in heads], axis=0)
                sink = jnp.concatenate([jnp.full((BLOCK, 1), sink_ref[0, hd], F32) for hd in heads], axis=0)
                m = jnp.maximum(jnp.max(s, axis=-1, keepdims=True), sink)
                e = jnp.exp(s - m)
                dens.append(jnp.sum(e, axis=-1, keepdims=True) + jnp.exp(sink - m))
                es.append(e.astype(BF16))
            o = _dot(jnp.concatenate(es, axis=0),
                     jnp.concatenate([vvar[2 * kv, keys, :], vvar[2 * kv + 1, keys, :]], axis=1))
            acc = (o[0:nk, 0:LANES] * (1.0 / dens[0]) + o[nk:, LANES:] * (1.0 / dens[1]))
            for n, p in enumerate(tiles):
                mix_ref[rows, D_POOL + p * LANES:D_POOL + (p + 1) * LANES] = (
                    acc[n * BLOCK:(n + 1) * BLOCK].astype(BF16))
            yield

    for i in range(4):
        kvar[i, 0:BLOCK, :] = kvar[i, tile:tile + BLOCK, :]
        vvar[i, 0:BLOCK, :] = vvar[i, tile:tile + BLOCK, :]
    uext[0:halo, :] = uext[tile:tile + halo, :]


def _sample_front_kernel(sink_ref, x_ref, ck_ref, cv_ref, st_ref, wf_out_ref, wf1_ref, wf2_ref,
                         g_ref, b_ref, w_in_ref, w_pool_ref, pscale_ref, bias_c_ref, bias_n_ref,
                         mix_ref, nk_ref, nv_ref, np_ref, wb_out_ref, wb1_ref, wb2_ref, o_s):
    for src, dst in ((wf_out_ref, wb_out_ref), (wf1_ref, wb1_ref), (wf2_ref, wb2_ref)):
        dst[...] = src[...].astype(BF16)
    ns, tokens, d_model = x_ref.shape
    lane = lax.broadcasted_iota(jnp.int32, (1, LANES), 1)
    low = lane < HEAD_DIM

    x = jnp.concatenate([x_ref[:, t, :] for t in range(tokens)], axis=0)
    h = _layer_norm(x, g_ref[...], b_ref[...])
    proj = _dot(h.astype(BF16), w_in_ref[...])
    u = proj[:, 0:D_POOL]
    q = proj[:, D_POOL:D_POOL + D_ATTN] * Q_SCALE
    k = proj[:, D_POOL + D_ATTN:D_POOL + D_ATTN + D_KV]
    v = proj[:, D_POOL + D_ATTN + D_KV:]

    ext = [st_ref[r] for r in range(POOL_BUF)]
    ext += [u[t * ns:(t + 1) * ns] for t in range(tokens)]
    ps = []
    for t in range(tokens):
        idx = POOL_BUF + t
        pos = PAST_LEN - POOL_BUF + idx
        parts = []
        for g, w in enumerate(POOL_WINDOWS):
            cols = slice(g * POOL_GROUP_DIM, (g + 1) * POOL_GROUP_DIM)
            lo = max(idx + 1 - w, 0)
            s = ext[idx][:, cols]
            for r in range(idx - 1, lo - 1, -1):
                s = s + ext[r][:, cols]
            parts.append(s * (1.0 / min(w, pos + 1)) - ext[idx][:, cols])
        ps.append(jnp.concatenate(parts, axis=1))
    p = jnp.concatenate(ps, axis=0).astype(BF16)
    zs = [_dot(p[:, g * POOL_GROUP_DIM:(g + 1) * POOL_GROUP_DIM], w_pool_ref[g])
          for g in range(len(POOL_WINDOWS))]
    z = (jnp.concatenate(zs, axis=1) * pscale_ref[...]).astype(BF16)
    for r in range(POOL_BUF):
        np_ref[r] = ext[r + tokens]

    qh = []
    for hd in range(N_HEADS):
        tl = q[:, (hd // 2) * LANES:(hd // 2 + 1) * LANES]
        kv = hd // (N_HEADS // N_KV_HEADS)
        want_low = kv == 0
        is_low = hd % 2 == 0
        src = tl if want_low == is_low else pltpu.roll(tl, HEAD_DIM, axis=1)
        qh.append(jnp.where(low, src, 0.0) if want_low else jnp.where(low, 0.0, src))

    gr = SEQ_GROUP
    rows_per_head = tokens * gr
    sub = lax.broadcasted_iota(jnp.int32, (N_HEADS * rows_per_head, 1), 0) % gr
    sink = jnp.concatenate([jnp.full((rows_per_head, 1), sink_ref[0, hd], F32) for hd in range(N_HEADS)],
                           axis=0)
    zpad = jnp.zeros((LANES - rows_per_head, LANES), F32)
    ztop = jnp.zeros((WINDOW - SUBLANES, LANES), F32)
    sub8 = lax.broadcasted_iota(jnp.int32, (SUBLANES, 1), 0)
    keep = WINDOW - tokens

    def appended(old_ref, news, j):
        tail = jnp.zeros((SUBLANES, LANES), F32)
        for t in range(tokens):
            row = SUBLANES - tokens + t
            tail = jnp.where(sub8 == row, pltpu.roll(news[t], (row - j) % SUBLANES, axis=0), tail)
        cols = jnp.concatenate([ztop, tail], axis=0).T
        return jnp.where(lane >= keep, cols, pltpu.roll(old_ref[...], keep, axis=1))

    for base in range(0, ns, gr):
        k_t = [k[t * ns + base:t * ns + base + gr] for t in range(tokens)]
        v_t = [v[t * ns + base:t * ns + base + gr] for t in range(tokens)]
        for s in range(gr):
            nk_ref[base + s] = appended(ck_ref.at[base + s], k_t, s)
            nv_ref[base + s] = appended(cv_ref.at[base + s], v_t, s)

    for gi in range(ns // gr):
        base = gi * gr
        lhs = jnp.concatenate([qh[hd][t * ns + base:t * ns + base + gr]
                               for hd in range(N_HEADS) for t in range(tokens)], axis=0).astype(BF16)
        sc = None
        for s in range(gr):
            s_one = _dot(lhs, ck_ref[base + s].astype(BF16))
            sc = s_one if sc is None else jnp.where(sub == s, s_one, sc)
        sc = sc + bias_c_ref[...]
        k_t = [k[t * ns + base:t * ns + base + gr] for t in range(tokens)]
        v_t = [v[t * ns + base:t * ns + base + gr] for t in range(tokens)]
        k_new = jnp.concatenate(k_t + [zpad], axis=0)
        v_new = jnp.concatenate(v_t + [zpad], axis=0)
        sn = _dot_nt(lhs, k_new.astype(BF16)) + bias_n_ref[...]
        m = jnp.maximum(jnp.maximum(jnp.max(sc, axis=-1, keepdims=True),
                                    jnp.max(sn, axis=-1, keepdims=True)), sink)
        ec = jnp.exp(sc - m)
        en = jnp.exp(sn - m)
        den = (jnp.sum(ec, axis=-1, keepdims=True) + jnp.sum(en, axis=-1, keepdims=True)
               + jnp.exp(sink - m))
        o = _dot(en.astype(BF16), v_new.astype(BF16))
        for s in range(gr):
            o = o + _dot_nt(jnp.where(sub == s, ec, 0.0).astype(BF16), cv_ref[base + s].astype(BF16))
        o = o * (1.0 / den)
        for t in range(tokens):
            for p in range(N_HEADS // 2):
                a = o[(2 * p * tokens + t) * gr:(2 * p * tokens + t + 1) * gr]
                c = o[((2 * p + 1) * tokens + t) * gr:((2 * p + 1) * tokens + t + 1) * gr]
                if p < N_HEADS // 4:
                    c = pltpu.roll(c, HEAD_DIM, axis=1)
                else:
                    a = pltpu.roll(a, HEAD_DIM, axis=1)
                o_s[t * ns + base:t * ns + base + gr, p * LANES:(p + 1) * LANES] = jnp.where(low, a, c)

    for t in range(tokens):
        mix_ref[:, t * d_model:t * d_model + D_POOL] = z[t * ns:(t + 1) * ns]
        mix_ref[:, t * d_model + D_POOL:(t + 1) * d_model] = o_s[t * ns:(t + 1) * ns, :].astype(BF16)


def _pre_pieces(rows, res, mix, w_out_ref, g1_ref, b1_ref, h1_s, hb_s):
    h1_s[rows, :] = res() + _dot(mix(), w_out_ref[...])
    yield
    h = _layer_norm(h1_s[rows, :], g1_ref[...], b1_ref[...])
    h1_s[rows, :] = h
    hb_s[rows, :] = h.astype(BF16)
    yield


def _mlp_pieces(rows, a_s, h1_s, hb_s, w1_ref, w2_ref, g2_ref, b2_ref, y_ref):
    for c in range(w1_ref.shape[1] // UP_CHUNK):
        cols = slice(c * UP_CHUNK, (c + 1) * UP_CHUNK)
        a = _dot(hb_s[rows, :], w1_ref[:, cols])
        a_s[:, cols] = jnp.square(jnp.maximum(a, 0.0)).astype(BF16)
        if c == 0:
            y_ref[rows, :] = ALPHA * h1_s[rows, :]
        yield
    for n in range(y_ref.shape[1] // DOWN_CHUNK):
        cols = slice(n * DOWN_CHUNK, (n + 1) * DOWN_CHUNK)
        y_ref[rows, cols] = y_ref[rows, cols] + _dot(a_s[...], w2_ref[:, cols])
        yield
    y_ref[rows, :] = _layer_norm(y_ref[rows, :], g2_ref[...], b2_ref[...])
    yield


def _weave(main, n_main, sides):
    order = []
    for i in range(n_main):
        order.append(main)
        order += sides[i * len(sides) // n_main:(i + 1) * len(sides) // n_main]
    return order


def _main_kernel(tokens, tiles_per_seq, n_tiles,
                 sink_ref, x_ref, xs_ref, mixs_ref, ge_ref, be_ref, w_in_ref, w_pool_ref,
                 pscale_ref, bias_ref, meta_proj_ref, w_out_ref, g1_ref, b1_ref, w1_ref, w2_ref,
                 g2_ref, b2_ref,
                 y_ref, ys_ref, nk_ref, nv_ref, np_ref,
                 mixbuf, kvar, vvar, uext, qbuf, pbuf, hres, h1_s, hb_s, a_s):
    s = pl.program_id(0)
    is_first = s == 0
    first_tile = jnp.minimum(s, n_tiles - 1) % tiles_per_seq == 0
    tile, d_model = x_ref.shape
    ns = xs_ref.shape[0]
    half = tile // 2
    halves = (slice(0, half), slice(half, tile))
    pre_w = (w_out_ref, g1_ref, b1_ref, h1_s, hb_s)

    @pl.when(is_first)
    def _():
        def toks(rows):
            return range(rows.start // ns, rows.stop // ns)
        for rows in halves:
            res = lambda: ALPHA * _layer_norm(jnp.concatenate([xs_ref[:, t, :] for t in toks(rows)], axis=0),
                                              ge_ref[...], be_ref[...])
            mix = lambda: jnp.concatenate([mixs_ref[:, t * d_model:(t + 1) * d_model] for t in toks(rows)],
                                          axis=0)
            for _ in _pre_pieces(rows, res, mix, *pre_w):
                pass

    @pl.when(first_tile)
    def _():
        _carry_init(meta_proj_ref, kvar, vvar, uext)

    mlp = [_mlp_pieces(rows, a_s, h1_s, hb_s, w1_ref, w2_ref, g2_ref, b2_ref, y_ref) for rows in halves]
    front = _front_pieces(first_tile, x_ref, sink_ref, ge_ref, be_ref, w_in_ref, w_pool_ref, pscale_ref,
                          bias_ref, mixbuf, nk_ref.at[0], nv_ref.at[0], np_ref.at[0], kvar, vvar, uext, qbuf, pbuf,
                          hres)
    pre = [_pre_pieces(rows, lambda rows=rows: hres[rows, :], lambda rows=rows: mixbuf[rows, :], *pre_w)
           for rows in halves]
    a, b, f, pa, pb = mlp[0], mlp[1], front, pre[0], pre[1]
    n_up = w1_ref.shape[1] // UP_CHUNK
    n_down = d_model // DOWN_CHUNK
    order = (
        _weave(a, n_up, [f] * 4)
        + _weave(a, n_down, [f] * 5)
        + _weave(b, n_up, [a] + [f] * 5)
        + _weave(b, n_down, [f, pa, pb, pa, pb]) + [b])
    for g in order:
        next(g, None)
    for g in (a, b, f, pa, pb):
        for _ in g:
            raise AssertionError("piece left unscheduled")

    @pl.when(is_first)
    def _():
        for t in range(tokens):
            ys_ref[:, t, :] = y_ref[t * ns:(t + 1) * ns, :]


def _const_spec(shape):
    return pl.BlockSpec(shape, lambda *_: (0,) * len(shape), pipeline_mode=pl.Buffered(1))


def _smem_spec():
    return pl.BlockSpec(memory_space=pltpu.SMEM)


def _prologue_call(tokens, rel_table, meta_tokens, ge, be, w_in, w_pool):
    bkt_p, bkt_c, bkt_n = _bucket_tables(tokens)
    srows = N_HEADS * tokens * SEQ_GROUP
    return pl.pallas_call(
        _prologue_kernel,
        in_specs=[_smem_spec()] + [pl.BlockSpec(memory_space=pltpu.VMEM)] * 8,
        out_specs=[pl.BlockSpec(memory_space=pltpu.VMEM)] * 7,
        out_shape=[jax.ShapeDtypeStruct((2, N_HEADS, BLOCK, 2 * BLOCK), F32),
                   jax.ShapeDtypeStruct((srows, WINDOW), F32),
                   jax.ShapeDtypeStruct((srows, LANES), F32),
                   jax.ShapeDtypeStruct((meta_tokens.shape[0], w_in.shape[1]), F32),
                   jax.ShapeDtypeStruct(w_in.shape, BF16),
                   jax.ShapeDtypeStruct(w_pool.shape, BF16),
                   jax.ShapeDtypeStruct((D_POOL, D_POOL), BF16)],
        compiler_params=pltpu.CompilerParams(vmem_limit_bytes=VMEM_LIMIT_BYTES),
        name="prologue",
    )(rel_table, jnp.asarray(bkt_p), jnp.asarray(bkt_c), jnp.asarray(bkt_n), meta_tokens, ge, be, w_in, w_pool)


def _sample_front_call(sink, xs, ck, cv, st, weights, *consts):
    n_seq, tokens, d_model = xs.shape
    steps = n_seq // SAMPLE_SEQS
    seq_spec = lambda *tail: pl.BlockSpec((SAMPLE_SEQS,) + tail, lambda i: (i,) + (0,) * len(tail))
    state_spec = pl.BlockSpec((POOL_BUF, SAMPLE_SEQS, D_POOL), lambda i: (0, i, 0))
    slab_specs = [pl.BlockSpec((w.shape[0] // steps, w.shape[1]), lambda i: (i, 0)) for w in weights]
    return pl.pallas_call(
        _sample_front_kernel,
        grid=(steps,),
        in_specs=[_smem_spec(), seq_spec(tokens, d_model), seq_spec(D_KV, WINDOW), seq_spec(D_KV, WINDOW),
                  state_spec] + slab_specs + [_const_spec(c.shape) for c in consts],
        out_specs=[seq_spec(tokens * d_model), seq_spec(D_KV, WINDOW), seq_spec(D_KV, WINDOW), state_spec]
        + slab_specs,
        out_shape=[jax.ShapeDtypeStruct((n_seq, tokens * d_model), BF16),
                   jax.ShapeDtypeStruct((n_seq, D_KV, WINDOW), F32),
                   jax.ShapeDtypeStruct((n_seq, D_KV, WINDOW), F32),
                   jax.ShapeDtypeStruct((POOL_BUF, n_seq, D_POOL), F32)]
        + [jax.ShapeDtypeStruct(w.shape, BF16) for w in weights],
        scratch_shapes=[pltpu.VMEM((tokens * SAMPLE_SEQS, D_ATTN), F32)],
        compiler_params=pltpu.CompilerParams(dimension_semantics=("arbitrary",),
                                             vmem_limit_bytes=VMEM_LIMIT_BYTES),
        name="sample_front",
    )(sink, xs, ck, cv, st, *weights, *consts)


def kernel(x_prompt, x_sample, cache_win_k, cache_win_v, state_pool, meta_tokens, ln_emb_g, ln_emb_b,
           rel_table, w_in, w_pool, pool_scale, sinks, w_out, ln1_g, ln1_b, w_mlp_in, w_mlp_out,
           ln2_g, ln2_b):
    batch, seq, d_model = x_prompt.shape
    n_seq, tokens, _ = x_sample.shape
    assert w_in.shape[0] == DEPTH and d_model == D_POOL + D_ATTN
    assert seq % PROMPT_TILE == 0 and n_seq % SAMPLE_SEQS == 0 and tokens * n_seq == PROMPT_TILE
    assert cache_win_k.shape[2] == WINDOW and state_pool.shape[2] == POOL_BUF and tokens <= POOL_BUF
    assert N_META == HALO and POOL_BUF <= HALO

    row = lambda a: a.reshape(1, -1).astype(F32)
    ge, be = row(ln_emb_g), row(ln_emb_b)
    pscale = row(pool_scale[0])
    sink = row(sinks[0])

    bias_p, bias_c, bias_n, meta_proj, w_in_b, w_pool_b, w_pool_bd = _prologue_call(
        tokens, rel_table.astype(F32), meta_tokens.astype(F32), ge, be, w_in[0], w_pool[0])

    to_kd_pos = lambda c: jnp.swapaxes(c[0].reshape(n_seq, WINDOW, D_KV), 1, 2)
    mix_s, nk_s, nv_s, np_s, w_out_b, w1_b, w2_b = _sample_front_call(
        sink, x_sample, to_kd_pos(cache_win_k), to_kd_pos(cache_win_v), jnp.swapaxes(state_pool[0], 0, 1),
        (w_out[0], w_mlp_in[0], w_mlp_out[0]), ge, be, w_in_b, w_pool_b, pscale, bias_c, bias_n)
    fin_w = (ge, be, w_out_b, row(ln1_g[0]), row(ln1_b[0]), w1_b, w2_b, row(ln2_g[0]), row(ln2_b[0]))
    xs = x_sample

    tile = PROMPT_TILE
    tiles_per_seq = seq // tile
    n_tiles = batch * tiles_per_seq
    xp = x_prompt.reshape(batch * seq, d_model)
    front_tile = lambda s: jnp.minimum(s, n_tiles - 1)
    finish_tile = lambda s: jnp.maximum(s - 1, 0)
    seq_of = lambda s: front_tile(s) // tiles_per_seq
    consts = (ge, be, w_in_b, w_pool_bd, pscale, bias_p, meta_proj) + fin_w[2:]
    y_p, y_s, nk_p, nv_p, np_p = pl.pallas_call(
        functools.partial(_main_kernel, tokens, tiles_per_seq, n_tiles),
        grid=(n_tiles + 1,),
        in_specs=[_smem_spec(),
                  pl.BlockSpec((tile, d_model), lambda s: (front_tile(s), 0)),
                  _const_spec(xs.shape), _const_spec(mix_s.shape)] + [_const_spec(c.shape) for c in consts],
        out_specs=[pl.BlockSpec((tile, d_model), lambda s: (finish_tile(s), 0)),
                   pl.BlockSpec(xs.shape, lambda s: (0, 0, 0)),
                   pl.BlockSpec((1, BLOCK, D_KV), lambda s: (seq_of(s), 0, 0)),
                   pl.BlockSpec((1, BLOCK, D_KV), lambda s: (seq_of(s), 0, 0)),
                   pl.BlockSpec((1, HALO, D_POOL), lambda s: (seq_of(s), 0, 0))],
        out_shape=[jax.ShapeDtypeStruct((batch * seq, d_model), F32),
                   jax.ShapeDtypeStruct(xs.shape, F32),
                   jax.ShapeDtypeStruct((batch, BLOCK, D_KV), F32),
                   jax.ShapeDtypeStruct((batch, BLOCK, D_KV), F32),
                   jax.ShapeDtypeStruct((batch, HALO, D_POOL), F32)],
        scratch_shapes=[pltpu.VMEM((tile, d_model), BF16),
                        pltpu.VMEM((4, tile + BLOCK, LANES), BF16),
                        pltpu.VMEM((4, tile + BLOCK, LANES), BF16),
                        pltpu.VMEM((tile + HALO, D_POOL), F32),
                        pltpu.VMEM((tile, D_ATTN), BF16),
                        pltpu.VMEM((tile, D_POOL), BF16),
                        pltpu.VMEM((tile, d_model), F32),
                        pltpu.VMEM((tile, d_model), F32),
                        pltpu.VMEM((tile, d_model), BF16),
                        pltpu.VMEM((tile // 2, w_mlp_in.shape[2]), BF16)],
        compiler_params=pltpu.CompilerParams(dimension_semantics=("arbitrary",),
                                             vmem_limit_bytes=VMEM_LIMIT_BYTES),
        name="main",
    )(sink, xp, xs, mix_s, *consts)

    kv_shape = (DEPTH, -1, WINDOW, N_KV_HEADS, HEAD_DIM)
    from_kd_pos = lambda c: jnp.swapaxes(c, 1, 2).reshape(kv_shape)
    return (y_p.reshape(batch, seq, d_model), y_s,
            from_kd_pos(nk_p), from_kd_pos(nv_p),
            np_p[:, HALO - POOL_BUF:, :].reshape(DEPTH, batch, POOL_BUF, D_POOL),
            from_kd_pos(nk_s), from_kd_pos(nv_s), jnp.swapaxes(np_s, 0, 1)[None])
```

```python
import functools
import math

import jax
import jax.numpy as jnp
import numpy as np
from jax import lax
from jax.experimental import pallas as pl
from jax.experimental.pallas import tpu as pltpu

N_META = 16
POOL_WINDOWS = (2, 4, 8, 16)
POOL_GROUP_DIM = 128
D_POOL = len(POOL_WINDOWS) * POOL_GROUP_DIM
POOL_BUF = max(POOL_WINDOWS) - 1
N_HEADS = 8
HEAD_DIM = 64
D_ATTN = N_HEADS * HEAD_DIM
N_KV_HEADS = 2
D_KV = N_KV_HEADS * HEAD_DIM
WINDOW = 128
BLOCK = 128
REL_BUCKETS = 32
REL_MAX_DIST = 128
PAST_LEN = 8192
DEPTH = 1
ALPHA = (2.0 * DEPTH) ** 0.25
LN_EPS = 1e-5
Q_SCALE = HEAD_DIM ** -0.5

LANES = 128
SUBLANES = 8
VMEM_LIMIT_BYTES = 60 * 1024 * 1024

PROMPT_TILE = 512
IN_ROWS = 256
UP_CHUNK = 512
DOWN_CHUNK = 256
SAMPLE_SEQS = 32
SEQ_GROUP = SUBLANES
HALO = 2 * SUBLANES
WEIGHT_CHUNK = (256, 1024)
WEIGHT_SLOTS = 2

F32 = jnp.float32
BF16 = jnp.bfloat16
NEG_INF = float("-inf")


def _rel_bucket(dist):
    n = np.maximum(dist, 0)
    max_exact = REL_BUCKETS // 2
    nf = np.maximum(n, 1).astype(np.float64)
    large = max_exact + (np.log(nf / max_exact) / math.log(REL_MAX_DIST / max_exact)
                         * (REL_BUCKETS - max_exact)).astype(np.int32)
    large = np.minimum(large, REL_BUCKETS - 1)
    return np.where(n < max_exact, n, large).astype(np.int32)


def _bucket_tables(tokens):
    prompt = np.tile(_rel_bucket((BLOCK - np.arange(BLOCK)) % BLOCK)[None, :], (SUBLANES, 1)).astype(np.int32)
    t = np.repeat(np.arange(tokens), SEQ_GROUP)
    s = np.tile(np.arange(SEQ_GROUP), tokens)
    dist = (t[:, None] + WINDOW) - np.arange(WINDOW)[None, :]
    ok = (dist >= 0) & (dist < WINDOW)
    cache = np.where(ok, _rel_bucket(dist), -1).astype(np.int32)
    dist = t[:, None] - t[None, :]
    ok = (dist >= 0) & (s[:, None] == s[None, :])
    new = np.full((tokens * SEQ_GROUP, LANES), -1, np.int32)
    new[:, :tokens * SEQ_GROUP] = np.where(ok, _rel_bucket(dist), -1)
    return prompt, cache, new


def _layer_norm(x, g, b):
    mu = jnp.mean(x, axis=-1, keepdims=True)
    xc = x - mu
    var = jnp.mean(xc * xc, axis=-1, keepdims=True)
    return xc * lax.rsqrt(var + LN_EPS) * g + b


def _dot(a, b):
    return jnp.dot(a, b, preferred_element_type=F32)


def _dot_nt(a, b):
    return lax.dot_general(a, b, (((1,), (1,)), ((), ())), preferred_element_type=F32)


def _kv_lane_variants(x, low):
    xr = pltpu.roll(x, HEAD_DIM, axis=1)
    zero = jnp.zeros_like(x)
    return (jnp.where(low, x, zero).astype(BF16), jnp.where(low, zero, xr).astype(BF16),
            jnp.where(low, xr, zero).astype(BF16), jnp.where(low, zero, x).astype(BF16))


def _prologue_kernel(tab_ref, bkt_p_ref, bkt_c_ref, bkt_n_ref, meta_ref, g_ref, b_ref, wf_in_ref, wf_pool_ref,
                     bias_p_ref, bias_c_ref, bias_n_ref, meta_proj_ref, w_in_ref, w_pool_ref, w_pool_bd_ref):
    w_in_ref[...] = wf_in_ref[...].astype(BF16)
    w_pool_ref[...] = wf_pool_ref[...].astype(BF16)
    w_pool_bd_ref[...] = jnp.zeros_like(w_pool_bd_ref)
    for g in range(len(POOL_WINDOWS)):
        cols = slice(g * POOL_GROUP_DIM, (g + 1) * POOL_GROUP_DIM)
        w_pool_bd_ref[cols, cols] = w_pool_ref[g]
    def lookup(bucket, h):
        def body(i, acc):
            return jnp.where(bucket == i, tab_ref[i, h], acc)
        return lax.fori_loop(0, REL_BUCKETS, body, jnp.full(bucket.shape, NEG_INF, F32))

    rows = bkt_c_ref.shape[0]
    q = lax.broadcasted_iota(jnp.int32, (BLOCK, BLOCK), 0)
    k = lax.broadcasted_iota(jnp.int32, (BLOCK, BLOCK), 1)
    for h in range(N_HEADS):
        per_dist = lookup(bkt_p_ref[...], h)
        toeplitz = pltpu.roll(jnp.concatenate([per_dist] * (BLOCK // SUBLANES), axis=0), 0, 1,
                              stride=1, stride_axis=0)
        cur = jnp.where(k <= q, toeplitz, NEG_INF)
        for table, prev_ok in enumerate(((k > q) & (k >= BLOCK - N_META), k > q)):
            bias_p_ref[table, h, :, 0:BLOCK] = jnp.where(prev_ok, toeplitz, NEG_INF)
            bias_p_ref[table, h, :, BLOCK:] = cur
        bias_c_ref[h * rows:(h + 1) * rows, :] = lookup(bkt_c_ref[...], h)
        bias_n_ref[h * rows:(h + 1) * rows, :] = lookup(bkt_n_ref[...], h)
    hm = _layer_norm(meta_ref[...], g_ref[...], b_ref[...])
    meta_proj_ref[...] = _dot(hm.astype(BF16), w_in_ref[...])


def _carry_init(meta_proj_ref, kvar, vvar, uext):
    low = lax.broadcasted_iota(jnp.int32, (1, LANES), 1) < HEAD_DIM
    pad = jnp.zeros((BLOCK - N_META, LANES), BF16)
    kq = _kv_lane_variants(meta_proj_ref[:, D_POOL + D_ATTN:D_POOL + D_ATTN + D_KV], low)
    vq = _kv_lane_variants(meta_proj_ref[:, D_POOL + D_ATTN + D_KV:], low)
    for i in range(4):
        kvar[i, 0:BLOCK, :] = jnp.concatenate([pad, kq[i]], axis=0)
        vvar[i, 0:BLOCK, :] = jnp.concatenate([pad, vq[i]], axis=0)
    uext[0:HALO, :] = meta_proj_ref[:, 0:D_POOL]


def _front_pieces(first_tile, x_ref, sink_ref, g_ref, b_ref, w_in_ref, w_pool_ref, pscale_ref, bias_ref,
                  mix_ref, nk_ref, nv_ref, np_ref, kvar, vvar, uext, qbuf, pbuf, hres):
    tile = x_ref.shape[0]
    halo = HALO
    low = lax.broadcasted_iota(jnp.int32, (1, LANES), 1) < HEAD_DIM

    for j in range(tile // IN_ROWS):
        rows = slice(j * IN_ROWS, (j + 1) * IN_ROWS)
        h = _layer_norm(x_ref[rows, :], g_ref[...], b_ref[...])
        hres[rows, :] = ALPHA * h
        proj = _dot(h.astype(BF16), w_in_ref[...])
        u = proj[:, 0:D_POOL]
        k = proj[:, D_POOL + D_ATTN:D_POOL + D_ATTN + D_KV]
        v = proj[:, D_POOL + D_ATTN + D_KV:]
        qbuf[rows, :] = (proj[:, D_POOL:D_POOL + D_ATTN] * Q_SCALE).astype(BF16)
        uext[halo + rows.start:halo + rows.stop, :] = u
        kq = _kv_lane_variants(k, low)
        vq = _kv_lane_variants(v, low)
        for i in range(4):
            kvar[i, BLOCK + rows.start:BLOCK + rows.stop, :] = kq[i]
            vvar[i, BLOCK + rows.start:BLOCK + rows.stop, :] = vq[i]
        if rows.stop == tile:
            nk_ref[...] = k[IN_ROWS - BLOCK:, :].T
            nv_ref[...] = v[IN_ROWS - BLOCK:, :].T
            np_ref[...] = u[IN_ROWS - halo:, :]
        yield

    for g, w in enumerate(POOL_WINDOWS):
        cols = slice(g * POOL_GROUP_DIM, (g + 1) * POOL_GROUP_DIM)
        ug = uext[halo:halo + tile, cols]
        s = ug
        for i in range(1, w):
            s = s + uext[halo - i:halo - i + tile, cols]
        pbuf[:, cols] = (s * (1.0 / w) - ug).astype(BF16)
        if g == len(POOL_WINDOWS) - 1:
            mix_ref[:, 0:D_POOL] = (_dot(pbuf[...], w_pool_ref[...]) * pscale_ref[...]).astype(BF16)
        yield

    for j in range(tile // BLOCK):
        rows = slice(j * BLOCK, (j + 1) * BLOCK)
        keys = slice(j * BLOCK, (j + 2) * BLOCK)
        sel = jnp.where(first_tile, 0, 1) if j == 0 else 1
        for kv in range(N_KV_HEADS):
            tiles = (2 * kv, 2 * kv + 1)
            lhs = jnp.concatenate([qbuf[rows, p * LANES:(p + 1) * LANES] for p in tiles], axis=0)
            nk = 2 * BLOCK
            s_all = _dot_nt(lhs, jnp.concatenate([kvar[2 * kv, keys, :], kvar[2 * kv + 1, keys, :]], axis=0))
            es, dens = [], []
            for half in range(2):
                heads = [2 * p + half for p in tiles]
                s = s_all[:, half * nk:(half + 1) * nk]
                s = s + jnp.concatenate([bias_ref[sel, hd] for hd in heads], axis=0)
                sink = jnp.concatenate([jnp.full((BLOCK, 1), sink_ref[0, hd], F32) for hd in heads], axis=0)
                m = jnp.maximum(jnp.max(s, axis=-1, keepdims=True), sink)
                e = jnp.exp(s - m)
                dens.append(jnp.sum(e, axis=-1, keepdims=True) + jnp.exp(sink - m))
                es.append(e.astype(BF16))
            o = _dot(jnp.concatenate(es, axis=0),
                     jnp.concatenate([vvar[2 * kv, keys, :], vvar[2 * kv + 1, keys, :]], axis=1))
            acc = (o[0:nk, 0:LANES] * (1.0 / dens[0]) + o[nk:, LANES:] * (1.0 / dens[1]))
            for n, p in enumerate(tiles):
                mix_ref[rows, D_POOL + p * LANES:D_POOL + (p + 1) * LANES] = (
                    acc[n * BLOCK:(n + 1) * BLOCK].astype(BF16))
            yield

    for i in range(4):
        kvar[i, 0:BLOCK, :] = kvar[i, tile:tile + BLOCK, :]
        vvar[i, 0:BLOCK, :] = vvar[i, tile:tile + BLOCK, :]
    uext[0:halo, :] = uext[tile:tile + halo, :]


def _sample_front_kernel(sink_ref, x_ref, ck_ref, cv_ref, st_ref, wf_out_ref,
                         g_ref, b_ref, w_in_ref, w_pool_ref, pscale_ref, bias_c_ref, bias_n_ref,
                         mix_ref, nk_ref, nv_ref, np_ref, wb_out_ref, o_s):
    wb_out_ref[...] = wf_out_ref[...].astype(BF16)
    ns, tokens, d_model = x_ref.shape
    lane = lax.broadcasted_iota(jnp.int32, (1, LANES), 1)
    low = lane < HEAD_DIM

    x = jnp.concatenate([x_ref[:, t, :] for t in range(tokens)], axis=0)
    h = _layer_norm(x, g_ref[...], b_ref[...])
    proj = _dot(h.astype(BF16), w_in_ref[...])
    u = proj[:, 0:D_POOL]
    q = proj[:, D_POOL:D_POOL + D_ATTN] * Q_SCALE
    k = proj[:, D_POOL + D_ATTN:D_POOL + D_ATTN + D_KV]
    v = proj[:, D_POOL + D_ATTN + D_KV:]

    ext = [st_ref[r] for r in range(POOL_BUF)]
    ext += [u[t * ns:(t + 1) * ns] for t in range(tokens)]
    ps = []
    for t in range(tokens):
        idx = POOL_BUF + t
        pos = PAST_LEN - POOL_BUF + idx
        parts = []
        for g, w in enumerate(POOL_WINDOWS):
            cols = slice(g * POOL_GROUP_DIM, (g + 1) * POOL_GROUP_DIM)
            lo = max(idx + 1 - w, 0)
            s = ext[idx][:, cols]
            for r in range(idx - 1, lo - 1, -1):
                s = s + ext[r][:, cols]
            parts.append(s * (1.0 / min(w, pos + 1)) - ext[idx][:, cols])
        ps.append(jnp.concatenate(parts, axis=1))
    p = jnp.concatenate(ps, axis=0).astype(BF16)
    zs = [_dot(p[:, g * POOL_GROUP_DIM:(g + 1) * POOL_GROUP_DIM], w_pool_ref[g])
          for g in range(len(POOL_WINDOWS))]
    z = (jnp.concatenate(zs, axis=1) * pscale_ref[...]).astype(BF16)
    for r in range(POOL_BUF):
        np_ref[r] = ext[r + tokens]

    qh = []
    for hd in range(N_HEADS):
        tl = q[:, (hd // 2) * LANES:(hd // 2 + 1) * LANES]
        kv = hd // (N_HEADS // N_KV_HEADS)
        want_low = kv == 0
        is_low = hd % 2 == 0
        src = tl if want_low == is_low else pltpu.roll(tl, HEAD_DIM, axis=1)
        qh.append(jnp.where(low, src, 0.0) if want_low else jnp.where(low, 0.0, src))

    gr = SEQ_GROUP
    rows_per_head = tokens * gr
    sub = lax.broadcasted_iota(jnp.int32, (N_HEADS * rows_per_head, 1), 0) % gr
    sink = jnp.concatenate([jnp.full((rows_per_head, 1), sink_ref[0, hd], F32) for hd in range(N_HEADS)],
                           axis=0)
    zpad = jnp.zeros((LANES - rows_per_head, LANES), F32)
    ztop = jnp.zeros((WINDOW - SUBLANES, LANES), F32)
    sub8 = lax.broadcasted_iota(jnp.int32, (SUBLANES, 1), 0)
    keep = WINDOW - tokens

    def appended(old_ref, news, j):
        tail = jnp.zeros((SUBLANES, LANES), F32)
        for t in range(tokens):
            row = SUBLANES - tokens + t
            tail = jnp.where(sub8 == row, pltpu.roll(news[t], (row - j) % SUBLANES, axis=0), tail)
        cols = jnp.concatenate([ztop, tail], axis=0).T
        return jnp.where(lane >= keep, cols, pltpu.roll(old_ref[...], keep, axis=1))

    for base in range(0, ns, gr):
        k_t = [k[t * ns + base:t * ns + base + gr] for t in range(tokens)]
        v_t = [v[t * ns + base:t * ns + base + gr] for t in range(tokens)]
        for s in range(gr):
            nk_ref[base + s] = appended(ck_ref.at[base + s], k_t, s)
            nv_ref[base + s] = appended(cv_ref.at[base + s], v_t, s)

    for gi in range(ns // gr):
        base = gi * gr
        lhs = jnp.concatenate([qh[hd][t * ns + base:t * ns + base + gr]
                               for hd in range(N_HEADS) for t in range(tokens)], axis=0).astype(BF16)
        sc = None
        for s in range(gr):
            s_one = _dot(lhs, ck_ref[base + s].astype(BF16))
            sc = s_one if sc is None else jnp.where(sub == s, s_one, sc)
        sc = sc + bias_c_ref[...]
        k_t = [k[t * ns + base:t * ns + base + gr] for t in range(tokens)]
        v_t = [v[t * ns + base:t * ns + base + gr] for t in range(tokens)]
        k_new = jnp.concatenate(k_t + [zpad], axis=0)
        v_new = jnp.concatenate(v_t + [zpad], axis=0)
        sn = _dot_nt(lhs, k_new.astype(BF16)) + bias_n_ref[...]
        m = jnp.maximum(jnp.maximum(jnp.max(sc, axis=-1, keepdims=True),
                                    jnp.max(sn, axis=-1, keepdims=True)), sink)
        ec = jnp.exp(sc - m)
        en = jnp.exp(sn - m)
        den = (jnp.sum(ec, axis=-1, keepdims=True) + jnp.sum(en, axis=-1, keepdims=True)
               + jnp.exp(sink - m))
        o = _dot(en.astype(BF16), v_new.astype(BF16))
        for s in range(gr):
            o = o + _dot_nt(jnp.where(sub == s, ec, 0.0).astype(BF16), cv_ref[base + s].astype(BF16))
        o = o * (1.0 / den)
        for t in range(tokens):
            for p in range(N_HEADS // 2):
                a = o[(2 * p * tokens + t) * gr:(2 * p * tokens + t + 1) * gr]
                c = o[((2 * p + 1) * tokens + t) * gr:((2 * p + 1) * tokens + t + 1) * gr]
                if p < N_HEADS // 4:
                    c = pltpu.roll(c, HEAD_DIM, axis=1)
                else:
                    a = pltpu.roll(a, HEAD_DIM, axis=1)
                o_s[t * ns + base:t * ns + base + gr, p * LANES:(p + 1) * LANES] = jnp.where(low, a, c)

    for t in range(tokens):
        mix_ref[:, t * d_model:t * d_model + D_POOL] = z[t * ns:(t + 1) * ns]
        mix_ref[:, t * d_model + D_POOL:(t + 1) * d_model] = o_s[t * ns:(t + 1) * ns, :].astype(BF16)


def _pre_pieces(rows, res, mix, w_out_ref, g1_ref, b1_ref, h1_s, hb_s):
    h1_s[rows, :] = res() + _dot(mix(), w_out_ref[...])
    yield
    h = _layer_norm(h1_s[rows, :], g1_ref[...], b1_ref[...])
    h1_s[rows, :] = h
    hb_s[rows, :] = h.astype(BF16)
    yield


def _mlp_pieces(rows, a_s, h1_s, hb_s, w1_ref, w2_ref, g2_ref, b2_ref, y_ref):
    for c in range(w1_ref.shape[1] // UP_CHUNK):
        cols = slice(c * UP_CHUNK, (c + 1) * UP_CHUNK)
        a = _dot(hb_s[rows, :], w1_ref[:, cols])
        a_s[:, cols] = jnp.square(jnp.maximum(a, 0.0)).astype(BF16)
        if c == 0:
            y_ref[rows, :] = ALPHA * h1_s[rows, :]
        yield
    for n in range(y_ref.shape[1] // DOWN_CHUNK):
        cols = slice(n * DOWN_CHUNK, (n + 1) * DOWN_CHUNK)
        y_ref[rows, cols] = y_ref[rows, cols] + _dot(a_s[...], w2_ref[:, cols])
        yield
    y_ref[rows, :] = _layer_norm(y_ref[rows, :], g2_ref[...], b2_ref[...])
    yield


def _weave(main, n_main, sides):
    order = []
    for i in range(n_main):
        order.append(main)
        order += sides[i * len(sides) // n_main:(i + 1) * len(sides) // n_main]
    return order


def _main_kernel(tokens, tiles_per_seq, n_tiles,
                 sink_ref, x_ref, xs_ref, mixs_ref, ge_ref, be_ref, w_in_ref, w_pool_ref,
                 pscale_ref, bias_ref, meta_proj_ref, w_out_ref, g1_ref, b1_ref, g2_ref, b2_ref,
                 w1_hbm, w2_hbm,
                 y_ref, ys_ref, nk_ref, nv_ref, np_ref,
                 mixbuf, kvar, vvar, uext, qbuf, pbuf, hres, h1_s, hb_s, a_s, w1_ref, w2_ref, stage, sem):
    s = pl.program_id(0)
    is_first = s == 0
    first_tile = jnp.minimum(s, n_tiles - 1) % tiles_per_seq == 0
    tile, d_model = x_ref.shape
    ns = xs_ref.shape[0]
    half = tile // 2
    halves = (slice(0, half), slice(half, tile))
    pre_w = (w_out_ref, g1_ref, b1_ref, h1_s, hb_s)

    @pl.when(is_first)
    def _():
        def toks(rows):
            return range(rows.start // ns, rows.stop // ns)

        def sample_pre():
            for rows in halves:
                res = lambda: ALPHA * _layer_norm(jnp.concatenate([xs_ref[:, t, :] for t in toks(rows)], axis=0),
                                                  ge_ref[...], be_ref[...])
                mix = lambda: jnp.concatenate([mixs_ref[:, t * d_model:(t + 1) * d_model] for t in toks(rows)],
                                              axis=0)
                yield from _pre_pieces(rows, res, mix, *pre_w)

        cr, cc = stage.shape[1:]
        chunks = [(src, dst, r, c) for src, dst in ((w1_hbm, w1_ref), (w2_hbm, w2_ref))
                  for r in range(0, src.shape[0], cr) for c in range(0, src.shape[1], cc)]
        slots = stage.shape[0]

        def copy(i):
            src, _, r, c = chunks[i]
            return pltpu.make_async_copy(src.at[pl.ds(r, cr), pl.ds(c, cc)], stage.at[i % slots], sem.at[i % slots])

        for i in range(slots):
            copy(i).start()
        pre = sample_pre()
        every = len(chunks) // 4
        for i, (_, dst, r, c) in enumerate(chunks):
            if i % every == every // 2:
                next(pre, None)
            copy(i).wait()
            dst[r:r + cr, c:c + cc] = stage[i % slots].astype(BF16)
            if i + slots < len(chunks):
                copy(i + slots).start()
        for _ in pre:
            raise AssertionError("piece left unscheduled")

    @pl.when(first_tile)
    def _():
        _carry_init(meta_proj_ref, kvar, vvar, uext)

    mlp = [_mlp_pieces(rows, a_s, h1_s, hb_s, w1_ref, w2_ref, g2_ref, b2_ref, y_ref) for rows in halves]
    front = _front_pieces(first_tile, x_ref, sink_ref, ge_ref, be_ref, w_in_ref, w_pool_ref, pscale_ref,
                          bias_ref, mixbuf, nk_ref.at[0], nv_ref.at[0], np_ref.at[0], kvar, vvar, uext, qbuf, pbuf,
                          hres)
    pre = [_pre_pieces(rows, lambda rows=rows: hres[rows, :], lambda rows=rows: mixbuf[rows, :], *pre_w)
           for rows in halves]
    a, b, f, pa, pb = mlp[0], mlp[1], front, pre[0], pre[1]
    n_up = w1_ref.shape[1] // UP_CHUNK
    n_down = d_model // DOWN_CHUNK
    order = (
        _weave(a, n_up, [f] * 4)
        + _weave(a, n_down, [f] * 5)
        + _weave(b, n_up, [a] + [f] * 5)
        + _weave(b, n_down, [f, pa, pb, pa, pb]) + [b])
    for g in order:
        next(g, None)
    for g in (a, b, f, pa, pb):
        for _ in g:
            raise AssertionError("piece left unscheduled")

    @pl.when(is_first)
    def _():
        for t in range(tokens):
            ys_ref[:, t, :] = y_ref[t * ns:(t + 1) * ns, :]


def _const_spec(shape):
    return pl.BlockSpec(shape, lambda *_: (0,) * len(shape), pipeline_mode=pl.Buffered(1))


def _smem_spec():
    return pl.BlockSpec(memory_space=pltpu.SMEM)


def _prologue_call(tokens, rel_table, meta_tokens, ge, be, w_in, w_pool):
    bkt_p, bkt_c, bkt_n = _bucket_tables(tokens)
    srows = N_HEADS * tokens * SEQ_GROUP
    return pl.pallas_call(
        _prologue_kernel,
        in_specs=[_smem_spec()] + [pl.BlockSpec(memory_space=pltpu.VMEM)] * 8,
        out_specs=[pl.BlockSpec(memory_space=pltpu.VMEM)] * 7,
        out_shape=[jax.ShapeDtypeStruct((2, N_HEADS, BLOCK, 2 * BLOCK), F32),
                   jax.ShapeDtypeStruct((srows, WINDOW), F32),
                   jax.ShapeDtypeStruct((srows, LANES), F32),
                   jax.ShapeDtypeStruct((meta_tokens.shape[0], w_in.shape[1]), F32),
                   jax.ShapeDtypeStruct(w_in.shape, BF16),
                   jax.ShapeDtypeStruct(w_pool.shape, BF16),
                   jax.ShapeDtypeStruct((D_POOL, D_POOL), BF16)],
        compiler_params=pltpu.CompilerParams(vmem_limit_bytes=VMEM_LIMIT_BYTES),
        name="prologue",
    )(rel_table, jnp.asarray(bkt_p), jnp.asarray(bkt_c), jnp.asarray(bkt_n), meta_tokens, ge, be, w_in, w_pool)


def _sample_front_call(sink, xs, ck, cv, st, weights, *consts):
    n_seq, tokens, d_model = xs.shape
    steps = n_seq // SAMPLE_SEQS
    seq_spec = lambda *tail: pl.BlockSpec((SAMPLE_SEQS,) + tail, lambda i: (i,) + (0,) * len(tail))
    state_spec = pl.BlockSpec((POOL_BUF, SAMPLE_SEQS, D_POOL), lambda i: (0, i, 0))
    slab_specs = [pl.BlockSpec((w.shape[0] // steps, w.shape[1]), lambda i: (i, 0)) for w in weights]
    return pl.pallas_call(
        _sample_front_kernel,
        grid=(steps,),
        in_specs=[_smem_spec(), seq_spec(tokens, d_model), seq_spec(D_KV, WINDOW), seq_spec(D_KV, WINDOW),
                  state_spec] + slab_specs + [_const_spec(c.shape) for c in consts],
        out_specs=[seq_spec(tokens * d_model), seq_spec(D_KV, WINDOW), seq_spec(D_KV, WINDOW), state_spec]
        + slab_specs,
        out_shape=[jax.ShapeDtypeStruct((n_seq, tokens * d_model), BF16),
                   jax.ShapeDtypeStruct((n_seq, D_KV, WINDOW), F32),
                   jax.ShapeDtypeStruct((n_seq, D_KV, WINDOW), F32),
                   jax.ShapeDtypeStruct((POOL_BUF, n_seq, D_POOL), F32)]
        + [jax.ShapeDtypeStruct(w.shape, BF16) for w in weights],
        scratch_shapes=[pltpu.VMEM((tokens * SAMPLE_SEQS, D_ATTN), F32)],
        compiler_params=pltpu.CompilerParams(dimension_semantics=("arbitrary",),
                                             vmem_limit_bytes=VMEM_LIMIT_BYTES),
        name="sample_front",
    )(sink, xs, ck, cv, st, *weights, *consts)


def kernel(x_prompt, x_sample, cache_win_k, cache_win_v, state_pool, meta_tokens, ln_emb_g, ln_emb_b,
           rel_table, w_in, w_pool, pool_scale, sinks, w_out, ln1_g, ln1_b, w_mlp_in, w_mlp_out,
           ln2_g, ln2_b):
    batch, seq, d_model = x_prompt.shape
    n_seq, tokens, _ = x_sample.shape
    assert w_in.shape[0] == DEPTH and d_model == D_POOL + D_ATTN
    assert seq % PROMPT_TILE == 0 and n_seq % SAMPLE_SEQS == 0 and tokens * n_seq == PROMPT_TILE
    assert cache_win_k.shape[2] == WINDOW and state_pool.shape[2] == POOL_BUF and tokens <= POOL_BUF
    assert N_META == HALO and POOL_BUF <= HALO

    row = lambda a: a.reshape(1, -1).astype(F32)
    ge, be = row(ln_emb_g), row(ln_emb_b)
    pscale = row(pool_scale[0])
    sink = row(sinks[0])

    bias_p, bias_c, bias_n, meta_proj, w_in_b, w_pool_b, w_pool_bd = _prologue_call(
        tokens, rel_table.astype(F32), meta_tokens.astype(F32), ge, be, w_in[0], w_pool[0])

    to_kd_pos = lambda c: jnp.swapaxes(c[0].reshape(n_seq, WINDOW, D_KV), 1, 2)
    mix_s, nk_s, nv_s, np_s, w_out_b = _sample_front_call(
        sink, x_sample, to_kd_pos(cache_win_k), to_kd_pos(cache_win_v), jnp.swapaxes(state_pool[0], 0, 1),
        (w_out[0],), ge, be, w_in_b, w_pool_b, pscale, bias_c, bias_n)
    xs = x_sample
    w1, w2 = w_mlp_in[0], w_mlp_out[0]

    tile = PROMPT_TILE
    tiles_per_seq = seq // tile
    n_tiles = batch * tiles_per_seq
    xp = x_prompt.reshape(batch * seq, d_model)
    front_tile = lambda s: jnp.minimum(s, n_tiles - 1)
    finish_tile = lambda s: jnp.maximum(s - 1, 0)
    seq_of = lambda s: front_tile(s) // tiles_per_seq
    consts = (ge, be, w_in_b, w_pool_bd, pscale, bias_p, meta_proj, w_out_b, row(ln1_g[0]), row(ln1_b[0]),
              row(ln2_g[0]), row(ln2_b[0]))
    y_p, y_s, nk_p, nv_p, np_p = pl.pallas_call(
        functools.partial(_main_kernel, tokens, tiles_per_seq, n_tiles),
        grid=(n_tiles + 1,),
        in_specs=[_smem_spec(),
                  pl.BlockSpec((tile, d_model), lambda s: (front_tile(s), 0)),
                  _const_spec(xs.shape), _const_spec(mix_s.shape)] + [_const_spec(c.shape) for c in consts]
        + [pl.BlockSpec(memory_space=pl.ANY)] * 2,
        out_specs=[pl.BlockSpec((tile, d_model), lambda s: (finish_tile(s), 0)),
                   pl.BlockSpec(xs.shape, lambda s: (0, 0, 0)),
                   pl.BlockSpec((1, BLOCK, D_KV), lambda s: (seq_of(s), 0, 0)),
                   pl.BlockSpec((1, BLOCK, D_KV), lambda s: (seq_of(s), 0, 0)),
                   pl.BlockSpec((1, HALO, D_POOL), lambda s: (seq_of(s), 0, 0))],
        out_shape=[jax.ShapeDtypeStruct((batch * seq, d_model), F32),
                   jax.ShapeDtypeStruct(xs.shape, F32),
                   jax.ShapeDtypeStruct((batch, BLOCK, D_KV), F32),
                   jax.ShapeDtypeStruct((batch, BLOCK, D_KV), F32),
                   jax.ShapeDtypeStruct((batch, HALO, D_POOL), F32)],
        scratch_shapes=[pltpu.VMEM((tile, d_model), BF16),
                        pltpu.VMEM((4, tile + BLOCK, LANES), BF16),
                        pltpu.VMEM((4, tile + BLOCK, LANES), BF16),
                        pltpu.VMEM((tile + HALO, D_POOL), F32),
                        pltpu.VMEM((tile, D_ATTN), BF16),
                        pltpu.VMEM((tile, D_POOL), BF16),
                        pltpu.VMEM((tile, d_model), F32),
                        pltpu.VMEM((tile, d_model), F32),
                        pltpu.VMEM((tile, d_model), BF16),
                        pltpu.VMEM((tile // 2, w1.shape[1]), BF16),
                        pltpu.VMEM(w1.shape, BF16),
                        pltpu.VMEM(w2.shape, BF16),
                        pltpu.VMEM((WEIGHT_SLOTS,) + WEIGHT_CHUNK, F32),
                        pltpu.SemaphoreType.DMA((WEIGHT_SLOTS,))],
        compiler_params=pltpu.CompilerParams(dimension_semantics=("arbitrary",),
                                             vmem_limit_bytes=VMEM_LIMIT_BYTES),
        name="main",
    )(sink, xp, xs, mix_s, *consts, w1, w2)

    kv_shape = (DEPTH, -1, WINDOW, N_KV_HEADS, HEAD_DIM)
    from_kd_pos = lambda c: jnp.swapaxes(c, 1, 2).reshape(kv_shape)
    return (y_p.reshape(batch, seq, d_model), y_s,
            from_kd_pos(nk_p), from_kd_pos(nv_p),
            np_p[:, HALO - POOL_BUF:, :].reshape(DEPTH, batch, POOL_BUF, D_POOL),
            from_kd_pos(nk_s), from_kd_pos(nv_s), jnp.swapaxes(np_s, 0, 1)[None])
```

```python
import functools
import math

import jax
import jax.numpy as jnp
import numpy as np
from jax import lax
from jax.experimental import pallas as pl
from jax.experimental.pallas import tpu as pltpu

N_META = 16
POOL_WINDOWS = (2, 4, 8, 16)
POOL_GROUP_DIM = 128
D_POOL = len(POOL_WINDOWS) * POOL_GROUP_DIM
POOL_BUF = max(POOL_WINDOWS) - 1
N_HEADS = 8
HEAD_DIM = 64
D_ATTN = N_HEADS * HEAD_DIM
N_KV_HEADS = 2
D_KV = N_KV_HEADS * HEAD_DIM
WINDOW = 128
BLOCK = 128
REL_BUCKETS = 32
REL_MAX_DIST = 128
PAST_LEN = 8192
DEPTH = 1
ALPHA = (2.0 * DEPTH) ** 0.25
LN_EPS = 1e-5
Q_SCALE = HEAD_DIM ** -0.5

LANES = 128
SUBLANES = 8
VMEM_LIMIT_BYTES = 60 * 1024 * 1024

PROMPT_TILE = 512
IN_ROWS = 256
UP_CHUNK = 512
DOWN_CHUNK = 256
SAMPLE_SEQS = 32
SEQ_GROUP = SUBLANES
HALO = 2 * SUBLANES
WEIGHT_CHUNK = (256, 1024)
WEIGHT_SLOTS = 2

F32 = jnp.float32
BF16 = jnp.bfloat16
NEG_INF = float("-inf")


def _rel_bucket(dist):
    n = np.maximum(dist, 0)
    max_exact = REL_BUCKETS // 2
    nf = np.maximum(n, 1).astype(np.float64)
    large = max_exact + (np.log(nf / max_exact) / math.log(REL_MAX_DIST / max_exact)
                         * (REL_BUCKETS - max_exact)).astype(np.int32)
    large = np.minimum(large, REL_BUCKETS - 1)
    return np.where(n < max_exact, n, large).astype(np.int32)


def _bucket_tables(tokens):
    prompt = np.tile(_rel_bucket((BLOCK - np.arange(BLOCK)) % BLOCK)[None, :], (SUBLANES, 1)).astype(np.int32)
    t = np.repeat(np.arange(tokens), SEQ_GROUP)
    s = np.tile(np.arange(SEQ_GROUP), tokens)
    dist = (t[:, None] + WINDOW) - np.arange(WINDOW)[None, :]
    ok = (dist >= 0) & (dist < WINDOW)
    cache = np.where(ok, _rel_bucket(dist), -1).astype(np.int32)
    dist = t[:, None] - t[None, :]
    ok = (dist >= 0) & (s[:, None] == s[None, :])
    new = np.full((tokens * SEQ_GROUP, LANES), -1, np.int32)
    new[:, :tokens * SEQ_GROUP] = np.where(ok, _rel_bucket(dist), -1)
    return prompt, cache, new


def _layer_norm(x, g, b):
    mu = jnp.mean(x, axis=-1, keepdims=True)
    xc = x - mu
    var = jnp.mean(xc * xc, axis=-1, keepdims=True)
    return xc * lax.rsqrt(var + LN_EPS) * g + b


def _dot(a, b):
    return jnp.dot(a, b, preferred_element_type=F32)


def _dot_nt(a, b):
    return lax.dot_general(a, b, (((1,), (1,)), ((), ())), preferred_element_type=F32)


def _kv_lane_variants(x, low):
    xr = pltpu.roll(x, HEAD_DIM, axis=1)
    zero = jnp.zeros_like(x)
    return (jnp.where(low, x, zero).astype(BF16), jnp.where(low, zero, xr).astype(BF16),
            jnp.where(low, xr, zero).astype(BF16), jnp.where(low, zero, x).astype(BF16))


def _prologue_kernel(tab_ref, bkt_p_ref, bkt_c_ref, bkt_n_ref, meta_ref, g_ref, b_ref, wf_in_ref, wf_pool_ref,
                     bias_p_ref, bias_c_ref, bias_n_ref, meta_proj_ref, w_in_ref, w_pool_ref, w_pool_bd_ref):
    w_in_ref[...] = wf_in_ref[...].astype(BF16)
    w_pool_ref[...] = wf_pool_ref[...].astype(BF16)
    w_pool_bd_ref[...] = jnp.zeros_like(w_pool_bd_ref)
    for g in range(len(POOL_WINDOWS)):
        cols = slice(g * POOL_GROUP_DIM, (g + 1) * POOL_GROUP_DIM)
        w_pool_bd_ref[cols, cols] = w_pool_ref[g]
    def lookup(bucket, h):
        def body(i, acc):
            return jnp.where(bucket == i, tab_ref[i, h], acc)
        return lax.fori_loop(0, REL_BUCKETS, body, jnp.full(bucket.shape, NEG_INF, F32))

    rows = bkt_c_ref.shape[0]
    q = lax.broadcasted_iota(jnp.int32, (BLOCK, BLOCK), 0)
    k = lax.broadcasted_iota(jnp.int32, (BLOCK, BLOCK), 1)
    for h in range(N_HEADS):
        per_dist = lookup(bkt_p_ref[...], h)
        toeplitz = pltpu.roll(jnp.concatenate([per_dist] * (BLOCK // SUBLANES), axis=0), 0, 1,
                              stride=1, stride_axis=0)
        cur = jnp.where(k <= q, toeplitz, NEG_INF)
        for table, prev_ok in enumerate(((k > q) & (k >= BLOCK - N_META), k > q)):
            bias_p_ref[table, h, :, 0:BLOCK] = jnp.where(prev_ok, toeplitz, NEG_INF)
            bias_p_ref[table, h, :, BLOCK:] = cur
        bias_c_ref[h * rows:(h + 1) * rows, :] = lookup(bkt_c_ref[...], h)
        bias_n_ref[h * rows:(h + 1) * rows, :] = lookup(bkt_n_ref[...], h)
    hm = _layer_norm(meta_ref[...], g_ref[...], b_ref[...])
    meta_proj_ref[...] = _dot(hm.astype(BF16), w_in_ref[...])


def _carry_init(meta_proj_ref, kvar, vvar, uext):
    low = lax.broadcasted_iota(jnp.int32, (1, LANES), 1) < HEAD_DIM
    pad = jnp.zeros((BLOCK - N_META, LANES), BF16)
    kq = _kv_lane_variants(meta_proj_ref[:, D_POOL + D_ATTN:D_POOL + D_ATTN + D_KV], low)
    vq = _kv_lane_variants(meta_proj_ref[:, D_POOL + D_ATTN + D_KV:], low)
    for i in range(4):
        kvar[i, 0:BLOCK, :] = jnp.concatenate([pad, kq[i]], axis=0)
        vvar[i, 0:BLOCK, :] = jnp.concatenate([pad, vq[i]], axis=0)
    uext[0:HALO, :] = meta_proj_ref[:, 0:D_POOL]


def _front_pieces(first_tile, x_ref, sink_ref, g_ref, b_ref, w_in_ref, w_pool_ref, pscale_ref, bias_ref,
                  mix_ref, nk_ref, nv_ref, np_ref, kvar, vvar, uext, qbuf, pbuf, hres):
    tile = x_ref.shape[0]
    halo = HALO
    low = lax.broadcasted_iota(jnp.int32, (1, LANES), 1) < HEAD_DIM

    for j in range(tile // IN_ROWS):
        rows = slice(j * IN_ROWS, (j + 1) * IN_ROWS)
        h = _layer_norm(x_ref[rows, :], g_ref[...], b_ref[...])
        hres[rows, :] = ALPHA * h
        proj = _dot(h.astype(BF16), w_in_ref[...])
        u = proj[:, 0:D_POOL]
        k = proj[:, D_POOL + D_ATTN:D_POOL + D_ATTN + D_KV]
        v = proj[:, D_POOL + D_ATTN + D_KV:]
        qbuf[rows, :] = (proj[:, D_POOL:D_POOL + D_ATTN] * Q_SCALE).astype(BF16)
        uext[halo + rows.start:halo + rows.stop, :] = u
        kq = _kv_lane_variants(k, low)
        vq = _kv_lane_variants(v, low)
        for i in range(4):
            kvar[i, BLOCK + rows.start:BLOCK + rows.stop, :] = kq[i]
            vvar[i, BLOCK + rows.start:BLOCK + rows.stop, :] = vq[i]
        if rows.stop == tile:
            nk_ref[...] = k[IN_ROWS - BLOCK:, :].T
            nv_ref[...] = v[IN_ROWS - BLOCK:, :].T
            np_ref[...] = u[IN_ROWS - halo:, :]
        yield

    for g, w in enumerate(POOL_WINDOWS):
        cols = slice(g * POOL_GROUP_DIM, (g + 1) * POOL_GROUP_DIM)
        ug = uext[halo:halo + tile, cols]
        s = ug
        for i in range(1, w):
            s = s + uext[halo - i:halo - i + tile, cols]
        pbuf[:, cols] = (s * (1.0 / w) - ug).astype(BF16)
        if g == len(POOL_WINDOWS) - 1:
            mix_ref[:, 0:D_POOL] = (_dot(pbuf[...], w_pool_ref[...]) * pscale_ref[...]).astype(BF16)
        yield

    for j in range(tile // BLOCK):
        rows = slice(j * BLOCK, (j + 1) * BLOCK)
        keys = slice(j * BLOCK, (j + 2) * BLOCK)
        sel = jnp.where(first_tile, 0, 1) if j == 0 else 1
        for kv in range(N_KV_HEADS):
            tiles = (2 * kv, 2 * kv + 1)
            lhs = jnp.concatenate([qbuf[rows, p * LANES:(p + 1) * LANES] for p in tiles], axis=0)
            nk = 2 * BLOCK
            s_all = _dot_nt(lhs, jnp.concatenate([kvar[2 * kv, keys, :], kvar[2 * kv + 1, keys, :]], axis=0))
            es, dens = [], []
            for half in range(2):
                heads = [2 * p + half for p in tiles]
                s = s_all[:, half * nk:(half + 1) * nk]
                s = s + jnp.concatenate([bias_ref[sel, hd] for hd in heads], axis=0)
                sink = jnp.concatenate([jnp.full((BLOCK, 1), sink_ref[0, hd], F32) for hd in heads], axis=0)
                m = jnp.maximum(jnp.max(s, axis=-1, keepdims=True), sink)
                e = jnp.exp(s - m)
                dens.append(jnp.sum(e, axis=-1, keepdims=True) + jnp.exp(sink - m))
                es.append(e.astype(BF16))
            o = _dot(jnp.concatenate(es, axis=0),
                     jnp.concatenate([vvar[2 * kv, keys, :], vvar[2 * kv + 1, keys, :]], axis=1))
            acc = (o[0:nk, 0:LANES] * (1.0 / dens[0]) + o[nk:, LANES:] * (1.0 / dens[1]))
            for n, p in enumerate(tiles):
                mix_ref[rows, D_POOL + p * LANES:D_POOL + (p + 1) * LANES] = (
                    acc[n * BLOCK:(n + 1) * BLOCK].astype(BF16))
            yield

    for i in range(4):
        kvar[i, 0:BLOCK, :] = kvar[i, tile:tile + BLOCK, :]
        vvar[i, 0:BLOCK, :] = vvar[i, tile:tile + BLOCK, :]
    uext[0:halo, :] = uext[tile:tile + halo, :]


def _sample_front_kernel(sink_ref, x_ref, ck_ref, cv_ref, st_ref, wf_out_ref,
                         g_ref, b_ref, w_in_ref, w_pool_ref, pscale_ref, bias_c_ref, bias_n_ref,
                         mix_ref, nk_ref, nv_ref, np_ref, wb_out_ref, o_s):
    wb_out_ref[...] = wf_out_ref[...].astype(BF16)
    ns, tokens, d_model = x_ref.shape
    lane = lax.broadcasted_iota(jnp.int32, (1, LANES), 1)
    low = lane < HEAD_DIM

    x = jnp.concatenate([x_ref[:, t, :] for t in range(tokens)], axis=0)
    h = _layer_norm(x, g_ref[...], b_ref[...])
    proj = _dot(h.astype(BF16), w_in_ref[...])
    u = proj[:, 0:D_POOL]
    q = proj[:, D_POOL:D_POOL + D_ATTN] * Q_SCALE
    k = proj[:, D_POOL + D_ATTN:D_POOL + D_ATTN + D_KV]
    v = proj[:, D_POOL + D_ATTN + D_KV:]

    ext = [st_ref[r] for r in range(POOL_BUF)]
    ext += [u[t * ns:(t + 1) * ns] for t in range(tokens)]
    ps = []
    for t in range(tokens):
        idx = POOL_BUF + t
        pos = PAST_LEN - POOL_BUF + idx
        parts = []
        for g, w in enumerate(POOL_WINDOWS):
            cols = slice(g * POOL_GROUP_DIM, (g + 1) * POOL_GROUP_DIM)
            lo = max(idx + 1 - w, 0)
            s = ext[idx][:, cols]
            for r in range(idx - 1, lo - 1, -1):
                s = s + ext[r][:, cols]
            parts.append(s * (1.0 / min(w, pos + 1)) - ext[idx][:, cols])
        ps.append(jnp.concatenate(parts, axis=1))
    p = jnp.concatenate(ps, axis=0).astype(BF16)
    zs = [_dot(p[:, g * POOL_GROUP_DIM:(g + 1) * POOL_GROUP_DIM], w_pool_ref[g])
          for g in range(len(POOL_WINDOWS))]
    z = (jnp.concatenate(zs, axis=1) * pscale_ref[...]).astype(BF16)
    for r in range(POOL_BUF):
        np_ref[r] = ext[r + tokens]

    qh = []
    for hd in range(N_HEADS):
        tl = q[:, (hd // 2) * LANES:(hd // 2 + 1) * LANES]
        kv = hd // (N_HEADS // N_KV_HEADS)
        want_low = kv == 0
        is_low = hd % 2 == 0
        src = tl if want_low == is_low else pltpu.roll(tl, HEAD_DIM, axis=1)
        qh.append(jnp.where(low, src, 0.0) if want_low else jnp.where(low, 0.0, src))

    gr = SEQ_GROUP
    rows_per_head = tokens * gr
    sub = lax.broadcasted_iota(jnp.int32, (N_HEADS * rows_per_head, 1), 0) % gr
    sink = jnp.concatenate([jnp.full((rows_per_head, 1), sink_ref[0, hd], F32) for hd in range(N_HEADS)],
                           axis=0)
    zpad = jnp.zeros((LANES - rows_per_head, LANES), F32)
    ztop = jnp.zeros((WINDOW - SUBLANES, LANES), F32)
    sub8 = lax.broadcasted_iota(jnp.int32, (SUBLANES, 1), 0)
    keep = WINDOW - tokens

    def appended(old_ref, news, j):
        tail = jnp.zeros((SUBLANES, LANES), F32)
        for t in range(tokens):
            row = SUBLANES - tokens + t
            tail = jnp.where(sub8 == row, pltpu.roll(news[t], (row - j) % SUBLANES, axis=0), tail)
        cols = jnp.concatenate([ztop, tail], axis=0).T
        return jnp.where(lane >= keep, cols, pltpu.roll(old_ref[...], keep, axis=1))

    for base in range(0, ns, gr):
        k_t = [k[t * ns + base:t * ns + base + gr] for t in range(tokens)]
        v_t = [v[t * ns + base:t * ns + base + gr] for t in range(tokens)]
        for s in range(gr):
            nk_ref[base + s] = appended(ck_ref.at[base + s], k_t, s)
            nv_ref[base + s] = appended(cv_ref.at[base + s], v_t, s)

    for gi in range(ns // gr):
        base = gi * gr
        lhs = jnp.concatenate([qh[hd][t * ns + base:t * ns + base + gr]
                               for hd in range(N_HEADS) for t in range(tokens)], axis=0).astype(BF16)
        sc = None
        for s in range(gr):
            s_one = _dot(lhs, ck_ref[base + s].astype(BF16))
            sc = s_one if sc is None else jnp.where(sub == s, s_one, sc)
        sc = sc + bias_c_ref[...]
        k_t = [k[t * ns + base:t * ns + base + gr] for t in range(tokens)]
        v_t = [v[t * ns + base:t * ns + base + gr] for t in range(tokens)]
        k_new = jnp.concatenate(k_t + [zpad], axis=0)
        v_new = jnp.concatenate(v_t + [zpad], axis=0)
        sn = _dot_nt(lhs, k_new.astype(BF16)) + bias_n_ref[...]
        m = jnp.maximum(jnp.maximum(jnp.max(sc, axis=-1, keepdims=True),
                                    jnp.max(sn, axis=-1, keepdims=True)), sink)
        ec = jnp.exp(sc - m)
        en = jnp.exp(sn - m)
        den = (jnp.sum(ec, axis=-1, keepdims=True) + jnp.sum(en, axis=-1, keepdims=True)
               + jnp.exp(sink - m))
        o = _dot(en.astype(BF16), v_new.astype(BF16))
        for s in range(gr):
            o = o + _dot_nt(jnp.where(sub == s, ec, 0.0).astype(BF16), cv_ref[base + s].astype(BF16))
        o = o * (1.0 / den)
        for t in range(tokens):
            for p in range(N_HEADS // 2):
                a = o[(2 * p * tokens + t) * gr:(2 * p * tokens + t + 1) * gr]
                c = o[((2 * p + 1) * tokens + t) * gr:((2 * p + 1) * tokens + t + 1) * gr]
                if p < N_HEADS // 4:
                    c = pltpu.roll(c, HEAD_DIM, axis=1)
                else:
                    a = pltpu.roll(a, HEAD_DIM, axis=1)
                o_s[t * ns + base:t * ns + base + gr, p * LANES:(p + 1) * LANES] = jnp.where(low, a, c)

    for t in range(tokens):
        mix_ref[:, t * d_model:t * d_model + D_POOL] = z[t * ns:(t + 1) * ns]
        mix_ref[:, t * d_model + D_POOL:(t + 1) * d_model] = o_s[t * ns:(t + 1) * ns, :].astype(BF16)


def _pre_pieces(rows, res, mix, w_out_ref, g1_ref, b1_ref, h1_s, hb_s):
    h1_s[rows, :] = res() + _dot(mix(), w_out_ref[...])
    yield
    h = _layer_norm(h1_s[rows, :], g1_ref[...], b1_ref[...])
    h1_s[rows, :] = h
    hb_s[rows, :] = h.astype(BF16)
    yield


def _mlp_pieces(rows, a_s, h1_s, hb_s, w1_ref, w2_ref, g2_ref, b2_ref, y_ref):
    for c in range(w1_ref.shape[1] // UP_CHUNK):
        cols = slice(c * UP_CHUNK, (c + 1) * UP_CHUNK)
        a = _dot(hb_s[rows, :], w1_ref[:, cols])
        a_s[:, cols] = jnp.square(jnp.maximum(a, 0.0)).astype(BF16)
        if c == 0:
            y_ref[rows, :] = ALPHA * h1_s[rows, :]
        yield
    for n in range(y_ref.shape[1] // DOWN_CHUNK):
        cols = slice(n * DOWN_CHUNK, (n + 1) * DOWN_CHUNK)
        y_ref[rows, cols] = y_ref[rows, cols] + _dot(a_s[...], w2_ref[:, cols])
        yield
    y_ref[rows, :] = _layer_norm(y_ref[rows, :], g2_ref[...], b2_ref[...])
    yield


def _weave(main, n_main, sides):
    order = []
    for i in range(n_main):
        order.append(main)
        order += sides[i * len(sides) // n_main:(i + 1) * len(sides) // n_main]
    return order


def _main_kernel(tokens, tiles_per_seq, n_tiles,
                 sink_ref, x_ref, xs_ref, mixs_ref, ge_ref, be_ref, w_in_ref, w_pool_ref,
                 pscale_ref, bias_ref, meta_proj_ref, w_out_ref, g1_ref, b1_ref, g2_ref, b2_ref,
                 w1_hbm, w2_hbm,
                 y_ref, ys_ref, nk_ref, nv_ref, np_ref,
                 mixbuf, kvar, vvar, uext, qbuf, pbuf, hres, h1_s, hb_s, a_s, w1_ref, w2_ref, stage, sem):
    s = pl.program_id(0)
    is_first = s == 0
    first_tile = jnp.minimum(s, n_tiles - 1) % tiles_per_seq == 0
    tile, d_model = x_ref.shape
    ns = xs_ref.shape[0]
    half = tile // 2
    halves = (slice(0, half), slice(half, tile))
    pre_w = (w_out_ref, g1_ref, b1_ref, h1_s, hb_s)

    @pl.when(is_first)
    def _():
        def toks(rows):
            return range(rows.start // ns, rows.stop // ns)

        def sample_pre():
            for rows in halves:
                res = lambda: ALPHA * _layer_norm(jnp.concatenate([xs_ref[:, t, :] for t in toks(rows)], axis=0),
                                                  ge_ref[...], be_ref[...])
                mix = lambda: jnp.concatenate([mixs_ref[:, t * d_model:(t + 1) * d_model] for t in toks(rows)],
                                              axis=0)
                yield from _pre_pieces(rows, res, mix, *pre_w)

        cr, cc = stage.shape[1:]
        chunks = [(src, dst, r, c) for src, dst in ((w1_hbm, w1_ref), (w2_hbm, w2_ref))
                  for r in range(0, src.shape[0], cr) for c in range(0, src.shape[1], cc)]
        ring = [stage.at[i] for i in range(stage.shape[0])]
        ring += [buf.at[pl.ds(r, cr)] for buf in (hres, y_ref) for r in range(0, tile, cr)]
        slots = len(ring)

        def copy(i):
            src, _, r, c = chunks[i]
            return pltpu.make_async_copy(src.at[pl.ds(r, cr), pl.ds(c, cc)], ring[i % slots], sem.at[i % slots])

        for i in range(slots):
            copy(i).start()
        pre = sample_pre()
        every = len(chunks) // 4
        for i, (_, dst, r, c) in enumerate(chunks):
            if i % every == every // 2:
                next(pre, None)
            copy(i).wait()
            dst[r:r + cr, c:c + cc] = ring[i % slots][...].astype(BF16)
            if i + slots < len(chunks):
                copy(i + slots).start()
        for _ in pre:
            raise AssertionError("piece left unscheduled")

    @pl.when(first_tile)
    def _():
        _carry_init(meta_proj_ref, kvar, vvar, uext)

    mlp = [_mlp_pieces(rows, a_s, h1_s, hb_s, w1_ref, w2_ref, g2_ref, b2_ref, y_ref) for rows in halves]
    front = _front_pieces(first_tile, x_ref, sink_ref, ge_ref, be_ref, w_in_ref, w_pool_ref, pscale_ref,
                          bias_ref, mixbuf, nk_ref.at[0], nv_ref.at[0], np_ref.at[0], kvar, vvar, uext, qbuf, pbuf,
                          hres)
    pre = [_pre_pieces(rows, lambda rows=rows: hres[rows, :], lambda rows=rows: mixbuf[rows, :], *pre_w)
           for rows in halves]
    a, b, f, pa, pb = mlp[0], mlp[1], front, pre[0], pre[1]
    n_up = w1_ref.shape[1] // UP_CHUNK
    n_down = d_model // DOWN_CHUNK
    order = (
        _weave(a, n_up, [f] * 4)
        + _weave(a, n_down, [f] * 5)
        + _weave(b, n_up, [a] + [f] * 5)
        + _weave(b, n_down, [f, pa, pb, pa, pb]) + [b])
    for g in order:
        next(g, None)
    for g in (a, b, f, pa, pb):
        for _ in g:
            raise AssertionError("piece left unscheduled")

    @pl.when(is_first)
    def _():
        for t in range(tokens):
            ys_ref[:, t, :] = y_ref[t * ns:(t + 1) * ns, :]


def _const_spec(shape):
    return pl.BlockSpec(shape, lambda *_: (0,) * len(shape), pipeline_mode=pl.Buffered(1))


def _smem_spec():
    return pl.BlockSpec(memory_space=pltpu.SMEM)


def _prologue_call(tokens, rel_table, meta_tokens, ge, be, w_in, w_pool):
    bkt_p, bkt_c, bkt_n = _bucket_tables(tokens)
    srows = N_HEADS * tokens * SEQ_GROUP
    return pl.pallas_call(
        _prologue_kernel,
        in_specs=[_smem_spec()] + [pl.BlockSpec(memory_space=pltpu.VMEM)] * 8,
        out_specs=[pl.BlockSpec(memory_space=pltpu.VMEM)] * 7,
        out_shape=[jax.ShapeDtypeStruct((2, N_HEADS, BLOCK, 2 * BLOCK), F32),
                   jax.ShapeDtypeStruct((srows, WINDOW), F32),
                   jax.ShapeDtypeStruct((srows, LANES), F32),
                   jax.ShapeDtypeStruct((meta_tokens.shape[0], w_in.shape[1]), F32),
                   jax.ShapeDtypeStruct(w_in.shape, BF16),
                   jax.ShapeDtypeStruct(w_pool.shape, BF16),
                   jax.ShapeDtypeStruct((D_POOL, D_POOL), BF16)],
        compiler_params=pltpu.CompilerParams(vmem_limit_bytes=VMEM_LIMIT_BYTES),
        name="prologue",
    )(rel_table, jnp.asarray(bkt_p), jnp.asarray(bkt_c), jnp.asarray(bkt_n), meta_tokens, ge, be, w_in, w_pool)


def _sample_front_call(sink, xs, ck, cv, st, weights, *consts):
    n_seq, tokens, d_model = xs.shape
    steps = n_seq // SAMPLE_SEQS
    seq_spec = lambda *tail: pl.BlockSpec((SAMPLE_SEQS,) + tail, lambda i: (i,) + (0,) * len(tail))
    state_spec = pl.BlockSpec((POOL_BUF, SAMPLE_SEQS, D_POOL), lambda i: (0, i, 0))
    slab_specs = [pl.BlockSpec((w.shape[0] // steps, w.shape[1]), lambda i: (i, 0)) for w in weights]
    return pl.pallas_call(
        _sample_front_kernel,
        grid=(steps,),
        in_specs=[_smem_spec(), seq_spec(tokens, d_model), seq_spec(D_KV, WINDOW), seq_spec(D_KV, WINDOW),
                  state_spec] + slab_specs + [_const_spec(c.shape) for c in consts],
        out_specs=[seq_spec(tokens * d_model), seq_spec(D_KV, WINDOW), seq_spec(D_KV, WINDOW), state_spec]
        + slab_specs,
        out_shape=[jax.ShapeDtypeStruct((n_seq, tokens * d_model), BF16),
                   jax.ShapeDtypeStruct((n_seq, D_KV, WINDOW), F32),
                   jax.ShapeDtypeStruct((n_seq, D_KV, WINDOW), F32),
                   jax.ShapeDtypeStruct((POOL_BUF, n_seq, D_POOL), F32)]
        + [jax.ShapeDtypeStruct(w.shape, BF16) for w in weights],
        scratch_shapes=[pltpu.VMEM((tokens * SAMPLE_SEQS, D_ATTN), F32)],
        compiler_params=pltpu.CompilerParams(dimension_semantics=("arbitrary",),
                                             vmem_limit_bytes=VMEM_LIMIT_BYTES),
        name="sample_front",
    )(sink, xs, ck, cv, st, *weights, *consts)


def kernel(x_prompt, x_sample, cache_win_k, cache_win_v, state_pool, meta_tokens, ln_emb_g, ln_emb_b,
           rel_table, w_in, w_pool, pool_scale, sinks, w_out, ln1_g, ln1_b, w_mlp_in, w_mlp_out,
           ln2_g, ln2_b):
    batch, seq, d_model = x_prompt.shape
    n_seq, tokens, _ = x_sample.shape
    assert w_in.shape[0] == DEPTH and d_model == D_POOL + D_ATTN
    assert seq % PROMPT_TILE == 0 and n_seq % SAMPLE_SEQS == 0 and tokens * n_seq == PROMPT_TILE
    assert cache_win_k.shape[2] == WINDOW and state_pool.shape[2] == POOL_BUF and tokens <= POOL_BUF
    assert N_META == HALO and POOL_BUF <= HALO

    row = lambda a: a.reshape(1, -1).astype(F32)
    ge, be = row(ln_emb_g), row(ln_emb_b)
    pscale = row(pool_scale[0])
    sink = row(sinks[0])

    bias_p, bias_c, bias_n, meta_proj, w_in_b, w_pool_b, w_pool_bd = _prologue_call(
        tokens, rel_table.astype(F32), meta_tokens.astype(F32), ge, be, w_in[0], w_pool[0])

    to_kd_pos = lambda c: jnp.swapaxes(c[0].reshape(n_seq, WINDOW, D_KV), 1, 2)
    mix_s, nk_s, nv_s, np_s, w_out_b = _sample_front_call(
        sink, x_sample, to_kd_pos(cache_win_k), to_kd_pos(cache_win_v), jnp.swapaxes(state_pool[0], 0, 1),
        (w_out[0],), ge, be, w_in_b, w_pool_b, pscale, bias_c, bias_n)
    xs = x_sample
    w1, w2 = w_mlp_in[0], w_mlp_out[0]

    tile = PROMPT_TILE
    tiles_per_seq = seq // tile
    n_tiles = batch * tiles_per_seq
    xp = x_prompt.reshape(batch * seq, d_model)
    front_tile = lambda s: jnp.minimum(s, n_tiles - 1)
    finish_tile = lambda s: jnp.maximum(s - 1, 0)
    seq_of = lambda s: front_tile(s) // tiles_per_seq
    consts = (ge, be, w_in_b, w_pool_bd, pscale, bias_p, meta_proj, w_out_b, row(ln1_g[0]), row(ln1_b[0]),
              row(ln2_g[0]), row(ln2_b[0]))
    y_p, y_s, nk_p, nv_p, np_p = pl.pallas_call(
        functools.partial(_main_kernel, tokens, tiles_per_seq, n_tiles),
        grid=(n_tiles + 1,),
        in_specs=[_smem_spec(),
                  pl.BlockSpec((tile, d_model), lambda s: (front_tile(s), 0)),
                  _const_spec(xs.shape), _const_spec(mix_s.shape)] + [_const_spec(c.shape) for c in consts]
        + [pl.BlockSpec(memory_space=pl.ANY)] * 2,
        out_specs=[pl.BlockSpec((tile, d_model), lambda s: (finish_tile(s), 0)),
                   pl.BlockSpec(xs.shape, lambda s: (0, 0, 0)),
                   pl.BlockSpec((1, BLOCK, D_KV), lambda s: (seq_of(s), 0, 0)),
                   pl.BlockSpec((1, BLOCK, D_KV), lambda s: (seq_of(s), 0, 0)),
                   pl.BlockSpec((1, HALO, D_POOL), lambda s: (seq_of(s), 0, 0))],
        out_shape=[jax.ShapeDtypeStruct((batch * seq, d_model), F32),
                   jax.ShapeDtypeStruct(xs.shape, F32),
                   jax.ShapeDtypeStruct((batch, BLOCK, D_KV), F32),
                   jax.ShapeDtypeStruct((batch, BLOCK, D_KV), F32),
                   jax.ShapeDtypeStruct((batch, HALO, D_POOL), F32)],
        scratch_shapes=[pltpu.VMEM((tile, d_model), BF16),
                        pltpu.VMEM((4, tile + BLOCK, LANES), BF16),
                        pltpu.VMEM((4, tile + BLOCK, LANES), BF16),
                        pltpu.VMEM((tile + HALO, D_POOL), F32),
                        pltpu.VMEM((tile, D_ATTN), BF16),
                        pltpu.VMEM((tile, D_POOL), BF16),
                        pltpu.VMEM((tile, d_model), F32),
                        pltpu.VMEM((tile, d_model), F32),
                        pltpu.VMEM((tile, d_model), BF16),
                        pltpu.VMEM((tile // 2, w1.shape[1]), BF16),
                        pltpu.VMEM(w1.shape, BF16),
                        pltpu.VMEM(w2.shape, BF16),
                        pltpu.VMEM((WEIGHT_SLOTS,) + WEIGHT_CHUNK, F32),
                        pltpu.SemaphoreType.DMA((WEIGHT_SLOTS + 2 * (tile // WEIGHT_CHUNK[0]),))],
        compiler_params=pltpu.CompilerParams(dimension_semantics=("arbitrary",),
                                             vmem_limit_bytes=VMEM_LIMIT_BYTES),
        name="main",
    )(sink, xp, xs, mix_s, *consts, w1, w2)

    kv_shape = (DEPTH, -1, WINDOW, N_KV_HEADS, HEAD_DIM)
    from_kd_pos = lambda c: jnp.swapaxes(c, 1, 2).reshape(kv_shape)
    return (y_p.reshape(batch, seq, d_model), y_s,
            from_kd_pos(nk_p), from_kd_pos(nv_p),
            np_p[:, HALO - POOL_BUF:, :].reshape(DEPTH, batch, POOL_BUF, D_POOL),
            from_kd_pos(nk_s), from_kd_pos(nv_s), jnp.swapaxes(np_s, 0, 1)[None])
```

```python
import functools
import math

import jax
import jax.numpy as jnp
import numpy as np
from jax import lax
from jax.experimental import pallas as pl
from jax.experimental.pallas import tpu as pltpu

N_META = 16
POOL_WINDOWS = (2, 4, 8, 16)
POOL_GROUP_DIM = 128
D_POOL = len(POOL_WINDOWS) * POOL_GROUP_DIM
POOL_BUF = max(POOL_WINDOWS) - 1
N_HEADS = 8
HEAD_DIM = 64
D_ATTN = N_HEADS * HEAD_DIM
N_KV_HEADS = 2
D_KV = N_KV_HEADS * HEAD_DIM
WINDOW = 128
BLOCK = 128
REL_BUCKETS = 32
REL_MAX_DIST = 128
PAST_LEN = 8192
DEPTH = 1
ALPHA = (2.0 * DEPTH) ** 0.25
LN_EPS = 1e-5
Q_SCALE = HEAD_DIM ** -0.5

LANES = 128
SUBLANES = 8
VMEM_LIMIT_BYTES = 60 * 1024 * 1024

PROMPT_TILE = 512
IN_ROWS = 256
UP_CHUNK = 512
DOWN_CHUNK = 256
SAMPLE_SEQS = 32
SEQ_GROUP = SUBLANES
HALO = 2 * SUBLANES
WEIGHT_CHUNK = (256, 1024)
WEIGHT_SLOTS = 2

F32 = jnp.float32
BF16 = jnp.bfloat16
NEG_INF = float("-inf")


def _rel_bucket(dist):
    n = np.maximum(dist, 0)
    max_exact = REL_BUCKETS // 2
    nf = np.maximum(n, 1).astype(np.float64)
    large = max_exact + (np.log(nf / max_exact) / math.log(REL_MAX_DIST / max_exact)
                         * (REL_BUCKETS - max_exact)).astype(np.int32)
    large = np.minimum(large, REL_BUCKETS - 1)
    return np.where(n < max_exact, n, large).astype(np.int32)


def _bucket_tables(tokens):
    prompt = np.tile(_rel_bucket((BLOCK - np.arange(BLOCK)) % BLOCK)[None, :], (SUBLANES, 1)).astype(np.int32)
    t = np.repeat(np.arange(tokens), SEQ_GROUP)
    s = np.tile(np.arange(SEQ_GROUP), tokens)
    dist = (t[:, None] + WINDOW) - np.arange(WINDOW)[None, :]
    ok = (dist >= 0) & (dist < WINDOW)
    cache = np.where(ok, _rel_bucket(dist), -1).astype(np.int32)
    dist = t[:, None] - t[None, :]
    ok = (dist >= 0) & (s[:, None] == s[None, :])
    new = np.full((tokens * SEQ_GROUP, LANES), -1, np.int32)
    new[:, :tokens * SEQ_GROUP] = np.where(ok, _rel_bucket(dist), -1)
    return prompt, cache, new


def _layer_norm(x, g, b):
    mu = jnp.mean(x, axis=-1, keepdims=True)
    xc = x - mu
    var = jnp.mean(xc * xc, axis=-1, keepdims=True)
    return xc * lax.rsqrt(var + LN_EPS) * g + b


def _dot(a, b):
    return jnp.dot(a, b, preferred_element_type=F32)


def _dot_nt(a, b):
    return lax.dot_general(a, b, (((1,), (1,)), ((), ())), preferred_element_type=F32)


def _kv_lane_variants(x, low):
    xr = pltpu.roll(x, HEAD_DIM, axis=1)
    zero = jnp.zeros_like(x)
    return (jnp.where(low, x, zero).astype(BF16), jnp.where(low, zero, xr).astype(BF16),
            jnp.where(low, xr, zero).astype(BF16), jnp.where(low, zero, x).astype(BF16))


def _setup_tables(tab_ref, bkt_p_ref, bkt_c_ref, bkt_n_ref, meta_ref, g_ref, b_ref, wf_in_ref, wf_pool_ref,
                  bias_p_ref, bias_c_ref, bias_n_ref, meta_proj_ref, w_in_ref, w_pool_ref, w_pool_bd_ref):
    w_in_ref[...] = wf_in_ref[...].astype(BF16)
    w_pool_ref[...] = wf_pool_ref[...].astype(BF16)
    w_pool_bd_ref[...] = jnp.zeros_like(w_pool_bd_ref)
    for g in range(len(POOL_WINDOWS)):
        cols = slice(g * POOL_GROUP_DIM, (g + 1) * POOL_GROUP_DIM)
        w_pool_bd_ref[cols, cols] = w_pool_ref[g]
    def lookup(bucket, h):
        def body(i, acc):
            return jnp.where(bucket == i, tab_ref[i, h], acc)
        return lax.fori_loop(0, REL_BUCKETS, body, jnp.full(bucket.shape, NEG_INF, F32))

    rows = bkt_c_ref.shape[0]
    q = lax.broadcasted_iota(jnp.int32, (BLOCK, BLOCK), 0)
    k = lax.broadcasted_iota(jnp.int32, (BLOCK, BLOCK), 1)
    for h in range(N_HEADS):
        per_dist = lookup(bkt_p_ref[...], h)
        toeplitz = pltpu.roll(jnp.concatenate([per_dist] * (BLOCK // SUBLANES), axis=0), 0, 1,
                              stride=1, stride_axis=0)
        cur = jnp.where(k <= q, toeplitz, NEG_INF)
        for table, prev_ok in enumerate(((k > q) & (k >= BLOCK - N_META), k > q)):
            bias_p_ref[table, h, :, 0:BLOCK] = jnp.where(prev_ok, toeplitz, NEG_INF)
            bias_p_ref[table, h, :, BLOCK:] = cur
        bias_c_ref[h * rows:(h + 1) * rows, :] = lookup(bkt_c_ref[...], h)
        bias_n_ref[h * rows:(h + 1) * rows, :] = lookup(bkt_n_ref[...], h)
    hm = _layer_norm(meta_ref[...], g_ref[...], b_ref[...])
    meta_proj_ref[...] = _dot(hm.astype(BF16), w_in_ref[...])


def _carry_init(meta_proj_ref, kvar, vvar, uext):
    low = lax.broadcasted_iota(jnp.int32, (1, LANES), 1) < HEAD_DIM
    pad = jnp.zeros((BLOCK - N_META, LANES), BF16)
    kq = _kv_lane_variants(meta_proj_ref[:, D_POOL + D_ATTN:D_POOL + D_ATTN + D_KV], low)
    vq = _kv_lane_variants(meta_proj_ref[:, D_POOL + D_ATTN + D_KV:], low)
    for i in range(4):
        kvar[i, 0:BLOCK, :] = jnp.concatenate([pad, kq[i]], axis=0)
        vvar[i, 0:BLOCK, :] = jnp.concatenate([pad, vq[i]], axis=0)
    uext[0:HALO, :] = meta_proj_ref[:, 0:D_POOL]


def _front_pieces(first_tile, x_ref, sink_ref, g_ref, b_ref, w_in_ref, w_pool_ref, pscale_ref, bias_ref,
                  mix_ref, nk_ref, nv_ref, np_ref, kvar, vvar, uext, qbuf, pbuf, hres):
    tile = x_ref.shape[0]
    halo = HALO
    low = lax.broadcasted_iota(jnp.int32, (1, LANES), 1) < HEAD_DIM

    for j in range(tile // IN_ROWS):
        rows = slice(j * IN_ROWS, (j + 1) * IN_ROWS)
        h = _layer_norm(x_ref[rows, :], g_ref[...], b_ref[...])
        hres[rows, :] = ALPHA * h
        proj = _dot(h.astype(BF16), w_in_ref[...])
        u = proj[:, 0:D_POOL]
        k = proj[:, D_POOL + D_ATTN:D_POOL + D_ATTN + D_KV]
        v = proj[:, D_POOL + D_ATTN + D_KV:]
        qbuf[rows, :] = (proj[:, D_POOL:D_POOL + D_ATTN] * Q_SCALE).astype(BF16)
        uext[halo + rows.start:halo + rows.stop, :] = u
        kq = _kv_lane_variants(k, low)
        vq = _kv_lane_variants(v, low)
        for i in range(4):
            kvar[i, BLOCK + rows.start:BLOCK + rows.stop, :] = kq[i]
            vvar[i, BLOCK + rows.start:BLOCK + rows.stop, :] = vq[i]
        if rows.stop == tile:
            nk_ref[...] = k[IN_ROWS - BLOCK:, :].T
            nv_ref[...] = v[IN_ROWS - BLOCK:, :].T
            np_ref[...] = u[IN_ROWS - halo:, :]
        yield

    for g, w in enumerate(POOL_WINDOWS):
        cols = slice(g * POOL_GROUP_DIM, (g + 1) * POOL_GROUP_DIM)
        ug = uext[halo:halo + tile, cols]
        s = ug
        for i in range(1, w):
            s = s + uext[halo - i:halo - i + tile, cols]
        pbuf[:, cols] = (s * (1.0 / w) - ug).astype(BF16)
        if g == len(POOL_WINDOWS) - 1:
            mix_ref[:, 0:D_POOL] = (_dot(pbuf[...], w_pool_ref[...]) * pscale_ref[...]).astype(BF16)
        yield

    for j in range(tile // BLOCK):
        rows = slice(j * BLOCK, (j + 1) * BLOCK)
        keys = slice(j * BLOCK, (j + 2) * BLOCK)
        sel = jnp.where(first_tile, 0, 1) if j == 0 else 1
        for kv in range(N_KV_HEADS):
            tiles = (2 * kv, 2 * kv + 1)
            lhs = jnp.concatenate([qbuf[rows, p * LANES:(p + 1) * LANES] for p in tiles], axis=0)
            nk = 2 * BLOCK
            s_all = _dot_nt(lhs, jnp.concatenate([kvar[2 * kv, keys, :], kvar[2 * kv + 1, keys, :]], axis=0))
            es, dens = [], []
            for half in range(2):
                heads = [2 * p + half for p in tiles]
                s = s_all[:, half * nk:(half + 1) * nk]
                s = s + jnp.concatenate([bias_ref[sel, hd] for hd in heads], axis=0)
                sink = jnp.concatenate([jnp.full((BLOCK, 1), sink_ref[0, hd], F32) for hd in heads], axis=0)
                m = jnp.maximum(jnp.max(s, axis=-1, keepdims=True), sink)
                e = jnp.exp(s - m)
                dens.append(jnp.sum(e, axis=-1, keepdims=True) + jnp.exp(sink - m))
                es.append(e.astype(BF16))
            o = _dot(jnp.concatenate(es, axis=0),
                     jnp.concatenate([vvar[2 * kv, keys, :], vvar[2 * kv + 1, keys, :]], axis=1))
            acc = (o[0:nk, 0:LANES] * (1.0 / dens[0]) + o[nk:, LANES:] * (1.0 / dens[1]))
            for n, p in enumerate(tiles):
                mix_ref[rows, D_POOL + p * LANES:D_POOL + (p + 1) * LANES] = (
                    acc[n * BLOCK:(n + 1) * BLOCK].astype(BF16))
            yield

    for i in range(4):
        kvar[i, 0:BLOCK, :] = kvar[i, tile:tile + BLOCK, :]
        vvar[i, 0:BLOCK, :] = vvar[i, tile:tile + BLOCK, :]
    uext[0:halo, :] = uext[tile:tile + halo, :]


def _sample_front_kernel(tab_ref, sink_ref, x_ref, ck_ref, cv_ref, st_ref, wf_out_ref,
                         bkt_p_ref, bkt_c_ref, bkt_n_ref, meta_ref, g_ref, b_ref, wf_in_ref, wf_pool_ref, pscale_ref,
                         mix_ref, nk_ref, nv_ref, np_ref, wb_out_ref,
                         bias_p_ref, meta_proj_ref, w_in_ref, w_pool_bd_ref,
                         o_s, bias_c_ref, bias_n_ref, w_pool_ref):
    @pl.when(pl.program_id(0) == 0)
    def _():
        _setup_tables(tab_ref, bkt_p_ref, bkt_c_ref, bkt_n_ref, meta_ref, g_ref, b_ref, wf_in_ref, wf_pool_ref,
                      bias_p_ref, bias_c_ref, bias_n_ref, meta_proj_ref, w_in_ref, w_pool_ref, w_pool_bd_ref)

    wb_out_ref[...] = wf_out_ref[...].astype(BF16)
    ns, tokens, d_model = x_ref.shape
    lane = lax.broadcasted_iota(jnp.int32, (1, LANES), 1)
    low = lane < HEAD_DIM

    x = jnp.concatenate([x_ref[:, t, :] for t in range(tokens)], axis=0)
    h = _layer_norm(x, g_ref[...], b_ref[...])
    proj = _dot(h.astype(BF16), w_in_ref[...])
    u = proj[:, 0:D_POOL]
    q = proj[:, D_POOL:D_POOL + D_ATTN] * Q_SCALE
    k = proj[:, D_POOL + D_ATTN:D_POOL + D_ATTN + D_KV]
    v = proj[:, D_POOL + D_ATTN + D_KV:]

    ext = [st_ref[r] for r in range(POOL_BUF)]
    ext += [u[t * ns:(t + 1) * ns] for t in range(tokens)]
    ps = []
    for t in range(tokens):
        idx = POOL_BUF + t
        pos = PAST_LEN - POOL_BUF + idx
        parts = []
        for g, w in enumerate(POOL_WINDOWS):
            cols = slice(g * POOL_GROUP_DIM, (g + 1) * POOL_GROUP_DIM)
            lo = max(idx + 1 - w, 0)
            s = ext[idx][:, cols]
            for r in range(idx - 1, lo - 1, -1):
                s = s + ext[r][:, cols]
            parts.append(s * (1.0 / min(w, pos + 1)) - ext[idx][:, cols])
        ps.append(jnp.concatenate(parts, axis=1))
    p = jnp.concatenate(ps, axis=0).astype(BF16)
    zs = [_dot(p[:, g * POOL_GROUP_DIM:(g + 1) * POOL_GROUP_DIM], w_pool_ref[g])
          for g in range(len(POOL_WINDOWS))]
    z = (jnp.concatenate(zs, axis=1) * pscale_ref[...]).astype(BF16)
    for r in range(POOL_BUF):
        np_ref[r] = ext[r + tokens]

    qh = []
    for hd in range(N_HEADS):
        tl = q[:, (hd // 2) * LANES:(hd // 2 + 1) * LANES]
        kv = hd // (N_HEADS // N_KV_HEADS)
        want_low = kv == 0
        is_low = hd % 2 == 0
        src = tl if want_low == is_low else pltpu.roll(tl, HEAD_DIM, axis=1)
        qh.append(jnp.where(low, src, 0.0) if want_low else jnp.where(low, 0.0, src))

    gr = SEQ_GROUP
    rows_per_head = tokens * gr
    sub = lax.broadcasted_iota(jnp.int32, (N_HEADS * rows_per_head, 1), 0) % gr
    sink = jnp.concatenate([jnp.full((rows_per_head, 1), sink_ref[0, hd], F32) for hd in range(N_HEADS)],
                           axis=0)
    zpad = jnp.zeros((LANES - rows_per_head, LANES), F32)
    ztop = jnp.zeros((WINDOW - SUBLANES, LANES), F32)
    sub8 = lax.broadcasted_iota(jnp.int32, (SUBLANES, 1), 0)
    keep = WINDOW - tokens

    def appended(old_ref, news, j):
        tail = jnp.zeros((SUBLANES, LANES), F32)
        for t in range(tokens):
            row = SUBLANES - tokens + t
            tail = jnp.where(sub8 == row, pltpu.roll(news[t], (row - j) % SUBLANES, axis=0), tail)
        cols = jnp.concatenate([ztop, tail], axis=0).T
        return jnp.where(lane >= keep, cols, pltpu.roll(old_ref[...], keep, axis=1))

    for base in range(0, ns, gr):
        k_t = [k[t * ns + base:t * ns + base + gr] for t in range(tokens)]
        v_t = [v[t * ns + base:t * ns + base + gr] for t in range(tokens)]
        for s in range(gr):
            nk_ref[base + s] = appended(ck_ref.at[base + s], k_t, s)
            nv_ref[base + s] = appended(cv_ref.at[base + s], v_t, s)

    for gi in range(ns // gr):
        base = gi * gr
        lhs = jnp.concatenate([qh[hd][t * ns + base:t * ns + base + gr]
                               for hd in range(N_HEADS) for t in range(tokens)], axis=0).astype(BF16)
        sc = None
        for s in range(gr):
            s_one = _dot(lhs, ck_ref[base + s].astype(BF16))
            sc = s_one if sc is None else jnp.where(sub == s, s_one, sc)
        sc = sc + bias_c_ref[...]
        k_t = [k[t * ns + base:t * ns + base + gr] for t in range(tokens)]
        v_t = [v[t * ns + base:t * ns + base + gr] for t in range(tokens)]
        k_new = jnp.concatenate(k_t + [zpad], axis=0)
        v_new = jnp.concatenate(v_t + [zpad], axis=0)
        sn = _dot_nt(lhs, k_new.astype(BF16)) + bias_n_ref[...]
        m = jnp.maximum(jnp.maximum(jnp.max(sc, axis=-1, keepdims=True),
                                    jnp.max(sn, axis=-1, keepdims=True)), sink)
        ec = jnp.exp(sc - m)
        en = jnp.exp(sn - m)
        den = (jnp.sum(ec, axis=-1, keepdims=True) + jnp.sum(en, axis=-1, keepdims=True)
               + jnp.exp(sink - m))
        o = _dot(en.astype(BF16), v_new.astype(BF16))
        for s in range(gr):
            o = o + _dot_nt(jnp.where(sub == s, ec, 0.0).astype(BF16), cv_ref[base + s].astype(BF16))
        o = o * (1.0 / den)
        for t in range(tokens):
            for p in range(N_HEADS // 2):
                a = o[(2 * p * tokens + t) * gr:(2 * p * tokens + t + 1) * gr]
                c = o[((2 * p + 1) * tokens + t) * gr:((2 * p + 1) * tokens + t + 1) * gr]
                if p < N_HEADS // 4:
                    c = pltpu.roll(c, HEAD_DIM, axis=1)
                else:
                    a = pltpu.roll(a, HEAD_DIM, axis=1)
                o_s[t * ns + base:t * ns + base + gr, p * LANES:(p + 1) * LANES] = jnp.where(low, a, c)

    for t in range(tokens):
        mix_ref[:, t * d_model:t * d_model + D_POOL] = z[t * ns:(t + 1) * ns]
        mix_ref[:, t * d_model + D_POOL:(t + 1) * d_model] = o_s[t * ns:(t + 1) * ns, :].astype(BF16)


def _pre_pieces(rows, res, mix, w_out_ref, g1_ref, b1_ref, h1_s, hb_s):
    h1_s[rows, :] = res() + _dot(mix(), w_out_ref[...])
    yield
    h = _layer_norm(h1_s[rows, :], g1_ref[...], b1_ref[...])
    h1_s[rows, :] = h
    hb_s[rows, :] = h.astype(BF16)
    yield


def _mlp_pieces(rows, a_s, h1_s, hb_s, w1_ref, w2_ref, g2_ref, b2_ref, y_ref):
    for c in range(w1_ref.shape[1] // UP_CHUNK):
        cols = slice(c * UP_CHUNK, (c + 1) * UP_CHUNK)
        a = _dot(hb_s[rows, :], w1_ref[:, cols])
        a_s[:, cols] = jnp.square(jnp.maximum(a, 0.0)).astype(BF16)
        if c == 0:
            y_ref[rows, :] = ALPHA * h1_s[rows, :]
        yield
    for n in range(y_ref.shape[1] // DOWN_CHUNK):
        cols = slice(n * DOWN_CHUNK, (n + 1) * DOWN_CHUNK)
        y_ref[rows, cols] = y_ref[rows, cols] + _dot(a_s[...], w2_ref[:, cols])
        yield
    y_ref[rows, :] = _layer_norm(y_ref[rows, :], g2_ref[...], b2_ref[...])
    yield


def _weave(main, n_main, sides):
    order = []
    for i in range(n_main):
        order.append(main)
        order += sides[i * len(sides) // n_main:(i + 1) * len(sides) // n_main]
    return order


def _main_kernel(tokens, tiles_per_seq, n_tiles,
                 sink_ref, x_ref, xs_ref, mixs_ref, ge_ref, be_ref, w_in_ref, w_pool_ref,
                 pscale_ref, bias_ref, meta_proj_ref, w_out_ref, g1_ref, b1_ref, g2_ref, b2_ref,
                 w1_hbm, w2_hbm,
                 y_ref, ys_ref, nk_ref, nv_ref, np_ref,
                 mixbuf, kvar, vvar, uext, qbuf, pbuf, hres, h1_s, hb_s, a_s, w1_ref, w2_ref, stage, sem):
    s = pl.program_id(0)
    is_first = s == 0
    first_tile = jnp.minimum(s, n_tiles - 1) % tiles_per_seq == 0
    tile, d_model = x_ref.shape
    ns = xs_ref.shape[0]
    half = tile // 2
    halves = (slice(0, half), slice(half, tile))
    pre_w = (w_out_ref, g1_ref, b1_ref, h1_s, hb_s)

    @pl.when(is_first)
    def _():
        def toks(rows):
            return range(rows.start // ns, rows.stop // ns)

        def sample_pre():
            for rows in halves:
                res = lambda: ALPHA * _layer_norm(jnp.concatenate([xs_ref[:, t, :] for t in toks(rows)], axis=0),
                                                  ge_ref[...], be_ref[...])
                mix = lambda: jnp.concatenate([mixs_ref[:, t * d_model:(t + 1) * d_model] for t in toks(rows)],
                                              axis=0)
                yield from _pre_pieces(rows, res, mix, *pre_w)

        cr, cc = stage.shape[1:]
        chunks = [(src, dst, r, c) for src, dst in ((w1_hbm, w1_ref), (w2_hbm, w2_ref))
                  for r in range(0, src.shape[0], cr) for c in range(0, src.shape[1], cc)]
        ring = [stage.at[i] for i in range(stage.shape[0])]
        ring += [buf.at[pl.ds(r, cr)] for buf in (hres, y_ref) for r in range(0, tile, cr)]
        slots = len(ring)

        def copy(i):
            src, _, r, c = chunks[i]
            return pltpu.make_async_copy(src.at[pl.ds(r, cr), pl.ds(c, cc)], ring[i % slots], sem.at[i % slots])

        for i in range(slots):
            copy(i).start()
        pre = sample_pre()
        every = len(chunks) // 4
        for i, (_, dst, r, c) in enumerate(chunks):
            if i % every == every // 2:
                next(pre, None)
            copy(i).wait()
            dst[r:r + cr, c:c + cc] = ring[i % slots][...].astype(BF16)
            if i + slots < len(chunks):
                copy(i + slots).start()
        for _ in pre:
            raise AssertionError("piece left unscheduled")

    @pl.when(first_tile)
    def _():
        _carry_init(meta_proj_ref, kvar, vvar, uext)

    mlp = [_mlp_pieces(rows, a_s, h1_s, hb_s, w1_ref, w2_ref, g2_ref, b2_ref, y_ref) for rows in halves]
    front = _front_pieces(first_tile, x_ref, sink_ref, ge_ref, be_ref, w_in_ref, w_pool_ref, pscale_ref,
                          bias_ref, mixbuf, nk_ref.at[0], nv_ref.at[0], np_ref.at[0], kvar, vvar, uext, qbuf, pbuf,
                          hres)
    pre = [_pre_pieces(rows, lambda rows=rows: hres[rows, :], lambda rows=rows: mixbuf[rows, :], *pre_w)
           for rows in halves]
    a, b, f, pa, pb = mlp[0], mlp[1], front, pre[0], pre[1]
    n_up = w1_ref.shape[1] // UP_CHUNK
    n_down = d_model // DOWN_CHUNK
    order = (
        _weave(a, n_up, [f] * 4)
        + _weave(a, n_down, [f] * 5)
        + _weave(b, n_up, [a] + [f] * 5)
        + _weave(b, n_down, [f, pa, pb, pa, pb]) + [b])
    for g in order:
        next(g, None)
    for g in (a, b, f, pa, pb):
        for _ in g:
            raise AssertionError("piece left unscheduled")

    @pl.when(is_first)
    def _():
        for t in range(tokens):
            ys_ref[:, t, :] = y_ref[t * ns:(t + 1) * ns, :]


def _const_spec(shape):
    return pl.BlockSpec(shape, lambda *_: (0,) * len(shape), pipeline_mode=pl.Buffered(1))


def _smem_spec():
    return pl.BlockSpec(memory_space=pltpu.SMEM)


def _sample_front_call(rel_table, sink, xs, ck, cv, st, w_out, meta_tokens, ge, be, w_in, w_pool, pscale):
    n_seq, tokens, d_model = xs.shape
    steps = n_seq // SAMPLE_SEQS
    srows = N_HEADS * tokens * SEQ_GROUP
    tables = [jnp.asarray(t) for t in _bucket_tables(tokens)]
    consts = tables + [meta_tokens, ge, be, w_in, w_pool, pscale]
    whole = lambda shape: pl.BlockSpec(shape, lambda i: (0,) * len(shape))
    seq_spec = lambda *tail: pl.BlockSpec((SAMPLE_SEQS,) + tail, lambda i: (i,) + (0,) * len(tail))
    state_spec = pl.BlockSpec((POOL_BUF, SAMPLE_SEQS, D_POOL), lambda i: (0, i, 0))
    slab_spec = pl.BlockSpec((w_out.shape[0] // steps, w_out.shape[1]), lambda i: (i, 0))
    once = [jax.ShapeDtypeStruct((2, N_HEADS, BLOCK, 2 * BLOCK), F32),
            jax.ShapeDtypeStruct((meta_tokens.shape[0], w_in.shape[1]), F32),
            jax.ShapeDtypeStruct(w_in.shape, BF16),
            jax.ShapeDtypeStruct((D_POOL, D_POOL), BF16)]
    return pl.pallas_call(
        _sample_front_kernel,
        grid=(steps,),
        in_specs=[_smem_spec(), _smem_spec(), seq_spec(tokens, d_model), seq_spec(D_KV, WINDOW),
                  seq_spec(D_KV, WINDOW), state_spec, slab_spec] + [_const_spec(c.shape) for c in consts],
        out_specs=[seq_spec(tokens * d_model), seq_spec(D_KV, WINDOW), seq_spec(D_KV, WINDOW), state_spec,
                   slab_spec] + [whole(o.shape) for o in once],
        out_shape=[jax.ShapeDtypeStruct((n_seq, tokens * d_model), BF16),
                   jax.ShapeDtypeStruct((n_seq, D_KV, WINDOW), F32),
                   jax.ShapeDtypeStruct((n_seq, D_KV, WINDOW), F32),
                   jax.ShapeDtypeStruct((POOL_BUF, n_seq, D_POOL), F32),
                   jax.ShapeDtypeStruct(w_out.shape, BF16)] + once,
        scratch_shapes=[pltpu.VMEM((tokens * SAMPLE_SEQS, D_ATTN), F32),
                        pltpu.VMEM((srows, WINDOW), F32),
                        pltpu.VMEM((srows, LANES), F32),
                        pltpu.VMEM(w_pool.shape, BF16)],
        compiler_params=pltpu.CompilerParams(dimension_semantics=("arbitrary",),
                                             vmem_limit_bytes=VMEM_LIMIT_BYTES),
        name="sample_front",
    )(rel_table, sink, xs, ck, cv, st, w_out, *consts)


def kernel(x_prompt, x_sample, cache_win_k, cache_win_v, state_pool, meta_tokens, ln_emb_g, ln_emb_b,
           rel_table, w_in, w_pool, pool_scale, sinks, w_out, ln1_g, ln1_b, w_mlp_in, w_mlp_out,
           ln2_g, ln2_b):
    batch, seq, d_model = x_prompt.shape
    n_seq, tokens, _ = x_sample.shape
    assert w_in.shape[0] == DEPTH and d_model == D_POOL + D_ATTN
    assert seq % PROMPT_TILE == 0 and n_seq % SAMPLE_SEQS == 0 and tokens * n_seq == PROMPT_TILE
    assert cache_win_k.shape[2] == WINDOW and state_pool.shape[2] == POOL_BUF and tokens <= POOL_BUF
    assert N_META == HALO and POOL_BUF <= HALO

    row = lambda a: a.reshape(1, -1).astype(F32)
    ge, be = row(ln_emb_g), row(ln_emb_b)
    pscale = row(pool_scale[0])
    sink = row(sinks[0])

    to_kd_pos = lambda c: jnp.swapaxes(c[0].reshape(n_seq, WINDOW, D_KV), 1, 2)
    mix_s, nk_s, nv_s, np_s, w_out_b, bias_p, meta_proj, w_in_b, w_pool_bd = _sample_front_call(
        rel_table.astype(F32), sink, x_sample, to_kd_pos(cache_win_k), to_kd_pos(cache_win_v),
        jnp.swapaxes(state_pool[0], 0, 1), w_out[0], meta_tokens.astype(F32), ge, be, w_in[0], w_pool[0], pscale)
    xs = x_sample
    w1, w2 = w_mlp_in[0], w_mlp_out[0]

    tile = PROMPT_TILE
    tiles_per_seq = seq // tile
    n_tiles = batch * tiles_per_seq
    xp = x_prompt.reshape(batch * seq, d_model)
    front_tile = lambda s: jnp.minimum(s, n_tiles - 1)
    finish_tile = lambda s: jnp.maximum(s - 1, 0)
    seq_of = lambda s: front_tile(s) // tiles_per_seq
    consts = (ge, be, w_in_b, w_pool_bd, pscale, bias_p, meta_proj, w_out_b, row(ln1_g[0]), row(ln1_b[0]),
              row(ln2_g[0]), row(ln2_b[0]))
    y_p, y_s, nk_p, nv_p, np_p = pl.pallas_call(
        functools.partial(_main_kernel, tokens, tiles_per_seq, n_tiles),
        grid=(n_tiles + 1,),
        in_specs=[_smem_spec(),
                  pl.BlockSpec((tile, d_model), lambda s: (front_tile(s), 0)),
                  _const_spec(xs.shape), _const_spec(mix_s.shape)] + [_const_spec(c.shape) for c in consts]
        + [pl.BlockSpec(memory_space=pl.ANY)] * 2,
        out_specs=[pl.BlockSpec((tile, d_model), lambda s: (finish_tile(s), 0)),
                   pl.BlockSpec(xs.shape, lambda s: (0, 0, 0)),
                   pl.BlockSpec((1, BLOCK, D_KV), lambda s: (seq_of(s), 0, 0)),
                   pl.BlockSpec((1, BLOCK, D_KV), lambda s: (seq_of(s), 0, 0)),
                   pl.BlockSpec((1, HALO, D_POOL), lambda s: (seq_of(s), 0, 0))],
        out_shape=[jax.ShapeDtypeStruct((batch * seq, d_model), F32),
                   jax.ShapeDtypeStruct(xs.shape, F32),
                   jax.ShapeDtypeStruct((batch, BLOCK, D_KV), F32),
                   jax.ShapeDtypeStruct((batch, BLOCK, D_KV), F32),
                   jax.ShapeDtypeStruct((batch, HALO, D_POOL), F32)],
        scratch_shapes=[pltpu.VMEM((tile, d_model), BF16),
                        pltpu.VMEM((4, tile + BLOCK, LANES), BF16),
                        pltpu.VMEM((4, tile + BLOCK, LANES), BF16),
                        pltpu.VMEM((tile + HALO, D_POOL), F32),
                        pltpu.VMEM((tile, D_ATTN), BF16),
                        pltpu.VMEM((tile, D_POOL), BF16),
                        pltpu.VMEM((tile, d_model), F32),
                        pltpu.VMEM((tile, d_model), F32),
                        pltpu.VMEM((tile, d_model), BF16),
                        pltpu.VMEM((tile // 2, w1.shape[1]), BF16),
                        pltpu.VMEM(w1.shape, BF16),
                        pltpu.VMEM(w2.shape, BF16),
                        pltpu.VMEM((WEIGHT_SLOTS,) + WEIGHT_CHUNK, F32),
                        pltpu.SemaphoreType.DMA((WEIGHT_SLOTS + 2 * (tile // WEIGHT_CHUNK[0]),))],
        compiler_params=pltpu.CompilerParams(dimension_semantics=("arbitrary",),
                                             vmem_limit_bytes=VMEM_LIMIT_BYTES),
        name="main",
    )(sink, xp, xs, mix_s, *consts, w1, w2)

    kv_shape = (DEPTH, -1, WINDOW, N_KV_HEADS, HEAD_DIM)
    from_kd_pos = lambda c: jnp.swapaxes(c, 1, 2).reshape(kv_shape)
    return (y_p.reshape(batch, seq, d_model), y_s,
            from_kd_pos(nk_p), from_kd_pos(nv_p),
            np_p[:, HALO - POOL_BUF:, :].reshape(DEPTH, batch, POOL_BUF, D_POOL),
            from_kd_pos(nk_s), from_kd_pos(nv_s), jnp.swapaxes(np_s, 0, 1)[None])
```

```python
import functools
import math

import jax
import jax.numpy as jnp
import numpy as np
from jax import lax
from jax.experimental import pallas as pl
from jax.experimental.pallas import tpu as pltpu

N_META = 16
POOL_WINDOWS = (2, 4, 8, 16)
POOL_GROUP_DIM = 128
D_POOL = len(POOL_WINDOWS) * POOL_GROUP_DIM
POOL_BUF = max(POOL_WINDOWS) - 1
N_HEADS = 8
HEAD_DIM = 64
D_ATTN = N_HEADS * HEAD_DIM
N_KV_HEADS = 2
D_KV = N_KV_HEADS * HEAD_DIM
WINDOW = 128
BLOCK = 128
REL_BUCKETS = 32
REL_MAX_DIST = 128
PAST_LEN = 8192
DEPTH = 1
ALPHA = (2.0 * DEPTH) ** 0.25
LN_EPS = 1e-5
Q_SCALE = HEAD_DIM ** -0.5

LANES = 128
SUBLANES = 8
VMEM_LIMIT_BYTES = 60 * 1024 * 1024

PROMPT_TILE = 512
IN_ROWS = 256
UP_CHUNK = 512
DOWN_CHUNK = 256
SAMPLE_SEQS = 32
SEQ_GROUP = SUBLANES
HALO = 2 * SUBLANES
WEIGHT_CHUNK = (256, 1024)
WEIGHT_SLOTS = 2

F32 = jnp.float32
BF16 = jnp.bfloat16
NEG_INF = float("-inf")


def _rel_bucket(dist):
    n = np.maximum(dist, 0)
    max_exact = REL_BUCKETS // 2
    nf = np.maximum(n, 1).astype(np.float64)
    large = max_exact + (np.log(nf / max_exact) / math.log(REL_MAX_DIST / max_exact)
                         * (REL_BUCKETS - max_exact)).astype(np.int32)
    large = np.minimum(large, REL_BUCKETS - 1)
    return np.where(n < max_exact, n, large).astype(np.int32)


def _bucket_tables(tokens):
    prompt = np.tile(_rel_bucket((BLOCK - np.arange(BLOCK)) % BLOCK)[None, :], (SUBLANES, 1)).astype(np.int32)
    t = np.repeat(np.arange(tokens), SEQ_GROUP)
    s = np.tile(np.arange(SEQ_GROUP), tokens)
    dist = (t[:, None] + WINDOW) - np.arange(WINDOW)[None, :]
    ok = (dist >= 0) & (dist < WINDOW)
    cache = np.where(ok, _rel_bucket(dist), -1).astype(np.int32)
    dist = t[:, None] - t[None, :]
    ok = (dist >= 0) & (s[:, None] == s[None, :])
    new = np.full((tokens * SEQ_GROUP, LANES), -1, np.int32)
    new[:, :tokens * SEQ_GROUP] = np.where(ok, _rel_bucket(dist), -1)
    return prompt, cache, new


def _layer_norm(x, g, b):
    mu = jnp.mean(x, axis=-1, keepdims=True)
    xc = x - mu
    var = jnp.mean(xc * xc, axis=-1, keepdims=True)
    return xc * lax.rsqrt(var + LN_EPS) * g + b


def _dot(a, b):
    return jnp.dot(a, b, preferred_element_type=F32)


def _dot_nt(a, b):
    return lax.dot_general(a, b, (((1,), (1,)), ((), ())), preferred_element_type=F32)


def _kv_lane_variants(x, low):
    xr = pltpu.roll(x, HEAD_DIM, axis=1)
    zero = jnp.zeros_like(x)
    return (jnp.where(low, x, zero).astype(BF16), jnp.where(low, zero, xr).astype(BF16),
            jnp.where(low, xr, zero).astype(BF16), jnp.where(low, zero, x).astype(BF16))


def _setup_tables(tab_ref, bkt_p_ref, bkt_c_ref, bkt_n_ref, meta_ref, g_ref, b_ref, wf_in_ref, wf_pool_ref,
                  bias_p_ref, bias_c_ref, bias_n_ref, meta_proj_ref, w_in_ref, w_pool_ref, w_pool_bd_ref):
    w_in_ref[...] = wf_in_ref[...].astype(BF16)
    w_pool_ref[...] = wf_pool_ref[...].astype(BF16)
    w_pool_bd_ref[...] = jnp.zeros_like(w_pool_bd_ref)
    for g in range(len(POOL_WINDOWS)):
        cols = slice(g * POOL_GROUP_DIM, (g + 1) * POOL_GROUP_DIM)
        w_pool_bd_ref[cols, cols] = w_pool_ref[g]
    def lookup(bucket):
        accs = [jnp.full(bucket.shape, NEG_INF, F32)] * N_HEADS
        for i in range(REL_BUCKETS):
            hit = bucket == i
            accs = [jnp.where(hit, tab_ref[i, h], acc) for h, acc in enumerate(accs)]
        return accs

    rows = bkt_c_ref.shape[0]
    q = lax.broadcasted_iota(jnp.int32, (BLOCK, BLOCK), 0)
    k = lax.broadcasted_iota(jnp.int32, (BLOCK, BLOCK), 1)
    per_dist, cache_bias, new_bias = lookup(bkt_p_ref[...]), lookup(bkt_c_ref[...]), lookup(bkt_n_ref[...])
    for h in range(N_HEADS):
        toeplitz = pltpu.roll(jnp.concatenate([per_dist[h]] * (BLOCK // SUBLANES), axis=0), 0, 1,
                              stride=1, stride_axis=0)
        cur = jnp.where(k <= q, toeplitz, NEG_INF)
        for table, prev_ok in enumerate(((k > q) & (k >= BLOCK - N_META), k > q)):
            bias_p_ref[table, h, :, 0:BLOCK] = jnp.where(prev_ok, toeplitz, NEG_INF)
            bias_p_ref[table, h, :, BLOCK:] = cur
        bias_c_ref[h * rows:(h + 1) * rows, :] = cache_bias[h]
        bias_n_ref[h * rows:(h + 1) * rows, :] = new_bias[h]
    hm = _layer_norm(meta_ref[...], g_ref[...], b_ref[...])
    meta_proj_ref[...] = _dot(hm.astype(BF16), w_in_ref[...])


def _carry_init(meta_proj_ref, kvar, vvar, uext):
    low = lax.broadcasted_iota(jnp.int32, (1, LANES), 1) < HEAD_DIM
    pad = jnp.zeros((BLOCK - N_META, LANES), BF16)
    kq = _kv_lane_variants(meta_proj_ref[:, D_POOL + D_ATTN:D_POOL + D_ATTN + D_KV], low)
    vq = _kv_lane_variants(meta_proj_ref[:, D_POOL + D_ATTN + D_KV:], low)
    for i in range(4):
        kvar[i, 0:BLOCK, :] = jnp.concatenate([pad, kq[i]], axis=0)
        vvar[i, 0:BLOCK, :] = jnp.concatenate([pad, vq[i]], axis=0)
    uext[0:HALO, :] = meta_proj_ref[:, 0:D_POOL]


def _front_pieces(first_tile, x_ref, sink_ref, g_ref, b_ref, w_in_ref, w_pool_ref, pscale_ref, bias_ref,
                  mix_ref, nk_ref, nv_ref, np_ref, kvar, vvar, uext, qbuf, pbuf, hres):
    tile = x_ref.shape[0]
    halo = HALO
    low = lax.broadcasted_iota(jnp.int32, (1, LANES), 1) < HEAD_DIM

    for j in range(tile // IN_ROWS):
        rows = slice(j * IN_ROWS, (j + 1) * IN_ROWS)
        h = _layer_norm(x_ref[rows, :], g_ref[...], b_ref[...])
        hres[rows, :] = ALPHA * h
        proj = _dot(h.astype(BF16), w_in_ref[...])
        u = proj[:, 0:D_POOL]
        k = proj[:, D_POOL + D_ATTN:D_POOL + D_ATTN + D_KV]
        v = proj[:, D_POOL + D_ATTN + D_KV:]
        qbuf[rows, :] = (proj[:, D_POOL:D_POOL + D_ATTN] * Q_SCALE).astype(BF16)
        uext[halo + rows.start:halo + rows.stop, :] = u
        kq = _kv_lane_variants(k, low)
        vq = _kv_lane_variants(v, low)
        for i in range(4):
            kvar[i, BLOCK + rows.start:BLOCK + rows.stop, :] = kq[i]
            vvar[i, BLOCK + rows.start:BLOCK + rows.stop, :] = vq[i]
        if rows.stop == tile:
            nk_ref[...] = k[IN_ROWS - BLOCK:, :].T
            nv_ref[...] = v[IN_ROWS - BLOCK:, :].T
            np_ref[...] = u[IN_ROWS - halo:, :]
        yield

    for g, w in enumerate(POOL_WINDOWS):
        cols = slice(g * POOL_GROUP_DIM, (g + 1) * POOL_GROUP_DIM)
        ug = uext[halo:halo + tile, cols]
        s = ug
        for i in range(1, w):
            s = s + uext[halo - i:halo - i + tile, cols]
        pbuf[:, cols] = (s * (1.0 / w) - ug).astype(BF16)
        if g == len(POOL_WINDOWS) - 1:
            mix_ref[:, 0:D_POOL] = (_dot(pbuf[...], w_pool_ref[...]) * pscale_ref[...]).astype(BF16)
        yield

    for j in range(tile // BLOCK):
        rows = slice(j * BLOCK, (j + 1) * BLOCK)
        keys = slice(j * BLOCK, (j + 2) * BLOCK)
        sel = jnp.where(first_tile, 0, 1) if j == 0 else 1
        for kv in range(N_KV_HEADS):
            tiles = (2 * kv, 2 * kv + 1)
            lhs = jnp.concatenate([qbuf[rows, p * LANES:(p + 1) * LANES] for p in tiles], axis=0)
            nk = 2 * BLOCK
            s_all = _dot_nt(lhs, jnp.concatenate([kvar[2 * kv, keys, :], kvar[2 * kv + 1, keys, :]], axis=0))
            es, dens = [], []
            for half in range(2):
                heads = [2 * p + half for p in tiles]
                s = s_all[:, half * nk:(half + 1) * nk]
                s = s + jnp.concatenate([bias_ref[sel, hd] for hd in heads], axis=0)
                sink = jnp.concatenate([jnp.full((BLOCK, 1), sink_ref[0, hd], F32) for hd in heads], axis=0)
                m = jnp.maximum(jnp.max(s, axis=-1, keepdims=True), sink)
                e = jnp.exp(s - m)
                dens.append(jnp.sum(e, axis=-1, keepdims=True) + jnp.exp(sink - m))
                es.append(e.astype(BF16))
            o = _dot(jnp.concatenate(es, axis=0),
                     jnp.concatenate([vvar[2 * kv, keys, :], vvar[2 * kv + 1, keys, :]], axis=1))
            acc = (o[0:nk, 0:LANES] * (1.0 / dens[0]) + o[nk:, LANES:] * (1.0 / dens[1]))
            for n, p in enumerate(tiles):
                mix_ref[rows, D_POOL + p * LANES:D_POOL + (p + 1) * LANES] = (
                    acc[n * BLOCK:(n + 1) * BLOCK].astype(BF16))
            yield

    for i in range(4):
        kvar[i, 0:BLOCK, :] = kvar[i, tile:tile + BLOCK, :]
        vvar[i, 0:BLOCK, :] = vvar[i, tile:tile + BLOCK, :]
    uext[0:halo, :] = uext[tile:tile + halo, :]


def _sample_front_kernel(tab_ref, sink_ref, x_ref, ck_ref, cv_ref, st_ref, wf_out_ref,
                         bkt_p_ref, bkt_c_ref, bkt_n_ref, meta_ref, g_ref, b_ref, wf_in_ref, wf_pool_ref, pscale_ref,
                         mix_ref, nk_ref, nv_ref, np_ref, wb_out_ref,
                         bias_p_ref, meta_proj_ref, w_in_ref, w_pool_bd_ref,
                         o_s, bias_c_ref, bias_n_ref, w_pool_ref):
    @pl.when(pl.program_id(0) == 0)
    def _():
        _setup_tables(tab_ref, bkt_p_ref, bkt_c_ref, bkt_n_ref, meta_ref, g_ref, b_ref, wf_in_ref, wf_pool_ref,
                      bias_p_ref, bias_c_ref, bias_n_ref, meta_proj_ref, w_in_ref, w_pool_ref, w_pool_bd_ref)

    wb_out_ref[...] = wf_out_ref[...].astype(BF16)
    ns, tokens, d_model = x_ref.shape
    lane = lax.broadcasted_iota(jnp.int32, (1, LANES), 1)
    low = lane < HEAD_DIM

    x = jnp.concatenate([x_ref[:, t, :] for t in range(tokens)], axis=0)
    h = _layer_norm(x, g_ref[...], b_ref[...])
    proj = _dot(h.astype(BF16), w_in_ref[...])
    u = proj[:, 0:D_POOL]
    q = proj[:, D_POOL:D_POOL + D_ATTN] * Q_SCALE
    k = proj[:, D_POOL + D_ATTN:D_POOL + D_ATTN + D_KV]
    v = proj[:, D_POOL + D_ATTN + D_KV:]

    ext = [st_ref[r] for r in range(POOL_BUF)]
    ext += [u[t * ns:(t + 1) * ns] for t in range(tokens)]
    ps = []
    for t in range(tokens):
        idx = POOL_BUF + t
        pos = PAST_LEN - POOL_BUF + idx
        parts = []
        for g, w in enumerate(POOL_WINDOWS):
            cols = slice(g * POOL_GROUP_DIM, (g + 1) * POOL_GROUP_DIM)
            lo = max(idx + 1 - w, 0)
            s = ext[idx][:, cols]
            for r in range(idx - 1, lo - 1, -1):
                s = s + ext[r][:, cols]
            parts.append(s * (1.0 / min(w, pos + 1)) - ext[idx][:, cols])
        ps.append(jnp.concatenate(parts, axis=1))
    p = jnp.concatenate(ps, axis=0).astype(BF16)
    zs = [_dot(p[:, g * POOL_GROUP_DIM:(g + 1) * POOL_GROUP_DIM], w_pool_ref[g])
          for g in range(len(POOL_WINDOWS))]
    z = (jnp.concatenate(zs, axis=1) * pscale_ref[...]).astype(BF16)
    for r in range(POOL_BUF):
        np_ref[r] = ext[r + tokens]

    qh = []
    for hd in range(N_HEADS):
        tl = q[:, (hd // 2) * LANES:(hd // 2 + 1) * LANES]
        kv = hd // (N_HEADS // N_KV_HEADS)
        want_low = kv == 0
        is_low = hd % 2 == 0
        src = tl if want_low == is_low else pltpu.roll(tl, HEAD_DIM, axis=1)
        qh.append(jnp.where(low, src, 0.0) if want_low else jnp.where(low, 0.0, src))

    gr = SEQ_GROUP
    rows_per_head = tokens * gr
    sub = lax.broadcasted_iota(jnp.int32, (N_HEADS * rows_per_head, 1), 0) % gr
    sink = jnp.concatenate([jnp.full((rows_per_head, 1), sink_ref[0, hd], F32) for hd in range(N_HEADS)],
                           axis=0)
    zpad = jnp.zeros((LANES - rows_per_head, LANES), F32)
    ztop = jnp.zeros((WINDOW - SUBLANES, LANES), F32)
    sub8 = lax.broadcasted_iota(jnp.int32, (SUBLANES, 1), 0)
    keep = WINDOW - tokens

    def appended(old_ref, news, j):
        tail = jnp.zeros((SUBLANES, LANES), F32)
        for t in range(tokens):
            row = SUBLANES - tokens + t
            tail = jnp.where(sub8 == row, pltpu.roll(news[t], (row - j) % SUBLANES, axis=0), tail)
        cols = jnp.concatenate([ztop, tail], axis=0).T
        return jnp.where(lane >= keep, cols, pltpu.roll(old_ref[...], keep, axis=1))

    for base in range(0, ns, gr):
        k_t = [k[t * ns + base:t * ns + base + gr] for t in range(tokens)]
        v_t = [v[t * ns + base:t * ns + base + gr] for t in range(tokens)]
        for s in range(gr):
            nk_ref[base + s] = appended(ck_ref.at[base + s], k_t, s)
            nv_ref[base + s] = appended(cv_ref.at[base + s], v_t, s)

    for gi in range(ns // gr):
        base = gi * gr
        lhs = jnp.concatenate([qh[hd][t * ns + base:t * ns + base + gr]
                               for hd in range(N_HEADS) for t in range(tokens)], axis=0).astype(BF16)
        sc = None
        for s in range(gr):
            s_one = _dot(lhs, ck_ref[base + s].astype(BF16))
            sc = s_one if sc is None else jnp.where(sub == s, s_one, sc)
        sc = sc + bias_c_ref[...]
        k_t = [k[t * ns + base:t * ns + base + gr] for t in range(tokens)]
        v_t = [v[t * ns + base:t * ns + base + gr] for t in range(tokens)]
        k_new = jnp.concatenate(k_t + [zpad], axis=0)
        v_new = jnp.concatenate(v_t + [zpad], axis=0)
        sn = _dot_nt(lhs, k_new.astype(BF16)) + bias_n_ref[...]
        m = jnp.maximum(jnp.maximum(jnp.max(sc, axis=-1, keepdims=True),
                                    jnp.max(sn, axis=-1, keepdims=True)), sink)
        ec = jnp.exp(sc - m)
        en = jnp.exp(sn - m)
        den = (jnp.sum(ec, axis=-1, keepdims=True) + jnp.sum(en, axis=-1, keepdims=True)
               + jnp.exp(sink - m))
        o = _dot(en.astype(BF16), v_new.astype(BF16))
        for s in range(gr):
            o = o + _dot_nt(jnp.where(sub == s, ec, 0.0).astype(BF16), cv_ref[base + s].astype(BF16))
        o = o * (1.0 / den)
        for t in range(tokens):
            for p in range(N_HEADS // 2):
                a = o[(2 * p * tokens + t) * gr:(2 * p * tokens + t + 1) * gr]
                c = o[((2 * p + 1) * tokens + t) * gr:((2 * p + 1) * tokens + t + 1) * gr]
                if p < N_HEADS // 4:
                    c = pltpu.roll(c, HEAD_DIM, axis=1)
                else:
                    a = pltpu.roll(a, HEAD_DIM, axis=1)
                o_s[t * ns + base:t * ns + base + gr, p * LANES:(p + 1) * LANES] = jnp.where(low, a, c)

    for t in range(tokens):
        mix_ref[:, t * d_model:t * d_model + D_POOL] = z[t * ns:(t + 1) * ns]
        mix_ref[:, t * d_model + D_POOL:(t + 1) * d_model] = o_s[t * ns:(t + 1) * ns, :].astype(BF16)


def _pre_pieces(rows, res, mix, w_out_ref, g1_ref, b1_ref, h1_s, hb_s):
    h1_s[rows, :] = res() + _dot(mix(), w_out_ref[...])
    yield
    h = _layer_norm(h1_s[rows, :], g1_ref[...], b1_ref[...])
    h1_s[rows, :] = h
    hb_s[rows, :] = h.astype(BF16)
    yield


def _mlp_pieces(rows, a_s, h1_s, hb_s, w1_ref, w2_ref, g2_ref, b2_ref, y_ref):
    for c in range(w1_ref.shape[1] // UP_CHUNK):
        cols = slice(c * UP_CHUNK, (c + 1) * UP_CHUNK)
        a = _dot(hb_s[rows, :], w1_ref[:, cols])
        a_s[:, cols] = jnp.square(jnp.maximum(a, 0.0)).astype(BF16)
        if c == 0:
            y_ref[rows, :] = ALPHA * h1_s[rows, :]
        yield
    for n in range(y_ref.shape[1] // DOWN_CHUNK):
        cols = slice(n * DOWN_CHUNK, (n + 1) * DOWN_CHUNK)
        y_ref[rows, cols] = y_ref[rows, cols] + _dot(a_s[...], w2_ref[:, cols])
        yield
    y_ref[rows, :] = _layer_norm(y_ref[rows, :], g2_ref[...], b2_ref[...])
    yield


def _weave(main, n_main, sides):
    order = []
    for i in range(n_main):
        order.append(main)
        order += sides[i * len(sides) // n_main:(i + 1) * len(sides) // n_main]
    return order


def _main_kernel(tokens, tiles_per_seq, n_tiles,
                 sink_ref, x_ref, xs_ref, mixs_ref, ge_ref, be_ref, w_in_ref, w_pool_ref,
                 pscale_ref, bias_ref, meta_proj_ref, w_out_ref, g1_ref, b1_ref, g2_ref, b2_ref,
                 w1_hbm, w2_hbm,
                 y_ref, ys_ref, nk_ref, nv_ref, np_ref,
                 mixbuf, kvar, vvar, uext, qbuf, pbuf, hres, h1_s, hb_s, a_s, w1_ref, w2_ref, stage, sem):
    s = pl.program_id(0)
    is_first = s == 0
    first_tile = jnp.minimum(s, n_tiles - 1) % tiles_per_seq == 0
    tile, d_model = x_ref.shape
    ns = xs_ref.shape[0]
    half = tile // 2
    halves = (slice(0, half), slice(half, tile))
    pre_w = (w_out_ref, g1_ref, b1_ref, h1_s, hb_s)

    @pl.when(is_first)
    def _():
        def toks(rows):
            return range(rows.start // ns, rows.stop // ns)

        def sample_pre():
            for rows in halves:
                res = lambda: ALPHA * _layer_norm(jnp.concatenate([xs_ref[:, t, :] for t in toks(rows)], axis=0),
                                                  ge_ref[...], be_ref[...])
                mix = lambda: jnp.concatenate([mixs_ref[:, t * d_model:(t + 1) * d_model] for t in toks(rows)],
                                              axis=0)
                yield from _pre_pieces(rows, res, mix, *pre_w)

        cr, cc = stage.shape[1:]
        chunks = [(src, dst, r, c) for src, dst in ((w1_hbm, w1_ref), (w2_hbm, w2_ref))
                  for r in range(0, src.shape[0], cr) for c in range(0, src.shape[1], cc)]
        ring = [stage.at[i] for i in range(stage.shape[0])]
        ring += [buf.at[pl.ds(r, cr)] for buf in (hres, y_ref) for r in range(0, tile, cr)]
        slots = len(ring)

        def copy(i):
            src, _, r, c = chunks[i]
            return pltpu.make_async_copy(src.at[pl.ds(r, cr), pl.ds(c, cc)], ring[i % slots], sem.at[i % slots])

        for i in range(slots):
            copy(i).start()
        pre = sample_pre()
        every = len(chunks) // 4
        for i, (_, dst, r, c) in enumerate(chunks):
            if i % every == every // 2:
                next(pre, None)
            copy(i).wait()
            dst[r:r + cr, c:c + cc] = ring[i % slots][...].astype(BF16)
            if i + slots < len(chunks):
                copy(i + slots).start()
        for _ in pre:
            raise AssertionError("piece left unscheduled")

    @pl.when(first_tile)
    def _():
        _carry_init(meta_proj_ref, kvar, vvar, uext)

    mlp = [_mlp_pieces(rows, a_s, h1_s, hb_s, w1_ref, w2_ref, g2_ref, b2_ref, y_ref) for rows in halves]
    front = _front_pieces(first_tile, x_ref, sink_ref, ge_ref, be_ref, w_in_ref, w_pool_ref, pscale_ref,
                          bias_ref, mixbuf, nk_ref.at[0], nv_ref.at[0], np_ref.at[0], kvar, vvar, uext, qbuf, pbuf,
                          hres)
    pre = [_pre_pieces(rows, lambda rows=rows: hres[rows, :], lambda rows=rows: mixbuf[rows, :], *pre_w)
           for rows in halves]
    a, b, f, pa, pb = mlp[0], mlp[1], front, pre[0], pre[1]
    n_up = w1_ref.shape[1] // UP_CHUNK
    n_down = d_model // DOWN_CHUNK
    order = (
        _weave(a, n_up, [f] * 4)
        + _weave(a, n_down, [f] * 5)
        + _weave(b, n_up, [a] + [f] * 5)
        + _weave(b, n_down, [f, pa, pb, pa, pb]) + [b])
    for g in order:
        next(g, None)
    for g in (a, b, f, pa, pb):
        for _ in g:
            raise AssertionError("piece left unscheduled")

    @pl.when(is_first)
    def _():
        for t in range(tokens):
            ys_ref[:, t, :] = y_ref[t * ns:(t + 1) * ns, :]


def _const_spec(shape):
    return pl.BlockSpec(shape, lambda *_: (0,) * len(shape), pipeline_mode=pl.Buffered(1))


def _smem_spec():
    return pl.BlockSpec(memory_space=pltpu.SMEM)


def _sample_front_call(rel_table, sink, xs, ck, cv, st, w_out, meta_tokens, ge, be, w_in, w_pool, pscale):
    n_seq, tokens, d_model = xs.shape
    steps = n_seq // SAMPLE_SEQS
    srows = N_HEADS * tokens * SEQ_GROUP
    tables = [jnp.asarray(t) for t in _bucket_tables(tokens)]
    consts = tables + [meta_tokens, ge, be, w_in, w_pool, pscale]
    whole = lambda shape: pl.BlockSpec(shape, lambda i: (0,) * len(shape))
    seq_spec = lambda *tail: pl.BlockSpec((SAMPLE_SEQS,) + tail, lambda i: (i,) + (0,) * len(tail))
    state_spec = pl.BlockSpec((POOL_BUF, SAMPLE_SEQS, D_POOL), lambda i: (0, i, 0))
    slab_spec = pl.BlockSpec((w_out.shape[0] // steps, w_out.shape[1]), lambda i: (i, 0))
    once = [jax.ShapeDtypeStruct((2, N_HEADS, BLOCK, 2 * BLOCK), F32),
            jax.ShapeDtypeStruct((meta_tokens.shape[0], w_in.shape[1]), F32),
            jax.ShapeDtypeStruct(w_in.shape, BF16),
            jax.ShapeDtypeStruct((D_POOL, D_POOL), BF16)]
    return pl.pallas_call(
        _sample_front_kernel,
        grid=(steps,),
        in_specs=[_smem_spec(), _smem_spec(), seq_spec(tokens, d_model), seq_spec(D_KV, WINDOW),
                  seq_spec(D_KV, WINDOW), state_spec, slab_spec] + [_const_spec(c.shape) for c in consts],
        out_specs=[seq_spec(tokens * d_model), seq_spec(D_KV, WINDOW), seq_spec(D_KV, WINDOW), state_spec,
                   slab_spec] + [whole(o.shape) for o in once],
        out_shape=[jax.ShapeDtypeStruct((n_seq, tokens * d_model), BF16),
                   jax.ShapeDtypeStruct((n_seq, D_KV, WINDOW), F32),
                   jax.ShapeDtypeStruct((n_seq, D_KV, WINDOW), F32),
                   jax.ShapeDtypeStruct((POOL_BUF, n_seq, D_POOL), F32),
                   jax.ShapeDtypeStruct(w_out.shape, BF16)] + once,
        scratch_shapes=[pltpu.VMEM((tokens * SAMPLE_SEQS, D_ATTN), F32),
                        pltpu.VMEM((srows, WINDOW), F32),
                        pltpu.VMEM((srows, LANES), F32),
                        pltpu.VMEM(w_pool.shape, BF16)],
        compiler_params=pltpu.CompilerParams(dimension_semantics=("arbitrary",),
                                             vmem_limit_bytes=VMEM_LIMIT_BYTES),
        name="sample_front",
    )(rel_table, sink, xs, ck, cv, st, w_out, *consts)


def kernel(x_prompt, x_sample, cache_win_k, cache_win_v, state_pool, meta_tokens, ln_emb_g, ln_emb_b,
           rel_table, w_in, w_pool, pool_scale, sinks, w_out, ln1_g, ln1_b, w_mlp_in, w_mlp_out,
           ln2_g, ln2_b):
    batch, seq, d_model = x_prompt.shape
    n_seq, tokens, _ = x_sample.shape
    assert w_in.shape[0] == DEPTH and d_model == D_POOL + D_ATTN
    assert seq % PROMPT_TILE == 0 and n_seq % SAMPLE_SEQS == 0 and tokens * n_seq == PROMPT_TILE
    assert cache_win_k.shape[2] == WINDOW and state_pool.shape[2] == POOL_BUF and tokens <= POOL_BUF
    assert N_META == HALO and POOL_BUF <= HALO

    row = lambda a: a.reshape(1, -1).astype(F32)
    ge, be = row(ln_emb_g), row(ln_emb_b)
    pscale = row(pool_scale[0])
    sink = row(sinks[0])

    to_kd_pos = lambda c: jnp.swapaxes(c[0].reshape(n_seq, WINDOW, D_KV), 1, 2)
    mix_s, nk_s, nv_s, np_s, w_out_b, bias_p, meta_proj, w_in_b, w_pool_bd = _sample_front_call(
        rel_table.astype(F32), sink, x_sample, to_kd_pos(cache_win_k), to_kd_pos(cache_win_v),
        jnp.swapaxes(state_pool[0], 0, 1), w_out[0], meta_tokens.astype(F32), ge, be, w_in[0], w_pool[0], pscale)
    xs = x_sample
    w1, w2 = w_mlp_in[0], w_mlp_out[0]

    tile = PROMPT_TILE
    tiles_per_seq = seq // tile
    n_tiles = batch * tiles_per_seq
    xp = x_prompt.reshape(batch * seq, d_model)
    front_tile = lambda s: jnp.minimum(s, n_tiles - 1)
    finish_tile = lambda s: jnp.maximum(s - 1, 0)
    seq_of = lambda s: front_tile(s) // tiles_per_seq
    consts = (ge, be, w_in_b, w_pool_bd, pscale, bias_p, meta_proj, w_out_b, row(ln1_g[0]), row(ln1_b[0]),
              row(ln2_g[0]), row(ln2_b[0]))
    y_p, y_s, nk_p, nv_p, np_p = pl.pallas_call(
        functools.partial(_main_kernel, tokens, tiles_per_seq, n_tiles),
        grid=(n_tiles + 1,),
        in_specs=[_smem_spec(),
                  pl.BlockSpec((tile, d_model), lambda s: (front_tile(s), 0)),
                  _const_spec(xs.shape), _const_spec(mix_s.shape)] + [_const_spec(c.shape) for c in consts]
        + [pl.BlockSpec(memory_space=pl.ANY)] * 2,
        out_specs=[pl.BlockSpec((tile, d_model), lambda s: (finish_tile(s), 0)),
                   pl.BlockSpec(xs.shape, lambda s: (0, 0, 0)),
                   pl.BlockSpec((1, BLOCK, D_KV), lambda s: (seq_of(s), 0, 0)),
                   pl.BlockSpec((1, BLOCK, D_KV), lambda s: (seq_of(s), 0, 0)),
                   pl.BlockSpec((1, HALO, D_POOL), lambda s: (seq_of(s), 0, 0))],
        out_shape=[jax.ShapeDtypeStruct((batch * seq, d_model), F32),
                   jax.ShapeDtypeStruct(xs.shape, F32),
                   jax.ShapeDtypeStruct((batch, BLOCK, D_KV), F32),
                   jax.ShapeDtypeStruct((batch, BLOCK, D_KV), F32),
                   jax.ShapeDtypeStruct((batch, HALO, D_POOL), F32)],
        scratch_shapes=[pltpu.VMEM((tile, d_model), BF16),
                        pltpu.VMEM((4, tile + BLOCK, LANES), BF16),
                        pltpu.VMEM((4, tile + BLOCK, LANES), BF16),
                        pltpu.VMEM((tile + HALO, D_POOL), F32),
                        pltpu.VMEM((tile, D_ATTN), BF16),
                        pltpu.VMEM((tile, D_POOL), BF16),
                        pltpu.VMEM((tile, d_model), F32),
                        pltpu.VMEM((tile, d_model), F32),
                        pltpu.VMEM((tile, d_model), BF16),
                        pltpu.VMEM((tile // 2, w1.shape[1]), BF16),
                        pltpu.VMEM(w1.shape, BF16),
                        pltpu.VMEM(w2.shape, BF16),
                        pltpu.VMEM((WEIGHT_SLOTS,) + WEIGHT_CHUNK, F32),
                        pltpu.SemaphoreType.DMA((WEIGHT_SLOTS + 2 * (tile // WEIGHT_CHUNK[0]),))],
        compiler_params=pltpu.CompilerParams(dimension_semantics=("arbitrary",),
                                             vmem_limit_bytes=VMEM_LIMIT_BYTES),
        name="main",
    )(sink, xp, xs, mix_s, *consts, w1, w2)

    kv_shape = (DEPTH, -1, WINDOW, N_KV_HEADS, HEAD_DIM)
    from_kd_pos = lambda c: jnp.swapaxes(c, 1, 2).reshape(kv_shape)
    return (y_p.reshape(batch, seq, d_model), y_s,
            from_kd_pos(nk_p), from_kd_pos(nv_p),
            np_p[:, HALO - POOL_BUF:, :].reshape(DEPTH, batch, POOL_BUF, D_POOL),
            from_kd_pos(nk_s), from_kd_pos(nv_s), jnp.swapaxes(np_s, 0, 1)[None])
```

```python
import functools
import math

import jax
import jax.numpy as jnp
import numpy as np
from jax import lax
from jax.experimental import pallas as pl
from jax.experimental.pallas import tpu as pltpu

N_META = 16
POOL_WINDOWS = (2, 4, 8, 16)
POOL_GROUP_DIM = 128
D_POOL = len(POOL_WINDOWS) * POOL_GROUP_DIM
POOL_BUF = max(POOL_WINDOWS) - 1
N_HEADS = 8
HEAD_DIM = 64
D_ATTN = N_HEADS * HEAD_DIM
N_KV_HEADS = 2
D_KV = N_KV_HEADS * HEAD_DIM
WINDOW = 128
BLOCK = 128
REL_BUCKETS = 32
REL_MAX_DIST = 128
PAST_LEN = 8192
DEPTH = 1
ALPHA = (2.0 * DEPTH) ** 0.25
LN_EPS = 1e-5
Q_SCALE = HEAD_DIM ** -0.5

LANES = 128
SUBLANES = 8
VMEM_LIMIT_BYTES = 60 * 1024 * 1024

PROMPT_TILE = 512
IN_ROWS = 256
UP_CHUNK = 512
DOWN_CHUNK = 256
SAMPLE_SEQS = 32
SEQ_GROUP = SUBLANES
HALO = 2 * SUBLANES
WEIGHT_CHUNK = (256, 1024)
WEIGHT_SLOTS = 2

F32 = jnp.float32
BF16 = jnp.bfloat16
NEG_INF = float("-inf")


def _rel_bucket(dist):
    n = np.maximum(dist, 0)
    max_exact = REL_BUCKETS // 2
    nf = np.maximum(n, 1).astype(np.float64)
    large = max_exact + (np.log(nf / max_exact) / math.log(REL_MAX_DIST / max_exact)
                         * (REL_BUCKETS - max_exact)).astype(np.int32)
    large = np.minimum(large, REL_BUCKETS - 1)
    return np.where(n < max_exact, n, large).astype(np.int32)


def _bucket_tables(tokens):
    prompt = np.tile(_rel_bucket((BLOCK - np.arange(BLOCK)) % BLOCK)[None, :], (SUBLANES, 1)).astype(np.int32)
    t = np.repeat(np.arange(tokens), SEQ_GROUP)
    s = np.tile(np.arange(SEQ_GROUP), tokens)
    dist = (t[:, None] + WINDOW) - np.arange(WINDOW)[None, :]
    ok = (dist >= 0) & (dist < WINDOW)
    cache = np.where(ok, _rel_bucket(dist), -1).astype(np.int32)
    dist = t[:, None] - t[None, :]
    ok = (dist >= 0) & (s[:, None] == s[None, :])
    new = np.full((tokens * SEQ_GROUP, LANES), -1, np.int32)
    new[:, :tokens * SEQ_GROUP] = np.where(ok, _rel_bucket(dist), -1)
    return prompt, cache, new


def _layer_norm(x, g, b):
    mu = jnp.mean(x, axis=-1, keepdims=True)
    xc = x - mu
    var = jnp.mean(xc * xc, axis=-1, keepdims=True)
    return xc * lax.rsqrt(var + LN_EPS) * g + b


def _dot(a, b):
    return jnp.dot(a, b, preferred_element_type=F32)


def _dot_nt(a, b):
    return lax.dot_general(a, b, (((1,), (1,)), ((), ())), preferred_element_type=F32)


def _kv_lane_variants(x, low):
    xr = pltpu.roll(x, HEAD_DIM, axis=1)
    zero = jnp.zeros_like(x)
    return (jnp.where(low, x, zero).astype(BF16), jnp.where(low, zero, xr).astype(BF16),
            jnp.where(low, xr, zero).astype(BF16), jnp.where(low, zero, x).astype(BF16))


def _setup_tables(tab_ref, bkt_p_ref, bkt_c_ref, bkt_n_ref, meta_ref, g_ref, b_ref, wf_in_ref, wf_pool_ref,
                  bias_p_ref, bias_c_ref, bias_n_ref, meta_proj_ref, w_in_ref, w_pool_ref, w_pool_bd_ref):
    w_in_ref[...] = wf_in_ref[...].astype(BF16)
    w_pool_ref[...] = wf_pool_ref[...].astype(BF16)
    w_pool_bd_ref[...] = jnp.zeros_like(w_pool_bd_ref)
    for g in range(len(POOL_WINDOWS)):
        cols = slice(g * POOL_GROUP_DIM, (g + 1) * POOL_GROUP_DIM)
        w_pool_bd_ref[cols, cols] = w_pool_ref[g]
    def lookup(bucket):
        accs = [jnp.full(bucket.shape, NEG_INF, F32)] * N_HEADS
        for i in range(REL_BUCKETS):
            hit = bucket == i
            accs = [jnp.where(hit, tab_ref[i, h], acc) for h, acc in enumerate(accs)]
        return accs

    rows = bkt_c_ref.shape[0]
    q = lax.broadcasted_iota(jnp.int32, (BLOCK, BLOCK), 0)
    k = lax.broadcasted_iota(jnp.int32, (BLOCK, BLOCK), 1)
    per_dist, cache_bias, new_bias = lookup(bkt_p_ref[...]), lookup(bkt_c_ref[...]), lookup(bkt_n_ref[...])
    for h in range(N_HEADS):
        toeplitz = pltpu.roll(jnp.concatenate([per_dist[h]] * (BLOCK // SUBLANES), axis=0), 0, 1,
                              stride=1, stride_axis=0)
        cur = jnp.where(k <= q, toeplitz, NEG_INF)
        for table, prev_ok in enumerate(((k > q) & (k >= BLOCK - N_META), k > q)):
            bias_p_ref[table, h, :, 0:BLOCK] = jnp.where(prev_ok, toeplitz, NEG_INF)
            bias_p_ref[table, h, :, BLOCK:] = cur
        bias_c_ref[h * rows:(h + 1) * rows, :] = cache_bias[h]
        bias_n_ref[h * rows:(h + 1) * rows, :] = new_bias[h]
    hm = _layer_norm(meta_ref[...], g_ref[...], b_ref[...])
    meta_proj_ref[...] = _dot(hm.astype(BF16), w_in_ref[...])


def _carry_init(meta_proj_ref, kvar, vvar, uext):
    low = lax.broadcasted_iota(jnp.int32, (1, LANES), 1) < HEAD_DIM
    pad = jnp.zeros((BLOCK - N_META, LANES), BF16)
    kq = _kv_lane_variants(meta_proj_ref[:, D_POOL + D_ATTN:D_POOL + D_ATTN + D_KV], low)
    vq = _kv_lane_variants(meta_proj_ref[:, D_POOL + D_ATTN + D_KV:], low)
    for i in range(4):
        kvar[i, 0:BLOCK, :] = jnp.concatenate([pad, kq[i]], axis=0)
        vvar[i, 0:BLOCK, :] = jnp.concatenate([pad, vq[i]], axis=0)
    uext[0:HALO, :] = meta_proj_ref[:, 0:D_POOL]


def _front_pieces(first_tile, seq_idx, x_ref, sink_ref, g_ref, b_ref, w_in_ref, w_pool_ref, pscale_ref, bias_ref,
                  mix_ref, nk_ref, nv_ref, np_ref, kvar, vvar, uext, qbuf, pbuf, hres):
    tile = x_ref.shape[0]
    halo = HALO
    low = lax.broadcasted_iota(jnp.int32, (1, LANES), 1) < HEAD_DIM

    for j in range(tile // IN_ROWS):
        rows = slice(j * IN_ROWS, (j + 1) * IN_ROWS)
        h = _layer_norm(x_ref[rows, :], g_ref[...], b_ref[...])
        hres[rows, :] = ALPHA * h
        proj = _dot(h.astype(BF16), w_in_ref[...])
        u = proj[:, 0:D_POOL]
        k = proj[:, D_POOL + D_ATTN:D_POOL + D_ATTN + D_KV]
        v = proj[:, D_POOL + D_ATTN + D_KV:]
        qbuf[rows, :] = (proj[:, D_POOL:D_POOL + D_ATTN] * Q_SCALE).astype(BF16)
        uext[halo + rows.start:halo + rows.stop, :] = u
        kq = _kv_lane_variants(k, low)
        vq = _kv_lane_variants(v, low)
        for i in range(4):
            kvar[i, BLOCK + rows.start:BLOCK + rows.stop, :] = kq[i]
            vvar[i, BLOCK + rows.start:BLOCK + rows.stop, :] = vq[i]
        if rows.stop == tile:
            nk_ref[...] = k[IN_ROWS - BLOCK:, :].T
            nv_ref[...] = v[IN_ROWS - BLOCK:, :].T
            mine = lax.broadcasted_iota(jnp.int32, (1, np_ref.shape[1], 1), 1) == seq_idx
            np_ref[...] = jnp.where(mine, u[IN_ROWS - halo:, :][:, None, :], np_ref[...])
        yield

    for g, w in enumerate(POOL_WINDOWS):
        cols = slice(g * POOL_GROUP_DIM, (g + 1) * POOL_GROUP_DIM)
        ug = uext[halo:halo + tile, cols]
        s = ug
        for i in range(1, w):
            s = s + uext[halo - i:halo - i + tile, cols]
        pbuf[:, cols] = (s * (1.0 / w) - ug).astype(BF16)
        if g == len(POOL_WINDOWS) - 1:
            mix_ref[:, 0:D_POOL] = (_dot(pbuf[...], w_pool_ref[...]) * pscale_ref[...]).astype(BF16)
        yield

    for j in range(tile // BLOCK):
        rows = slice(j * BLOCK, (j + 1) * BLOCK)
        keys = slice(j * BLOCK, (j + 2) * BLOCK)
        sel = jnp.where(first_tile, 0, 1) if j == 0 else 1
        for kv in range(N_KV_HEADS):
            tiles = (2 * kv, 2 * kv + 1)
            lhs = jnp.concatenate([qbuf[rows, p * LANES:(p + 1) * LANES] for p in tiles], axis=0)
            nk = 2 * BLOCK
            s_all = _dot_nt(lhs, jnp.concatenate([kvar[2 * kv, keys, :], kvar[2 * kv + 1, keys, :]], axis=0))
            es, dens = [], []
            for half in range(2):
                heads = [2 * p + half for p in tiles]
                s = s_all[:, half * nk:(half + 1) * nk]
                s = s + jnp.concatenate([bias_ref[sel, hd] for hd in heads], axis=0)
                sink = jnp.concatenate([jnp.full((BLOCK, 1), sink_ref[0, hd], F32) for hd in heads], axis=0)
                m = jnp.maximum(jnp.max(s, axis=-1, keepdims=True), sink)
                e = jnp.exp(s - m)
                dens.append(jnp.sum(e, axis=-1, keepdims=True) + jnp.exp(sink - m))
                es.append(e.astype(BF16))
            o = _dot(jnp.concatenate(es, axis=0),
                     jnp.concatenate([vvar[2 * kv, keys, :], vvar[2 * kv + 1, keys, :]], axis=1))
            acc = (o[0:nk, 0:LANES] * (1.0 / dens[0]) + o[nk:, LANES:] * (1.0 / dens[1]))
            for n, p in enumerate(tiles):
                mix_ref[rows, D_POOL + p * LANES:D_POOL + (p + 1) * LANES] = (
                    acc[n * BLOCK:(n + 1) * BLOCK].astype(BF16))
            yield

    for i in range(4):
        kvar[i, 0:BLOCK, :] = kvar[i, tile:tile + BLOCK, :]
        vvar[i, 0:BLOCK, :] = vvar[i, tile:tile + BLOCK, :]
    uext[0:halo, :] = uext[tile:tile + halo, :]


def _sample_front_kernel(tab_ref, sink_ref, x_ref, ck_ref, cv_ref, st_ref, wf_out_ref,
                         bkt_p_ref, bkt_c_ref, bkt_n_ref, meta_ref, g_ref, b_ref, wf_in_ref, wf_pool_ref, pscale_ref,
                         mix_ref, nk_ref, nv_ref, np_ref, wb_out_ref,
                         bias_p_ref, meta_proj_ref, w_in_ref, w_pool_bd_ref,
                         o_s, bias_c_ref, bias_n_ref, w_pool_ref):
    @pl.when(pl.program_id(0) == 0)
    def _():
        _setup_tables(tab_ref, bkt_p_ref, bkt_c_ref, bkt_n_ref, meta_ref, g_ref, b_ref, wf_in_ref, wf_pool_ref,
                      bias_p_ref, bias_c_ref, bias_n_ref, meta_proj_ref, w_in_ref, w_pool_ref, w_pool_bd_ref)

    wb_out_ref[...] = wf_out_ref[...].astype(BF16)
    ns, tokens, d_model = x_ref.shape
    lane = lax.broadcasted_iota(jnp.int32, (1, LANES), 1)
    low = lane < HEAD_DIM

    x = jnp.concatenate([x_ref[:, t, :] for t in range(tokens)], axis=0)
    h = _layer_norm(x, g_ref[...], b_ref[...])
    proj = _dot(h.astype(BF16), w_in_ref[...])
    u = proj[:, 0:D_POOL]
    q = proj[:, D_POOL:D_POOL + D_ATTN] * Q_SCALE
    k = proj[:, D_POOL + D_ATTN:D_POOL + D_ATTN + D_KV]
    v = proj[:, D_POOL + D_ATTN + D_KV:]

    ext = [st_ref[r] for r in range(POOL_BUF)]
    ext += [u[t * ns:(t + 1) * ns] for t in range(tokens)]
    ps = []
    for t in range(tokens):
        idx = POOL_BUF + t
        pos = PAST_LEN - POOL_BUF + idx
        parts = []
        for g, w in enumerate(POOL_WINDOWS):
            cols = slice(g * POOL_GROUP_DIM, (g + 1) * POOL_GROUP_DIM)
            lo = max(idx + 1 - w, 0)
            s = ext[idx][:, cols]
            for r in range(idx - 1, lo - 1, -1):
                s = s + ext[r][:, cols]
            parts.append(s * (1.0 / min(w, pos + 1)) - ext[idx][:, cols])
        ps.append(jnp.concatenate(parts, axis=1))
    p = jnp.concatenate(ps, axis=0).astype(BF16)
    zs = [_dot(p[:, g * POOL_GROUP_DIM:(g + 1) * POOL_GROUP_DIM], w_pool_ref[g])
          for g in range(len(POOL_WINDOWS))]
    z = (jnp.concatenate(zs, axis=1) * pscale_ref[...]).astype(BF16)
    for r in range(POOL_BUF):
        np_ref[r] = ext[r + tokens]

    qh = []
    for hd in range(N_HEADS):
        tl = q[:, (hd // 2) * LANES:(hd // 2 + 1) * LANES]
        kv = hd // (N_HEADS // N_KV_HEADS)
        want_low = kv == 0
        is_low = hd % 2 == 0
        src = tl if want_low == is_low else pltpu.roll(tl, HEAD_DIM, axis=1)
        qh.append(jnp.where(low, src, 0.0) if want_low else jnp.where(low, 0.0, src))

    gr = SEQ_GROUP
    rows_per_head = tokens * gr
    sub = lax.broadcasted_iota(jnp.int32, (N_HEADS * rows_per_head, 1), 0) % gr
    sink = jnp.concatenate([jnp.full((rows_per_head, 1), sink_ref[0, hd], F32) for hd in range(N_HEADS)],
                           axis=0)
    zpad = jnp.zeros((LANES - rows_per_head, LANES), F32)
    ztop = jnp.zeros((WINDOW - SUBLANES, LANES), F32)
    sub8 = lax.broadcasted_iota(jnp.int32, (SUBLANES, 1), 0)
    keep = WINDOW - tokens

    def appended(old_ref, news, j):
        tail = jnp.zeros((SUBLANES, LANES), F32)
        for t in range(tokens):
            row = SUBLANES - tokens + t
            tail = jnp.where(sub8 == row, pltpu.roll(news[t], (row - j) % SUBLANES, axis=0), tail)
        cols = jnp.concatenate([ztop, tail], axis=0).T
        return jnp.where(lane >= keep, cols, pltpu.roll(old_ref[...], keep, axis=1))

    for base in range(0, ns, gr):
        k_t = [k[t * ns + base:t * ns + base + gr] for t in range(tokens)]
        v_t = [v[t * ns + base:t * ns + base + gr] for t in range(tokens)]
        for s in range(gr):
            nk_ref[base + s] = appended(ck_ref.at[base + s], k_t, s)
            nv_ref[base + s] = appended(cv_ref.at[base + s], v_t, s)

    for gi in range(ns // gr):
        base = gi * gr
        lhs = jnp.concatenate([qh[hd][t * ns + base:t * ns + base + gr]
                               for hd in range(N_HEADS) for t in range(tokens)], axis=0).astype(BF16)
        sc = None
        for s in range(gr):
            s_one = _dot(lhs, ck_ref[base + s].astype(BF16))
            sc = s_one if sc is None else jnp.where(sub == s, s_one, sc)
        sc = sc + bias_c_ref[...]
        k_t = [k[t * ns + base:t * ns + base + gr] for t in range(tokens)]
        v_t = [v[t * ns + base:t * ns + base + gr] for t in range(tokens)]
        k_new = jnp.concatenate(k_t + [zpad], axis=0)
        v_new = jnp.concatenate(v_t + [zpad], axis=0)
        sn = _dot_nt(lhs, k_new.astype(BF16)) + bias_n_ref[...]
        m = jnp.maximum(jnp.maximum(jnp.max(sc, axis=-1, keepdims=True),
                                    jnp.max(sn, axis=-1, keepdims=True)), sink)
        ec = jnp.exp(sc - m)
        en = jnp.exp(sn - m)
        den = (jnp.sum(ec, axis=-1, keepdims=True) + jnp.sum(en, axis=-1, keepdims=True)
               + jnp.exp(sink - m))
        o = _dot(en.astype(BF16), v_new.astype(BF16))
        for s in range(gr):
            o = o + _dot_nt(jnp.where(sub == s, ec, 0.0).astype(BF16), cv_ref[base + s].astype(BF16))
        o = o * (1.0 / den)
        for t in range(tokens):
            for p in range(N_HEADS // 2):
                a = o[(2 * p * tokens + t) * gr:(2 * p * tokens + t + 1) * gr]
                c = o[((2 * p + 1) * tokens + t) * gr:((2 * p + 1) * tokens + t + 1) * gr]
                if p < N_HEADS // 4:
                    c = pltpu.roll(c, HEAD_DIM, axis=1)
                else:
                    a = pltpu.roll(a, HEAD_DIM, axis=1)
                o_s[t * ns + base:t * ns + base + gr, p * LANES:(p + 1) * LANES] = jnp.where(low, a, c)

    for t in range(tokens):
        mix_ref[:, t * d_model:t * d_model + D_POOL] = z[t * ns:(t + 1) * ns]
        mix_ref[:, t * d_model + D_POOL:(t + 1) * d_model] = o_s[t * ns:(t + 1) * ns, :].astype(BF16)


def _pre_pieces(rows, res, mix, w_out_ref, g1_ref, b1_ref, h1_s, hb_s):
    h1_s[rows, :] = res() + _dot(mix(), w_out_ref[...])
    yield
    h = _layer_norm(h1_s[rows, :], g1_ref[...], b1_ref[...])
    h1_s[rows, :] = h
    hb_s[rows, :] = h.astype(BF16)
    yield


def _mlp_pieces(rows, a_s, h1_s, hb_s, w1_ref, w2_ref, g2_ref, b2_ref, y_ref):
    for c in range(w1_ref.shape[1] // UP_CHUNK):
        cols = slice(c * UP_CHUNK, (c + 1) * UP_CHUNK)
        a = _dot(hb_s[rows, :], w1_ref[:, cols])
        a_s[:, cols] = jnp.square(jnp.maximum(a, 0.0)).astype(BF16)
        if c == 0:
            y_ref[rows, :] = ALPHA * h1_s[rows, :]
        yield
    for n in range(y_ref.shape[1] // DOWN_CHUNK):
        cols = slice(n * DOWN_CHUNK, (n + 1) * DOWN_CHUNK)
        y_ref[rows, cols] = y_ref[rows, cols] + _dot(a_s[...], w2_ref[:, cols])
        yield
    y_ref[rows, :] = _layer_norm(y_ref[rows, :], g2_ref[...], b2_ref[...])
    yield


def _weave(main, n_main, sides):
    order = []
    for i in range(n_main):
        order.append(main)
        order += sides[i * len(sides) // n_main:(i + 1) * len(sides) // n_main]
    return order


def _main_kernel(tokens, tiles_per_seq, n_tiles,
                 sink_ref, x_ref, xs_ref, mixs_ref, ge_ref, be_ref, w_in_ref, w_pool_ref,
                 pscale_ref, bias_ref, meta_proj_ref, w_out_ref, g1_ref, b1_ref, g2_ref, b2_ref,
                 w1_hbm, w2_hbm,
                 y_ref, ys_ref, nk_ref, nv_ref, np_ref,
                 mixbuf, kvar, vvar, uext, qbuf, pbuf, hres, h1_s, hb_s, a_s, w1_ref, w2_ref, stage, sem):
    s = pl.program_id(0)
    is_first = s == 0
    front_tile = jnp.minimum(s, n_tiles - 1)
    first_tile = front_tile % tiles_per_seq == 0
    tile, d_model = x_ref.shape
    ns = xs_ref.shape[0]
    half = tile // 2
    halves = (slice(0, half), slice(half, tile))
    pre_w = (w_out_ref, g1_ref, b1_ref, h1_s, hb_s)

    @pl.when(is_first)
    def _():
        np_ref[...] = jnp.zeros_like(np_ref)
        def toks(rows):
            return range(rows.start // ns, rows.stop // ns)

        def sample_pre():
            for rows in halves:
                res = lambda: ALPHA * _layer_norm(jnp.concatenate([xs_ref[:, t, :] for t in toks(rows)], axis=0),
                                                  ge_ref[...], be_ref[...])
                mix = lambda: jnp.concatenate([mixs_ref[:, t * d_model:(t + 1) * d_model] for t in toks(rows)],
                                              axis=0)
                yield from _pre_pieces(rows, res, mix, *pre_w)

        cr, cc = stage.shape[1:]
        chunks = [(src, dst, r, c) for src, dst in ((w1_hbm, w1_ref), (w2_hbm, w2_ref))
                  for r in range(0, src.shape[0], cr) for c in range(0, src.shape[1], cc)]
        ring = [stage.at[i] for i in range(stage.shape[0])]
        ring += [buf.at[pl.ds(r, cr)] for buf in (hres, y_ref) for r in range(0, tile, cr)]
        slots = len(ring)

        def copy(i):
            src, _, r, c = chunks[i]
            return pltpu.make_async_copy(src.at[pl.ds(r, cr), pl.ds(c, cc)], ring[i % slots], sem.at[i % slots])

        for i in range(slots):
            copy(i).start()
        pre = sample_pre()
        every = len(chunks) // 4
        for i, (_, dst, r, c) in enumerate(chunks):
            if i % every == every // 2:
                next(pre, None)
            copy(i).wait()
            dst[r:r + cr, c:c + cc] = ring[i % slots][...].astype(BF16)
            if i + slots < len(chunks):
                copy(i + slots).start()
        for _ in pre:
            raise AssertionError("piece left unscheduled")

    @pl.when(first_tile)
    def _():
        _carry_init(meta_proj_ref, kvar, vvar, uext)

    mlp = [_mlp_pieces(rows, a_s, h1_s, hb_s, w1_ref, w2_ref, g2_ref, b2_ref, y_ref) for rows in halves]
    front = _front_pieces(first_tile, front_tile // tiles_per_seq, x_ref, sink_ref, ge_ref, be_ref, w_in_ref,
                          w_pool_ref, pscale_ref,
                          bias_ref, mixbuf, nk_ref.at[0], nv_ref.at[0], np_ref, kvar, vvar, uext, qbuf, pbuf,
                          hres)
    pre = [_pre_pieces(rows, lambda rows=rows: hres[rows, :], lambda rows=rows: mixbuf[rows, :], *pre_w)
           for rows in halves]
    a, b, f, pa, pb = mlp[0], mlp[1], front, pre[0], pre[1]
    n_up = w1_ref.shape[1] // UP_CHUNK
    n_down = d_model // DOWN_CHUNK
    order = (
        _weave(a, n_up, [f] * 4)
        + _weave(a, n_down, [f] * 5)
        + _weave(b, n_up, [a] + [f] * 5)
        + _weave(b, n_down, [f, pa, pb, pa, pb]) + [b])
    for g in order:
        next(g, None)
    for g in (a, b, f, pa, pb):
        for _ in g:
            raise AssertionError("piece left unscheduled")

    @pl.when(is_first)
    def _():
        for t in range(tokens):
            ys_ref[:, t, :] = y_ref[t * ns:(t + 1) * ns, :]


def _const_spec(shape):
    return pl.BlockSpec(shape, lambda *_: (0,) * len(shape), pipeline_mode=pl.Buffered(1))


def _smem_spec():
    return pl.BlockSpec(memory_space=pltpu.SMEM)


def _sample_front_call(rel_table, sink, xs, ck, cv, st, w_out, meta_tokens, ge, be, w_in, w_pool, pscale):
    n_seq, tokens, d_model = xs.shape
    steps = n_seq // SAMPLE_SEQS
    srows = N_HEADS * tokens * SEQ_GROUP
    tables = [jnp.asarray(t) for t in _bucket_tables(tokens)]
    consts = tables + [meta_tokens, ge, be, w_in, w_pool, pscale]
    whole = lambda shape: pl.BlockSpec(shape, lambda i: (0,) * len(shape))
    seq_spec = lambda *tail: pl.BlockSpec((SAMPLE_SEQS,) + tail, lambda i: (i,) + (0,) * len(tail))
    state_spec = pl.BlockSpec((POOL_BUF, SAMPLE_SEQS, D_POOL), lambda i: (0, i, 0))
    slab_spec = pl.BlockSpec((w_out.shape[0] // steps, w_out.shape[1]), lambda i: (i, 0))
    once = [jax.ShapeDtypeStruct((2, N_HEADS, BLOCK, 2 * BLOCK), F32),
            jax.ShapeDtypeStruct((meta_tokens.shape[0], w_in.shape[1]), F32),
            jax.ShapeDtypeStruct(w_in.shape, BF16),
            jax.ShapeDtypeStruct((D_POOL, D_POOL), BF16)]
    return pl.pallas_call(
        _sample_front_kernel,
        grid=(steps,),
        in_specs=[_smem_spec(), _smem_spec(), seq_spec(tokens, d_model), seq_spec(D_KV, WINDOW),
                  seq_spec(D_KV, WINDOW), state_spec, slab_spec] + [_const_spec(c.shape) for c in consts],
        out_specs=[seq_spec(tokens * d_model), seq_spec(D_KV, WINDOW), seq_spec(D_KV, WINDOW), state_spec,
                   slab_spec] + [whole(o.shape) for o in once],
        out_shape=[jax.ShapeDtypeStruct((n_seq, tokens * d_model), BF16),
                   jax.ShapeDtypeStruct((n_seq, D_KV, WINDOW), F32),
                   jax.ShapeDtypeStruct((n_seq, D_KV, WINDOW), F32),
                   jax.ShapeDtypeStruct((POOL_BUF, n_seq, D_POOL), F32),
                   jax.ShapeDtypeStruct(w_out.shape, BF16)] + once,
        scratch_shapes=[pltpu.VMEM((tokens * SAMPLE_SEQS, D_ATTN), F32),
                        pltpu.VMEM((srows, WINDOW), F32),
                        pltpu.VMEM((srows, LANES), F32),
                        pltpu.VMEM(w_pool.shape, BF16)],
        compiler_params=pltpu.CompilerParams(dimension_semantics=("arbitrary",),
                                             vmem_limit_bytes=VMEM_LIMIT_BYTES),
        name="sample_front",
    )(rel_table, sink, xs, ck, cv, st, w_out, *consts)


def kernel(x_prompt, x_sample, cache_win_k, cache_win_v, state_pool, meta_tokens, ln_emb_g, ln_emb_b,
           rel_table, w_in, w_pool, pool_scale, sinks, w_out, ln1_g, ln1_b, w_mlp_in, w_mlp_out,
           ln2_g, ln2_b):
    batch, seq, d_model = x_prompt.shape
    n_seq, tokens, _ = x_sample.shape
    assert w_in.shape[0] == DEPTH and d_model == D_POOL + D_ATTN
    assert seq % PROMPT_TILE == 0 and n_seq % SAMPLE_SEQS == 0 and tokens * n_seq == PROMPT_TILE
    assert cache_win_k.shape[2] == WINDOW and state_pool.shape[2] == POOL_BUF and tokens <= POOL_BUF
    assert N_META == HALO and POOL_BUF <= HALO

    row = lambda a: a.reshape(1, -1).astype(F32)
    ge, be = row(ln_emb_g), row(ln_emb_b)
    pscale = row(pool_scale[0])
    sink = row(sinks[0])

    to_kd_pos = lambda c: jnp.swapaxes(c[0].reshape(n_seq, WINDOW, D_KV), 1, 2)
    mix_s, nk_s, nv_s, np_s, w_out_b, bias_p, meta_proj, w_in_b, w_pool_bd = _sample_front_call(
        rel_table.astype(F32), sink, x_sample, to_kd_pos(cache_win_k), to_kd_pos(cache_win_v),
        jnp.swapaxes(state_pool[0], 0, 1), w_out[0], meta_tokens.astype(F32), ge, be, w_in[0], w_pool[0], pscale)
    xs = x_sample
    w1, w2 = w_mlp_in[0], w_mlp_out[0]

    tile = PROMPT_TILE
    tiles_per_seq = seq // tile
    n_tiles = batch * tiles_per_seq
    xp = x_prompt.reshape(batch * seq, d_model)
    front_tile = lambda s: jnp.minimum(s, n_tiles - 1)
    finish_tile = lambda s: jnp.maximum(s - 1, 0)
    seq_of = lambda s: front_tile(s) // tiles_per_seq
    consts = (ge, be, w_in_b, w_pool_bd, pscale, bias_p, meta_proj, w_out_b, row(ln1_g[0]), row(ln1_b[0]),
              row(ln2_g[0]), row(ln2_b[0]))
    y_p, y_s, nk_p, nv_p, np_p = pl.pallas_call(
        functools.partial(_main_kernel, tokens, tiles_per_seq, n_tiles),
        grid=(n_tiles + 1,),
        in_specs=[_smem_spec(),
                  pl.BlockSpec((tile, d_model), lambda s: (front_tile(s), 0)),
                  _const_spec(xs.shape), _const_spec(mix_s.shape)] + [_const_spec(c.shape) for c in consts]
        + [pl.BlockSpec(memory_space=pl.ANY)] * 2,
        out_specs=[pl.BlockSpec((tile, d_model), lambda s: (finish_tile(s), 0)),
                   pl.BlockSpec(xs.shape, lambda s: (0, 0, 0)),
                   pl.BlockSpec((1, BLOCK, D_KV), lambda s: (seq_of(s), 0, 0)),
                   pl.BlockSpec((1, BLOCK, D_KV), lambda s: (seq_of(s), 0, 0)),
                   pl.BlockSpec((HALO, batch, D_POOL), lambda s: (0, 0, 0))],
        out_shape=[jax.ShapeDtypeStruct((batch * seq, d_model), F32),
                   jax.ShapeDtypeStruct(xs.shape, F32),
                   jax.ShapeDtypeStruct((batch, BLOCK, D_KV), F32),
                   jax.ShapeDtypeStruct((batch, BLOCK, D_KV), F32),
                   jax.ShapeDtypeStruct((HALO, batch, D_POOL), F32)],
        scratch_shapes=[pltpu.VMEM((tile, d_model), BF16),
                        pltpu.VMEM((4, tile + BLOCK, LANES), BF16),
                        pltpu.VMEM((4, tile + BLOCK, LANES), BF16),
                        pltpu.VMEM((tile + HALO, D_POOL), F32),
                        pltpu.VMEM((tile, D_ATTN), BF16),
                        pltpu.VMEM((tile, D_POOL), BF16),
                        pltpu.VMEM((tile, d_model), F32),
                        pltpu.VMEM((tile, d_model), F32),
                        pltpu.VMEM((tile, d_model), BF16),
                        pltpu.VMEM((tile // 2, w1.shape[1]), BF16),
                        pltpu.VMEM(w1.shape, BF16),
                        pltpu.VMEM(w2.shape, BF16),
                        pltpu.VMEM((WEIGHT_SLOTS,) + WEIGHT_CHUNK, F32),
                        pltpu.SemaphoreType.DMA((WEIGHT_SLOTS + 2 * (tile // WEIGHT_CHUNK[0]),))],
        compiler_params=pltpu.CompilerParams(dimension_semantics=("arbitrary",),
                                             vmem_limit_bytes=VMEM_LIMIT_BYTES),
        name="main",
    )(sink, xp, xs, mix_s, *consts, w1, w2)

    kv_shape = (DEPTH, -1, WINDOW, N_KV_HEADS, HEAD_DIM)
    from_kd_pos = lambda c: jnp.swapaxes(c, 1, 2).reshape(kv_shape)
    return (y_p.reshape(batch, seq, d_model), y_s,
            from_kd_pos(nk_p), from_kd_pos(nv_p),
            jnp.swapaxes(np_p[HALO - POOL_BUF:], 0, 1)[None],
            from_kd_pos(nk_s), from_kd_pos(nv_s), jnp.swapaxes(np_s, 0, 1)[None])
```

```python
import functools
import math

import jax
import jax.numpy as jnp
import numpy as np
from jax import lax
from jax.experimental import pallas as pl
from jax.experimental.pallas import tpu as pltpu

N_META = 16
POOL_WINDOWS = (2, 4, 8, 16)
POOL_GROUP_DIM = 128
D_POOL = len(POOL_WINDOWS) * POOL_GROUP_DIM
POOL_BUF = max(POOL_WINDOWS) - 1
N_HEADS = 8
HEAD_DIM = 64
D_ATTN = N_HEADS * HEAD_DIM
N_KV_HEADS = 2
D_KV = N_KV_HEADS * HEAD_DIM
WINDOW = 128
BLOCK = 128
REL_BUCKETS = 32
REL_MAX_DIST = 128
PAST_LEN = 8192
DEPTH = 1
ALPHA = (2.0 * DEPTH) ** 0.25
LN_EPS = 1e-5
LOG2E = math.log2(math.e)
Q_SCALE = HEAD_DIM ** -0.5 * LOG2E

LANES = 128
SUBLANES = 8
VMEM_LIMIT_BYTES = 60 * 1024 * 1024

PROMPT_TILE = 512
IN_ROWS = 256
UP_CHUNK = 512
DOWN_CHUNK = 256
SAMPLE_SEQS = 32
SEQ_GROUP = SUBLANES
HALO = 2 * SUBLANES
WEIGHT_CHUNK = (256, 1024)
WEIGHT_SLOTS = 2

F32 = jnp.float32
BF16 = jnp.bfloat16
NEG_INF = float("-inf")


def _rel_bucket(dist):
    n = np.maximum(dist, 0)
    max_exact = REL_BUCKETS // 2
    nf = np.maximum(n, 1).astype(np.float64)
    large = max_exact + (np.log(nf / max_exact) / math.log(REL_MAX_DIST / max_exact)
                         * (REL_BUCKETS - max_exact)).astype(np.int32)
    large = np.minimum(large, REL_BUCKETS - 1)
    return np.where(n < max_exact, n, large).astype(np.int32)


def _bucket_tables(tokens):
    prompt = np.tile(_rel_bucket((BLOCK - np.arange(BLOCK)) % BLOCK)[None, :], (SUBLANES, 1)).astype(np.int32)
    t = np.repeat(np.arange(tokens), SEQ_GROUP)
    s = np.tile(np.arange(SEQ_GROUP), tokens)
    dist = (t[:, None] + WINDOW) - np.arange(WINDOW)[None, :]
    ok = (dist >= 0) & (dist < WINDOW)
    cache = np.where(ok, _rel_bucket(dist), -1).astype(np.int32)
    dist = t[:, None] - t[None, :]
    ok = (dist >= 0) & (s[:, None] == s[None, :])
    new = np.full((tokens * SEQ_GROUP, LANES), -1, np.int32)
    new[:, :tokens * SEQ_GROUP] = np.where(ok, _rel_bucket(dist), -1)
    return prompt, cache, new


def _layer_norm(x, g, b):
    mu = jnp.mean(x, axis=-1, keepdims=True)
    xc = x - mu
    var = jnp.mean(xc * xc, axis=-1, keepdims=True)
    return xc * lax.rsqrt(var + LN_EPS) * g + b


def _dot(a, b):
    return jnp.dot(a, b, preferred_element_type=F32)


def _dot_nt(a, b):
    return lax.dot_general(a, b, (((1,), (1,)), ((), ())), preferred_element_type=F32)


def _kv_lane_variants(x, low):
    xr = pltpu.roll(x, HEAD_DIM, axis=1)
    zero = jnp.zeros_like(x)
    return (jnp.where(low, x, zero).astype(BF16), jnp.where(low, zero, xr).astype(BF16),
            jnp.where(low, xr, zero).astype(BF16), jnp.where(low, zero, x).astype(BF16))


def _setup_tables(tab_ref, bkt_p_ref, bkt_c_ref, bkt_n_ref, meta_ref, g_ref, b_ref, wf_in_ref, wf_pool_ref,
                  bias_p_ref, bias_c_ref, bias_n_ref, meta_proj_ref, w_in_ref, w_pool_ref, w_pool_bd_ref):
    w_in_ref[...] = wf_in_ref[...].astype(BF16)
    w_pool_ref[...] = wf_pool_ref[...].astype(BF16)
    w_pool_bd_ref[...] = jnp.zeros_like(w_pool_bd_ref)
    for g in range(len(POOL_WINDOWS)):
        cols = slice(g * POOL_GROUP_DIM, (g + 1) * POOL_GROUP_DIM)
        w_pool_bd_ref[cols, cols] = w_pool_ref[g]
    def lookup(bucket):
        accs = [jnp.full(bucket.shape, NEG_INF, F32)] * N_HEADS
        for i in range(REL_BUCKETS):
            hit = bucket == i
            accs = [jnp.where(hit, tab_ref[i, h] * LOG2E, acc) for h, acc in enumerate(accs)]
        return accs

    rows = bkt_c_ref.shape[0]
    q = lax.broadcasted_iota(jnp.int32, (BLOCK, BLOCK), 0)
    k = lax.broadcasted_iota(jnp.int32, (BLOCK, BLOCK), 1)
    per_dist, cache_bias, new_bias = lookup(bkt_p_ref[...]), lookup(bkt_c_ref[...]), lookup(bkt_n_ref[...])
    for h in range(N_HEADS):
        toeplitz = pltpu.roll(jnp.concatenate([per_dist[h]] * (BLOCK // SUBLANES), axis=0), 0, 1,
                              stride=1, stride_axis=0)
        cur = jnp.where(k <= q, toeplitz, NEG_INF)
        for table, prev_ok in enumerate(((k > q) & (k >= BLOCK - N_META), k > q)):
            bias_p_ref[table, h, :, 0:BLOCK] = jnp.where(prev_ok, toeplitz, NEG_INF)
            bias_p_ref[table, h, :, BLOCK:] = cur
        bias_c_ref[h * rows:(h + 1) * rows, :] = cache_bias[h]
        bias_n_ref[h * rows:(h + 1) * rows, :] = new_bias[h]
    hm = _layer_norm(meta_ref[...], g_ref[...], b_ref[...])
    meta_proj_ref[...] = _dot(hm.astype(BF16), w_in_ref[...])


def _carry_init(meta_proj_ref, kvar, vvar, uext):
    low = lax.broadcasted_iota(jnp.int32, (1, LANES), 1) < HEAD_DIM
    pad = jnp.zeros((BLOCK - N_META, LANES), BF16)
    kq = _kv_lane_variants(meta_proj_ref[:, D_POOL + D_ATTN:D_POOL + D_ATTN + D_KV], low)
    vq = _kv_lane_variants(meta_proj_ref[:, D_POOL + D_ATTN + D_KV:], low)
    for i in range(4):
        kvar[i, 0:BLOCK, :] = jnp.concatenate([pad, kq[i]], axis=0)
        vvar[i, 0:BLOCK, :] = jnp.concatenate([pad, vq[i]], axis=0)
    uext[0:HALO, :] = meta_proj_ref[:, 0:D_POOL]


def _front_pieces(first_tile, x_ref, sink_ref, g_ref, b_ref, w_in_ref, w_pool_ref, pscale_ref, bias_ref,
                  mix_ref, nk_ref, nv_ref, np_ref, kvar, vvar, uext, qbuf, pbuf, hres):
    tile = x_ref.shape[0]
    halo = HALO
    low = lax.broadcasted_iota(jnp.int32, (1, LANES), 1) < HEAD_DIM

    for j in range(tile // IN_ROWS):
        rows = slice(j * IN_ROWS, (j + 1) * IN_ROWS)
        h = _layer_norm(x_ref[rows, :], g_ref[...], b_ref[...])
        hres[rows, :] = ALPHA * h
        proj = _dot(h.astype(BF16), w_in_ref[...])
        u = proj[:, 0:D_POOL]
        k = proj[:, D_POOL + D_ATTN:D_POOL + D_ATTN + D_KV]
        v = proj[:, D_POOL + D_ATTN + D_KV:]
        qbuf[rows, :] = (proj[:, D_POOL:D_POOL + D_ATTN] * Q_SCALE).astype(BF16)
        uext[halo + rows.start:halo + rows.stop, :] = u
        kq = _kv_lane_variants(k, low)
        vq = _kv_lane_variants(v, low)
        for i in range(4):
            kvar[i, BLOCK + rows.start:BLOCK + rows.stop, :] = kq[i]
            vvar[i, BLOCK + rows.start:BLOCK + rows.stop, :] = vq[i]
        if rows.stop == tile:
            nk_ref[...] = k[IN_ROWS - BLOCK:, :].T
            nv_ref[...] = v[IN_ROWS - BLOCK:, :].T
            np_ref[...] = u[IN_ROWS - halo:, :]
        yield

    for g, w in enumerate(POOL_WINDOWS):
        cols = slice(g * POOL_GROUP_DIM, (g + 1) * POOL_GROUP_DIM)
        ug = uext[halo:halo + tile, cols]
        s = ug
        for i in range(1, w):
            s = s + uext[halo - i:halo - i + tile, cols]
        pbuf[:, cols] = (s * (1.0 / w) - ug).astype(BF16)
        if g == len(POOL_WINDOWS) - 1:
            mix_ref[:, 0:D_POOL] = (_dot(pbuf[...], w_pool_ref[...]) * pscale_ref[...]).astype(BF16)
        yield

    for j in range(tile // BLOCK):
        rows = slice(j * BLOCK, (j + 1) * BLOCK)
        keys = slice(j * BLOCK, (j + 2) * BLOCK)
        sel = jnp.where(first_tile, 0, 1) if j == 0 else 1
        for kv in range(N_KV_HEADS):
            tiles = (2 * kv, 2 * kv + 1)
            lhs = jnp.concatenate([qbuf[rows, p * LANES:(p + 1) * LANES] for p in tiles], axis=0)
            nk = 2 * BLOCK
            s_all = _dot_nt(lhs, jnp.concatenate([kvar[2 * kv, keys, :], kvar[2 * kv + 1, keys, :]], axis=0))
            es, dens = [], []
            for half in range(2):
                heads = [2 * p + half for p in tiles]
                s = s_all[:, half * nk:(half + 1) * nk]
                s = s + jnp.concatenate([bias_ref[sel, hd] for hd in heads], axis=0)
                sink = jnp.concatenate([jnp.full((BLOCK, 1), sink_ref[0, hd] * LOG2E, F32) for hd in heads],
                                       axis=0)
                m = jnp.maximum(jnp.max(s, axis=-1, keepdims=True), sink)
                e = jnp.exp2(s - m)
                dens.append(jnp.sum(e, axis=-1, keepdims=True) + jnp.exp2(sink - m))
                es.append(e.astype(BF16))
            o = _dot(jnp.concatenate(es, axis=0),
                     jnp.concatenate([vvar[2 * kv, keys, :], vvar[2 * kv + 1, keys, :]], axis=1))
            acc = (o[0:nk, 0:LANES] * (1.0 / dens[0]) + o[nk:, LANES:] * (1.0 / dens[1]))
            for n, p in enumerate(tiles):
                mix_ref[rows, D_POOL + p * LANES:D_POOL + (p + 1) * LANES] = (
                    acc[n * BLOCK:(n + 1) * BLOCK].astype(BF16))
            yield

    for i in range(4):
        kvar[i, 0:BLOCK, :] = kvar[i, tile:tile + BLOCK, :]
        vvar[i, 0:BLOCK, :] = vvar[i, tile:tile + BLOCK, :]
    uext[0:halo, :] = uext[tile:tile + halo, :]


def _sample_front_kernel(tab_ref, sink_ref, x_ref, ck_ref, cv_ref, st_ref, wf_out_ref,
                         bkt_p_ref, bkt_c_ref, bkt_n_ref, meta_ref, g_ref, b_ref, wf_in_ref, wf_pool_ref, pscale_ref,
                         mix_ref, nk_ref, nv_ref, np_ref, wb_out_ref,
                         bias_p_ref, meta_proj_ref, w_in_ref, w_pool_bd_ref,
                         o_s, bias_c_ref, bias_n_ref, w_pool_ref):
    @pl.when(pl.program_id(0) == 0)
    def _():
        _setup_tables(tab_ref, bkt_p_ref, bkt_c_ref, bkt_n_ref, meta_ref, g_ref, b_ref, wf_in_ref, wf_pool_ref,
                      bias_p_ref, bias_c_ref, bias_n_ref, meta_proj_ref, w_in_ref, w_pool_ref, w_pool_bd_ref)

    wb_out_ref[...] = wf_out_ref[...].astype(BF16)
    ns, tokens, d_model = x_ref.shape
    lane = lax.broadcasted_iota(jnp.int32, (1, LANES), 1)
    low = lane < HEAD_DIM

    x = jnp.concatenate([x_ref[:, t, :] for t in range(tokens)], axis=0)
    h = _layer_norm(x, g_ref[...], b_ref[...])
    proj = _dot(h.astype(BF16), w_in_ref[...])
    u = proj[:, 0:D_POOL]
    q = proj[:, D_POOL:D_POOL + D_ATTN] * Q_SCALE
    k = proj[:, D_POOL + D_ATTN:D_POOL + D_ATTN + D_KV]
    v = proj[:, D_POOL + D_ATTN + D_KV:]

    ext = [st_ref[r] for r in range(POOL_BUF)]
    ext += [u[t * ns:(t + 1) * ns] for t in range(tokens)]
    ps = []
    for t in range(tokens):
        idx = POOL_BUF + t
        pos = PAST_LEN - POOL_BUF + idx
        parts = []
        for g, w in enumerate(POOL_WINDOWS):
            cols = slice(g * POOL_GROUP_DIM, (g + 1) * POOL_GROUP_DIM)
            lo = max(idx + 1 - w, 0)
            s = ext[idx][:, cols]
            for r in range(idx - 1, lo - 1, -1):
                s = s + ext[r][:, cols]
            parts.append(s * (1.0 / min(w, pos + 1)) - ext[idx][:, cols])
        ps.append(jnp.concatenate(parts, axis=1))
    p = jnp.concatenate(ps, axis=0).astype(BF16)
    zs = [_dot(p[:, g * POOL_GROUP_DIM:(g + 1) * POOL_GROUP_DIM], w_pool_ref[g])
          for g in range(len(POOL_WINDOWS))]
    z = (jnp.concatenate(zs, axis=1) * pscale_ref[...]).astype(BF16)
    for r in range(POOL_BUF):
        np_ref[r] = ext[r + tokens]

    qh = []
    for hd in range(N_HEADS):
        tl = q[:, (hd // 2) * LANES:(hd // 2 + 1) * LANES]
        kv = hd // (N_HEADS // N_KV_HEADS)
        want_low = kv == 0
        is_low = hd % 2 == 0
        src = tl if want_low == is_low else pltpu.roll(tl, HEAD_DIM, axis=1)
        qh.append(jnp.where(low, src, 0.0) if want_low else jnp.where(low, 0.0, src))

    gr = SEQ_GROUP
    rows_per_head = tokens * gr
    sub = lax.broadcasted_iota(jnp.int32, (N_HEADS * rows_per_head, 1), 0) % gr
    sink = jnp.concatenate([jnp.full((rows_per_head, 1), sink_ref[0, hd] * LOG2E, F32) for hd in range(N_HEADS)],
                           axis=0)
    zpad = jnp.zeros((LANES - rows_per_head, LANES), F32)
    ztop = jnp.zeros((WINDOW - SUBLANES, LANES), F32)
    sub8 = lax.broadcasted_iota(jnp.int32, (SUBLANES, 1), 0)
    keep = WINDOW - tokens

    def appended(old_ref, news, j):
        tail = jnp.zeros((SUBLANES, LANES), F32)
        for t in range(tokens):
            row = SUBLANES - tokens + t
            tail = jnp.where(sub8 == row, pltpu.roll(news[t], (row - j) % SUBLANES, axis=0), tail)
        cols = jnp.concatenate([ztop, tail], axis=0).T
        return jnp.where(lane >= keep, cols, pltpu.roll(old_ref[...], keep, axis=1))

    for base in range(0, ns, gr):
        k_t = [k[t * ns + base:t * ns + base + gr] for t in range(tokens)]
        v_t = [v[t * ns + base:t * ns + base + gr] for t in range(tokens)]
        for s in range(gr):
            nk_ref[base + s] = appended(ck_ref.at[base + s], k_t, s)
            nv_ref[base + s] = appended(cv_ref.at[base + s], v_t, s)

    for gi in range(ns // gr):
        base = gi * gr
        lhs = jnp.concatenate([qh[hd][t * ns + base:t * ns + base + gr]
                               for hd in range(N_HEADS) for t in range(tokens)], axis=0).astype(BF16)
        sc = None
        for s in range(gr):
            s_one = _dot(lhs, ck_ref[base + s].astype(BF16))
            sc = s_one if sc is None else jnp.where(sub == s, s_one, sc)
        sc = sc + bias_c_ref[...]
        k_t = [k[t * ns + base:t * ns + base + gr] for t in range(tokens)]
        v_t = [v[t * ns + base:t * ns + base + gr] for t in range(tokens)]
        k_new = jnp.concatenate(k_t + [zpad], axis=0)
        v_new = jnp.concatenate(v_t + [zpad], axis=0)
        sn = _dot_nt(lhs, k_new.astype(BF16)) + bias_n_ref[...]
        m = jnp.maximum(jnp.maximum(jnp.max(sc, axis=-1, keepdims=True),
                                    jnp.max(sn, axis=-1, keepdims=True)), sink)
        ec = jnp.exp2(sc - m)
        en = jnp.exp2(sn - m)
        den = (jnp.sum(ec, axis=-1, keepdims=True) + jnp.sum(en, axis=-1, keepdims=True)
               + jnp.exp2(sink - m))
        o = _dot(en.astype(BF16), v_new.astype(BF16))
        for s in range(gr):
            o = o + _dot_nt(jnp.where(sub == s, ec, 0.0).astype(BF16), cv_ref[base + s].astype(BF16))
        o = o * (1.0 / den)
        for t in range(tokens):
            for p in range(N_HEADS // 2):
                a = o[(2 * p * tokens + t) * gr:(2 * p * tokens + t + 1) * gr]
                c = o[((2 * p + 1) * tokens + t) * gr:((2 * p + 1) * tokens + t + 1) * gr]
                if p < N_HEADS // 4:
                    c = pltpu.roll(c, HEAD_DIM, axis=1)
                else:
                    a = pltpu.roll(a, HEAD_DIM, axis=1)
                o_s[t * ns + base:t * ns + base + gr, p * LANES:(p + 1) * LANES] = jnp.where(low, a, c)

    for t in range(tokens):
        mix_ref[:, t * d_model:t * d_model + D_POOL] = z[t * ns:(t + 1) * ns]
        mix_ref[:, t * d_model + D_POOL:(t + 1) * d_model] = o_s[t * ns:(t + 1) * ns, :].astype(BF16)


def _pre_pieces(rows, res, mix, w_out_ref, g1_ref, b1_ref, h1_s, hb_s):
    h1_s[rows, :] = res() + _dot(mix(), w_out_ref[...])
    yield
    h = _layer_norm(h1_s[rows, :], g1_ref[...], b1_ref[...])
    h1_s[rows, :] = h
    hb_s[rows, :] = h.astype(BF16)
    yield


def _mlp_pieces(rows, a_s, h1_s, hb_s, w1_ref, w2_ref, g2_ref, b2_ref, y_ref):
    for c in range(w1_ref.shape[1] // UP_CHUNK):
        cols = slice(c * UP_CHUNK, (c + 1) * UP_CHUNK)
        a = _dot(hb_s[rows, :], w1_ref[:, cols])
        a_s[:, cols] = jnp.square(jnp.maximum(a, 0.0)).astype(BF16)
        if c == 0:
            y_ref[rows, :] = ALPHA * h1_s[rows, :]
        yield
    for n in range(y_ref.shape[1] // DOWN_CHUNK):
        cols = slice(n * DOWN_CHUNK, (n + 1) * DOWN_CHUNK)
        y_ref[rows, cols] = y_ref[rows, cols] + _dot(a_s[...], w2_ref[:, cols])
        yield
    y_ref[rows, :] = _layer_norm(y_ref[rows, :], g2_ref[...], b2_ref[...])
    yield


def _weave(main, n_main, sides):
    order = []
    for i in range(n_main):
        order.append(main)
        order += sides[i * len(sides) // n_main:(i + 1) * len(sides) // n_main]
    return order


def _main_kernel(tokens, tiles_per_seq, n_tiles,
                 sink_ref, x_ref, xs_ref, mixs_ref, ge_ref, be_ref, w_in_ref, w_pool_ref,
                 pscale_ref, bias_ref, meta_proj_ref, w_out_ref, g1_ref, b1_ref, g2_ref, b2_ref,
                 w1_hbm, w2_hbm,
                 y_ref, ys_ref, nk_ref, nv_ref, np_ref,
                 mixbuf, kvar, vvar, uext, qbuf, pbuf, hres, h1_s, hb_s, a_s, w1_ref, w2_ref, stage, sem):
    s = pl.program_id(0)
    is_first = s == 0
    first_tile = jnp.minimum(s, n_tiles - 1) % tiles_per_seq == 0
    tile, d_model = x_ref.shape
    ns = xs_ref.shape[0]
    half = tile // 2
    halves = (slice(0, half), slice(half, tile))
    pre_w = (w_out_ref, g1_ref, b1_ref, h1_s, hb_s)

    @pl.when(is_first)
    def _():
        def toks(rows):
            return range(rows.start // ns, rows.stop // ns)

        def sample_pre():
            for rows in halves:
                res = lambda: ALPHA * _layer_norm(jnp.concatenate([xs_ref[:, t, :] for t in toks(rows)], axis=0),
                                                  ge_ref[...], be_ref[...])
                mix = lambda: jnp.concatenate([mixs_ref[:, t * d_model:(t + 1) * d_model] for t in toks(rows)],
                                              axis=0)
                yield from _pre_pieces(rows, res, mix, *pre_w)

        cr, cc = stage.shape[1:]
        chunks = [(src, dst, r, c) for src, dst in ((w1_hbm, w1_ref), (w2_hbm, w2_ref))
                  for r in range(0, src.shape[0], cr) for c in range(0, src.shape[1], cc)]
        ring = [stage.at[i] for i in range(stage.shape[0])]
        ring += [buf.at[pl.ds(r, cr)] for buf in (hres, y_ref) for r in range(0, tile, cr)]
        slots = len(ring)

        def copy(i):
            src, _, r, c = chunks[i]
            return pltpu.make_async_copy(src.at[pl.ds(r, cr), pl.ds(c, cc)], ring[i % slots], sem.at[i % slots])

        for i in range(slots):
            copy(i).start()
        pre = sample_pre()
        every = len(chunks) // 4
        for i, (_, dst, r, c) in enumerate(chunks):
            if i % every == every // 2:
                next(pre, None)
            copy(i).wait()
            dst[r:r + cr, c:c + cc] = ring[i % slots][...].astype(BF16)
            if i + slots < len(chunks):
                copy(i + slots).start()
        for _ in pre:
            raise AssertionError("piece left unscheduled")

    @pl.when(first_tile)
    def _():
        _carry_init(meta_proj_ref, kvar, vvar, uext)

    mlp = [_mlp_pieces(rows, a_s, h1_s, hb_s, w1_ref, w2_ref, g2_ref, b2_ref, y_ref) for rows in halves]
    front = _front_pieces(first_tile, x_ref, sink_ref, ge_ref, be_ref, w_in_ref, w_pool_ref, pscale_ref,
                          bias_ref, mixbuf, nk_ref.at[0], nv_ref.at[0], np_ref.at[0], kvar, vvar, uext, qbuf, pbuf,
                          hres)
    pre = [_pre_pieces(rows, lambda rows=rows: hres[rows, :], lambda rows=rows: mixbuf[rows, :], *pre_w)
           for rows in halves]
    a, b, f, pa, pb = mlp[0], mlp[1], front, pre[0], pre[1]
    n_up = w1_ref.shape[1] // UP_CHUNK
    n_down = d_model // DOWN_CHUNK
    order = (
        _weave(a, n_up, [f] * 4)
        + _weave(a, n_down, [f] * 5)
        + _weave(b, n_up, [a] + [f] * 5)
        + _weave(b, n_down, [f, pa, pb, pa, pb]) + [b])
    for g in order:
        next(g, None)
    for g in (a, b, f, pa, pb):
        for _ in g:
            raise AssertionError("piece left unscheduled")

    @pl.when(is_first)
    def _():
        for t in range(tokens):
            ys_ref[:, t, :] = y_ref[t * ns:(t + 1) * ns, :]


def _const_spec(shape):
    return pl.BlockSpec(shape, lambda *_: (0,) * len(shape), pipeline_mode=pl.Buffered(1))


def _smem_spec():
    return pl.BlockSpec(memory_space=pltpu.SMEM)


def _sample_front_call(rel_table, sink, xs, ck, cv, st, w_out, meta_tokens, ge, be, w_in, w_pool, pscale):
    n_seq, tokens, d_model = xs.shape
    steps = n_seq // SAMPLE_SEQS
    srows = N_HEADS * tokens * SEQ_GROUP
    tables = [jnp.asarray(t) for t in _bucket_tables(tokens)]
    consts = tables + [meta_tokens, ge, be, w_in, w_pool, pscale]
    whole = lambda shape: pl.BlockSpec(shape, lambda i: (0,) * len(shape))
    seq_spec = lambda *tail: pl.BlockSpec((SAMPLE_SEQS,) + tail, lambda i: (i,) + (0,) * len(tail))
    state_spec = pl.BlockSpec((POOL_BUF, SAMPLE_SEQS, D_POOL), lambda i: (0, i, 0))
    slab_spec = pl.BlockSpec((w_out.shape[0] // steps, w_out.shape[1]), lambda i: (i, 0))
    once = [jax.ShapeDtypeStruct((2, N_HEADS, BLOCK, 2 * BLOCK), F32),
            jax.ShapeDtypeStruct((meta_tokens.shape[0], w_in.shape[1]), F32),
            jax.ShapeDtypeStruct(w_in.shape, BF16),
            jax.ShapeDtypeStruct((D_POOL, D_POOL), BF16)]
    return pl.pallas_call(
        _sample_front_kernel,
        grid=(steps,),
        in_specs=[_smem_spec(), _smem_spec(), seq_spec(tokens, d_model), seq_spec(D_KV, WINDOW),
                  seq_spec(D_KV, WINDOW), state_spec, slab_spec] + [_const_spec(c.shape) for c in consts],
        out_specs=[seq_spec(tokens * d_model), seq_spec(D_KV, WINDOW), seq_spec(D_KV, WINDOW), state_spec,
                   slab_spec] + [whole(o.shape) for o in once],
        out_shape=[jax.ShapeDtypeStruct((n_seq, tokens * d_model), BF16),
                   jax.ShapeDtypeStruct((n_seq, D_KV, WINDOW), F32),
                   jax.ShapeDtypeStruct((n_seq, D_KV, WINDOW), F32),
                   jax.ShapeDtypeStruct((POOL_BUF, n_seq, D_POOL), F32),
                   jax.ShapeDtypeStruct(w_out.shape, BF16)] + once,
        scratch_shapes=[pltpu.VMEM((tokens * SAMPLE_SEQS, D_ATTN), F32),
                        pltpu.VMEM((srows, WINDOW), F32),
                        pltpu.VMEM((srows, LANES), F32),
                        pltpu.VMEM(w_pool.shape, BF16)],
        compiler_params=pltpu.CompilerParams(dimension_semantics=("arbitrary",),
                                             vmem_limit_bytes=VMEM_LIMIT_BYTES),
        name="sample_front",
    )(rel_table, sink, xs, ck, cv, st, w_out, *consts)


def kernel(x_prompt, x_sample, cache_win_k, cache_win_v, state_pool, meta_tokens, ln_emb_g, ln_emb_b,
           rel_table, w_in, w_pool, pool_scale, sinks, w_out, ln1_g, ln1_b, w_mlp_in, w_mlp_out,
           ln2_g, ln2_b):
    batch, seq, d_model = x_prompt.shape
    n_seq, tokens, _ = x_sample.shape
    assert w_in.shape[0] == DEPTH and d_model == D_POOL + D_ATTN
    assert seq % PROMPT_TILE == 0 and n_seq % SAMPLE_SEQS == 0 and tokens * n_seq == PROMPT_TILE
    assert cache_win_k.shape[2] == WINDOW and state_pool.shape[2] == POOL_BUF and tokens <= POOL_BUF
    assert N_META == HALO and POOL_BUF <= HALO

    row = lambda a: a.reshape(1, -1).astype(F32)
    ge, be = row(ln_emb_g), row(ln_emb_b)
    pscale = row(pool_scale[0])
    sink = row(sinks[0])

    to_kd_pos = lambda c: jnp.swapaxes(c[0].reshape(n_seq, WINDOW, D_KV), 1, 2)
    mix_s, nk_s, nv_s, np_s, w_out_b, bias_p, meta_proj, w_in_b, w_pool_bd = _sample_front_call(
        rel_table.astype(F32), sink, x_sample, to_kd_pos(cache_win_k), to_kd_pos(cache_win_v),
        jnp.swapaxes(state_pool[0], 0, 1), w_out[0], meta_tokens.astype(F32), ge, be, w_in[0], w_pool[0], pscale)
    xs = x_sample
    w1, w2 = w_mlp_in[0], w_mlp_out[0]

    tile = PROMPT_TILE
    tiles_per_seq = seq // tile
    n_tiles = batch * tiles_per_seq
    xp = x_prompt.reshape(batch * seq, d_model)
    front_tile = lambda s: jnp.minimum(s, n_tiles - 1)
    finish_tile = lambda s: jnp.maximum(s - 1, 0)
    seq_of = lambda s: front_tile(s) // tiles_per_seq
    consts = (ge, be, w_in_b, w_pool_bd, pscale, bias_p, meta_proj, w_out_b, row(ln1_g[0]), row(ln1_b[0]),
              row(ln2_g[0]), row(ln2_b[0]))
    y_p, y_s, nk_p, nv_p, np_p = pl.pallas_call(
        functools.partial(_main_kernel, tokens, tiles_per_seq, n_tiles),
        grid=(n_tiles + 1,),
        in_specs=[_smem_spec(),
                  pl.BlockSpec((tile, d_model), lambda s: (front_tile(s), 0)),
                  _const_spec(xs.shape), _const_spec(mix_s.shape)] + [_const_spec(c.shape) for c in consts]
        + [pl.BlockSpec(memory_space=pl.ANY)] * 2,
        out_specs=[pl.BlockSpec((tile, d_model), lambda s: (finish_tile(s), 0)),
                   pl.BlockSpec(xs.shape, lambda s: (0, 0, 0)),
                   pl.BlockSpec((1, BLOCK, D_KV), lambda s: (seq_of(s), 0, 0)),
                   pl.BlockSpec((1, BLOCK, D_KV), lambda s: (seq_of(s), 0, 0)),
                   pl.BlockSpec((1, HALO, D_POOL), lambda s: (seq_of(s), 0, 0))],
        out_shape=[jax.ShapeDtypeStruct((batch * seq, d_model), F32),
                   jax.ShapeDtypeStruct(xs.shape, F32),
                   jax.ShapeDtypeStruct((batch, BLOCK, D_KV), F32),
                   jax.ShapeDtypeStruct((batch, BLOCK, D_KV), F32),
                   jax.ShapeDtypeStruct((batch, HALO, D_POOL), F32)],
        scratch_shapes=[pltpu.VMEM((tile, d_model), BF16),
                        pltpu.VMEM((4, tile + BLOCK, LANES), BF16),
                        pltpu.VMEM((4, tile + BLOCK, LANES), BF16),
                        pltpu.VMEM((tile + HALO, D_POOL), F32),
                        pltpu.VMEM((tile, D_ATTN), BF16),
                        pltpu.VMEM((tile, D_POOL), BF16),
                        pltpu.VMEM((tile, d_model), F32),
                        pltpu.VMEM((tile, d_model), F32),
                        pltpu.VMEM((tile, d_model), BF16),
                        pltpu.VMEM((tile // 2, w1.shape[1]), BF16),
                        pltpu.VMEM(w1.shape, BF16),
                        pltpu.VMEM(w2.shape, BF16),
                        pltpu.VMEM((WEIGHT_SLOTS,) + WEIGHT_CHUNK, F32),
                        pltpu.SemaphoreType.DMA((WEIGHT_SLOTS + 2 * (tile // WEIGHT_CHUNK[0]),))],
        compiler_params=pltpu.CompilerParams(dimension_semantics=("arbitrary",),
                                             vmem_limit_bytes=VMEM_LIMIT_BYTES),
        name="main",
    )(sink, xp, xs, mix_s, *consts, w1, w2)

    kv_shape = (DEPTH, -1, WINDOW, N_KV_HEADS, HEAD_DIM)
    from_kd_pos = lambda c: jnp.swapaxes(c, 1, 2).reshape(kv_shape)
    return (y_p.reshape(batch, seq, d_model), y_s,
            from_kd_pos(nk_p), from_kd_pos(nv_p),
            np_p[:, HALO - POOL_BUF:, :].reshape(DEPTH, batch, POOL_BUF, D_POOL),
            from_kd_pos(nk_s), from_kd_pos(nv_s), jnp.swapaxes(np_s, 0, 1)[None])
```

```python
import functools
import math

import jax
import jax.numpy as jnp
import numpy as np
from jax import lax
from jax.experimental import pallas as pl
from jax.experimental.pallas import tpu as pltpu

N_META = 16
POOL_WINDOWS = (2, 4, 8, 16)
POOL_GROUP_DIM = 128
D_POOL = len(POOL_WINDOWS) * POOL_GROUP_DIM
POOL_BUF = max(POOL_WINDOWS) - 1
N_HEADS = 8
HEAD_DIM = 64
D_ATTN = N_HEADS * HEAD_DIM
N_KV_HEADS = 2
D_KV = N_KV_HEADS * HEAD_DIM
WINDOW = 128
BLOCK = 128
REL_BUCKETS = 32
REL_MAX_DIST = 128
PAST_LEN = 8192
DEPTH = 1
ALPHA = (2.0 * DEPTH) ** 0.25
LN_EPS = 1e-5
Q_SCALE = HEAD_DIM ** -0.5

LANES = 128
SUBLANES = 8
VMEM_LIMIT_BYTES = 60 * 1024 * 1024

PROMPT_TILE = 512
IN_ROWS = 256
UP_CHUNK = 512
DOWN_CHUNK = 256
SAMPLE_SEQS = 32
SEQ_GROUP = SUBLANES
HALO = 2 * SUBLANES
WEIGHT_CHUNK = (256, 1024)
WEIGHT_SLOTS = 2

F32 = jnp.float32
BF16 = jnp.bfloat16
NEG_INF = float("-inf")


def _rel_bucket(dist):
    n = np.maximum(dist, 0)
    max_exact = REL_BUCKETS // 2
    nf = np.maximum(n, 1).astype(np.float64)
    large = max_exact + (np.log(nf / max_exact) / math.log(REL_MAX_DIST / max_exact)
                         * (REL_BUCKETS - max_exact)).astype(np.int32)
    large = np.minimum(large, REL_BUCKETS - 1)
    return np.where(n < max_exact, n, large).astype(np.int32)


def _bucket_tables(tokens):
    prompt = np.tile(_rel_bucket((BLOCK - np.arange(BLOCK)) % BLOCK)[None, :], (SUBLANES, 1)).astype(np.int32)
    t = np.repeat(np.arange(tokens), SEQ_GROUP)
    s = np.tile(np.arange(SEQ_GROUP), tokens)
    dist = (t[:, None] + WINDOW) - np.arange(WINDOW)[None, :]
    ok = (dist >= 0) & (dist < WINDOW)
    cache = np.where(ok, _rel_bucket(dist), -1).astype(np.int32)
    dist = t[:, None] - t[None, :]
    ok = (dist >= 0) & (s[:, None] == s[None, :])
    new = np.full((tokens * SEQ_GROUP, LANES), -1, np.int32)
    new[:, :tokens * SEQ_GROUP] = np.where(ok, _rel_bucket(dist), -1)
    return prompt, cache, new


def _layer_norm(x, g, b):
    mu = jnp.mean(x, axis=-1, keepdims=True)
    xc = x - mu
    var = jnp.mean(xc * xc, axis=-1, keepdims=True)
    return xc * lax.rsqrt(var + LN_EPS) * g + b


def _dot(a, b):
    return jnp.dot(a, b, preferred_element_type=F32)


def _dot_nt(a, b):
    return lax.dot_general(a, b, (((1,), (1,)), ((), ())), preferred_element_type=F32)


def _kv_lane_variants(x, low):
    xr = pltpu.roll(x, HEAD_DIM, axis=1)
    zero = jnp.zeros_like(x)
    return (jnp.where(low, x, zero).astype(BF16), jnp.where(low, zero, xr).astype(BF16),
            jnp.where(low, xr, zero).astype(BF16), jnp.where(low, zero, x).astype(BF16))


def _setup_tables(tab_ref, bkt_p_ref, bkt_c_ref, bkt_n_ref, meta_ref, g_ref, b_ref, wf_in_ref, wf_pool_ref,
                  bias_p_ref, bias_c_ref, bias_n_ref, meta_proj_ref, w_in_ref, w_pool_ref, w_pool_bd_ref):
    w_in_ref[...] = wf_in_ref[...].astype(BF16)
    w_pool_ref[...] = wf_pool_ref[...].astype(BF16)
    w_pool_bd_ref[...] = jnp.zeros_like(w_pool_bd_ref)
    for g in range(len(POOL_WINDOWS)):
        cols = slice(g * POOL_GROUP_DIM, (g + 1) * POOL_GROUP_DIM)
        w_pool_bd_ref[cols, cols] = w_pool_ref[g]
    def lookup(bucket):
        accs = [jnp.full(bucket.shape, NEG_INF, F32)] * N_HEADS
        for i in range(REL_BUCKETS):
            hit = bucket == i
            accs = [jnp.where(hit, tab_ref[h, i], acc) for h, acc in enumerate(accs)]
        return accs

    rows = bkt_c_ref.shape[0]
    q = lax.broadcasted_iota(jnp.int32, (BLOCK, BLOCK), 0)
    k = lax.broadcasted_iota(jnp.int32, (BLOCK, BLOCK), 1)
    per_dist, cache_bias, new_bias = lookup(bkt_p_ref[...]), lookup(bkt_c_ref[...]), lookup(bkt_n_ref[...])
    for h in range(N_HEADS):
        toeplitz = pltpu.roll(jnp.concatenate([per_dist[h]] * (BLOCK // SUBLANES), axis=0), 0, 1,
                              stride=1, stride_axis=0)
        cur = jnp.where(k <= q, toeplitz, NEG_INF)
        for table, prev_ok in enumerate(((k > q) & (k >= BLOCK - N_META), k > q)):
            bias_p_ref[table, h, :, 0:BLOCK] = jnp.where(prev_ok, toeplitz, NEG_INF)
            bias_p_ref[table, h, :, BLOCK:] = cur
        bias_c_ref[h * rows:(h + 1) * rows, :] = cache_bias[h]
        bias_n_ref[h * rows:(h + 1) * rows, :] = new_bias[h]
    hm = _layer_norm(meta_ref[...], g_ref[...], b_ref[...])
    meta_proj_ref[...] = _dot(hm.astype(BF16), w_in_ref[...])


def _carry_init(meta_proj_ref, kvar, vvar, uext):
    low = lax.broadcasted_iota(jnp.int32, (1, LANES), 1) < HEAD_DIM
    pad = jnp.zeros((BLOCK - N_META, LANES), BF16)
    kq = _kv_lane_variants(meta_proj_ref[:, D_POOL + D_ATTN:D_POOL + D_ATTN + D_KV], low)
    vq = _kv_lane_variants(meta_proj_ref[:, D_POOL + D_ATTN + D_KV:], low)
    for i in range(4):
        kvar[i, 0:BLOCK, :] = jnp.concatenate([pad, kq[i]], axis=0)
        vvar[i, 0:BLOCK, :] = jnp.concatenate([pad, vq[i]], axis=0)
    uext[0:HALO, :] = meta_proj_ref[:, 0:D_POOL]


def _front_pieces(first_tile, x_ref, sink_ref, g_ref, b_ref, w_in_ref, w_pool_ref, pscale_ref, bias_ref,
                  mix_ref, nk_ref, nv_ref, np_ref, kvar, vvar, uext, qbuf, pbuf, hres):
    tile = x_ref.shape[0]
    halo = HALO
    low = lax.broadcasted_iota(jnp.int32, (1, LANES), 1) < HEAD_DIM

    for j in range(tile // IN_ROWS):
        rows = slice(j * IN_ROWS, (j + 1) * IN_ROWS)
        h = _layer_norm(x_ref[rows, :], g_ref[...], b_ref[...])
        hres[rows, :] = ALPHA * h
        proj = _dot(h.astype(BF16), w_in_ref[...])
        u = proj[:, 0:D_POOL]
        k = proj[:, D_POOL + D_ATTN:D_POOL + D_ATTN + D_KV]
        v = proj[:, D_POOL + D_ATTN + D_KV:]
        qbuf[rows, :] = (proj[:, D_POOL:D_POOL + D_ATTN] * Q_SCALE).astype(BF16)
        uext[halo + rows.start:halo + rows.stop, :] = u
        kq = _kv_lane_variants(k, low)
        vq = _kv_lane_variants(v, low)
        for i in range(4):
            kvar[i, BLOCK + rows.start:BLOCK + rows.stop, :] = kq[i]
            vvar[i, BLOCK + rows.start:BLOCK + rows.stop, :] = vq[i]
        if rows.stop == tile:
            nk_ref[...] = k[IN_ROWS - BLOCK:, :].T
            nv_ref[...] = v[IN_ROWS - BLOCK:, :].T
            np_ref[...] = u[IN_ROWS - halo:, :]
        yield

    for g, w in enumerate(POOL_WINDOWS):
        cols = slice(g * POOL_GROUP_DIM, (g + 1) * POOL_GROUP_DIM)
        ug = uext[halo:halo + tile, cols]
        s = ug
        for i in range(1, w):
            s = s + uext[halo - i:halo - i + tile, cols]
        pbuf[:, cols] = (s * (1.0 / w) - ug).astype(BF16)
        if g == len(POOL_WINDOWS) - 1:
            mix_ref[:, 0:D_POOL] = (_dot(pbuf[...], w_pool_ref[...]) * pscale_ref[...]).astype(BF16)
        yield

    for j in range(tile // BLOCK):
        rows = slice(j * BLOCK, (j + 1) * BLOCK)
        keys = slice(j * BLOCK, (j + 2) * BLOCK)
        sel = jnp.where(first_tile, 0, 1) if j == 0 else 1
        for kv in range(N_KV_HEADS):
            tiles = (2 * kv, 2 * kv + 1)
            lhs = jnp.concatenate([qbuf[rows, p * LANES:(p + 1) * LANES] for p in tiles], axis=0)
            nk = 2 * BLOCK
            s_all = _dot_nt(lhs, jnp.concatenate([kvar[2 * kv, keys, :], kvar[2 * kv + 1, keys, :]], axis=0))
            es, dens = [], []
            for half in range(2):
                heads = [2 * p + half for p in tiles]
                s = s_all[:, half * nk:(half + 1) * nk]
                s = s + jnp.concatenate([bias_ref[sel, hd] for hd in heads], axis=0)
                sink = jnp.concatenate([jnp.full((BLOCK, 1), sink_ref[0, hd], F32) for hd in heads], axis=0)
                m = jnp.maximum(jnp.max(s, axis=-1, keepdims=True), sink)
                e = jnp.exp(s - m)
                dens.append(jnp.sum(e, axis=-1, keepdims=True) + jnp.exp(sink - m))
                es.append(e.astype(BF16))
            o = _dot(jnp.concatenate(es, axis=0),
                     jnp.concatenate([vvar[2 * kv, keys, :], vvar[2 * kv + 1, keys, :]], axis=1))
            acc = (o[0:nk, 0:LANES] * (1.0 / dens[0]) + o[nk:, LANES:] * (1.0 / dens[1]))
            for n, p in enumerate(tiles):
                mix_ref[rows, D_POOL + p * LANES:D_POOL + (p + 1) * LANES] = (
                    acc[n * BLOCK:(n + 1) * BLOCK].astype(BF16))
            yield

    for i in range(4):
        kvar[i, 0:BLOCK, :] = kvar[i, tile:tile + BLOCK, :]
        vvar[i, 0:BLOCK, :] = vvar[i, tile:tile + BLOCK, :]
    uext[0:halo, :] = uext[tile:tile + halo, :]


def _sample_front_kernel(tab_ref, sink_ref, x_ref, ck_ref, cv_ref, st_ref, wf_out_ref,
                         bkt_p_ref, bkt_c_ref, bkt_n_ref, meta_ref, g_ref, b_ref, wf_in_ref, wf_pool_ref, pscale_ref,
                         mix_ref, nk_ref, nv_ref, np_ref, wb_out_ref,
                         bias_p_ref, meta_proj_ref, w_in_ref, w_pool_bd_ref,
                         o_s, bias_c_ref, bias_n_ref, w_pool_ref):
    @pl.when(pl.program_id(0) == 0)
    def _():
        _setup_tables(tab_ref, bkt_p_ref, bkt_c_ref, bkt_n_ref, meta_ref, g_ref, b_ref, wf_in_ref, wf_pool_ref,
                      bias_p_ref, bias_c_ref, bias_n_ref, meta_proj_ref, w_in_ref, w_pool_ref, w_pool_bd_ref)

    wb_out_ref[...] = wf_out_ref[...].astype(BF16)
    ns, tokens, d_model = x_ref.shape
    lane = lax.broadcasted_iota(jnp.int32, (1, LANES), 1)
    low = lane < HEAD_DIM

    x = jnp.concatenate([x_ref[:, t, :] for t in range(tokens)], axis=0)
    h = _layer_norm(x, g_ref[...], b_ref[...])
    proj = _dot(h.astype(BF16), w_in_ref[...])
    u = proj[:, 0:D_POOL]
    q = proj[:, D_POOL:D_POOL + D_ATTN] * Q_SCALE
    k = proj[:, D_POOL + D_ATTN:D_POOL + D_ATTN + D_KV]
    v = proj[:, D_POOL + D_ATTN + D_KV:]

    ext = [st_ref[r] for r in range(POOL_BUF)]
    ext += [u[t * ns:(t + 1) * ns] for t in range(tokens)]
    ps = []
    for t in range(tokens):
        idx = POOL_BUF + t
        pos = PAST_LEN - POOL_BUF + idx
        parts = []
        for g, w in enumerate(POOL_WINDOWS):
            cols = slice(g * POOL_GROUP_DIM, (g + 1) * POOL_GROUP_DIM)
            lo = max(idx + 1 - w, 0)
            s = ext[idx][:, cols]
            for r in range(idx - 1, lo - 1, -1):
                s = s + ext[r][:, cols]
            parts.append(s * (1.0 / min(w, pos + 1)) - ext[idx][:, cols])
        ps.append(jnp.concatenate(parts, axis=1))
    p = jnp.concatenate(ps, axis=0).astype(BF16)
    zs = [_dot(p[:, g * POOL_GROUP_DIM:(g + 1) * POOL_GROUP_DIM], w_pool_ref[g])
          for g in range(len(POOL_WINDOWS))]
    z = (jnp.concatenate(zs, axis=1) * pscale_ref[...]).astype(BF16)
    for r in range(POOL_BUF):
        np_ref[r] = ext[r + tokens]

    qh = []
    for hd in range(N_HEADS):
        tl = q[:, (hd // 2) * LANES:(hd // 2 + 1) * LANES]
        kv = hd // (N_HEADS // N_KV_HEADS)
        want_low = kv == 0
        is_low = hd % 2 == 0
        src = tl if want_low == is_low else pltpu.roll(tl, HEAD_DIM, axis=1)
        qh.append(jnp.where(low, src, 0.0) if want_low else jnp.where(low, 0.0, src))

    gr = SEQ_GROUP
    rows_per_head = tokens * gr
    sub = lax.broadcasted_iota(jnp.int32, (N_HEADS * rows_per_head, 1), 0) % gr
    sink = jnp.concatenate([jnp.full((rows_per_head, 1), sink_ref[0, hd], F32) for hd in range(N_HEADS)],
                           axis=0)
    zpad = jnp.zeros((LANES - rows_per_head, LANES), F32)
    ztop = jnp.zeros((WINDOW - SUBLANES, LANES), F32)
    sub8 = lax.broadcasted_iota(jnp.int32, (SUBLANES, 1), 0)
    keep = WINDOW - tokens

    def appended(old_ref, news, j):
        tail = jnp.zeros((SUBLANES, LANES), F32)
        for t in range(tokens):
            row = SUBLANES - tokens + t
            tail = jnp.where(sub8 == row, pltpu.roll(news[t], (row - j) % SUBLANES, axis=0), tail)
        cols = jnp.concatenate([ztop, tail], axis=0).T
        return jnp.where(lane >= keep, cols, pltpu.roll(old_ref[...], keep, axis=1))

    for base in range(0, ns, gr):
        k_t = [k[t * ns + base:t * ns + base + gr] for t in range(tokens)]
        v_t = [v[t * ns + base:t * ns + base + gr] for t in range(tokens)]
        for s in range(gr):
            nk_ref[base + s] = appended(ck_ref.at[base + s], k_t, s)
            nv_ref[base + s] = appended(cv_ref.at[base + s], v_t, s)

    for gi in range(ns // gr):
        base = gi * gr
        lhs = jnp.concatenate([qh[hd][t * ns + base:t * ns + base + gr]
                               for hd in range(N_HEADS) for t in range(tokens)], axis=0).astype(BF16)
        sc = None
        for s in range(gr):
            s_one = _dot(lhs, ck_ref[base + s].astype(BF16))
            sc = s_one if sc is None else jnp.where(sub == s, s_one, sc)
        sc = sc + bias_c_ref[...]
        k_t = [k[t * ns + base:t * ns + base + gr] for t in range(tokens)]
        v_t = [v[t * ns + base:t * ns + base + gr] for t in range(tokens)]
        k_new = jnp.concatenate(k_t + [zpad], axis=0)
        v_new = jnp.concatenate(v_t + [zpad], axis=0)
        sn = _dot_nt(lhs, k_new.astype(BF16)) + bias_n_ref[...]
        m = jnp.maximum(jnp.maximum(jnp.max(sc, axis=-1, keepdims=True),
                                    jnp.max(sn, axis=-1, keepdims=True)), sink)
        ec = jnp.exp(sc - m)
        en = jnp.exp(sn - m)
        den = (jnp.sum(ec, axis=-1, keepdims=True) + jnp.sum(en, axis=-1, keepdims=True)
               + jnp.exp(sink - m))
        o = _dot(en.astype(BF16), v_new.astype(BF16))
        for s in range(gr):
            o = o + _dot_nt(jnp.where(sub == s, ec, 0.0).astype(BF16), cv_ref[base + s].astype(BF16))
        o = o * (1.0 / den)
        for t in range(tokens):
            for p in range(N_HEADS // 2):
                a = o[(2 * p * tokens + t) * gr:(2 * p * tokens + t + 1) * gr]
                c = o[((2 * p + 1) * tokens + t) * gr:((2 * p + 1) * tokens + t + 1) * gr]
                if p < N_HEADS // 4:
                    c = pltpu.roll(c, HEAD_DIM, axis=1)
                else:
                    a = pltpu.roll(a, HEAD_DIM, axis=1)
                o_s[t * ns + base:t * ns + base + gr, p * LANES:(p + 1) * LANES] = jnp.where(low, a, c)

    for t in range(tokens):
        mix_ref[:, t * d_model:t * d_model + D_POOL] = z[t * ns:(t + 1) * ns]
        mix_ref[:, t * d_model + D_POOL:(t + 1) * d_model] = o_s[t * ns:(t + 1) * ns, :].astype(BF16)


def _pre_pieces(rows, res, mix, w_out_ref, g1_ref, b1_ref, h1_s, hb_s):
    h1_s[rows, :] = res() + _dot(mix(), w_out_ref[...])
    yield
    h = _layer_norm(h1_s[rows, :], g1_ref[...], b1_ref[...])
    h1_s[rows, :] = h
    hb_s[rows, :] = h.astype(BF16)
    yield


def _mlp_pieces(rows, a_s, h1_s, hb_s, w1_ref, w2_ref, g2_ref, b2_ref, y_ref):
    for c in range(w1_ref.shape[1] // UP_CHUNK):
        cols = slice(c * UP_CHUNK, (c + 1) * UP_CHUNK)
        a = _dot(hb_s[rows, :], w1_ref[:, cols])
        a_s[:, cols] = jnp.square(jnp.maximum(a, 0.0)).astype(BF16)
        if c == 0:
            y_ref[rows, :] = ALPHA * h1_s[rows, :]
        yield
    for n in range(y_ref.shape[1] // DOWN_CHUNK):
        cols = slice(n * DOWN_CHUNK, (n + 1) * DOWN_CHUNK)
        y_ref[rows, cols] = y_ref[rows, cols] + _dot(a_s[...], w2_ref[:, cols])
        yield
    y_ref[rows, :] = _layer_norm(y_ref[rows, :], g2_ref[...], b2_ref[...])
    yield


def _weave(main, n_main, sides):
    order = []
    for i in range(n_main):
        order.append(main)
        order += sides[i * len(sides) // n_main:(i + 1) * len(sides) // n_main]
    return order


def _main_kernel(tokens, tiles_per_seq, n_tiles,
                 sink_ref, x_ref, xs_ref, mixs_ref, ge_ref, be_ref, w_in_ref, w_pool_ref,
                 pscale_ref, bias_ref, meta_proj_ref, w_out_ref, g1_ref, b1_ref, g2_ref, b2_ref,
                 w1_hbm, w2_hbm,
                 y_ref, ys_ref, nk_ref, nv_ref, np_ref,
                 mixbuf, kvar, vvar, uext, qbuf, pbuf, hres, h1_s, hb_s, a_s, w1_ref, w2_ref, stage, sem):
    s = pl.program_id(0)
    is_first = s == 0
    first_tile = jnp.minimum(s, n_tiles - 1) % tiles_per_seq == 0
    tile, d_model = x_ref.shape
    ns = xs_ref.shape[0]
    half = tile // 2
    halves = (slice(0, half), slice(half, tile))
    pre_w = (w_out_ref, g1_ref, b1_ref, h1_s, hb_s)

    @pl.when(is_first)
    def _():
        def toks(rows):
            return range(rows.start // ns, rows.stop // ns)

        def sample_pre():
            for rows in halves:
                res = lambda: ALPHA * _layer_norm(jnp.concatenate([xs_ref[:, t, :] for t in toks(rows)], axis=0),
                                                  ge_ref[...], be_ref[...])
                mix = lambda: jnp.concatenate([mixs_ref[:, t * d_model:(t + 1) * d_model] for t in toks(rows)],
                                              axis=0)
                yield from _pre_pieces(rows, res, mix, *pre_w)

        cr, cc = stage.shape[1:]
        chunks = [(src, dst, r, c) for src, dst in ((w1_hbm, w1_ref), (w2_hbm, w2_ref))
                  for r in range(0, src.shape[0], cr) for c in range(0, src.shape[1], cc)]
        ring = [stage.at[i] for i in range(stage.shape[0])]
        ring += [buf.at[pl.ds(r, cr)] for buf in (hres, y_ref) for r in range(0, tile, cr)]
        slots = len(ring)

        def copy(i):
            src, _, r, c = chunks[i]
            return pltpu.make_async_copy(src.at[pl.ds(r, cr), pl.ds(c, cc)], ring[i % slots], sem.at[i % slots])

        for i in range(slots):
            copy(i).start()
        pre = sample_pre()
        every = len(chunks) // 4
        for i, (_, dst, r, c) in enumerate(chunks):
            if i % every == every // 2:
                next(pre, None)
            copy(i).wait()
            dst[r:r + cr, c:c + cc] = ring[i % slots][...].astype(BF16)
            if i + slots < len(chunks):
                copy(i + slots).start()
        for _ in pre:
            raise AssertionError("piece left unscheduled")

    @pl.when(first_tile)
    def _():
        _carry_init(meta_proj_ref, kvar, vvar, uext)

    mlp = [_mlp_pieces(rows, a_s, h1_s, hb_s, w1_ref, w2_ref, g2_ref, b2_ref, y_ref) for rows in halves]
    front = _front_pieces(first_tile, x_ref, sink_ref, ge_ref, be_ref, w_in_ref, w_pool_ref, pscale_ref,
                          bias_ref, mixbuf, nk_ref.at[0], nv_ref.at[0], np_ref.at[0], kvar, vvar, uext, qbuf, pbuf,
                          hres)
    pre = [_pre_pieces(rows, lambda rows=rows: hres[rows, :], lambda rows=rows: mixbuf[rows, :], *pre_w)
           for rows in halves]
    a, b, f, pa, pb = mlp[0], mlp[1], front, pre[0], pre[1]
    n_up = w1_ref.shape[1] // UP_CHUNK
    n_down = d_model // DOWN_CHUNK
    order = (
        _weave(a, n_up, [f] * 4)
        + _weave(a, n_down, [f] * 5)
        + _weave(b, n_up, [a] + [f] * 5)
        + _weave(b, n_down, [f, pa, pb, pa, pb]) + [b])
    for g in order:
        next(g, None)
    for g in (a, b, f, pa, pb):
        for _ in g:
            raise AssertionError("piece left unscheduled")

    @pl.when(is_first)
    def _():
        for t in range(tokens):
            ys_ref[:, t, :] = y_ref[t * ns:(t + 1) * ns, :]


def _const_spec(shape):
    return pl.BlockSpec(shape, lambda *_: (0,) * len(shape), pipeline_mode=pl.Buffered(1))


def _smem_spec():
    return pl.BlockSpec(memory_space=pltpu.SMEM)


def _sample_front_call(rel_table, sink, xs, ck, cv, st, w_out, meta_tokens, ge, be, w_in, w_pool, pscale):
    n_seq, tokens, d_model = xs.shape
    steps = n_seq // SAMPLE_SEQS
    srows = N_HEADS * tokens * SEQ_GROUP
    tables = [jnp.asarray(t) for t in _bucket_tables(tokens)]
    consts = tables + [meta_tokens, ge, be, w_in, w_pool, pscale]
    whole = lambda shape: pl.BlockSpec(shape, lambda i: (0,) * len(shape))
    seq_spec = lambda *tail: pl.BlockSpec((SAMPLE_SEQS,) + tail, lambda i: (i,) + (0,) * len(tail))
    state_spec = pl.BlockSpec((POOL_BUF, SAMPLE_SEQS, D_POOL), lambda i: (0, i, 0))
    slab_spec = pl.BlockSpec((w_out.shape[0] // steps, w_out.shape[1]), lambda i: (i, 0))
    once = [jax.ShapeDtypeStruct((2, N_HEADS, BLOCK, 2 * BLOCK), F32),
            jax.ShapeDtypeStruct((meta_tokens.shape[0], w_in.shape[1]), F32),
            jax.ShapeDtypeStruct(w_in.shape, BF16),
            jax.ShapeDtypeStruct((D_POOL, D_POOL), BF16)]
    return pl.pallas_call(
        _sample_front_kernel,
        grid=(steps,),
        in_specs=[_smem_spec(), _smem_spec(), seq_spec(tokens, d_model), seq_spec(D_KV, WINDOW),
                  seq_spec(D_KV, WINDOW), state_spec, slab_spec] + [_const_spec(c.shape) for c in consts],
        out_specs=[seq_spec(tokens * d_model), seq_spec(D_KV, WINDOW), seq_spec(D_KV, WINDOW), state_spec,
                   slab_spec] + [whole(o.shape) for o in once],
        out_shape=[jax.ShapeDtypeStruct((n_seq, tokens * d_model), BF16),
                   jax.ShapeDtypeStruct((n_seq, D_KV, WINDOW), F32),
                   jax.ShapeDtypeStruct((n_seq, D_KV, WINDOW), F32),
                   jax.ShapeDtypeStruct((POOL_BUF, n_seq, D_POOL), F32),
                   jax.ShapeDtypeStruct(w_out.shape, BF16)] + once,
        scratch_shapes=[pltpu.VMEM((tokens * SAMPLE_SEQS, D_ATTN), F32),
                        pltpu.VMEM((srows, WINDOW), F32),
                        pltpu.VMEM((srows, LANES), F32),
                        pltpu.VMEM(w_pool.shape, BF16)],
        compiler_params=pltpu.CompilerParams(dimension_semantics=("arbitrary",),
                                             vmem_limit_bytes=VMEM_LIMIT_BYTES),
        name="sample_front",
    )(rel_table, sink, xs, ck, cv, st, w_out, *consts)


def kernel(x_prompt, x_sample, cache_win_k, cache_win_v, state_pool, meta_tokens, ln_emb_g, ln_emb_b,
           rel_table, w_in, w_pool, pool_scale, sinks, w_out, ln1_g, ln1_b, w_mlp_in, w_mlp_out,
           ln2_g, ln2_b):
    batch, seq, d_model = x_prompt.shape
    n_seq, tokens, _ = x_sample.shape
    assert w_in.shape[0] == DEPTH and d_model == D_POOL + D_ATTN
    assert seq % PROMPT_TILE == 0 and n_seq % SAMPLE_SEQS == 0 and tokens * n_seq == PROMPT_TILE
    assert cache_win_k.shape[2] == WINDOW and state_pool.shape[2] == POOL_BUF and tokens <= POOL_BUF
    assert N_META == HALO and POOL_BUF <= HALO

    row = lambda a: a.reshape(1, -1).astype(F32)
    ge, be = row(ln_emb_g), row(ln_emb_b)
    pscale = row(pool_scale[0])
    sink = row(sinks[0])

    to_kd_pos = lambda c: jnp.swapaxes(c[0].reshape(n_seq, WINDOW, D_KV), 1, 2)
    mix_s, nk_s, nv_s, np_s, w_out_b, bias_p, meta_proj, w_in_b, w_pool_bd = _sample_front_call(
        rel_table.astype(F32).T, sink, x_sample, to_kd_pos(cache_win_k), to_kd_pos(cache_win_v),
        jnp.swapaxes(state_pool[0], 0, 1), w_out[0], meta_tokens.astype(F32), ge, be, w_in[0], w_pool[0], pscale)
    xs = x_sample
    w1, w2 = w_mlp_in[0], w_mlp_out[0]

    tile = PROMPT_TILE
    tiles_per_seq = seq // tile
    n_tiles = batch * tiles_per_seq
    xp = x_prompt.reshape(batch * seq, d_model)
    front_tile = lambda s: jnp.minimum(s, n_tiles - 1)
    finish_tile = lambda s: jnp.maximum(s - 1, 0)
    seq_of = lambda s: front_tile(s) // tiles_per_seq
    consts = (ge, be, w_in_b, w_pool_bd, pscale, bias_p, meta_proj, w_out_b, row(ln1_g[0]), row(ln1_b[0]),
              row(ln2_g[0]), row(ln2_b[0]))
    y_p, y_s, nk_p, nv_p, np_p = pl.pallas_call(
        functools.partial(_main_kernel, tokens, tiles_per_seq, n_tiles),
        grid=(n_tiles + 1,),
        in_specs=[_smem_spec(),
                  pl.BlockSpec((tile, d_model), lambda s: (front_tile(s), 0)),
                  _const_spec(xs.shape), _const_spec(mix_s.shape)] + [_const_spec(c.shape) for c in consts]
        + [pl.BlockSpec(memory_space=pl.ANY)] * 2,
        out_specs=[pl.BlockSpec((tile, d_model), lambda s: (finish_tile(s), 0)),
                   pl.BlockSpec(xs.shape, lambda s: (0, 0, 0)),
                   pl.BlockSpec((1, BLOCK, D_KV), lambda s: (seq_of(s), 0, 0)),
                   pl.BlockSpec((1, BLOCK, D_KV), lambda s: (seq_of(s), 0, 0)),
                   pl.BlockSpec((1, HALO, D_POOL), lambda s: (seq_of(s), 0, 0))],
        out_shape=[jax.ShapeDtypeStruct((batch * seq, d_model), F32),
                   jax.ShapeDtypeStruct(xs.shape, F32),
                   jax.ShapeDtypeStruct((batch, BLOCK, D_KV), F32),
                   jax.ShapeDtypeStruct((batch, BLOCK, D_KV), F32),
                   jax.ShapeDtypeStruct((batch, HALO, D_POOL), F32)],
        scratch_shapes=[pltpu.VMEM((tile, d_model), BF16),
                        pltpu.VMEM((4, tile + BLOCK, LANES), BF16),
                        pltpu.VMEM((4, tile + BLOCK, LANES), BF16),
                        pltpu.VMEM((tile + HALO, D_POOL), F32),
                        pltpu.VMEM((tile, D_ATTN), BF16),
                        pltpu.VMEM((tile, D_POOL), BF16),
                        pltpu.VMEM((tile, d_model), F32),
                        pltpu.VMEM((tile, d_model), F32),
                        pltpu.VMEM((tile, d_model), BF16),
                        pltpu.VMEM((tile // 2, w1.shape[1]), BF16),
                        pltpu.VMEM(w1.shape, BF16),
                        pltpu.VMEM(w2.shape, BF16),
                        pltpu.VMEM((WEIGHT_SLOTS,) + WEIGHT_CHUNK, F32),
                        pltpu.SemaphoreType.DMA((WEIGHT_SLOTS + 2 * (tile // WEIGHT_CHUNK[0]),))],
        compiler_params=pltpu.CompilerParams(dimension_semantics=("arbitrary",),
                                             vmem_limit_bytes=VMEM_LIMIT_BYTES),
        name="main",
    )(sink, xp, xs, mix_s, *consts, w1, w2)

    kv_shape = (DEPTH, -1, WINDOW, N_KV_HEADS, HEAD_DIM)
    from_kd_pos = lambda c: jnp.swapaxes(c, 1, 2).reshape(kv_shape)
    return (y_p.reshape(batch, seq, d_model), y_s,
            from_kd_pos(nk_p), from_kd_pos(nv_p),
            np_p[:, HALO - POOL_BUF:, :].reshape(DEPTH, batch, POOL_BUF, D_POOL),
            from_kd_pos(nk_s), from_kd_pos(nv_s), jnp.swapaxes(np_s, 0, 1)[None])
```

```python
import functools
import math

import jax
import jax.numpy as jnp
import numpy as np
from jax import lax
from jax.experimental import pallas as pl
from jax.experimental.pallas import tpu as pltpu

N_META = 16
POOL_WINDOWS = (2, 4, 8, 16)
POOL_GROUP_DIM = 128
D_POOL = len(POOL_WINDOWS) * POOL_GROUP_DIM
POOL_BUF = max(POOL_WINDOWS) - 1
N_HEADS = 8
HEAD_DIM = 64
D_ATTN = N_HEADS * HEAD_DIM
N_KV_HEADS = 2
D_KV = N_KV_HEADS * HEAD_DIM
WINDOW = 128
BLOCK = 128
REL_BUCKETS = 32
REL_MAX_DIST = 128
PAST_LEN = 8192
DEPTH = 1
ALPHA = (2.0 * DEPTH) ** 0.25
LN_EPS = 1e-5
Q_SCALE = HEAD_DIM ** -0.5

LANES = 128
SUBLANES = 8
VMEM_LIMIT_BYTES = 60 * 1024 * 1024

PROMPT_TILE = 512
IN_ROWS = 256
UP_CHUNK = 512
DOWN_CHUNK = 256
SAMPLE_SEQS = 32
SEQ_GROUP = SUBLANES
HALO = 2 * SUBLANES
WEIGHT_CHUNK = (256, 1024)
WEIGHT_SLOTS = 2

F32 = jnp.float32
BF16 = jnp.bfloat16
NEG_INF = float("-inf")


def _rel_bucket(dist):
    n = np.maximum(dist, 0)
    max_exact = REL_BUCKETS // 2
    nf = np.maximum(n, 1).astype(np.float64)
    large = max_exact + (np.log(nf / max_exact) / math.log(REL_MAX_DIST / max_exact)
                         * (REL_BUCKETS - max_exact)).astype(np.int32)
    large = np.minimum(large, REL_BUCKETS - 1)
    return np.where(n < max_exact, n, large).astype(np.int32)


def _bucket_tables(tokens):
    prompt = np.tile(_rel_bucket((BLOCK - np.arange(BLOCK)) % BLOCK)[None, :], (SUBLANES, 1)).astype(np.int32)
    t = np.repeat(np.arange(tokens), SEQ_GROUP)
    s = np.tile(np.arange(SEQ_GROUP), tokens)
    dist = (t[:, None] + WINDOW) - np.arange(WINDOW)[None, :]
    ok = (dist >= 0) & (dist < WINDOW)
    cache = np.where(ok, _rel_bucket(dist), -1).astype(np.int32)
    dist = t[:, None] - t[None, :]
    ok = (dist >= 0) & (s[:, None] == s[None, :])
    new = np.full((tokens * SEQ_GROUP, LANES), -1, np.int32)
    new[:, :tokens * SEQ_GROUP] = np.where(ok, _rel_bucket(dist), -1)
    return prompt, cache, new


def _layer_norm(x, g, b):
    mu = jnp.mean(x, axis=-1, keepdims=True)
    xc = x - mu
    var = jnp.mean(xc * xc, axis=-1, keepdims=True)
    return xc * lax.rsqrt(var + LN_EPS) * g + b


def _dot(a, b):
    return jnp.dot(a, b, preferred_element_type=F32)


def _dot_nt(a, b):
    return lax.dot_general(a, b, (((1,), (1,)), ((), ())), preferred_element_type=F32)


def _kv_lane_variants(x, low):
    xr = pltpu.roll(x, HEAD_DIM, axis=1)
    zero = jnp.zeros_like(x)
    return (jnp.where(low, x, zero).astype(BF16), jnp.where(low, zero, xr).astype(BF16),
            jnp.where(low, xr, zero).astype(BF16), jnp.where(low, zero, x).astype(BF16))


def _setup_tables(tab_ref, bkt_p_ref, bkt_c_ref, bkt_n_ref, meta_ref, g_ref, b_ref, wf_in_ref, wf_pool_ref,
                  bias_p_ref, bias_c_ref, bias_n_ref, meta_proj_ref, w_in_ref, w_pool_ref, w_pool_bd_ref):
    w_in_ref[...] = wf_in_ref[...].astype(BF16)
    w_pool_ref[...] = wf_pool_ref[...].astype(BF16)
    w_pool_bd_ref[...] = jnp.zeros_like(w_pool_bd_ref)
    for g in range(len(POOL_WINDOWS)):
        cols = slice(g * POOL_GROUP_DIM, (g + 1) * POOL_GROUP_DIM)
        w_pool_bd_ref[cols, cols] = w_pool_ref[g]
    def lookup(bucket):
        accs = [jnp.full(bucket.shape, NEG_INF, F32)] * N_HEADS
        for i in range(REL_BUCKETS):
            hit = bucket == i
            accs = [jnp.where(hit, tab_ref[h, i], acc) for h, acc in enumerate(accs)]
        return accs

    rows = bkt_c_ref.shape[0]
    q = lax.broadcasted_iota(jnp.int32, (BLOCK, BLOCK), 0)
    k = lax.broadcasted_iota(jnp.int32, (BLOCK, BLOCK), 1)
    per_dist, cache_bias, new_bias = lookup(bkt_p_ref[...]), lookup(bkt_c_ref[...]), lookup(bkt_n_ref[...])
    for h in range(N_HEADS):
        toeplitz = pltpu.roll(jnp.concatenate([per_dist[h]] * (BLOCK // SUBLANES), axis=0), 0, 1,
                              stride=1, stride_axis=0)
        cur = jnp.where(k <= q, toeplitz, NEG_INF)
        for table, prev_ok in enumerate(((k > q) & (k >= BLOCK - N_META), k > q)):
            bias_p_ref[table, h, :, 0:BLOCK] = jnp.where(prev_ok, toeplitz, NEG_INF)
            bias_p_ref[table, h, :, BLOCK:] = cur
        bias_c_ref[h * rows:(h + 1) * rows, :] = cache_bias[h]
        bias_n_ref[h * rows:(h + 1) * rows, :] = new_bias[h]
    hm = _layer_norm(meta_ref[...], g_ref[...], b_ref[...])
    meta_proj_ref[...] = _dot(hm.astype(BF16), w_in_ref[...])


def _carry_init(meta_proj_ref, kvar, vvar, uext):
    low = lax.broadcasted_iota(jnp.int32, (1, LANES), 1) < HEAD_DIM
    pad = jnp.zeros((BLOCK - N_META, LANES), BF16)
    kq = _kv_lane_variants(meta_proj_ref[:, D_POOL + D_ATTN:D_POOL + D_ATTN + D_KV], low)
    vq = _kv_lane_variants(meta_proj_ref[:, D_POOL + D_ATTN + D_KV:], low)
    for i in range(4):
        kvar[i, 0:BLOCK, :] = jnp.concatenate([pad, kq[i]], axis=0)
        vvar[i, 0:BLOCK, :] = jnp.concatenate([pad, vq[i]], axis=0)
    uext[0:HALO, :] = meta_proj_ref[:, 0:D_POOL]


def _front_pieces(first_tile, x_ref, sink_ref, g_ref, b_ref, w_in_ref, w_pool_ref, pscale_ref, bias_ref,
                  mix_ref, nk_ref, nv_ref, np_ref, kvar, vvar, uext, qbuf, pbuf, hres):
    tile = x_ref.shape[0]
    halo = HALO
    low = lax.broadcasted_iota(jnp.int32, (1, LANES), 1) < HEAD_DIM

    for j in range(tile // IN_ROWS):
        rows = slice(j * IN_ROWS, (j + 1) * IN_ROWS)
        h = _layer_norm(x_ref[rows, :], g_ref[...], b_ref[...])
        hres[rows, :] = ALPHA * h
        proj = _dot(h.astype(BF16), w_in_ref[...])
        u = proj[:, 0:D_POOL]
        k = proj[:, D_POOL + D_ATTN:D_POOL + D_ATTN + D_KV]
        v = proj[:, D_POOL + D_ATTN + D_KV:]
        qbuf[rows, :] = (proj[:, D_POOL:D_POOL + D_ATTN] * Q_SCALE).astype(BF16)
        uext[halo + rows.start:halo + rows.stop, :] = u
        kq = _kv_lane_variants(k, low)
        vq = _kv_lane_variants(v, low)
        for i in range(4):
            kvar[i, BLOCK + rows.start:BLOCK + rows.stop, :] = kq[i]
            vvar[i, BLOCK + rows.start:BLOCK + rows.stop, :] = vq[i]
        if rows.stop == tile:
            nk_ref[...] = k[IN_ROWS - BLOCK:, :].T
            nv_ref[...] = v[IN_ROWS - BLOCK:, :].T
            np_ref[...] = u[IN_ROWS - halo:, :]
        yield

    for g, w in enumerate(POOL_WINDOWS):
        cols = slice(g * POOL_GROUP_DIM, (g + 1) * POOL_GROUP_DIM)
        ug = uext[halo:halo + tile, cols]
        s = ug
        for i in range(1, w):
            s = s + uext[halo - i:halo - i + tile, cols]
        pbuf[:, cols] = (s * (1.0 / w) - ug).astype(BF16)
        if g == len(POOL_WINDOWS) - 1:
            mix_ref[:, 0:D_POOL] = (_dot(pbuf[...], w_pool_ref[...]) * pscale_ref[...]).astype(BF16)
        yield

    for j in range(tile // BLOCK):
        rows = slice(j * BLOCK, (j + 1) * BLOCK)
        keys = slice(j * BLOCK, (j + 2) * BLOCK)
        sel = jnp.where(first_tile, 0, 1) if j == 0 else 1
        for kv in range(N_KV_HEADS):
            tiles = (2 * kv, 2 * kv + 1)
            lhs = jnp.concatenate([qbuf[rows, p * LANES:(p + 1) * LANES] for p in tiles], axis=0)
            nk = 2 * BLOCK
            s_all = _dot_nt(lhs, jnp.concatenate([kvar[2 * kv, keys, :], kvar[2 * kv + 1, keys, :]], axis=0))
            es, dens = [], []
            for half in range(2):
                heads = [2 * p + half for p in tiles]
                s = s_all[:, half * nk:(half + 1) * nk]
                s = s + jnp.concatenate([bias_ref[sel, hd] for hd in heads], axis=0)
                sink = jnp.concatenate([jnp.full((BLOCK, 1), sink_ref[0, hd], F32) for hd in heads], axis=0)
                m = jnp.maximum(jnp.max(s, axis=-1, keepdims=True), sink)
                e = jnp.exp(s - m)
                dens.append(jnp.sum(e, axis=-1, keepdims=True) + jnp.exp(sink - m))
                es.append(e.astype(BF16))
            o = _dot(jnp.concatenate(es, axis=0),
                     jnp.concatenate([vvar[2 * kv, keys, :], vvar[2 * kv + 1, keys, :]], axis=1))
            acc = (o[0:nk, 0:LANES] * (1.0 / dens[0]) + o[nk:, LANES:] * (1.0 / dens[1]))
            for n, p in enumerate(tiles):
                mix_ref[rows, D_POOL + p * LANES:D_POOL + (p + 1) * LANES] = (
                    acc[n * BLOCK:(n + 1) * BLOCK].astype(BF16))
            yield

    for i in range(4):
        kvar[i, 0:BLOCK, :] = kvar[i, tile:tile + BLOCK, :]
        vvar[i, 0:BLOCK, :] = vvar[i, tile:tile + BLOCK, :]
    uext[0:halo, :] = uext[tile:tile + halo, :]


def _sample_front_kernel(tab_ref, sink_ref, x_ref, ck_ref, cv_ref, st_ref, wf_out_ref,
                         bkt_p_ref, bkt_c_ref, bkt_n_ref, meta_ref, g_ref, b_ref, wf_in_ref, wf_pool_ref, pscale_ref,
                         mix_ref, nk_ref, nv_ref, np_ref, wb_out_ref,
                         bias_p_ref, meta_proj_ref, w_in_ref, w_pool_bd_ref,
                         o_s, bias_c_ref, bias_n_ref, w_pool_ref):
    @pl.when(pl.program_id(0) == 0)
    def _():
        _setup_tables(tab_ref, bkt_p_ref, bkt_c_ref, bkt_n_ref, meta_ref, g_ref, b_ref, wf_in_ref, wf_pool_ref,
                      bias_p_ref, bias_c_ref, bias_n_ref, meta_proj_ref, w_in_ref, w_pool_ref, w_pool_bd_ref)

    wb_out_ref[...] = wf_out_ref[...].astype(BF16)
    ns, tokens, d_model = x_ref.shape
    lane = lax.broadcasted_iota(jnp.int32, (1, LANES), 1)
    low = lane < HEAD_DIM

    x = jnp.concatenate([x_ref[:, t, :] for t in range(tokens)], axis=0)
    h = _layer_norm(x, g_ref[...], b_ref[...])
    proj = _dot(h.astype(BF16), w_in_ref[...])
    u = proj[:, 0:D_POOL]
    q = proj[:, D_POOL:D_POOL + D_ATTN] * Q_SCALE
    k = proj[:, D_POOL + D_ATTN:D_POOL + D_ATTN + D_KV]
    v = proj[:, D_POOL + D_ATTN + D_KV:]

    ext = [st_ref[r] for r in range(POOL_BUF)]
    ext += [u[t * ns:(t + 1) * ns] for t in range(tokens)]
    ps = []
    for t in range(tokens):
        idx = POOL_BUF + t
        pos = PAST_LEN - POOL_BUF + idx
        parts = []
        for g, w in enumerate(POOL_WINDOWS):
            cols = slice(g * POOL_GROUP_DIM, (g + 1) * POOL_GROUP_DIM)
            lo = max(idx + 1 - w, 0)
            s = ext[idx][:, cols]
            for r in range(idx - 1, lo - 1, -1):
                s = s + ext[r][:, cols]
            parts.append(s * (1.0 / min(w, pos + 1)) - ext[idx][:, cols])
        ps.append(jnp.concatenate(parts, axis=1))
    p = jnp.concatenate(ps, axis=0).astype(BF16)
    zs = [_dot(p[:, g * POOL_GROUP_DIM:(g + 1) * POOL_GROUP_DIM], w_pool_ref[g])
          for g in range(len(POOL_WINDOWS))]
    z = (jnp.concatenate(zs, axis=1) * pscale_ref[...]).astype(BF16)
    for r in range(POOL_BUF):
        np_ref[r] = ext[r + tokens]

    qh = []
    for hd in range(N_HEADS):
        tl = q[:, (hd // 2) * LANES:(hd // 2 + 1) * LANES]
        kv = hd // (N_HEADS // N_KV_HEADS)
        want_low = kv == 0
        is_low = hd % 2 == 0
        src = tl if want_low == is_low else pltpu.roll(tl, HEAD_DIM, axis=1)
        qh.append(jnp.where(low, src, 0.0) if want_low else jnp.where(low, 0.0, src))

    gr = SEQ_GROUP
    rows_per_head = tokens * gr
    sub = lax.broadcasted_iota(jnp.int32, (N_HEADS * rows_per_head, 1), 0) % gr
    sink = jnp.concatenate([jnp.full((rows_per_head, 1), sink_ref[0, hd], F32) for hd in range(N_HEADS)],
                           axis=0)
    zpad = jnp.zeros((LANES - rows_per_head, LANES), F32)
    ztop = jnp.zeros((WINDOW - SUBLANES, LANES), F32)
    sub8 = lax.broadcasted_iota(jnp.int32, (SUBLANES, 1), 0)
    keep = WINDOW - tokens

    def appended(old_ref, news, j):
        tail = jnp.zeros((SUBLANES, LANES), F32)
        for t in range(tokens):
            row = SUBLANES - tokens + t
            tail = jnp.where(sub8 == row, pltpu.roll(news[t], (row - j) % SUBLANES, axis=0), tail)
        cols = jnp.concatenate([ztop, tail], axis=0).T
        return jnp.where(lane >= keep, cols, pltpu.roll(old_ref[...], keep, axis=1))

    for gi in range(ns // gr):
        base = gi * gr
        lhs = jnp.concatenate([qh[hd][t * ns + base:t * ns + base + gr]
                               for hd in range(N_HEADS) for t in range(tokens)], axis=0).astype(BF16)
        k_t = [k[t * ns + base:t * ns + base + gr] for t in range(tokens)]
        v_t = [v[t * ns + base:t * ns + base + gr] for t in range(tokens)]
        sc = None
        for s in range(gr):
            s_one = _dot(lhs, ck_ref[base + s].astype(BF16))
            nk_ref[base + s] = appended(ck_ref.at[base + s], k_t, s)
            sc = s_one if sc is None else jnp.where(sub == s, s_one, sc)
        sc = sc + bias_c_ref[...]
        k_new = jnp.concatenate(k_t + [zpad], axis=0)
        v_new = jnp.concatenate(v_t + [zpad], axis=0)
        sn = _dot_nt(lhs, k_new.astype(BF16)) + bias_n_ref[...]
        m = jnp.maximum(jnp.max(jnp.maximum(sc, sn), axis=-1, keepdims=True), sink)
        ec = jnp.exp(sc - m)
        en = jnp.exp(sn - m)
        den = jnp.sum(ec + en, axis=-1, keepdims=True) + jnp.exp(sink - m)
        o = _dot(en.astype(BF16), v_new.astype(BF16))
        for s in range(gr):
            o = o + _dot_nt(jnp.where(sub == s, ec, 0.0).astype(BF16), cv_ref[base + s].astype(BF16))
            nv_ref[base + s] = appended(cv_ref.at[base + s], v_t, s)
        o = o * (1.0 / den)
        for t in range(tokens):
            for p in range(N_HEADS // 2):
                a = o[(2 * p * tokens + t) * gr:(2 * p * tokens + t + 1) * gr]
                c = o[((2 * p + 1) * tokens + t) * gr:((2 * p + 1) * tokens + t + 1) * gr]
                if p < N_HEADS // 4:
                    c = pltpu.roll(c, HEAD_DIM, axis=1)
                else:
                    a = pltpu.roll(a, HEAD_DIM, axis=1)
                o_s[t * ns + base:t * ns + base + gr, p * LANES:(p + 1) * LANES] = jnp.where(low, a, c)

    for t in range(tokens):
        mix_ref[:, t * d_model:t * d_model + D_POOL] = z[t * ns:(t + 1) * ns]
        mix_ref[:, t * d_model + D_POOL:(t + 1) * d_model] = o_s[t * ns:(t + 1) * ns, :].astype(BF16)


def _pre_pieces(rows, res, mix, w_out_ref, g1_ref, b1_ref, h1_s, hb_s):
    h1_s[rows, :] = res() + _dot(mix(), w_out_ref[...])
    yield
    h = _layer_norm(h1_s[rows, :], g1_ref[...], b1_ref[...])
    h1_s[rows, :] = h
    hb_s[rows, :] = h.astype(BF16)
    yield


def _mlp_pieces(rows, a_s, h1_s, hb_s, w1_ref, w2_ref, g2_ref, b2_ref, y_ref):
    for c in range(w1_ref.shape[1] // UP_CHUNK):
        cols = slice(c * UP_CHUNK, (c + 1) * UP_CHUNK)
        a = _dot(hb_s[rows, :], w1_ref[:, cols])
        a_s[:, cols] = jnp.square(jnp.maximum(a, 0.0)).astype(BF16)
        if c == 0:
            y_ref[rows, :] = ALPHA * h1_s[rows, :]
        yield
    for n in range(y_ref.shape[1] // DOWN_CHUNK):
        cols = slice(n * DOWN_CHUNK, (n + 1) * DOWN_CHUNK)
        y_ref[rows, cols] = y_ref[rows, cols] + _dot(a_s[...], w2_ref[:, cols])
        yield
    y_ref[rows, :] = _layer_norm(y_ref[rows, :], g2_ref[...], b2_ref[...])
    yield


def _weave(main, n_main, sides):
    order = []
    for i in range(n_main):
        order.append(main)
        order += sides[i * len(sides) // n_main:(i + 1) * len(sides) // n_main]
    return order


def _main_kernel(tokens, tiles_per_seq, n_tiles,
                 sink_ref, x_ref, xs_ref, mixs_ref, ge_ref, be_ref, w_in_ref, w_pool_ref,
                 pscale_ref, bias_ref, meta_proj_ref, w_out_ref, g1_ref, b1_ref, g2_ref, b2_ref,
                 w1_hbm, w2_hbm,
                 y_ref, ys_ref, nk_ref, nv_ref, np_ref,
                 mixbuf, kvar, vvar, uext, qbuf, pbuf, hres, h1_s, hb_s, a_s, w1_ref, w2_ref, stage, sem):
    s = pl.program_id(0)
    is_first = s == 0
    first_tile = jnp.minimum(s, n_tiles - 1) % tiles_per_seq == 0
    tile, d_model = x_ref.shape
    ns = xs_ref.shape[0]
    half = tile // 2
    halves = (slice(0, half), slice(half, tile))
    pre_w = (w_out_ref, g1_ref, b1_ref, h1_s, hb_s)

    @pl.when(is_first)
    def _():
        def toks(rows):
            return range(rows.start // ns, rows.stop // ns)

        def sample_pre():
            for rows in halves:
                res = lambda: ALPHA * _layer_norm(jnp.concatenate([xs_ref[:, t, :] for t in toks(rows)], axis=0),
                                                  ge_ref[...], be_ref[...])
                mix = lambda: jnp.concatenate([mixs_ref[:, t * d_model:(t + 1) * d_model] for t in toks(rows)],
                                              axis=0)
                yield from _pre_pieces(rows, res, mix, *pre_w)

        cr, cc = stage.shape[1:]
        chunks = [(src, dst, r, c) for src, dst in ((w1_hbm, w1_ref), (w2_hbm, w2_ref))
                  for r in range(0, src.shape[0], cr) for c in range(0, src.shape[1], cc)]
        ring = [stage.at[i] for i in range(stage.shape[0])]
        ring += [buf.at[pl.ds(r, cr)] for buf in (hres, y_ref) for r in range(0, tile, cr)]
        slots = len(ring)

        def copy(i):
            src, _, r, c = chunks[i]
            return pltpu.make_async_copy(src.at[pl.ds(r, cr), pl.ds(c, cc)], ring[i % slots], sem.at[i % slots])

        for i in range(slots):
            copy(i).start()
        pre = sample_pre()
        every = len(chunks) // 4
        for i, (_, dst, r, c) in enumerate(chunks):
            if i % every == every // 2:
                next(pre, None)
            copy(i).wait()
            dst[r:r + cr, c:c + cc] = ring[i % slots][...].astype(BF16)
            if i + slots < len(chunks):
                copy(i + slots).start()
        for _ in pre:
            raise AssertionError("piece left unscheduled")

    @pl.when(first_tile)
    def _():
        _carry_init(meta_proj_ref, kvar, vvar, uext)

    mlp = [_mlp_pieces(rows, a_s, h1_s, hb_s, w1_ref, w2_ref, g2_ref, b2_ref, y_ref) for rows in halves]
    front = _front_pieces(first_tile, x_ref, sink_ref, ge_ref, be_ref, w_in_ref, w_pool_ref, pscale_ref,
                          bias_ref, mixbuf, nk_ref.at[0], nv_ref.at[0], np_ref.at[0], kvar, vvar, uext, qbuf, pbuf,
                          hres)
    pre = [_pre_pieces(rows, lambda rows=rows: hres[rows, :], lambda rows=rows: mixbuf[rows, :], *pre_w)
           for rows in halves]
    a, b, f, pa, pb = mlp[0], mlp[1], front, pre[0], pre[1]
    n_up = w1_ref.shape[1] // UP_CHUNK
    n_down = d_model // DOWN_CHUNK
    order = (
        _weave(a, n_up, [f] * 4)
        + _weave(a, n_down, [f] * 5)
        + _weave(b, n_up, [a] + [f] * 5)
        + _weave(b, n_down, [f, pa, pb, pa, pb]) + [b])
    for g in order:
        next(g, None)
    for g in (a, b, f, pa, pb):
        for _ in g:
            raise AssertionError("piece left unscheduled")

    @pl.when(is_first)
    def _():
        for t in range(tokens):
            ys_ref[:, t, :] = y_ref[t * ns:(t + 1) * ns, :]


def _const_spec(shape):
    return pl.BlockSpec(shape, lambda *_: (0,) * len(shape), pipeline_mode=pl.Buffered(1))


def _smem_spec():
    return pl.BlockSpec(memory_space=pltpu.SMEM)


def _sample_front_call(rel_table, sink, xs, ck, cv, st, w_out, meta_tokens, ge, be, w_in, w_pool, pscale):
    n_seq, tokens, d_model = xs.shape
    steps = n_seq // SAMPLE_SEQS
    srows = N_HEADS * tokens * SEQ_GROUP
    tables = [jnp.asarray(t) for t in _bucket_tables(tokens)]
    consts = tables + [meta_tokens, ge, be, w_in, w_pool, pscale]
    whole = lambda shape: pl.BlockSpec(shape, lambda i: (0,) * len(shape))
    seq_spec = lambda *tail: pl.BlockSpec((SAMPLE_SEQS,) + tail, lambda i: (i,) + (0,) * len(tail))
    state_spec = pl.BlockSpec((POOL_BUF, SAMPLE_SEQS, D_POOL), lambda i: (0, i, 0))
    slab_spec = pl.BlockSpec((w_out.shape[0] // steps, w_out.shape[1]), lambda i: (i, 0))
    once = [jax.ShapeDtypeStruct((2, N_HEADS, BLOCK, 2 * BLOCK), F32),
            jax.ShapeDtypeStruct((meta_tokens.shape[0], w_in.shape[1]), F32),
            jax.ShapeDtypeStruct(w_in.shape, BF16),
            jax.ShapeDtypeStruct((D_POOL, D_POOL), BF16)]
    return pl.pallas_call(
        _sample_front_kernel,
        grid=(steps,),
        in_specs=[_smem_spec(), _smem_spec(), seq_spec(tokens, d_model), seq_spec(D_KV, WINDOW),
                  seq_spec(D_KV, WINDOW), state_spec, slab_spec] + [_const_spec(c.shape) for c in consts],
        out_specs=[seq_spec(tokens * d_model), seq_spec(D_KV, WINDOW), seq_spec(D_KV, WINDOW), state_spec,
                   slab_spec] + [whole(o.shape) for o in once],
        out_shape=[jax.ShapeDtypeStruct((n_seq, tokens * d_model), BF16),
                   jax.ShapeDtypeStruct((n_seq, D_KV, WINDOW), F32),
                   jax.ShapeDtypeStruct((n_seq, D_KV, WINDOW), F32),
                   jax.ShapeDtypeStruct((POOL_BUF, n_seq, D_POOL), F32),
                   jax.ShapeDtypeStruct(w_out.shape, BF16)] + once,
        scratch_shapes=[pltpu.VMEM((tokens * SAMPLE_SEQS, D_ATTN), F32),
                        pltpu.VMEM((srows, WINDOW), F32),
                        pltpu.VMEM((srows, LANES), F32),
                        pltpu.VMEM(w_pool.shape, BF16)],
        compiler_params=pltpu.CompilerParams(dimension_semantics=("arbitrary",),
                                             vmem_limit_bytes=VMEM_LIMIT_BYTES),
        name="sample_front",
    )(rel_table, sink, xs, ck, cv, st, w_out, *consts)


def kernel(x_prompt, x_sample, cache_win_k, cache_win_v, state_pool, meta_tokens, ln_emb_g, ln_emb_b,
           rel_table, w_in, w_pool, pool_scale, sinks, w_out, ln1_g, ln1_b, w_mlp_in, w_mlp_out,
           ln2_g, ln2_b):
    batch, seq, d_model = x_prompt.shape
    n_seq, tokens, _ = x_sample.shape
    assert w_in.shape[0] == DEPTH and d_model == D_POOL + D_ATTN
    assert seq % PROMPT_TILE == 0 and n_seq % SAMPLE_SEQS == 0 and tokens * n_seq == PROMPT_TILE
    assert cache_win_k.shape[2] == WINDOW and state_pool.shape[2] == POOL_BUF and tokens <= POOL_BUF
    assert N_META == HALO and POOL_BUF <= HALO

    row = lambda a: a.reshape(1, -1).astype(F32)
    ge, be = row(ln_emb_g), row(ln_emb_b)
    pscale = row(pool_scale[0])
    sink = row(sinks[0])

    to_kd_pos = lambda c: jnp.swapaxes(c[0].reshape(n_seq, WINDOW, D_KV), 1, 2)
    mix_s, nk_s, nv_s, np_s, w_out_b, bias_p, meta_proj, w_in_b, w_pool_bd = _sample_front_call(
        rel_table.astype(F32).T, sink, x_sample, to_kd_pos(cache_win_k), to_kd_pos(cache_win_v),
        jnp.swapaxes(state_pool[0], 0, 1), w_out[0], meta_tokens.astype(F32), ge, be, w_in[0], w_pool[0], pscale)
    xs = x_sample
    w1, w2 = w_mlp_in[0], w_mlp_out[0]

    tile = PROMPT_TILE
    tiles_per_seq = seq // tile
    n_tiles = batch * tiles_per_seq
    xp = x_prompt.reshape(batch * seq, d_model)
    front_tile = lambda s: jnp.minimum(s, n_tiles - 1)
    finish_tile = lambda s: jnp.maximum(s - 1, 0)
    seq_of = lambda s: front_tile(s) // tiles_per_seq
    consts = (ge, be, w_in_b, w_pool_bd, pscale, bias_p, meta_proj, w_out_b, row(ln1_g[0]), row(ln1_b[0]),
              row(ln2_g[0]), row(ln2_b[0]))
    y_p, y_s, nk_p, nv_p, np_p = pl.pallas_call(
        functools.partial(_main_kernel, tokens, tiles_per_seq, n_tiles),
        grid=(n_tiles + 1,),
        in_specs=[_smem_spec(),
                  pl.BlockSpec((tile, d_model), lambda s: (front_tile(s), 0)),
                  _const_spec(xs.shape), _const_spec(mix_s.shape)] + [_const_spec(c.shape) for c in consts]
        + [pl.BlockSpec(memory_space=pl.ANY)] * 2,
        out_specs=[pl.BlockSpec((tile, d_model), lambda s: (finish_tile(s), 0)),
                   pl.BlockSpec(xs.shape, lambda s: (0, 0, 0)),
                   pl.BlockSpec((1, BLOCK, D_KV), lambda s: (seq_of(s), 0, 0)),
                   pl.BlockSpec((1, BLOCK, D_KV), lambda s: (seq_of(s), 0, 0)),
                   pl.BlockSpec((1, HALO, D_POOL), lambda s: (seq_of(s), 0, 0))],
        out_shape=[jax.ShapeDtypeStruct((batch * seq, d_model), F32),
                   jax.ShapeDtypeStruct(xs.shape, F32),
                   jax.ShapeDtypeStruct((batch, BLOCK, D_KV), F32),
                   jax.ShapeDtypeStruct((batch, BLOCK, D_KV), F32),
                   jax.ShapeDtypeStruct((batch, HALO, D_POOL), F32)],
        scratch_shapes=[pltpu.VMEM((tile, d_model), BF16),
                        pltpu.VMEM((4, tile + BLOCK, LANES), BF16),
                        pltpu.VMEM((4, tile + BLOCK, LANES), BF16),
                        pltpu.VMEM((tile + HALO, D_POOL), F32),
                        pltpu.VMEM((tile, D_ATTN), BF16),
                        pltpu.VMEM((tile, D_POOL), BF16),
                        pltpu.VMEM((tile, d_model), F32),
                        pltpu.VMEM((tile, d_model), F32),
                        pltpu.VMEM((tile, d_model), BF16),
                        pltpu.VMEM((tile // 2, w1.shape[1]), BF16),
                        pltpu.VMEM(w1.shape, BF16),
                        pltpu.VMEM(w2.shape, BF16),
                        pltpu.VMEM((WEIGHT_SLOTS,) + WEIGHT_CHUNK, F32),
                        pltpu.SemaphoreType.DMA((WEIGHT_SLOTS + 2 * (tile // WEIGHT_CHUNK[0]),))],
        compiler_params=pltpu.CompilerParams(dimension_semantics=("arbitrary",),
                                             vmem_limit_bytes=VMEM_LIMIT_BYTES),
        name="main",
    )(sink, xp, xs, mix_s, *consts, w1, w2)

    kv_shape = (DEPTH, -1, WINDOW, N_KV_HEADS, HEAD_DIM)
    from_kd_pos = lambda c: jnp.swapaxes(c, 1, 2).reshape(kv_shape)
    return (y_p.reshape(batch, seq, d_model), y_s,
            from_kd_pos(nk_p), from_kd_pos(nv_p),
            np_p[:, HALO - POOL_BUF:, :].reshape(DEPTH, batch, POOL_BUF, D_POOL),
            from_kd_pos(nk_s), from_kd_pos(nv_s), jnp.swapaxes(np_s, 0, 1)[None])
```

```python
import functools
import math

import jax
import jax.numpy as jnp
import numpy as np
from jax import lax
from jax.experimental import pallas as pl
from jax.experimental.pallas import tpu as pltpu

N_META = 16
POOL_WINDOWS = (2, 4, 8, 16)
POOL_GROUP_DIM = 128
D_POOL = len(POOL_WINDOWS) * POOL_GROUP_DIM
POOL_BUF = max(POOL_WINDOWS) - 1
N_HEADS = 8
HEAD_DIM = 64
D_ATTN = N_HEADS * HEAD_DIM
N_KV_HEADS = 2
D_KV = N_KV_HEADS * HEAD_DIM
WINDOW = 128
BLOCK = 128
REL_BUCKETS = 32
REL_MAX_DIST = 128
PAST_LEN = 8192
DEPTH = 1
ALPHA = (2.0 * DEPTH) ** 0.25
LN_EPS = 1e-5
Q_SCALE = HEAD_DIM ** -0.5

LANES = 128
SUBLANES = 8
VMEM_LIMIT_BYTES = 60 * 1024 * 1024

PROMPT_TILE = 512
IN_ROWS = 256
UP_CHUNK = 512
DOWN_CHUNK = 256
SAMPLE_SEQS = 32
SEQ_GROUP = SUBLANES
HALO = 2 * SUBLANES
WEIGHT_CHUNK = (256, 1024)
WEIGHT_SLOTS = 2

F32 = jnp.float32
BF16 = jnp.bfloat16
NEG_INF = float("-inf")


def _rel_bucket(dist):
    n = np.maximum(dist, 0)
    max_exact = REL_BUCKETS // 2
    nf = np.maximum(n, 1).astype(np.float64)
    large = max_exact + (np.log(nf / max_exact) / math.log(REL_MAX_DIST / max_exact)
                         * (REL_BUCKETS - max_exact)).astype(np.int32)
    large = np.minimum(large, REL_BUCKETS - 1)
    return np.where(n < max_exact, n, large).astype(np.int32)


def _bucket_tables(tokens):
    prompt = np.tile(_rel_bucket((BLOCK - np.arange(BLOCK)) % BLOCK)[None, :], (SUBLANES, 1)).astype(np.int32)
    t = np.repeat(np.arange(tokens), SEQ_GROUP)
    s = np.tile(np.arange(SEQ_GROUP), tokens)
    dist = (t[:, None] + WINDOW) - np.arange(WINDOW)[None, :]
    ok = (dist >= 0) & (dist < WINDOW)
    cache = np.where(ok, _rel_bucket(dist), -1).astype(np.int32)
    dist = t[:, None] - t[None, :]
    ok = (dist >= 0) & (s[:, None] == s[None, :])
    new = np.full((tokens * SEQ_GROUP, LANES), -1, np.int32)
    new[:, :tokens * SEQ_GROUP] = np.where(ok, _rel_bucket(dist), -1)
    return prompt, cache, new


def _layer_norm(x, g, b):
    mu = jnp.mean(x, axis=-1, keepdims=True)
    xc = x - mu
    var = jnp.mean(xc * xc, axis=-1, keepdims=True)
    return xc * lax.rsqrt(var + LN_EPS) * g + b


def _dot(a, b):
    return jnp.dot(a, b, preferred_element_type=F32)


def _dot_nt(a, b):
    return lax.dot_general(a, b, (((1,), (1,)), ((), ())), preferred_element_type=F32)


def _kv_lane_variants(x, low):
    xr = pltpu.roll(x, HEAD_DIM, axis=1)
    zero = jnp.zeros_like(x)
    return (jnp.where(low, x, zero).astype(BF16), jnp.where(low, zero, xr).astype(BF16),
            jnp.where(low, xr, zero).astype(BF16), jnp.where(low, zero, x).astype(BF16))


def _setup_tables(tab_ref, bkt_p_ref, bkt_c_ref, bkt_n_ref, meta_ref, g_ref, b_ref, wf_in_copy, wf_in_ref, wf_pool_ref,
                  bias_p_ref, bias_c_ref, bias_n_ref, meta_proj_ref, w_in_ref, w_pool_ref, w_pool_bd_ref):
    wf_in_copy.start()
    w_pool_ref[...] = wf_pool_ref[...].astype(BF16)
    w_pool_bd_ref[...] = jnp.zeros_like(w_pool_bd_ref)
    for g in range(len(POOL_WINDOWS)):
        cols = slice(g * POOL_GROUP_DIM, (g + 1) * POOL_GROUP_DIM)
        w_pool_bd_ref[cols, cols] = w_pool_ref[g]
    def lookup(bucket):
        accs = [jnp.full(bucket.shape, NEG_INF, F32)] * N_HEADS
        for i in range(REL_BUCKETS):
            hit = bucket == i
            accs = [jnp.where(hit, tab_ref[h, i], acc) for h, acc in enumerate(accs)]
        return accs

    rows = bkt_c_ref.shape[0]
    q = lax.broadcasted_iota(jnp.int32, (BLOCK, BLOCK), 0)
    k = lax.broadcasted_iota(jnp.int32, (BLOCK, BLOCK), 1)
    per_dist, cache_bias, new_bias = lookup(bkt_p_ref[...]), lookup(bkt_c_ref[...]), lookup(bkt_n_ref[...])
    for h in range(N_HEADS):
        toeplitz = pltpu.roll(jnp.concatenate([per_dist[h]] * (BLOCK // SUBLANES), axis=0), 0, 1,
                              stride=1, stride_axis=0)
        cur = jnp.where(k <= q, toeplitz, NEG_INF)
        for table, prev_ok in enumerate(((k > q) & (k >= BLOCK - N_META), k > q)):
            bias_p_ref[table, h, :, 0:BLOCK] = jnp.where(prev_ok, toeplitz, NEG_INF)
            bias_p_ref[table, h, :, BLOCK:] = cur
        bias_c_ref[h * rows:(h + 1) * rows, :] = cache_bias[h]
        bias_n_ref[h * rows:(h + 1) * rows, :] = new_bias[h]
    wf_in_copy.wait()
    w_in_ref[...] = wf_in_ref[...].astype(BF16)
    hm = _layer_norm(meta_ref[...], g_ref[...], b_ref[...])
    meta_proj_ref[...] = _dot(hm.astype(BF16), w_in_ref[...])


def _carry_init(meta_proj_ref, kvar, vvar, uext):
    low = lax.broadcasted_iota(jnp.int32, (1, LANES), 1) < HEAD_DIM
    pad = jnp.zeros((BLOCK - N_META, LANES), BF16)
    kq = _kv_lane_variants(meta_proj_ref[:, D_POOL + D_ATTN:D_POOL + D_ATTN + D_KV], low)
    vq = _kv_lane_variants(meta_proj_ref[:, D_POOL + D_ATTN + D_KV:], low)
    for i in range(4):
        kvar[i, 0:BLOCK, :] = jnp.concatenate([pad, kq[i]], axis=0)
        vvar[i, 0:BLOCK, :] = jnp.concatenate([pad, vq[i]], axis=0)
    uext[0:HALO, :] = meta_proj_ref[:, 0:D_POOL]


def _front_pieces(first_tile, x_ref, sink_ref, g_ref, b_ref, w_in_ref, w_pool_ref, pscale_ref, bias_ref,
                  mix_ref, nk_ref, nv_ref, np_ref, kvar, vvar, uext, qbuf, pbuf, hres):
    tile = x_ref.shape[0]
    halo = HALO
    low = lax.broadcasted_iota(jnp.int32, (1, LANES), 1) < HEAD_DIM

    for j in range(tile // IN_ROWS):
        rows = slice(j * IN_ROWS, (j + 1) * IN_ROWS)
        h = _layer_norm(x_ref[rows, :], g_ref[...], b_ref[...])
        hres[rows, :] = ALPHA * h
        proj = _dot(h.astype(BF16), w_in_ref[...])
        u = proj[:, 0:D_POOL]
        k = proj[:, D_POOL + D_ATTN:D_POOL + D_ATTN + D_KV]
        v = proj[:, D_POOL + D_ATTN + D_KV:]
        qbuf[rows, :] = (proj[:, D_POOL:D_POOL + D_ATTN] * Q_SCALE).astype(BF16)
        uext[halo + rows.start:halo + rows.stop, :] = u
        kq = _kv_lane_variants(k, low)
        vq = _kv_lane_variants(v, low)
        for i in range(4):
            kvar[i, BLOCK + rows.start:BLOCK + rows.stop, :] = kq[i]
            vvar[i, BLOCK + rows.start:BLOCK + rows.stop, :] = vq[i]
        if rows.stop == tile:
            nk_ref[...] = k[IN_ROWS - BLOCK:, :].T
            nv_ref[...] = v[IN_ROWS - BLOCK:, :].T
            np_ref[...] = u[IN_ROWS - halo:, :]
        yield

    for g, w in enumerate(POOL_WINDOWS):
        cols = slice(g * POOL_GROUP_DIM, (g + 1) * POOL_GROUP_DIM)
        ug = uext[halo:halo + tile, cols]
        s = ug
        for i in range(1, w):
            s = s + uext[halo - i:halo - i + tile, cols]
        pbuf[:, cols] = (s * (1.0 / w) - ug).astype(BF16)
        if g == len(POOL_WINDOWS) - 1:
            mix_ref[:, 0:D_POOL] = (_dot(pbuf[...], w_pool_ref[...]) * pscale_ref[...]).astype(BF16)
        yield

    for j in range(tile // BLOCK):
        rows = slice(j * BLOCK, (j + 1) * BLOCK)
        keys = slice(j * BLOCK, (j + 2) * BLOCK)
        sel = jnp.where(first_tile, 0, 1) if j == 0 else 1
        for kv in range(N_KV_HEADS):
            tiles = (2 * kv, 2 * kv + 1)
            lhs = jnp.concatenate([qbuf[rows, p * LANES:(p + 1) * LANES] for p in tiles], axis=0)
            nk = 2 * BLOCK
            s_all = _dot_nt(lhs, jnp.concatenate([kvar[2 * kv, keys, :], kvar[2 * kv + 1, keys, :]], axis=0))
            es, dens = [], []
            for half in range(2):
                heads = [2 * p + half for p in tiles]
                s = s_all[:, half * nk:(half + 1) * nk]
                s = s + jnp.concatenate([bias_ref[sel, hd] for hd in heads], axis=0)
                sink = jnp.concatenate([jnp.full((BLOCK, 1), sink_ref[0, hd], F32) for hd in heads], axis=0)
                m = jnp.maximum(jnp.max(s, axis=-1, keepdims=True), sink)
                e = jnp.exp(s - m)
                dens.append(jnp.sum(e, axis=-1, keepdims=True) + jnp.exp(sink - m))
                es.append(e.astype(BF16))
            o = _dot(jnp.concatenate(es, axis=0),
                     jnp.concatenate([vvar[2 * kv, keys, :], vvar[2 * kv + 1, keys, :]], axis=1))
            acc = (o[0:nk, 0:LANES] * (1.0 / dens[0]) + o[nk:, LANES:] * (1.0 / dens[1]))
            for n, p in enumerate(tiles):
                mix_ref[rows, D_POOL + p * LANES:D_POOL + (p + 1) * LANES] = (
                    acc[n * BLOCK:(n + 1) * BLOCK].astype(BF16))
            yield

    for i in range(4):
        kvar[i, 0:BLOCK, :] = kvar[i, tile:tile + BLOCK, :]
        vvar[i, 0:BLOCK, :] = vvar[i, tile:tile + BLOCK, :]
    uext[0:halo, :] = uext[tile:tile + halo, :]


def _sample_front_kernel(tab_ref, sink_ref, x_ref, ck_ref, cv_ref, st_ref, wf_out_ref,
                         bkt_p_ref, bkt_c_ref, bkt_n_ref, meta_ref, g_ref, b_ref, wf_in_hbm, wf_pool_ref, pscale_ref,
                         mix_ref, nk_ref, nv_ref, np_ref, wb_out_ref,
                         bias_p_hbm, meta_proj_hbm, w_in_hbm, w_pool_bd_hbm,
                         o_s, bias_c_ref, bias_n_ref, w_pool_ref,
                         bias_p_ref, meta_proj_ref, w_in_ref, w_pool_bd_ref, wf_in_ref, sem):
    once = ((bias_p_ref, bias_p_hbm), (meta_proj_ref, meta_proj_hbm), (w_in_ref, w_in_hbm),
            (w_pool_bd_ref, w_pool_bd_hbm))

    def once_copy(i):
        return pltpu.make_async_copy(once[i][0], once[i][1], sem.at[1 + i])

    @pl.when(pl.program_id(0) == 0)
    def _():
        _setup_tables(tab_ref, bkt_p_ref, bkt_c_ref, bkt_n_ref, meta_ref, g_ref, b_ref,
                      pltpu.make_async_copy(wf_in_hbm, wf_in_ref, sem.at[0]), wf_in_ref, wf_pool_ref,
                      bias_p_ref, bias_c_ref, bias_n_ref, meta_proj_ref, w_in_ref, w_pool_ref, w_pool_bd_ref)
        for i in range(len(once)):
            once_copy(i).start()

    wb_out_ref[...] = wf_out_ref[...].astype(BF16)
    ns, tokens, d_model = x_ref.shape
    lane = lax.broadcasted_iota(jnp.int32, (1, LANES), 1)
    low = lane < HEAD_DIM

    x = jnp.concatenate([x_ref[:, t, :] for t in range(tokens)], axis=0)
    h = _layer_norm(x, g_ref[...], b_ref[...])
    proj = _dot(h.astype(BF16), w_in_ref[...])
    u = proj[:, 0:D_POOL]
    q = proj[:, D_POOL:D_POOL + D_ATTN] * Q_SCALE
    k = proj[:, D_POOL + D_ATTN:D_POOL + D_ATTN + D_KV]
    v = proj[:, D_POOL + D_ATTN + D_KV:]

    ext = [st_ref[r] for r in range(POOL_BUF)]
    ext += [u[t * ns:(t + 1) * ns] for t in range(tokens)]
    ps = []
    for t in range(tokens):
        idx = POOL_BUF + t
        pos = PAST_LEN - POOL_BUF + idx
        parts = []
        for g, w in enumerate(POOL_WINDOWS):
            cols = slice(g * POOL_GROUP_DIM, (g + 1) * POOL_GROUP_DIM)
            lo = max(idx + 1 - w, 0)
            s = ext[idx][:, cols]
            for r in range(idx - 1, lo - 1, -1):
                s = s + ext[r][:, cols]
            parts.append(s * (1.0 / min(w, pos + 1)) - ext[idx][:, cols])
        ps.append(jnp.concatenate(parts, axis=1))
    p = jnp.concatenate(ps, axis=0).astype(BF16)
    zs = [_dot(p[:, g * POOL_GROUP_DIM:(g + 1) * POOL_GROUP_DIM], w_pool_ref[g])
          for g in range(len(POOL_WINDOWS))]
    z = (jnp.concatenate(zs, axis=1) * pscale_ref[...]).astype(BF16)
    for r in range(POOL_BUF):
        np_ref[r] = ext[r + tokens]

    qh = []
    for hd in range(N_HEADS):
        tl = q[:, (hd // 2) * LANES:(hd // 2 + 1) * LANES]
        kv = hd // (N_HEADS // N_KV_HEADS)
        want_low = kv == 0
        is_low = hd % 2 == 0
        src = tl if want_low == is_low else pltpu.roll(tl, HEAD_DIM, axis=1)
        qh.append(jnp.where(low, src, 0.0) if want_low else jnp.where(low, 0.0, src))

    gr = SEQ_GROUP
    rows_per_head = tokens * gr
    sub = lax.broadcasted_iota(jnp.int32, (N_HEADS * rows_per_head, 1), 0) % gr
    sink = jnp.concatenate([jnp.full((rows_per_head, 1), sink_ref[0, hd], F32) for hd in range(N_HEADS)],
                           axis=0)
    zpad = jnp.zeros((LANES - rows_per_head, LANES), F32)
    ztop = jnp.zeros((WINDOW - SUBLANES, LANES), F32)
    sub8 = lax.broadcasted_iota(jnp.int32, (SUBLANES, 1), 0)
    keep = WINDOW - tokens

    def appended(old_ref, news, j):
        tail = jnp.zeros((SUBLANES, LANES), F32)
        for t in range(tokens):
            row = SUBLANES - tokens + t
            tail = jnp.where(sub8 == row, pltpu.roll(news[t], (row - j) % SUBLANES, axis=0), tail)
        cols = jnp.concatenate([ztop, tail], axis=0).T
        return jnp.where(lane >= keep, cols, pltpu.roll(old_ref[...], keep, axis=1))

    for gi in range(ns // gr):
        base = gi * gr
        lhs = jnp.concatenate([qh[hd][t * ns + base:t * ns + base + gr]
                               for hd in range(N_HEADS) for t in range(tokens)], axis=0).astype(BF16)
        k_t = [k[t * ns + base:t * ns + base + gr] for t in range(tokens)]
        v_t = [v[t * ns + base:t * ns + base + gr] for t in range(tokens)]
        sc = None
        for s in range(gr):
            s_one = _dot(lhs, ck_ref[base + s].astype(BF16))
            nk_ref[base + s] = appended(ck_ref.at[base + s], k_t, s)
            sc = s_one if sc is None else jnp.where(sub == s, s_one, sc)
        sc = sc + bias_c_ref[...]
        k_new = jnp.concatenate(k_t + [zpad], axis=0)
        v_new = jnp.concatenate(v_t + [zpad], axis=0)
        sn = _dot_nt(lhs, k_new.astype(BF16)) + bias_n_ref[...]
        m = jnp.maximum(jnp.max(jnp.maximum(sc, sn), axis=-1, keepdims=True), sink)
        ec = jnp.exp(sc - m)
        en = jnp.exp(sn - m)
        den = jnp.sum(ec + en, axis=-1, keepdims=True) + jnp.exp(sink - m)
        o = _dot(en.astype(BF16), v_new.astype(BF16))
        for s in range(gr):
            o = o + _dot_nt(jnp.where(sub == s, ec, 0.0).astype(BF16), cv_ref[base + s].astype(BF16))
            nv_ref[base + s] = appended(cv_ref.at[base + s], v_t, s)
        o = o * (1.0 / den)
        for t in range(tokens):
            for p in range(N_HEADS // 2):
                a = o[(2 * p * tokens + t) * gr:(2 * p * tokens + t + 1) * gr]
                c = o[((2 * p + 1) * tokens + t) * gr:((2 * p + 1) * tokens + t + 1) * gr]
                if p < N_HEADS // 4:
                    c = pltpu.roll(c, HEAD_DIM, axis=1)
                else:
                    a = pltpu.roll(a, HEAD_DIM, axis=1)
                o_s[t * ns + base:t * ns + base + gr, p * LANES:(p + 1) * LANES] = jnp.where(low, a, c)

    for t in range(tokens):
        mix_ref[:, t * d_model:t * d_model + D_POOL] = z[t * ns:(t + 1) * ns]
        mix_ref[:, t * d_model + D_POOL:(t + 1) * d_model] = o_s[t * ns:(t + 1) * ns, :].astype(BF16)

    @pl.when(pl.program_id(0) == pl.num_programs(0) - 1)
    def _():
        for i in range(len(once)):
            once_copy(i).wait()


def _pre_pieces(rows, res, mix, w_out_ref, g1_ref, b1_ref, h1_s, hb_s):
    h1_s[rows, :] = res() + _dot(mix(), w_out_ref[...])
    yield
    h = _layer_norm(h1_s[rows, :], g1_ref[...], b1_ref[...])
    h1_s[rows, :] = h
    hb_s[rows, :] = h.astype(BF16)
    yield


def _mlp_pieces(rows, a_s, h1_s, hb_s, w1_ref, w2_ref, g2_ref, b2_ref, y_ref):
    for c in range(w1_ref.shape[1] // UP_CHUNK):
        cols = slice(c * UP_CHUNK, (c + 1) * UP_CHUNK)
        a = _dot(hb_s[rows, :], w1_ref[:, cols])
        a_s[:, cols] = jnp.square(jnp.maximum(a, 0.0)).astype(BF16)
        if c == 0:
            y_ref[rows, :] = ALPHA * h1_s[rows, :]
        yield
    for n in range(y_ref.shape[1] // DOWN_CHUNK):
        cols = slice(n * DOWN_CHUNK, (n + 1) * DOWN_CHUNK)
        y_ref[rows, cols] = y_ref[rows, cols] + _dot(a_s[...], w2_ref[:, cols])
        yield
    y_ref[rows, :] = _layer_norm(y_ref[rows, :], g2_ref[...], b2_ref[...])
    yield


def _weave(main, n_main, sides):
    order = []
    for i in range(n_main):
        order.append(main)
        order += sides[i * len(sides) // n_main:(i + 1) * len(sides) // n_main]
    return order


def _main_kernel(tokens, tiles_per_seq, n_tiles,
                 sink_ref, x_ref, xs_ref, mixs_ref, ge_ref, be_ref, w_in_ref, w_pool_ref,
                 pscale_ref, bias_ref, meta_proj_ref, w_out_ref, g1_ref, b1_ref, g2_ref, b2_ref,
                 w1_hbm, w2_hbm,
                 y_ref, ys_ref, nk_ref, nv_ref, np_ref,
                 mixbuf, kvar, vvar, uext, qbuf, pbuf, hres, h1_s, hb_s, a_s, w1_ref, w2_ref, stage, sem):
    s = pl.program_id(0)
    is_first = s == 0
    first_tile = jnp.minimum(s, n_tiles - 1) % tiles_per_seq == 0
    tile, d_model = x_ref.shape
    ns = xs_ref.shape[0]
    half = tile // 2
    halves = (slice(0, half), slice(half, tile))
    pre_w = (w_out_ref, g1_ref, b1_ref, h1_s, hb_s)

    @pl.when(is_first)
    def _():
        def toks(rows):
            return range(rows.start // ns, rows.stop // ns)

        def sample_pre():
            for rows in halves:
                res = lambda: ALPHA * _layer_norm(jnp.concatenate([xs_ref[:, t, :] for t in toks(rows)], axis=0),
                                                  ge_ref[...], be_ref[...])
                mix = lambda: jnp.concatenate([mixs_ref[:, t * d_model:(t + 1) * d_model] for t in toks(rows)],
                                              axis=0)
                yield from _pre_pieces(rows, res, mix, *pre_w)

        cr, cc = stage.shape[1:]
        chunks = [(src, dst, r, c) for src, dst in ((w1_hbm, w1_ref), (w2_hbm, w2_ref))
                  for r in range(0, src.shape[0], cr) for c in range(0, src.shape[1], cc)]
        ring = [stage.at[i] for i in range(stage.shape[0])]
        ring += [buf.at[pl.ds(r, cr)] for buf in (hres, y_ref) for r in range(0, tile, cr)]
        slots = len(ring)

        def copy(i):
            src, _, r, c = chunks[i]
            return pltpu.make_async_copy(src.at[pl.ds(r, cr), pl.ds(c, cc)], ring[i % slots], sem.at[i % slots])

        for i in range(slots):
            copy(i).start()
        pre = sample_pre()
        every = len(chunks) // 4
        for i, (_, dst, r, c) in enumerate(chunks):
            if i % every == every // 2:
                next(pre, None)
            copy(i).wait()
            dst[r:r + cr, c:c + cc] = ring[i % slots][...].astype(BF16)
            if i + slots < len(chunks):
                copy(i + slots).start()
        for _ in pre:
            raise AssertionError("piece left unscheduled")

    @pl.when(first_tile)
    def _():
        _carry_init(meta_proj_ref, kvar, vvar, uext)

    mlp = [_mlp_pieces(rows, a_s, h1_s, hb_s, w1_ref, w2_ref, g2_ref, b2_ref, y_ref) for rows in halves]
    front = _front_pieces(first_tile, x_ref, sink_ref, ge_ref, be_ref, w_in_ref, w_pool_ref, pscale_ref,
                          bias_ref, mixbuf, nk_ref.at[0], nv_ref.at[0], np_ref.at[0], kvar, vvar, uext, qbuf, pbuf,
                          hres)
    pre = [_pre_pieces(rows, lambda rows=rows: hres[rows, :], lambda rows=rows: mixbuf[rows, :], *pre_w)
           for rows in halves]
    a, b, f, pa, pb = mlp[0], mlp[1], front, pre[0], pre[1]
    n_up = w1_ref.shape[1] // UP_CHUNK
    n_down = d_model // DOWN_CHUNK
    order = (
        _weave(a, n_up, [f] * 4)
        + _weave(a, n_down, [f] * 5)
        + _weave(b, n_up, [a] + [f] * 5)
        + _weave(b, n_down, [f, pa, pb, pa, pb]) + [b])
    for g in order:
        next(g, None)
    for g in (a, b, f, pa, pb):
        for _ in g:
            raise AssertionError("piece left unscheduled")

    @pl.when(is_first)
    def _():
        for t in range(tokens):
            ys_ref[:, t, :] = y_ref[t * ns:(t + 1) * ns, :]


def _const_spec(shape):
    return pl.BlockSpec(shape, lambda *_: (0,) * len(shape), pipeline_mode=pl.Buffered(1))


def _smem_spec():
    return pl.BlockSpec(memory_space=pltpu.SMEM)


def _sample_front_call(rel_table, sink, xs, ck, cv, st, w_out, meta_tokens, ge, be, w_in, w_pool, pscale):
    n_seq, tokens, d_model = xs.shape
    steps = n_seq // SAMPLE_SEQS
    srows = N_HEADS * tokens * SEQ_GROUP
    tables = [jnp.asarray(t) for t in _bucket_tables(tokens)]
    consts = tables + [meta_tokens, ge, be, w_in, w_pool, pscale]
    seq_spec = lambda *tail: pl.BlockSpec((SAMPLE_SEQS,) + tail, lambda i: (i,) + (0,) * len(tail))
    state_spec = pl.BlockSpec((POOL_BUF, SAMPLE_SEQS, D_POOL), lambda i: (0, i, 0))
    slab_spec = pl.BlockSpec((w_out.shape[0] // steps, w_out.shape[1]), lambda i: (i, 0))
    once = [jax.ShapeDtypeStruct((2, N_HEADS, BLOCK, 2 * BLOCK), F32),
            jax.ShapeDtypeStruct((meta_tokens.shape[0], w_in.shape[1]), F32),
            jax.ShapeDtypeStruct(w_in.shape, BF16),
            jax.ShapeDtypeStruct((D_POOL, D_POOL), BF16)]
    return pl.pallas_call(
        _sample_front_kernel,
        grid=(steps,),
        in_specs=[_smem_spec(), _smem_spec(), seq_spec(tokens, d_model), seq_spec(D_KV, WINDOW),
                  seq_spec(D_KV, WINDOW), state_spec, slab_spec]
                 + [pl.BlockSpec(memory_space=pl.ANY) if c is w_in else _const_spec(c.shape) for c in consts],
        out_specs=[seq_spec(tokens * d_model), seq_spec(D_KV, WINDOW), seq_spec(D_KV, WINDOW), state_spec,
                   slab_spec] + [pl.BlockSpec(memory_space=pl.ANY) for _ in once],
        out_shape=[jax.ShapeDtypeStruct((n_seq, tokens * d_model), BF16),
                   jax.ShapeDtypeStruct((n_seq, D_KV, WINDOW), F32),
                   jax.ShapeDtypeStruct((n_seq, D_KV, WINDOW), F32),
                   jax.ShapeDtypeStruct((POOL_BUF, n_seq, D_POOL), F32),
                   jax.ShapeDtypeStruct(w_out.shape, BF16)] + once,
        scratch_shapes=[pltpu.VMEM((tokens * SAMPLE_SEQS, D_ATTN), F32),
                        pltpu.VMEM((srows, WINDOW), F32),
                        pltpu.VMEM((srows, LANES), F32),
                        pltpu.VMEM(w_pool.shape, BF16)]
                       + [pltpu.VMEM(o.shape, o.dtype) for o in once]
                       + [pltpu.VMEM(w_in.shape, F32), pltpu.SemaphoreType.DMA((1 + len(once),))],
        compiler_params=pltpu.CompilerParams(dimension_semantics=("arbitrary",),
                                             vmem_limit_bytes=VMEM_LIMIT_BYTES),
        name="sample_front",
    )(rel_table, sink, xs, ck, cv, st, w_out, *consts)


def kernel(x_prompt, x_sample, cache_win_k, cache_win_v, state_pool, meta_tokens, ln_emb_g, ln_emb_b,
           rel_table, w_in, w_pool, pool_scale, sinks, w_out, ln1_g, ln1_b, w_mlp_in, w_mlp_out,
           ln2_g, ln2_b):
    batch, seq, d_model = x_prompt.shape
    n_seq, tokens, _ = x_sample.shape
    assert w_in.shape[0] == DEPTH and d_model == D_POOL + D_ATTN
    assert seq % PROMPT_TILE == 0 and n_seq % SAMPLE_SEQS == 0 and tokens * n_seq == PROMPT_TILE
    assert cache_win_k.shape[2] == WINDOW and state_pool.shape[2] == POOL_BUF and tokens <= POOL_BUF
    assert N_META == HALO and POOL_BUF <= HALO

    row = lambda a: a.reshape(1, -1).astype(F32)
    ge, be = row(ln_emb_g), row(ln_emb_b)
    pscale = row(pool_scale[0])
    sink = row(sinks[0])

    to_kd_pos = lambda c: jnp.swapaxes(c[0].reshape(n_seq, WINDOW, D_KV), 1, 2)
    mix_s, nk_s, nv_s, np_s, w_out_b, bias_p, meta_proj, w_in_b, w_pool_bd = _sample_front_call(
        rel_table.astype(F32).T, sink, x_sample, to_kd_pos(cache_win_k), to_kd_pos(cache_win_v),
        jnp.swapaxes(state_pool[0], 0, 1), w_out[0], meta_tokens.astype(F32), ge, be, w_in[0], w_pool[0], pscale)
    xs = x_sample
    w1, w2 = w_mlp_in[0], w_mlp_out[0]

    tile = PROMPT_TILE
    tiles_per_seq = seq // tile
    n_tiles = batch * tiles_per_seq
    xp = x_prompt.reshape(batch * seq, d_model)
    front_tile = lambda s: jnp.minimum(s, n_tiles - 1)
    finish_tile = lambda s: jnp.maximum(s - 1, 0)
    seq_of = lambda s: front_tile(s) // tiles_per_seq
    consts = (ge, be, w_in_b, w_pool_bd, pscale, bias_p, meta_proj, w_out_b, row(ln1_g[0]), row(ln1_b[0]),
              row(ln2_g[0]), row(ln2_b[0]))
    y_p, y_s, nk_p, nv_p, np_p = pl.pallas_call(
        functools.partial(_main_kernel, tokens, tiles_per_seq, n_tiles),
        grid=(n_tiles + 1,),
        in_specs=[_smem_spec(),
                  pl.BlockSpec((tile, d_model), lambda s: (front_tile(s), 0)),
                  _const_spec(xs.shape), _const_spec(mix_s.shape)] + [_const_spec(c.shape) for c in consts]
        + [pl.BlockSpec(memory_space=pl.ANY)] * 2,
        out_specs=[pl.BlockSpec((tile, d_model), lambda s: (finish_tile(s), 0)),
                   pl.BlockSpec(xs.shape, lambda s: (0, 0, 0)),
                   pl.BlockSpec((1, BLOCK, D_KV), lambda s: (seq_of(s), 0, 0)),
                   pl.BlockSpec((1, BLOCK, D_KV), lambda s: (seq_of(s), 0, 0)),
                   pl.BlockSpec((1, HALO, D_POOL), lambda s: (seq_of(s), 0, 0))],
        out_shape=[jax.ShapeDtypeStruct((batch * seq, d_model), F32),
                   jax.ShapeDtypeStruct(xs.shape, F32),
                   jax.ShapeDtypeStruct((batch, BLOCK, D_KV), F32),
                   jax.ShapeDtypeStruct((batch, BLOCK, D_KV), F32),
                   jax.ShapeDtypeStruct((batch, HALO, D_POOL), F32)],
        scratch_shapes=[pltpu.VMEM((tile, d_model), BF16),
                        pltpu.VMEM((4, tile + BLOCK, LANES), BF16),
                        pltpu.VMEM((4, tile + BLOCK, LANES), BF16),
                        pltpu.VMEM((tile + HALO, D_POOL), F32),
                        pltpu.VMEM((tile, D_ATTN), BF16),
                        pltpu.VMEM((tile, D_POOL), BF16),
                        pltpu.VMEM((tile, d_model), F32),
                        pltpu.VMEM((tile, d_model), F32),
                        pltpu.VMEM((tile, d_model), BF16),
                        pltpu.VMEM((tile // 2, w1.shape[1]), BF16),
                        pltpu.VMEM(w1.shape, BF16),
                        pltpu.VMEM(w2.shape, BF16),
                        pltpu.VMEM((WEIGHT_SLOTS,) + WEIGHT_CHUNK, F32),
                        pltpu.SemaphoreType.DMA((WEIGHT_SLOTS + 2 * (tile // WEIGHT_CHUNK[0]),))],
        compiler_params=pltpu.CompilerParams(dimension_semantics=("arbitrary",),
                                             vmem_limit_bytes=VMEM_LIMIT_BYTES),
        name="main",
    )(sink, xp, xs, mix_s, *consts, w1, w2)

    kv_shape = (DEPTH, -1, WINDOW, N_KV_HEADS, HEAD_DIM)
    from_kd_pos = lambda c: jnp.swapaxes(c, 1, 2).reshape(kv_shape)
    return (y_p.reshape(batch, seq, d_model), y_s,
            from_kd_pos(nk_p), from_kd_pos(nv_p),
            np_p[:, HALO - POOL_BUF:, :].reshape(DEPTH, batch, POOL_BUF, D_POOL),
            from_kd_pos(nk_s), from_kd_pos(nv_s), jnp.swapaxes(np_s, 0, 1)[None])
```

```python
import functools
import math

import jax
import jax.numpy as jnp
import numpy as np
from jax import lax
from jax.experimental import pallas as pl
from jax.experimental.pallas import tpu as pltpu

N_META = 16
POOL_WINDOWS = (2, 4, 8, 16)
POOL_GROUP_DIM = 128
D_POOL = len(POOL_WINDOWS) * POOL_GROUP_DIM
POOL_BUF = max(POOL_WINDOWS) - 1
N_HEADS = 8
HEAD_DIM = 64
D_ATTN = N_HEADS * HEAD_DIM
N_KV_HEADS = 2
D_KV = N_KV_HEADS * HEAD_DIM
WINDOW = 128
BLOCK = 128
REL_BUCKETS = 32
REL_MAX_DIST = 128
PAST_LEN = 8192
DEPTH = 1
ALPHA = (2.0 * DEPTH) ** 0.25
LN_EPS = 1e-5
Q_SCALE = HEAD_DIM ** -0.5

LANES = 128
SUBLANES = 8
VMEM_LIMIT_BYTES = 60 * 1024 * 1024

PROMPT_TILE = 512
IN_ROWS = 256
UP_CHUNK = 512
DOWN_CHUNK = 256
SAMPLE_SEQS = 32
SEQ_GROUP = SUBLANES
HALO = 2 * SUBLANES
WEIGHT_CHUNK = (256, 1024)
WEIGHT_SLOTS = 2

F32 = jnp.float32
BF16 = jnp.bfloat16
NEG_INF = float("-inf")


def _rel_bucket(dist):
    n = np.maximum(dist, 0)
    max_exact = REL_BUCKETS // 2
    nf = np.maximum(n, 1).astype(np.float64)
    large = max_exact + (np.log(nf / max_exact) / math.log(REL_MAX_DIST / max_exact)
                         * (REL_BUCKETS - max_exact)).astype(np.int32)
    large = np.minimum(large, REL_BUCKETS - 1)
    return np.where(n < max_exact, n, large).astype(np.int32)


def _bucket_tables(tokens):
    prompt = np.tile(_rel_bucket((BLOCK - np.arange(BLOCK)) % BLOCK)[None, :], (SUBLANES, 1)).astype(np.int32)
    t = np.repeat(np.arange(tokens), SEQ_GROUP)
    s = np.tile(np.arange(SEQ_GROUP), tokens)
    dist = (t[:, None] + WINDOW) - np.arange(WINDOW)[None, :]
    ok = (dist >= 0) & (dist < WINDOW)
    cache = np.where(ok, _rel_bucket(dist), -1).astype(np.int32)
    dist = t[:, None] - t[None, :]
    ok = (dist >= 0) & (s[:, None] == s[None, :])
    new = np.full((tokens * SEQ_GROUP, LANES), -1, np.int32)
    new[:, :tokens * SEQ_GROUP] = np.where(ok, _rel_bucket(dist), -1)
    return prompt, cache, new


def _layer_norm(x, g, b):
    mu = jnp.mean(x, axis=-1, keepdims=True)
    xc = x - mu
    var = jnp.mean(xc * xc, axis=-1, keepdims=True)
    return xc * lax.rsqrt(var + LN_EPS) * g + b


def _dot(a, b):
    return jnp.dot(a, b, preferred_element_type=F32)


def _dot_nt(a, b):
    return lax.dot_general(a, b, (((1,), (1,)), ((), ())), preferred_element_type=F32)


def _kv_lane_variants(x, low):
    xr = pltpu.roll(x, HEAD_DIM, axis=1)
    zero = jnp.zeros_like(x)
    return (jnp.where(low, x, zero).astype(BF16), jnp.where(low, zero, xr).astype(BF16),
            jnp.where(low, xr, zero).astype(BF16), jnp.where(low, zero, x).astype(BF16))


def _setup_tables(tab_ref, bkt_p_ref, bkt_c_ref, bkt_n_ref, meta_ref, g_ref, b_ref, wf_in_ref, wf_pool_ref,
                  bias_p_ref, bias_c_ref, bias_n_ref, meta_proj_ref, w_in_ref, w_pool_ref, w_pool_bd_ref):
    w_in_ref[...] = wf_in_ref[...].astype(BF16)
    w_pool_ref[...] = wf_pool_ref[...].astype(BF16)
    w_pool_bd_ref[...] = jnp.zeros_like(w_pool_bd_ref)
    for g in range(len(POOL_WINDOWS)):
        cols = slice(g * POOL_GROUP_DIM, (g + 1) * POOL_GROUP_DIM)
        w_pool_bd_ref[cols, cols] = w_pool_ref[g]
    def lookup(bucket):
        accs = [jnp.full(bucket.shape, NEG_INF, F32)] * N_HEADS
        for i in range(REL_BUCKETS):
            hit = bucket == i
            accs = [jnp.where(hit, tab_ref[h, i], acc) for h, acc in enumerate(accs)]
        return accs

    rows = bkt_c_ref.shape[0]
    q = lax.broadcasted_iota(jnp.int32, (BLOCK, BLOCK), 0)
    k = lax.broadcasted_iota(jnp.int32, (BLOCK, BLOCK), 1)
    per_dist, cache_bias, new_bias = lookup(bkt_p_ref[...]), lookup(bkt_c_ref[...]), lookup(bkt_n_ref[...])
    for h in range(N_HEADS):
        toeplitz = pltpu.roll(jnp.concatenate([per_dist[h]] * (BLOCK // SUBLANES), axis=0), 0, 1,
                              stride=1, stride_axis=0)
        cur = jnp.where(k <= q, toeplitz, NEG_INF)
        for table, prev_ok in enumerate(((k > q) & (k >= BLOCK - N_META), k > q)):
            bias_p_ref[table, h, :, 0:BLOCK] = jnp.where(prev_ok, toeplitz, NEG_INF)
            bias_p_ref[table, h, :, BLOCK:] = cur
        bias_c_ref[h * rows:(h + 1) * rows, :] = cache_bias[h]
        bias_n_ref[h * rows:(h + 1) * rows, :] = new_bias[h]
    hm = _layer_norm(meta_ref[...], g_ref[...], b_ref[...])
    meta_proj_ref[...] = _dot(hm.astype(BF16), w_in_ref[...])


def _carry_init(meta_proj_ref, kvar, vvar, uext):
    low = lax.broadcasted_iota(jnp.int32, (1, LANES), 1) < HEAD_DIM
    pad = jnp.zeros((BLOCK - N_META, LANES), BF16)
    kq = _kv_lane_variants(meta_proj_ref[:, D_POOL + D_ATTN:D_POOL + D_ATTN + D_KV], low)
    vq = _kv_lane_variants(meta_proj_ref[:, D_POOL + D_ATTN + D_KV:], low)
    for i in range(4):
        kvar[i, 0:BLOCK, :] = jnp.concatenate([pad, kq[i]], axis=0)
        vvar[i, 0:BLOCK, :] = jnp.concatenate([pad, vq[i]], axis=0)
    uext[0:HALO, :] = meta_proj_ref[:, 0:D_POOL]


def _front_pieces(first_tile, x_ref, sink_ref, g_ref, b_ref, w_in_ref, w_pool_ref, pscale_ref, bias_ref,
                  mix_ref, nk_ref, nv_ref, np_ref, kvar, vvar, uext, qbuf, pbuf, hres):
    tile = x_ref.shape[0]
    halo = HALO
    low = lax.broadcasted_iota(jnp.int32, (1, LANES), 1) < HEAD_DIM

    for j in range(tile // IN_ROWS):
        rows = slice(j * IN_ROWS, (j + 1) * IN_ROWS)
        h = _layer_norm(x_ref[rows, :], g_ref[...], b_ref[...])
        hres[rows, :] = ALPHA * h
        proj = _dot(h.astype(BF16), w_in_ref[...])
        u = proj[:, 0:D_POOL]
        k = proj[:, D_POOL + D_ATTN:D_POOL + D_ATTN + D_KV]
        v = proj[:, D_POOL + D_ATTN + D_KV:]
        qbuf[rows, :] = (proj[:, D_POOL:D_POOL + D_ATTN] * Q_SCALE).astype(BF16)
        uext[halo + rows.start:halo + rows.stop, :] = u
        kq = _kv_lane_variants(k, low)
        vq = _kv_lane_variants(v, low)
        for i in range(4):
            kvar[i, BLOCK + rows.start:BLOCK + rows.stop, :] = kq[i]
            vvar[i, BLOCK + rows.start:BLOCK + rows.stop, :] = vq[i]
        if rows.stop == tile:
            nk_ref[...] = k[IN_ROWS - BLOCK:, :].T
            nv_ref[...] = v[IN_ROWS - BLOCK:, :].T
            np_ref[...] = u[IN_ROWS - halo:, :]
        yield

    for g, w in enumerate(POOL_WINDOWS):
        cols = slice(g * POOL_GROUP_DIM, (g + 1) * POOL_GROUP_DIM)
        ug = uext[halo:halo + tile, cols]
        s = ug
        for i in range(1, w):
            s = s + uext[halo - i:halo - i + tile, cols]
        pbuf[:, cols] = (s * (1.0 / w) - ug).astype(BF16)
        if g == len(POOL_WINDOWS) - 1:
            mix_ref[:, 0:D_POOL] = (_dot(pbuf[...], w_pool_ref[...]) * pscale_ref[...]).astype(BF16)
        yield

    for j in range(tile // BLOCK):
        rows = slice(j * BLOCK, (j + 1) * BLOCK)
        keys = slice(j * BLOCK, (j + 2) * BLOCK)
        sel = jnp.where(first_tile, 0, 1) if j == 0 else 1
        for kv in range(N_KV_HEADS):
            tiles = (2 * kv, 2 * kv + 1)
            lhs = jnp.concatenate([qbuf[rows, p * LANES:(p + 1) * LANES] for p in tiles], axis=0)
            nk = 2 * BLOCK
            s_all = _dot_nt(lhs, jnp.concatenate([kvar[2 * kv, keys, :], kvar[2 * kv + 1, keys, :]], axis=0))
            es, dens = [], []
            for half in range(2):
                heads = [2 * p + half for p in tiles]
                s = s_all[:, half * nk:(half + 1) * nk]
                s = s + jnp.concatenate([bias_ref[sel, hd] for hd in heads], axis=0)
                sink = jnp.concatenate([jnp.full((BLOCK, 1), sink_ref[0, hd], F32) for hd in heads], axis=0)
                m = jnp.maximum(jnp.max(s, axis=-1, keepdims=True), sink)
                e = jnp.exp(s - m)
                dens.append(jnp.sum(e, axis=-1, keepdims=True) + jnp.exp(sink - m))
                es.append(e.astype(BF16))
            o = _dot(jnp.concatenate(es, axis=0),
                     jnp.concatenate([vvar[2 * kv, keys, :], vvar[2 * kv + 1, keys, :]], axis=1))
            acc = (o[0:nk, 0:LANES] * (1.0 / dens[0]) + o[nk:, LANES:] * (1.0 / dens[1]))
            for n, p in enumerate(tiles):
                mix_ref[rows, D_POOL + p * LANES:D_POOL + (p + 1) * LANES] = (
                    acc[n * BLOCK:(n + 1) * BLOCK].astype(BF16))
            yield

    for i in range(4):
        kvar[i, 0:BLOCK, :] = kvar[i, tile:tile + BLOCK, :]
        vvar[i, 0:BLOCK, :] = vvar[i, tile:tile + BLOCK, :]
    uext[0:halo, :] = uext[tile:tile + halo, :]


def _sample_front_kernel(tab_ref, sink_ref, x_ref, ck_ref, cv_ref, st_ref, wf_out_ref,
                         bkt_p_ref, bkt_c_ref, bkt_n_ref, meta_ref, g_ref, b_ref, wf_in_ref, wf_pool_ref, pscale_ref,
                         mix_ref, nk_ref, nv_ref, np_ref, wb_out_ref,
                         bias_p_ref, meta_proj_ref, w_in_ref, w_pool_bd_ref,
                         o_s, bias_c_ref, bias_n_ref, w_pool_ref):
    @pl.when(pl.program_id(0) == 0)
    def _():
        _setup_tables(tab_ref, bkt_p_ref, bkt_c_ref, bkt_n_ref, meta_ref, g_ref, b_ref, wf_in_ref, wf_pool_ref,
                      bias_p_ref, bias_c_ref, bias_n_ref, meta_proj_ref, w_in_ref, w_pool_ref, w_pool_bd_ref)

    wb_out_ref[...] = wf_out_ref[...].astype(BF16)
    ns, tokens, d_model = x_ref.shape
    lane = lax.broadcasted_iota(jnp.int32, (1, LANES), 1)
    low = lane < HEAD_DIM

    x = jnp.concatenate([x_ref[:, t, :] for t in range(tokens)], axis=0)
    h = _layer_norm(x, g_ref[...], b_ref[...])
    proj = _dot(h.astype(BF16), w_in_ref[...])
    u = proj[:, 0:D_POOL]
    q = proj[:, D_POOL:D_POOL + D_ATTN] * Q_SCALE
    k = proj[:, D_POOL + D_ATTN:D_POOL + D_ATTN + D_KV]
    v = proj[:, D_POOL + D_ATTN + D_KV:]

    ext = [st_ref[r] for r in range(POOL_BUF)]
    ext += [u[t * ns:(t + 1) * ns] for t in range(tokens)]
    ps = []
    for t in range(tokens):
        idx = POOL_BUF + t
        pos = PAST_LEN - POOL_BUF + idx
        parts = []
        for g, w in enumerate(POOL_WINDOWS):
            cols = slice(g * POOL_GROUP_DIM, (g + 1) * POOL_GROUP_DIM)
            lo = max(idx + 1 - w, 0)
            s = ext[idx][:, cols]
            for r in range(idx - 1, lo - 1, -1):
                s = s + ext[r][:, cols]
            parts.append(s * (1.0 / min(w, pos + 1)) - ext[idx][:, cols])
        ps.append(jnp.concatenate(parts, axis=1))
    p = jnp.concatenate(ps, axis=0).astype(BF16)
    zs = [_dot(p[:, g * POOL_GROUP_DIM:(g + 1) * POOL_GROUP_DIM], w_pool_ref[g])
          for g in range(len(POOL_WINDOWS))]
    z = (jnp.concatenate(zs, axis=1) * pscale_ref[...]).astype(BF16)
    for r in range(POOL_BUF):
        np_ref[r] = ext[r + tokens]

    qh = []
    for hd in range(N_HEADS):
        tl = q[:, (hd // 2) * LANES:(hd // 2 + 1) * LANES]
        kv = hd // (N_HEADS // N_KV_HEADS)
        want_low = kv == 0
        is_low = hd % 2 == 0
        src = tl if want_low == is_low else pltpu.roll(tl, HEAD_DIM, axis=1)
        qh.append(jnp.where(low, src, 0.0) if want_low else jnp.where(low, 0.0, src))

    gr = SEQ_GROUP
    rows_per_head = tokens * gr
    sub = lax.broadcasted_iota(jnp.int32, (N_HEADS * rows_per_head, 1), 0) % gr
    sink = jnp.concatenate([jnp.full((rows_per_head, 1), sink_ref[0, hd], F32) for hd in range(N_HEADS)],
                           axis=0)
    zpad = jnp.zeros((LANES - rows_per_head, LANES), F32)
    zmid = jnp.zeros((WINDOW - 2 * SUBLANES, LANES), F32)
    sub8 = lax.broadcasted_iota(jnp.int32, (SUBLANES, 1), 0)
    keep = WINDOW - tokens

    def new_columns(k_news, v_news, j):
        head = jnp.zeros((SUBLANES, LANES), F32)
        tail = jnp.zeros((SUBLANES, LANES), F32)
        for t in range(tokens):
            row = SUBLANES - tokens + t
            tail = jnp.where(sub8 == row, pltpu.roll(k_news[t], (row - j) % SUBLANES, axis=0), tail)
            head = jnp.where(sub8 == t, pltpu.roll(v_news[t], (t - j) % SUBLANES, axis=0), head)
        return jnp.concatenate([head, zmid, tail], axis=0).T

    def appended_k(old_ref, cols):
        return jnp.where(lane >= keep, cols, pltpu.roll(old_ref[...], keep, axis=1))

    def appended_v(old_ref, cols):
        return pltpu.roll(jnp.where(lane < tokens, cols, old_ref[...]), keep, axis=1)

    for gi in range(ns // gr):
        base = gi * gr
        lhs = jnp.concatenate([qh[hd][t * ns + base:t * ns + base + gr]
                               for hd in range(N_HEADS) for t in range(tokens)], axis=0).astype(BF16)
        k_t = [k[t * ns + base:t * ns + base + gr] for t in range(tokens)]
        v_t = [v[t * ns + base:t * ns + base + gr] for t in range(tokens)]
        sc = None
        cols_g = []
        for s in range(gr):
            s_one = _dot(lhs, ck_ref[base + s].astype(BF16))
            cols_g.append(new_columns(k_t, v_t, s))
            nv_ref[base + s] = appended_v(cv_ref.at[base + s], cols_g[s])
            sc = s_one if sc is None else jnp.where(sub == s, s_one, sc)
        sc = sc + bias_c_ref[...]
        k_new = jnp.concatenate(k_t + [zpad], axis=0)
        v_new = jnp.concatenate(v_t + [zpad], axis=0)
        sn = _dot_nt(lhs, k_new.astype(BF16)) + bias_n_ref[...]
        m = jnp.maximum(jnp.max(jnp.maximum(sc, sn), axis=-1, keepdims=True), sink)
        ec = jnp.exp(sc - m)
        en = jnp.exp(sn - m)
        den = jnp.sum(ec + en, axis=-1, keepdims=True) + jnp.exp(sink - m)
        o = _dot(en.astype(BF16), v_new.astype(BF16))
        for s in range(gr):
            o = o + _dot_nt(jnp.where(sub == s, ec, 0.0).astype(BF16), cv_ref[base + s].astype(BF16))
            nk_ref[base + s] = appended_k(ck_ref.at[base + s], cols_g[s])
        o = o * (1.0 / den)
        for t in range(tokens):
            for p in range(N_HEADS // 2):
                a = o[(2 * p * tokens + t) * gr:(2 * p * tokens + t + 1) * gr]
                c = o[((2 * p + 1) * tokens + t) * gr:((2 * p + 1) * tokens + t + 1) * gr]
                if p < N_HEADS // 4:
                    c = pltpu.roll(c, HEAD_DIM, axis=1)
                else:
                    a = pltpu.roll(a, HEAD_DIM, axis=1)
                o_s[t * ns + base:t * ns + base + gr, p * LANES:(p + 1) * LANES] = jnp.where(low, a, c)

    for t in range(tokens):
        mix_ref[:, t * d_model:t * d_model + D_POOL] = z[t * ns:(t + 1) * ns]
        mix_ref[:, t * d_model + D_POOL:(t + 1) * d_model] = o_s[t * ns:(t + 1) * ns, :].astype(BF16)


def _pre_pieces(rows, res, mix, w_out_ref, g1_ref, b1_ref, h1_s, hb_s):
    h1_s[rows, :] = res() + _dot(mix(), w_out_ref[...])
    yield
    h = _layer_norm(h1_s[rows, :], g1_ref[...], b1_ref[...])
    h1_s[rows, :] = h
    hb_s[rows, :] = h.astype(BF16)
    yield


def _mlp_pieces(rows, a_s, h1_s, hb_s, w1_ref, w2_ref, g2_ref, b2_ref, y_ref):
    for c in range(w1_ref.shape[1] // UP_CHUNK):
        cols = slice(c * UP_CHUNK, (c + 1) * UP_CHUNK)
        a = _dot(hb_s[rows, :], w1_ref[:, cols])
        a_s[:, cols] = jnp.square(jnp.maximum(a, 0.0)).astype(BF16)
        if c == 0:
            y_ref[rows, :] = ALPHA * h1_s[rows, :]
        yield
    for n in range(y_ref.shape[1] // DOWN_CHUNK):
        cols = slice(n * DOWN_CHUNK, (n + 1) * DOWN_CHUNK)
        y_ref[rows, cols] = y_ref[rows, cols] + _dot(a_s[...], w2_ref[:, cols])
        yield
    y_ref[rows, :] = _layer_norm(y_ref[rows, :], g2_ref[...], b2_ref[...])
    yield


def _weave(main, n_main, sides):
    order = []
    for i in range(n_main):
        order.append(main)
        order += sides[i * len(sides) // n_main:(i + 1) * len(sides) // n_main]
    return order


def _main_kernel(tokens, tiles_per_seq, n_tiles,
                 sink_ref, x_ref, xs_ref, mixs_ref, ge_ref, be_ref, w_in_ref, w_pool_ref,
                 pscale_ref, bias_ref, meta_proj_ref, w_out_ref, g1_ref, b1_ref, g2_ref, b2_ref,
                 w1_hbm, w2_hbm,
                 y_ref, ys_ref, nk_ref, nv_ref, np_ref,
                 mixbuf, kvar, vvar, uext, qbuf, pbuf, hres, h1_s, hb_s, a_s, w1_ref, w2_ref, stage, sem):
    s = pl.program_id(0)
    is_first = s == 0
    first_tile = jnp.minimum(s, n_tiles - 1) % tiles_per_seq == 0
    tile, d_model = x_ref.shape
    ns = xs_ref.shape[0]
    half = tile // 2
    halves = (slice(0, half), slice(half, tile))
    pre_w = (w_out_ref, g1_ref, b1_ref, h1_s, hb_s)

    @pl.when(is_first)
    def _():
        def toks(rows):
            return range(rows.start // ns, rows.stop // ns)

        def sample_pre():
            for rows in halves:
                res = lambda: ALPHA * _layer_norm(jnp.concatenate([xs_ref[:, t, :] for t in toks(rows)], axis=0),
                                                  ge_ref[...], be_ref[...])
                mix = lambda: jnp.concatenate([mixs_ref[:, t * d_model:(t + 1) * d_model] for t in toks(rows)],
                                              axis=0)
                yield from _pre_pieces(rows, res, mix, *pre_w)

        cr, cc = stage.shape[1:]
        chunks = [(src, dst, r, c) for src, dst in ((w1_hbm, w1_ref), (w2_hbm, w2_ref))
                  for r in range(0, src.shape[0], cr) for c in range(0, src.shape[1], cc)]
        ring = [stage.at[i] for i in range(stage.shape[0])]
        ring += [buf.at[pl.ds(r, cr)] for buf in (hres, y_ref) for r in range(0, tile, cr)]
        slots = len(ring)

        def copy(i):
            src, _, r, c = chunks[i]
            return pltpu.make_async_copy(src.at[pl.ds(r, cr), pl.ds(c, cc)], ring[i % slots], sem.at[i % slots])

        for i in range(slots):
            copy(i).start()
        pre = sample_pre()
        every = len(chunks) // 4
        for i, (_, dst, r, c) in enumerate(chunks):
            if i % every == every // 2:
                next(pre, None)
            copy(i).wait()
            dst[r:r + cr, c:c + cc] = ring[i % slots][...].astype(BF16)
            if i + slots < len(chunks):
                copy(i + slots).start()
        for _ in pre:
            raise AssertionError("piece left unscheduled")

    @pl.when(first_tile)
    def _():
        _carry_init(meta_proj_ref, kvar, vvar, uext)

    mlp = [_mlp_pieces(rows, a_s, h1_s, hb_s, w1_ref, w2_ref, g2_ref, b2_ref, y_ref) for rows in halves]
    front = _front_pieces(first_tile, x_ref, sink_ref, ge_ref, be_ref, w_in_ref, w_pool_ref, pscale_ref,
                          bias_ref, mixbuf, nk_ref.at[0], nv_ref.at[0], np_ref.at[0], kvar, vvar, uext, qbuf, pbuf,
                          hres)
    pre = [_pre_pieces(rows, lambda rows=rows: hres[rows, :], lambda rows=rows: mixbuf[rows, :], *pre_w)
           for rows in halves]
    a, b, f, pa, pb = mlp[0], mlp[1], front, pre[0], pre[1]
    n_up = w1_ref.shape[1] // UP_CHUNK
    n_down = d_model // DOWN_CHUNK
    order = (
        _weave(a, n_up, [f] * 4)
        + _weave(a, n_down, [f] * 5)
        + _weave(b, n_up, [a] + [f] * 5)
        + _weave(b, n_down, [f, pa, pb, pa, pb]) + [b])
    for g in order:
        next(g, None)
    for g in (a, b, f, pa, pb):
        for _ in g:
            raise AssertionError("piece left unscheduled")

    @pl.when(is_first)
    def _():
        for t in range(tokens):
            ys_ref[:, t, :] = y_ref[t * ns:(t + 1) * ns, :]


def _const_spec(shape):
    return pl.BlockSpec(shape, lambda *_: (0,) * len(shape), pipeline_mode=pl.Buffered(1))


def _smem_spec():
    return pl.BlockSpec(memory_space=pltpu.SMEM)


def _sample_front_call(rel_table, sink, xs, ck, cv, st, w_out, meta_tokens, ge, be, w_in, w_pool, pscale):
    n_seq, tokens, d_model = xs.shape
    steps = n_seq // SAMPLE_SEQS
    srows = N_HEADS * tokens * SEQ_GROUP
    tables = [jnp.asarray(t) for t in _bucket_tables(tokens)]
    consts = tables + [meta_tokens, ge, be, w_in, w_pool, pscale]
    whole = lambda shape: pl.BlockSpec(shape, lambda i: (0,) * len(shape))
    seq_spec = lambda *tail: pl.BlockSpec((SAMPLE_SEQS,) + tail, lambda i: (i,) + (0,) * len(tail))
    state_spec = pl.BlockSpec((POOL_BUF, SAMPLE_SEQS, D_POOL), lambda i: (0, i, 0))
    slab_spec = pl.BlockSpec((w_out.shape[0] // steps, w_out.shape[1]), lambda i: (i, 0))
    once = [jax.ShapeDtypeStruct((2, N_HEADS, BLOCK, 2 * BLOCK), F32),
            jax.ShapeDtypeStruct((meta_tokens.shape[0], w_in.shape[1]), F32),
            jax.ShapeDtypeStruct(w_in.shape, BF16),
            jax.ShapeDtypeStruct((D_POOL, D_POOL), BF16)]
    return pl.pallas_call(
        _sample_front_kernel,
        grid=(steps,),
        in_specs=[_smem_spec(), _smem_spec(), seq_spec(tokens, d_model), seq_spec(D_KV, WINDOW),
                  seq_spec(D_KV, WINDOW), state_spec, slab_spec] + [_const_spec(c.shape) for c in consts],
        out_specs=[seq_spec(tokens * d_model), seq_spec(D_KV, WINDOW), seq_spec(D_KV, WINDOW), state_spec,
                   slab_spec] + [whole(o.shape) for o in once],
        out_shape=[jax.ShapeDtypeStruct((n_seq, tokens * d_model), BF16),
                   jax.ShapeDtypeStruct((n_seq, D_KV, WINDOW), F32),
                   jax.ShapeDtypeStruct((n_seq, D_KV, WINDOW), F32),
                   jax.ShapeDtypeStruct((POOL_BUF, n_seq, D_POOL), F32),
                   jax.ShapeDtypeStruct(w_out.shape, BF16)] + once,
        scratch_shapes=[pltpu.VMEM((tokens * SAMPLE_SEQS, D_ATTN), F32),
                        pltpu.VMEM((srows, WINDOW), F32),
                        pltpu.VMEM((srows, LANES), F32),
                        pltpu.VMEM(w_pool.shape, BF16)],
        compiler_params=pltpu.CompilerParams(dimension_semantics=("arbitrary",),
                                             vmem_limit_bytes=VMEM_LIMIT_BYTES),
        name="sample_front",
    )(rel_table, sink, xs, ck, cv, st, w_out, *consts)


def kernel(x_prompt, x_sample, cache_win_k, cache_win_v, state_pool, meta_tokens, ln_emb_g, ln_emb_b,
           rel_table, w_in, w_pool, pool_scale, sinks, w_out, ln1_g, ln1_b, w_mlp_in, w_mlp_out,
           ln2_g, ln2_b):
    batch, seq, d_model = x_prompt.shape
    n_seq, tokens, _ = x_sample.shape
    assert w_in.shape[0] == DEPTH and d_model == D_POOL + D_ATTN
    assert seq % PROMPT_TILE == 0 and n_seq % SAMPLE_SEQS == 0 and tokens * n_seq == PROMPT_TILE
    assert cache_win_k.shape[2] == WINDOW and state_pool.shape[2] == POOL_BUF and tokens <= POOL_BUF
    assert N_META == HALO and POOL_BUF <= HALO

    row = lambda a: a.reshape(1, -1).astype(F32)
    ge, be = row(ln_emb_g), row(ln_emb_b)
    pscale = row(pool_scale[0])
    sink = row(sinks[0])

    to_kd_pos = lambda c: jnp.swapaxes(c[0].reshape(n_seq, WINDOW, D_KV), 1, 2)
    mix_s, nk_s, nv_s, np_s, w_out_b, bias_p, meta_proj, w_in_b, w_pool_bd = _sample_front_call(
        rel_table.astype(F32).T, sink, x_sample, to_kd_pos(cache_win_k), to_kd_pos(cache_win_v),
        jnp.swapaxes(state_pool[0], 0, 1), w_out[0], meta_tokens.astype(F32), ge, be, w_in[0], w_pool[0], pscale)
    xs = x_sample
    w1, w2 = w_mlp_in[0], w_mlp_out[0]

    tile = PROMPT_TILE
    tiles_per_seq = seq // tile
    n_tiles = batch * tiles_per_seq
    xp = x_prompt.reshape(batch * seq, d_model)
    front_tile = lambda s: jnp.minimum(s, n_tiles - 1)
    finish_tile = lambda s: jnp.maximum(s - 1, 0)
    seq_of = lambda s: front_tile(s) // tiles_per_seq
    consts = (ge, be, w_in_b, w_pool_bd, pscale, bias_p, meta_proj, w_out_b, row(ln1_g[0]), row(ln1_b[0]),
              row(ln2_g[0]), row(ln2_b[0]))
    y_p, y_s, nk_p, nv_p, np_p = pl.pallas_call(
        functools.partial(_main_kernel, tokens, tiles_per_seq, n_tiles),
        grid=(n_tiles + 1,),
        in_specs=[_smem_spec(),
                  pl.BlockSpec((tile, d_model), lambda s: (front_tile(s), 0)),
                  _const_spec(xs.shape), _const_spec(mix_s.shape)] + [_const_spec(c.shape) for c in consts]
        + [pl.BlockSpec(memory_space=pl.ANY)] * 2,
        out_specs=[pl.BlockSpec((tile, d_model), lambda s: (finish_tile(s), 0)),
                   pl.BlockSpec(xs.shape, lambda s: (0, 0, 0)),
                   pl.BlockSpec((1, BLOCK, D_KV), lambda s: (seq_of(s), 0, 0)),
                   pl.BlockSpec((1, BLOCK, D_KV), lambda s: (seq_of(s), 0, 0)),
                   pl.BlockSpec((1, HALO, D_POOL), lambda s: (seq_of(s), 0, 0))],
        out_shape=[jax.ShapeDtypeStruct((batch * seq, d_model), F32),
                   jax.ShapeDtypeStruct(xs.shape, F32),
                   jax.ShapeDtypeStruct((batch, BLOCK, D_KV), F32),
                   jax.ShapeDtypeStruct((batch, BLOCK, D_KV), F32),
                   jax.ShapeDtypeStruct((batch, HALO, D_POOL), F32)],
        scratch_shapes=[pltpu.VMEM((tile, d_model), BF16),
                        pltpu.VMEM((4, tile + BLOCK, LANES), BF16),
                        pltpu.VMEM((4, tile + BLOCK, LANES), BF16),
                        pltpu.VMEM((tile + HALO, D_POOL), F32),
                        pltpu.VMEM((tile, D_ATTN), BF16),
                        pltpu.VMEM((tile, D_POOL), BF16),
                        pltpu.VMEM((tile, d_model), F32),
                        pltpu.VMEM((tile, d_model), F32),
                        pltpu.VMEM((tile, d_model), BF16),
                        pltpu.VMEM((tile // 2, w1.shape[1]), BF16),
                        pltpu.VMEM(w1.shape, BF16),
                        pltpu.VMEM(w2.shape, BF16),
                        pltpu.VMEM((WEIGHT_SLOTS,) + WEIGHT_CHUNK, F32),
                        pltpu.SemaphoreType.DMA((WEIGHT_SLOTS + 2 * (tile // WEIGHT_CHUNK[0]),))],
        compiler_params=pltpu.CompilerParams(dimension_semantics=("arbitrary",),
                                             vmem_limit_bytes=VMEM_LIMIT_BYTES),
        name="main",
    )(sink, xp, xs, mix_s, *consts, w1, w2)

    kv_shape = (DEPTH, -1, WINDOW, N_KV_HEADS, HEAD_DIM)
    from_kd_pos = lambda c: jnp.swapaxes(c, 1, 2).reshape(kv_shape)
    return (y_p.reshape(batch, seq, d_model), y_s,
            from_kd_pos(nk_p), from_kd_pos(nv_p),
            np_p[:, HALO - POOL_BUF:, :].reshape(DEPTH, batch, POOL_BUF, D_POOL),
            from_kd_pos(nk_s), from_kd_pos(nv_s), jnp.swapaxes(np_s, 0, 1)[None])
```
